```python
import math
import jax, jax.numpy as jnp
from jax import lax
import numpy as np

D_MODEL = 2048
BATCH = 2
SEQ = 16384
DEPTH = 1

GRID_W = 64
CTX_LEN = 256
NORM_EPS = 1e-6
S5_WIDTH = 1024
S5_GROUP = 16
S5_GROUPS = S5_WIDTH // S5_GROUP
S5_STATE = 64
SCAN_CHUNK = 128
S5_DT_MIN = 1e-3
S5_DT_MAX = 1e-1
HY_WIDTH = 1024
HY_ORDER = 2
HY_SHORT = 3
HY_EMB = 33
HY_BANDS = (HY_EMB - 1) // 2
HY_FFN = 64
HY_FILT = 2 * HY_ORDER * HY_WIDTH
HY_DECAY_MIN = -math.log(1e-2) / 1.5
HY_DECAY_MAX = -math.log(1e-2) / 0.3
HY_NORM_EPS = 1e-6
IN_COLS = S5_WIDTH + 3 * HY_WIDTH + 2 * D_MODEL
N_EXPERTS = 32
TOP_K = 4
D_EXPERT = 2048
SWIGLU_LIMIT = 7.0
SWIGLU_ALPHA = 1.702
EXPERT_BLOCK = 256

kernel_name = 'hybrid_s5_hyena_moe_diffusion_block'


def rms_norm(x, g):
    xf = x.astype(jnp.float32)
    y = xf * lax.rsqrt(jnp.mean(xf * xf, axis=-1, keepdims=True) + NORM_EPS)
    return (y * g.astype(jnp.float32)).astype(x.dtype)


def modulate(x, shift, scale):
    return x * (1 + scale) + shift


def to_col_major(z):
    b, n, ch = z.shape
    rows = n // GRID_W
    return z.reshape(b, rows, GRID_W, ch).swapaxes(1, 2).reshape(b, n, ch)


def from_col_major(z):
    b, n, ch = z.shape
    rows = n // GRID_W
    return z.reshape(b, GRID_W, rows, ch).swapaxes(1, 2).reshape(b, n, ch)


def _scan_combine(left, right):
    a_l, b_l = left
    a_r, b_r = right
    return a_l * a_r, a_r * b_l + b_r


def s5_scan(u, lam_bar, b_bar, c_out, s0):
    bsz, seqlen = u.shape[0], u.shape[1]
    n_chunks = seqlen // SCAN_CHUNK
    u_chunks = u.reshape(bsz, n_chunks, SCAN_CHUNK, S5_GROUPS, S5_GROUP).swapaxes(0, 1)
    a = jnp.broadcast_to(lam_bar, (bsz, SCAN_CHUNK, S5_GROUPS, S5_STATE))

    def step(s, u_blk):
        bu = jnp.einsum('btgc,gpc->btgp', u_blk.astype(jnp.complex64), b_bar)
        a_cum, b_cum = lax.associative_scan(_scan_combine, (a, bu), axis=1)
        states = b_cum + a_cum * s[:, None]
        y = jnp.einsum('btgp,gcp->btgc', states, c_out).real
        return states[:, -1], y

    s_last, y = lax.scan(step, s0, u_chunks)
    return y.swapaxes(0, 1).reshape(bsz, seqlen, S5_WIDTH), s_last


def s5_discretise(lam_re, lam_im, log_dt, b_re, b_im, c_re, c_im):
    lam = lax.complex(lam_re.astype(jnp.float32), lam_im.astype(jnp.float32))
    dt = jnp.exp(log_dt.astype(jnp.float32))[:, None]
    lam_bar = jnp.exp(lam * dt)
    b = lax.complex(b_re.astype(jnp.float32), b_im.astype(jnp.float32))
    b_bar = ((lam_bar - 1) / lam)[..., None] * b
    c_out = lax.complex(c_re.astype(jnp.float32), c_im.astype(jnp.float32))
    return lam_bar, b_bar, c_out


def s5_branch(ua, uca, lam_re, lam_im, log_dt, b_re, b_im, c_re, c_im, d_skip, w_glu, ctx_out):
    out_dtype = ua.dtype
    fwd = s5_discretise(lam_re[0], lam_im[0], log_dt[0], b_re[0], b_im[0], c_re[0], c_im[0])
    bwd = s5_discretise(lam_re[1], lam_im[1], log_dt[1], b_re[1], b_im[1], c_re[1], c_im[1])
    u_lat = ua.astype(jnp.float32)
    u_ctx = uca.astype(jnp.float32)

    def grouped(u):
        return u.reshape(u.shape[0], u.shape[1], S5_GROUPS, S5_GROUP)

    s0 = jnp.zeros((u_ctx.shape[0], S5_GROUPS, S5_STATE), jnp.complex64)
    yc_f, sc_f = s5_scan(grouped(u_ctx), *fwd, s0)
    yc_b, sc_b = s5_scan(grouped(u_ctx[:, ::-1]), *bwd, s0)
    yl_f, _ = s5_scan(grouped(u_lat), *fwd, sc_f)
    yl_b, _ = s5_scan(grouped(u_lat[:, ::-1]), *bwd, sc_b)
    d = d_skip.astype(jnp.float32)
    wg = w_glu.astype(jnp.float32)

    def readout(y, u):
        y = jax.nn.gelu(y + u * d)
        return (y * jax.nn.sigmoid(y @ wg)).astype(out_dtype)

    y_lat = readout(yl_f + yl_b[:, ::-1], u_lat)
    y_ctx = readout(yc_f + yc_b[:, ::-1], u_ctx) if ctx_out else None
    return y_lat, y_ctx


def short_conv(z, w):
    ch = z.shape[-1]
    return lax.conv_general_dilated(z, w[:, None, :], window_strides=(1,),
                                    padding=[(HY_SHORT // 2, HY_SHORT // 2)],
                                    dimension_numbers=('NWC', 'WIO', 'NWC'),
                                    feature_group_count=ch)


def hyena_kernel_fft(seqlen, w1, b1, f1, w2, b2, f2, w3, b3, decay):
    f32 = jnp.float32
    t_idx = jnp.arange(seqlen, dtype=f32)
    t = t_idx / (seqlen - 1)
    bands = jnp.linspace(1e-4, HY_BANDS - 1, HY_BANDS, dtype=f32)
    ang = (2 * jnp.pi / seqlen) * t_idx[:, None] * bands[None, :]
    feats = jnp.concatenate([t[:, None], jnp.cos(ang), -jnp.sin(ang)], axis=-1)
    hid = jnp.sin(f1.astype(f32) * (feats @ w1.astype(f32) + b1.astype(f32)))
    hid = jnp.sin(f2.astype(f32) * (hid @ w2.astype(f32) + b2.astype(f32)))
    filt = (hid @ w3.astype(f32) + b3.astype(f32)) * jnp.exp(-t[:, None] * jnp.abs(decay.astype(f32)))
    filt = filt.reshape(seqlen, 2, HY_ORDER, HY_WIDTH)
    two_sided = jnp.concatenate([filt[:, 0], jnp.zeros((1, HY_ORDER, HY_WIDTH), f32),
                                 filt[:0:-1, 1]], axis=0)
    two_sided = two_sided / (jnp.sum(jnp.abs(two_sided), axis=0, keepdims=True) + HY_NORM_EPS)
    return jnp.fft.rfft(two_sided, axis=0)


def fft_conv(z, k_f, bias):
    seqlen = z.shape[1]
    n = 2 * seqlen
    y = jnp.fft.irfft(jnp.fft.rfft(z, n=n, axis=1) * k_f[None], n=n, axis=1)[:, :seqlen]
    return y + z * bias


def hyena_branch(z_in, short_w, w1, b1, f1, w2, b2, f2, w3, b3, decay, hy_bias):
    out_dtype = z_in.dtype
    seqlen = z_in.shape[1]
    z = short_conv(z_in, short_w.astype(out_dtype)).astype(jnp.float32)
    v, x1, x2 = jnp.split(z, 3, axis=-1)
    k_f = hyena_kernel_fft(seqlen, w1, b1, f1, w2, b2, f2, w3, b3, decay)
    bias = hy_bias.astype(jnp.float32)
    y = x1 * fft_conv(v, k_f[:, 0], bias[0])
    y = x2 * fft_conv(y, k_f[:, 1], bias[1])
    return y.astype(out_dtype)


def merge_branches(ya, yh, ug, w_bs5, w_bhy, w_o):
    ga, gh = jnp.split(jax.nn.sigmoid(ug), 2, axis=-1)
    return (ga * (ya @ w_bs5) + gh * (yh @ w_bhy)) @ w_o


def moe_ffn(h, router_w, router_b, w_gate_up, b_gate_up, w_down, b_down):
    n_tok, d = h.shape
    logits = (h @ router_w + router_b).astype(jnp.float32)
    top_val, top_idx = lax.top_k(logits, TOP_K)
    gates = jax.nn.softmax(top_val, axis=-1)
    n_assign = n_tok * TOP_K
    flat_e = top_idx.reshape(-1).astype(jnp.int32)
    flat_tok = jnp.arange(n_assign, dtype=jnp.int32) // TOP_K
    order = jnp.argsort(flat_e)
    e_sorted = flat_e[order]
    counts = jnp.bincount(flat_e, length=N_EXPERTS).astype(jnp.int32)
    padded = (counts + EXPERT_BLOCK - 1) // EXPERT_BLOCK * EXPERT_BLOCK
    start = jnp.cumsum(counts) - counts
    padded_end = jnp.cumsum(padded)
    padded_start = padded_end - padded
    dest = padded_start[e_sorted] + jnp.arange(n_assign, dtype=jnp.int32) - start[e_sorted]
    n_blocks = -(-n_assign // EXPERT_BLOCK) + N_EXPERTS
    cap = n_blocks * EXPERT_BLOCK
    slot_tok = jnp.full((cap,), n_tok, jnp.int32).at[dest].set(flat_tok[order])
    slot_gate = jnp.zeros((cap,), jnp.float32).at[dest].set(gates.reshape(-1)[order])
    block_start = jnp.arange(n_blocks, dtype=jnp.int32) * EXPERT_BLOCK
    block_expert = jnp.minimum(jnp.searchsorted(padded_end, block_start, side='right'),
                               N_EXPERTS - 1).astype(jnp.int32)
    h_pad = jnp.concatenate([h, jnp.zeros((1, d), h.dtype)], axis=0)

    def expert_block(acc, blk):
        tok, gate, e = blk
        gu = h_pad[tok] @ w_gate_up[e] + b_gate_up[e]
        glu = jnp.minimum(gu[:, 0::2], SWIGLU_LIMIT)
        lin = jnp.clip(gu[:, 1::2], -SWIGLU_LIMIT, SWIGLU_LIMIT)
        act = glu * jax.nn.sigmoid(SWIGLU_ALPHA * glu) * (lin + 1)
        y = act @ w_down[e] + b_down[e]
        return acc.at[tok].add(y * gate[:, None].astype(y.dtype)), None

    acc, _ = lax.scan(expert_block, jnp.zeros((n_tok + 1, d), h.dtype),
                      (slot_tok.reshape(n_blocks, EXPERT_BLOCK),
                       slot_gate.reshape(n_blocks, EXPERT_BLOCK), block_expert))
    return acc[:n_tok]


def setup_inputs(seed: int = 0) -> dict:
    key = jax.random.key(seed)
    ks = iter(jax.random.split(key, 48))
    f32 = jnp.float32

    def nrm(shape, scale):
        return scale * jax.random.normal(next(ks), shape, f32)

    def unif(shape, lo, hi):
        return jax.random.uniform(next(ks), shape, f32, lo, hi)

    L, G, P, GS = DEPTH, S5_GROUPS, S5_STATE, S5_GROUP
    lam_im_base = jnp.pi * jnp.arange(P, dtype=f32)
    return {
        'x': nrm((BATCH, SEQ, D_MODEL), 1.0),
        'c': nrm((BATCH, D_MODEL), 1.0),
        'ctx': nrm((BATCH, CTX_LEN, D_MODEL), 1.0),
        'c_ctx': nrm((D_MODEL,), 1.0),
        'w_ada': nrm((L, D_MODEL, 6 * D_MODEL), 0.5 * D_MODEL ** -0.5),
        'b_ada': nrm((L, 6 * D_MODEL), 0.02),
        'norm1_g': 1.0 + nrm((L, D_MODEL), 0.01),
        'norm2_g': 1.0 + nrm((L, D_MODEL), 0.01),
        'w_in': nrm((L, D_MODEL, IN_COLS), D_MODEL ** -0.5),
        's5_lam_re': -0.5 * jnp.exp(nrm((L, 2, G, P), 0.05)),
        's5_lam_im': lam_im_base + nrm((L, 2, G, P), 0.01),
        's5_log_dt': unif((L, 2, G), math.log(S5_DT_MIN), math.log(S5_DT_MAX)),
        's5_b_re': nrm((L, 2, G, P, GS), (2 * GS) ** -0.5),
        's5_b_im': nrm((L, 2, G, P, GS), (2 * GS) ** -0.5),
        's5_c_re': nrm((L, 2, G, GS, P), (2 * P) ** -0.5),
        's5_c_im': nrm((L, 2, G, GS, P), (2 * P) ** -0.5),
        's5_d': nrm((L, S5_WIDTH), 1.0),
        's5_w_glu': nrm((L, S5_WIDTH, S5_WIDTH), S5_WIDTH ** -0.5),
        'hy_short_w': nrm((L, HY_SHORT, 3 * HY_WIDTH), HY_SHORT ** -0.5),
        'hy_pos_w1': nrm((L, HY_EMB, HY_FFN), HY_EMB ** -0.5),
        'hy_pos_b1': nrm((L, HY_FFN), 0.1),
        'hy_freq1': 1.0 + nrm((L, HY_FFN), 0.1),
        'hy_pos_w2': nrm((L, HY_FFN, HY_FFN), HY_FFN ** -0.5),
        'hy_pos_b2': nrm((L, HY_FFN), 0.1),
        'hy_freq2': 1.0 + nrm((L, HY_FFN), 0.1),
        'hy_pos_w3': nrm((L, HY_FFN, HY_FILT), HY_FFN ** -0.5),
        'hy_pos_b3': nrm((L, HY_FILT), 0.02),
        'hy_decay': unif((L, HY_FILT), HY_DECAY_MIN, HY_DECAY_MAX),
        'hy_bias': nrm((L, HY_ORDER, HY_WIDTH), 1.0),
        'w_branch_s5': nrm((L, S5_WIDTH, D_MODEL), S5_WIDTH ** -0.5),
        'w_branch_hy': nrm((L, HY_WIDTH, D_MODEL), HY_WIDTH ** -0.5),
        'w_out': nrm((L, D_MODEL, D_MODEL), D_MODEL ** -0.5),
        'router_w': nrm((L, D_MODEL, N_EXPERTS), D_MODEL ** -0.5),
        'router_b': nrm((L, N_EXPERTS), 0.01),
        'w_gate_up': nrm((L, N_EXPERTS, D_MODEL, 2 * D_EXPERT), D_MODEL ** -0.5),
        'b_gate_up': nrm((L, N_EXPERTS, 2 * D_EXPERT), 0.01),
        'w_down': nrm((L, N_EXPERTS, D_EXPERT, D_MODEL), D_EXPERT ** -0.5),
        'b_down': nrm((L, N_EXPERTS, D_MODEL), 0.01),
        'final_g': 1.0 + nrm((D_MODEL,), 0.01),
    }


def reference(x, c, ctx, c_ctx, w_ada, b_ada, norm1_g, norm2_g, w_in,
              s5_lam_re, s5_lam_im, s5_log_dt, s5_b_re, s5_b_im, s5_c_re, s5_c_im,
              s5_d, s5_w_glu, hy_short_w, hy_pos_w1, hy_pos_b1, hy_freq1,
              hy_pos_w2, hy_pos_b2, hy_freq2, hy_pos_w3, hy_pos_b3, hy_decay, hy_bias,
              w_branch_s5, w_branch_hy, w_out, router_w, router_b,
              w_gate_up, b_gate_up, w_down, b_down, final_g):
    h, hc = x, ctx
    for i in range(DEPTH):
        last = i + 1 == DEPTH
        mod = jax.nn.silu(c) @ w_ada[i] + b_ada[i]
        sh1, sc1, g1, sh2, sc2, g2 = [m[:, None, :] for m in jnp.split(mod, 6, axis=-1)]
        mod_c = jax.nn.silu(c_ctx) @ w_ada[i] + b_ada[i]
        csh1, csc1, cg1, csh2, csc2, cg2 = jnp.split(mod_c, 6, axis=-1)
        filt_p = (hy_pos_w1[i], hy_pos_b1[i], hy_freq1[i], hy_pos_w2[i], hy_pos_b2[i],
                  hy_freq2[i], hy_pos_w3[i], hy_pos_b3[i], hy_decay[i], hy_bias[i])

        u = modulate(rms_norm(h, norm1_g[i]), sh1, sc1)
        uc = modulate(rms_norm(hc, norm1_g[i]), csh1, csc1)
        ua, uh, ug = jnp.split(u @ w_in[i], [S5_WIDTH, S5_WIDTH + 3 * HY_WIDTH], axis=-1)
        if last:
            uca = uc @ w_in[i][:, :S5_WIDTH]
        else:
            uca, uch, ucg = jnp.split(uc @ w_in[i], [S5_WIDTH, S5_WIDTH + 3 * HY_WIDTH], axis=-1)
        ya, yca = s5_branch(ua, uca, s5_lam_re[i], s5_lam_im[i], s5_log_dt[i],
                            s5_b_re[i], s5_b_im[i], s5_c_re[i], s5_c_im[i],
                            s5_d[i], s5_w_glu[i], not last)
        yh = from_col_major(hyena_branch(to_col_major(uh), hy_short_w[i], *filt_p))
        h = h + g1 * merge_branches(ya, yh, ug, w_branch_s5[i], w_branch_hy[i], w_out[i])

        hm = modulate(rms_norm(h, norm2_g[i]), sh2, sc2)
        h = h + g2 * moe_ffn(hm.reshape(-1, D_MODEL), router_w[i], router_b[i], w_gate_up[i],
                             b_gate_up[i], w_down[i], b_down[i]).reshape(h.shape)

        if not last:
            ych = hyena_branch(uch, hy_short_w[i], *filt_p)
            hc = hc + cg1 * merge_branches(yca, ych, ucg, w_branch_s5[i], w_branch_hy[i], w_out[i])
            hcm = modulate(rms_norm(hc, norm2_g[i]), csh2, csc2)
            hc = hc + cg2 * moe_ffn(hcm.reshape(-1, D_MODEL), router_w[i], router_b[i], w_gate_up[i],
                                    b_gate_up[i], w_down[i], b_down[i]).reshape(hc.shape)
    return rms_norm(h, final_g)
```

```python
import functools
import math

import jax
import jax.numpy as jnp
from jax import lax
from jax.experimental import pallas as pl
from jax.experimental.pallas import tpu as pltpu

F32 = jnp.float32
BF16 = jnp.bfloat16
HIGHEST = lax.Precision.HIGHEST

GRID_W = 64
NORM_EPS = 1e-6
S5_GROUP = 16
S5_CHUNK = 32
HY_ORDER = 2
HY_SHORT = 3
HY_NORM_EPS = 1e-6
TOP_K = 4
SWIGLU_LIMIT = 7.0
SWIGLU_ALPHA = 1.702
MOE_TM = 512
VMEM_LIMIT = 56 * 1024 * 1024


def _params(sem, vmem=VMEM_LIMIT):
    return pltpu.CompilerParams(dimension_semantics=sem, vmem_limit_bytes=vmem)


def _dot(a, b):
    return jnp.dot(a, b, preferred_element_type=F32)


def _ada_kernel(c_ref, w_ref, b_ref, o_ref):
    c = c_ref[...]
    a = c * jax.nn.sigmoid(c)
    o_ref[...] = jnp.dot(a, w_ref[...], preferred_element_type=F32, precision=HIGHEST) + b_ref[...]


def _ada_mod(cc, w, b):
    d, n6 = w.shape
    tn = min(512, n6)
    return pl.pallas_call(
        _ada_kernel,
        grid=(n6 // tn,),
        in_specs=[pl.BlockSpec((8, d), lambda j: (0, 0)),
                  pl.BlockSpec((d, tn), lambda j: (0, j)),
                  pl.BlockSpec((1, tn), lambda j: (0, j))],
        out_specs=pl.BlockSpec((8, tn), lambda j: (0, j)),
        out_shape=jax.ShapeDtypeStruct((8, n6), F32),
        compiler_params=_params(("parallel",)),
        name="ada_mod",
    )(cc, w, b)


def _inproj_kernel(x_ref, g_ref, sh_ref, sc_ref, w_ref, *rest, bounds, acts):
    outs = rest[:len(bounds)]
    xn_ref = rest[len(bounds)]
    j = pl.program_id(1)

    @pl.when(j == 0)
    def _():
        x = x_ref[...].astype(F32)
        y = x * lax.rsqrt(jnp.mean(x * x, axis=-1, keepdims=True) + NORM_EPS)
        y = y * g_ref[...]
        y = y * (1.0 + sc_ref[...]) + sh_ref[...]
        xn_ref[...] = y.astype(BF16)

    for (j0, j1), act, o_ref in zip(bounds, acts, outs):
        @pl.when((j >= j0) & (j < j1))
        def _(o_ref=o_ref, act=act):
            r = _dot(xn_ref[...], w_ref[...])
            if act:
                r = jax.nn.sigmoid(r)
            o_ref[...] = r.astype(o_ref.dtype)


def _norm_mod_proj(x2d, g, sh, sc, w, widths, acts, rows_per_mod, tm, tn):
    n, d = x2d.shape
    ncols = w.shape[1]
    tm = min(tm, n)
    tn = min(tn, min(widths))
    bounds, off = [], 0
    for wd in widths:
        bounds.append((off // tn, (off + wd) // tn))
        off += wd
    tiles_per_mod = rows_per_mod // tm

    def out_map(i, j, j0, nj):
        return (i, jnp.clip(j - j0, 0, nj - 1))

    out_specs = [pl.BlockSpec((tm, tn), functools.partial(out_map, j0=j0, nj=j1 - j0))
                 for (j0, j1) in bounds]
    out_shape = [jax.ShapeDtypeStruct((n, wd), BF16) for wd in widths]
    return pl.pallas_call(
        functools.partial(_inproj_kernel, bounds=tuple(bounds), acts=tuple(acts)),
        grid=(n // tm, ncols // tn),
        in_specs=[pl.BlockSpec((tm, d), lambda i, j: (i, 0)),
                  pl.BlockSpec((1, d), lambda i, j: (0, 0)),
                  pl.BlockSpec((None, 1, d), lambda i, j: (i // tiles_per_mod, 0, 0)),
                  pl.BlockSpec((None, 1, d), lambda i, j: (i // tiles_per_mod, 0, 0)),
                  pl.BlockSpec((d, tn), lambda i, j: (0, j))],
        out_specs=out_specs,
        out_shape=out_shape,
        scratch_shapes=[pltpu.VMEM((tm, d), BF16)],
        compiler_params=_params(("parallel", "arbitrary")),
        name="norm_mod_proj",
    )(x2d, g, sh, sc, w)


def _s5_matrices(lam_re, lam_im, log_dt, b_re, b_im, c_re, c_im, d_skip, t):
    lam = lax.complex(lam_re.astype(F32), lam_im.astype(F32))
    dt = jnp.exp(log_dt.astype(F32))[..., None]
    lam_dt = lam * dt
    lam_bar = jnp.exp(lam_dt)
    b_bar = ((lam_bar - 1) / lam)[..., None] * lax.complex(b_re.astype(F32), b_im.astype(F32))
    c_out = lax.complex(c_re.astype(F32), c_im.astype(F32))
    g, p = lam.shape[1], lam.shape[2]
    gs = b_bar.shape[-1]
    k = jnp.arange(t + 1, dtype=F32)
    pw = jnp.exp(lam_dt[..., None] * k)
    kk = jnp.einsum('dgcp,dgpk,dgpe->dgkce', c_out, pw[..., :t], b_bar, precision=HIGHEST).real
    kf, kb = kk[0], kk[1]
    zero_lag = kf[:, :1] + kb[:, :1]
    k_all = jnp.concatenate([kb[:, :0:-1], zero_lag, kf[:, 1:]], axis=1)
    jj = jnp.arange(t)
    lag_idx = jj[None, :] - jj[:, None] + (t - 1)
    m_intra = k_all[:, lag_idx]
    m_intra = m_intra.transpose(0, 1, 4, 2, 3).reshape(g, t * gs, t * gs)
    m_intra = m_intra + jnp.eye(t * gs, dtype=F32)[None] * jnp.tile(
        d_skip.astype(F32).reshape(g, 1, gs), (1, t, 1)).reshape(g, 1, t * gs)
    in_f = pw[0][..., t - 1 - jj][..., None] * b_bar[0][:, :, None, :]
    in_b = pw[1][..., jj][..., None] * b_bar[1][:, :, None, :]

    def to_in(z):
        return z.transpose(0, 2, 3, 1).reshape(g, t * gs, p)

    m_in = jnp.concatenate([to_in(in_f.real), to_in(in_f.imag), to_in(in_b.real), to_in(in_b.imag)], axis=-1)
    out_f = c_out[0][:, :, :, None] * pw[0][:, None, :, 1 + jj]
    out_b = c_out[1][:, :, :, None] * pw[1][:, None, :, t - jj]

    def to_out(z):
        return z.transpose(0, 2, 3, 1).reshape(g, p, t * gs)

    m_out = jnp.concatenate([to_out(out_f.real), -to_out(out_f.imag), to_out(out_b.real), -to_out(out_b.imag)],
                            axis=1)
    mu = pw[..., t]
    mre, mim = mu.real, mu.imag
    coef = jnp.stack([
        jnp.concatenate([mre[0], mre[0], mre[1], mre[1]], axis=-1),
        jnp.concatenate([-mim[0], mim[0], -mim[1], mim[1]], axis=-1),
        jnp.concatenate([mim[0], -mim[0], mim[1], -mim[1]], axis=-1)])
    return m_intra.astype(BF16), m_in.astype(BF16), m_out.astype(BF16), coef


def _s5_in_kernel(a_ref, m_ref, o_ref):
    o_ref[...] = _dot(a_ref[...], m_ref[...])


def _s5_chunk_states(a, m_in):
    g, nc, kdim = a.shape
    s = m_in.shape[2]
    nb = min(nc, 1024)
    return pl.pallas_call(
        _s5_in_kernel,
        grid=(g, nc // nb),
        in_specs=[pl.BlockSpec((None, nb, kdim), lambda gi, i: (gi, i, 0)),
                  pl.BlockSpec((None, kdim, s), lambda gi, i: (gi, 0, 0))],
        out_specs=pl.BlockSpec((None, nb, s), lambda gi, i: (gi, i, 0)),
        out_shape=jax.ShapeDtypeStruct((g, nc, s), F32),
        compiler_params=_params(("parallel", "arbitrary")),
        name="s5_chunk_states",
    )(a, m_in)


def _s5_scan_kernel(sf_ref, sb_ref, init_ref, coef_ref, ef_ref, eb_ref, fin_ref, st_ref, *, nblk, half):
    k = pl.program_id(1)
    nk = pl.num_programs(1)
    lanes = 2 * half

    @pl.when(k == 0)
    def _():
        s0f = init_ref[:, 0:lanes]
        s0b = init_ref[:, lanes:2 * lanes]
        st_ref[0] = s0f
        st_ref[1] = pltpu.roll(s0f, half, 1)
        st_ref[2] = s0b
        st_ref[3] = pltpu.roll(s0b, half, 1)

    def body(r, carry):
        sf, sfw, sb, sbw = carry
        rb = nblk - 1 - r
        xf = sf_ref[r]
        xb = sb_ref[rb]
        ef_ref[r] = sf
        eb_ref[rb] = sb
        xfw = pltpu.roll(xf, half, 1)
        xbw = pltpu.roll(xb, half, 1)
        af, bf, bfw = coef_ref[0, :, 0:lanes], coef_ref[1, :, 0:lanes], coef_ref[2, :, 0:lanes]
        ab, bb, bbw = (coef_ref[0, :, lanes:2 * lanes], coef_ref[1, :, lanes:2 * lanes],
                       coef_ref[2, :, lanes:2 * lanes])
        return (sf * af + sfw * bf + xf, sfw * af + sf * bfw + xfw,
                sb * ab + sbw * bb + xb, sbw * ab + sb * bbw + xbw)

    sf, sfw, sb, sbw = lax.fori_loop(0, nblk, body, (st_ref[0], st_ref[1], st_ref[2], st_ref[3]))
    st_ref[0] = sf
    st_ref[1] = sfw
    st_ref[2] = sb
    st_ref[3] = sbw

    @pl.when(k == nk - 1)
    def _():
        fin_ref[:, 0:lanes] = sf
        fin_ref[:, lanes:2 * lanes] = sb


def _s5_scan(s_t, init, coef):
    bsz, nc, g, s4 = s_t.shape
    lanes = s4 // 2
    nblk = min(nc, 128)
    nk = nc // nblk
    return pl.pallas_call(
        functools.partial(_s5_scan_kernel, nblk=nblk, half=lanes // 2),
        grid=(bsz, nk),
        in_specs=[pl.BlockSpec((None, nblk, g, lanes), lambda b, k: (b, k, 0, 0)),
                  pl.BlockSpec((None, nblk, g, lanes), lambda b, k: (b, nk - 1 - k, 0, 1)),
                  pl.BlockSpec((None, g, s4), lambda b, k: (b, 0, 0)),
                  pl.BlockSpec((3, g, s4), lambda b, k: (0, 0, 0))],
        out_specs=[pl.BlockSpec((None, nblk, g, lanes), lambda b, k: (b, k, 0, 0)),
                   pl.BlockSpec((None, nblk, g, lanes), lambda b, k: (b, nk - 1 - k, 0, 0)),
                   pl.BlockSpec((None, g, s4), lambda b, k: (b, 0, 0))],
        out_shape=[jax.ShapeDtypeStruct((bsz, nc, g, lanes), F32),
                   jax.ShapeDtypeStruct((bsz, nc, g, lanes), F32),
                   jax.ShapeDtypeStruct((bsz, g, s4), F32)],
        scratch_shapes=[pltpu.VMEM((4, g, lanes), F32)],
        compiler_params=_params(("parallel", "arbitrary")),
        name="s5_scan",
    )(s_t, s_t, init, coef)


def _s5_out_kernel(a_ref, e_ref, mi_ref, mo_ref, o_ref):
    y = _dot(a_ref[...], mi_ref[...]) + _dot(e_ref[...], mo_ref[...])
    o_ref[...] = y.astype(o_ref.dtype)


def _s5_chunk_outputs(a, e_in, m_intra, m_out):
    g, nc, kdim = a.shape
    s = e_in.shape[2]
    nb = min(nc, 1024)
    return pl.pallas_call(
        _s5_out_kernel,
        grid=(g, nc // nb),
        in_specs=[pl.BlockSpec((None, nb, kdim), lambda gi, i: (gi, i, 0)),
                  pl.BlockSpec((None, nb, s), lambda gi, i: (gi, i, 0)),
                  pl.BlockSpec((None, kdim, kdim), lambda gi, i: (gi, 0, 0)),
                  pl.BlockSpec((None, s, kdim), lambda gi, i: (gi, 0, 0))],
        out_specs=pl.BlockSpec((None, nb, kdim), lambda gi, i: (gi, i, 0)),
        out_shape=jax.ShapeDtypeStruct((g, nc, kdim), BF16),
        compiler_params=_params(("parallel", "arbitrary")),
        name="s5_chunk_outputs",
    )(a, e_in, m_intra, m_out)


def _s5_readout_kernel(y_ref, wg_ref, o_ref):
    y = jax.nn.gelu(y_ref[...].astype(F32))
    gate = _dot(y.astype(BF16), wg_ref[...])
    o_ref[...] = (y * jax.nn.sigmoid(gate)).astype(o_ref.dtype)


def _s5_readout(y2d, wg):
    n, w = y2d.shape
    tm = min(n, 1024)
    return pl.pallas_call(
        _s5_readout_kernel,
        grid=(n // tm,),
        in_specs=[pl.BlockSpec((tm, w), lambda i: (i, 0)),
                  pl.BlockSpec((w, w), lambda i: (0, 0))],
        out_specs=pl.BlockSpec((tm, w), lambda i: (i, 0)),
        out_shape=jax.ShapeDtypeStruct((n, w), BF16),
        compiler_params=_params(("parallel",)),
        name="s5_readout",
    )(y2d, wg)


def _to_chunks(u2d, g, t):
    n = u2d.shape[0]
    return u2d.reshape(n // t, t, g, S5_GROUP).transpose(2, 0, 1, 3).reshape(g, n // t, t * S5_GROUP)


def _from_chunks(y, g, t):
    nc = y.shape[1]
    return y.reshape(g, nc, t, S5_GROUP).transpose(1, 2, 0, 3).reshape(nc * t, g * S5_GROUP)


def _s5_branch(ua, uca, bsz, lam_re, lam_im, log_dt, b_re, b_im, c_re, c_im, d_skip, w_glu):
    g = lam_re.shape[1]
    t = S5_CHUNK
    m_intra, m_in, m_out, coef = _s5_matrices(lam_re, lam_im, log_dt, b_re, b_im, c_re, c_im, d_skip, t)
    s4 = m_in.shape[2]

    def states(u2d):
        a = _to_chunks(u2d, g, t)
        s = _s5_chunk_states(a, m_in)
        nc = s.shape[1] // bsz
        return a, s.reshape(g, bsz, nc, s4).transpose(1, 2, 0, 3)

    _, s_ctx = states(uca)
    _, _, seed = _s5_scan(s_ctx, jnp.zeros((bsz, g, s4), F32), coef)
    a_lat, s_lat = states(ua)
    e_f, e_b, _ = _s5_scan(s_lat, seed, coef)
    e_in = jnp.concatenate([e_f, e_b], axis=-1).astype(BF16)
    e_in = e_in.transpose(2, 0, 1, 3).reshape(g, -1, s4)
    y = _s5_chunk_outputs(a_lat, e_in, m_intra, m_out)
    return _s5_readout(_from_chunks(y, g, t), w_glu.astype(BF16))


def _to_col_major(z):
    b, n, ch = z.shape
    rows = n // GRID_W
    return z.reshape(b, rows, GRID_W, ch).swapaxes(1, 2).reshape(b, n, ch)


def _from_col_major(z):
    b, n, ch = z.shape
    rows = n // GRID_W
    return z.reshape(b, GRID_W, rows, ch).swapaxes(1, 2).reshape(b, n, ch)


def _hyena_kernel_fft(seqlen, w1, b1, f1, w2, b2, f2, w3, b3, decay, hy_width):
    t_idx = jnp.arange(seqlen, dtype=F32)
    t = t_idx / (seqlen - 1)
    n_bands = (w1.shape[0] - 1) // 2
    bands = jnp.linspace(1e-4, n_bands - 1, n_bands, dtype=F32)
    ang = (2 * jnp.pi / seqlen) * t_idx[:, None] * bands[None, :]
    feats = jnp.concatenate([t[:, None], jnp.cos(ang), -jnp.sin(ang)], axis=-1)
    hid = jnp.sin(f1 * (feats @ w1 + b1))
    hid = jnp.sin(f2 * (hid @ w2 + b2))
    filt = (hid @ w3 + b3) * jnp.exp(-t[:, None] * jnp.abs(decay))
    filt = filt.reshape(seqlen, 2, HY_ORDER, hy_width)
    two_sided = jnp.concatenate([filt[:, 0], jnp.zeros((1, HY_ORDER, hy_width), F32), filt[:0:-1, 1]], axis=0)
    two_sided = two_sided / (jnp.sum(jnp.abs(two_sided), axis=0, keepdims=True) + HY_NORM_EPS)
    return jnp.fft.rfft(two_sided, axis=0)


def _fft_conv(z, k_f, bias):
    seqlen = z.shape[1]
    n = 2 * seqlen
    y = jnp.fft.irfft(jnp.fft.rfft(z, n=n, axis=1) * k_f[None], n=n, axis=1)[:, :seqlen]
    return y + z * bias


def _hyena_branch(z_in, short_w, w1, b1, f1, w2, b2, f2, w3, b3, decay, hy_bias):
    seqlen = z_in.shape[1]
    ch = z_in.shape[-1]
    hy_width = ch // 3
    z = lax.conv_general_dilated(z_in, short_w[:, None, :], window_strides=(1,),
                                 padding=[(HY_SHORT // 2, HY_SHORT // 2)],
                                 dimension_numbers=('NWC', 'WIO', 'NWC'), feature_group_count=ch)
    v, x1, x2 = jnp.split(z, 3, axis=-1)
    k_f = _hyena_kernel_fft(seqlen, w1, b1, f1, w2, b2, f2, w3, b3, decay, hy_width)
    y = x1 * _fft_conv(v, k_f[:, 0], hy_bias[0])
    y = x2 * _fft_conv(y, k_f[:, 1], hy_bias[1])
    return y


def _merge1_kernel(ya_ref, yh_ref, ug_ref, wa_ref, wh_ref, o_ref):
    d = o_ref.shape[1]
    a = _dot(ya_ref[...], wa_ref[...])
    h = _dot(yh_ref[...], wh_ref[...])
    ga = ug_ref[:, :d].astype(F32)
    gh = ug_ref[:, d:].astype(F32)
    o_ref[...] = (ga * a + gh * h).astype(o_ref.dtype)


def _merge1(ya, yh, ug, wa, wh):
    n, ws = ya.shape
    wh_in = yh.shape[1]
    d = wa.shape[1]
    tm = min(n, 512)
    return pl.pallas_call(
        _merge1_kernel,
        grid=(n // tm,),
        in_specs=[pl.BlockSpec((tm, ws), lambda i: (i, 0)),
                  pl.BlockSpec((tm, wh_in), lambda i: (i, 0)),
                  pl.BlockSpec((tm, 2 * d), lambda i: (i, 0)),
                  pl.BlockSpec((ws, d), lambda i: (0, 0)),
                  pl.BlockSpec((wh_in, d), lambda i: (0, 0))],
        out_specs=pl.BlockSpec((tm, d), lambda i: (i, 0)),
        out_shape=jax.ShapeDtypeStruct((n, d), BF16),
        compiler_params=_params(("parallel",)),
        name="merge_branches",
    )(ya, yh, ug, wa, wh)


def _merge2_kernel(m_ref, x_ref, g1_ref, ng_ref, sh_ref, sc_ref, wo_ref, wr_ref,
                   h1_ref, hm_ref, r1_ref, r2_ref):
    mix = _dot(m_ref[...], wo_ref[...])
    h1 = x_ref[...] + g1_ref[...] * mix
    h1_ref[...] = h1
    y = h1 * lax.rsqrt(jnp.mean(h1 * h1, axis=-1, keepdims=True) + NORM_EPS)
    y = y * ng_ref[...]
    y = y * (1.0 + sc_ref[...]) + sh_ref[...]
    hi = y.astype(BF16)
    lo = (y - hi.astype(F32)).astype(BF16)
    hm_ref[...] = hi
    r1_ref[...] = _dot(hi, wr_ref[...])
    r2_ref[...] = _dot(lo, wr_ref[...])


def _merge2(m, x2d, g1, ng, sh2, sc2, wo, wr, rows_per_mod):
    n, d = x2d.shape
    tm = min(n, 512)
    tiles_per_mod = rows_per_mod // tm
    nr = wr.shape[1]
    mod_spec = pl.BlockSpec((None, 1, d), lambda i: (i // tiles_per_mod, 0, 0))
    return pl.pallas_call(
        _merge2_kernel,
        grid=(n // tm,),
        in_specs=[pl.BlockSpec((tm, d), lambda i: (i, 0)),
                  pl.BlockSpec((tm, d), lambda i: (i, 0)),
                  mod_spec,
                  pl.BlockSpec((1, d), lambda i: (0, 0)),
                  mod_spec, mod_spec,
                  pl.BlockSpec((d, d), lambda i: (0, 0)),
                  pl.BlockSpec((d, nr), lambda i: (0, 0))],
        out_specs=[pl.BlockSpec((tm, d), lambda i: (i, 0)),
                   pl.BlockSpec((tm, d), lambda i: (i, 0)),
                   pl.BlockSpec((tm, nr), lambda i: (i, 0)),
                   pl.BlockSpec((tm, nr), lambda i: (i, 0))],
        out_shape=[jax.ShapeDtypeStruct((n, d), F32),
                   jax.ShapeDtypeStruct((n, d), BF16),
                   jax.ShapeDtypeStruct((n, nr), F32),
                   jax.ShapeDtypeStruct((n, nr), F32)],
        compiler_params=_params(("parallel",)),
        name="out_proj_norm_router",
    )(m, x2d, g1, ng, sh2, sc2, wo, wr)


def _moe_kernel(be_ref, nv_ref, x_ref, wg_ref, wl_ref, bg_ref, bl_ref, wd_ref, bd_ref, gate_ref,
                o_ref, acc_ref):
    i = pl.program_id(0)
    j = pl.program_id(1)
    nj = pl.num_programs(1)

    @pl.when(i < nv_ref[0])
    def _():
        x = x_ref[...]
        glu = jnp.minimum(_dot(x, wg_ref[...]) + bg_ref[...], SWIGLU_LIMIT)
        lin = jnp.clip(_dot(x, wl_ref[...]) + bl_ref[...], -SWIGLU_LIMIT, SWIGLU_LIMIT)
        act = glu * jax.nn.sigmoid(SWIGLU_ALPHA * glu) * (lin + 1.0)
        part = _dot(act.astype(BF16), wd_ref[...])

        @pl.when(j == 0)
        def _():
            acc_ref[...] = part

        @pl.when(j > 0)
        def _():
            acc_ref[...] += part

        @pl.when(j == nj - 1)
        def _():
            o_ref[...] = ((acc_ref[...] + bd_ref[...]) * gate_ref[...]).astype(o_ref.dtype)


def _moe_blocks(xg, gate, block_expert, n_valid, wg, wl, bg, bl, wd, bd):
    cap, d = xg.shape
    f = wg.shape[2]
    tm = MOE_TM
    tf = min(f, 512)
    nj = f // tf
    nblk = cap // tm

    def row_map(i, j, be, nv):
        return (jnp.minimum(i, nv[0] - 1), 0)

    def jeff(i, j, nv):
        return jnp.where(i < nv[0], j, nj - 1)

    grid_spec = pltpu.PrefetchScalarGridSpec(
        num_scalar_prefetch=2,
        grid=(nblk, nj),
        in_specs=[pl.BlockSpec((tm, d), row_map),
                  pl.BlockSpec((None, d, tf), lambda i, j, be, nv: (be[i], 0, jeff(i, j, nv))),
                  pl.BlockSpec((None, d, tf), lambda i, j, be, nv: (be[i], 0, jeff(i, j, nv))),
                  pl.BlockSpec((None, 1, tf), lambda i, j, be, nv: (be[i], 0, jeff(i, j, nv))),
                  pl.BlockSpec((None, 1, tf), lambda i, j, be, nv: (be[i], 0, jeff(i, j, nv))),
                  pl.BlockSpec((None, tf, d), lambda i, j, be, nv: (be[i], jeff(i, j, nv), 0)),
                  pl.BlockSpec((None, 1, d), lambda i, j, be, nv: (be[i], 0, 0)),
                  pl.BlockSpec((tm, 1), row_map)],
        out_specs=pl.BlockSpec((tm, d), row_map),
        scratch_shapes=[pltpu.VMEM((tm, d), F32)])
    return pl.pallas_call(
        _moe_kernel,
        grid_spec=grid_spec,
        out_shape=jax.ShapeDtypeStruct((cap, d), BF16),
        compiler_params=_params(("arbitrary", "arbitrary")),
        name="moe_experts",
    )(block_expert, n_valid, xg, wg, wl, bg, bl, wd, bd, gate)


def _route(logits, n_experts):
    n_tok = logits.shape[0]
    top_val, top_idx = lax.top_k(logits, TOP_K)
    gates = jax.nn.softmax(top_val, axis=-1)
    n_assign = n_tok * TOP_K
    flat_e = top_idx.reshape(-1).astype(jnp.int32)
    order = jnp.argsort(flat_e)
    e_sorted = flat_e[order]
    counts = jnp.bincount(flat_e, length=n_experts).astype(jnp.int32)
    padded = (counts + MOE_TM - 1) // MOE_TM * MOE_TM
    start = jnp.cumsum(counts) - counts
    padded_end = jnp.cumsum(padded)
    padded_start = padded_end - padded
    dest_sorted = padded_start[e_sorted] + jnp.arange(n_assign, dtype=jnp.int32) - start[e_sorted]
    n_blocks = -(-n_assign // MOE_TM) + n_experts
    cap = n_blocks * MOE_TM
    flat_tok = jnp.arange(n_assign, dtype=jnp.int32) // TOP_K
    slot_tok = jnp.full((cap,), n_tok, jnp.int32).at[dest_sorted].set(flat_tok[order])
    slot_gate = jnp.zeros((cap,), F32).at[dest_sorted].set(gates.reshape(-1)[order])
    dest = jnp.zeros((n_assign,), jnp.int32).at[order].set(dest_sorted)
    n_valid = (padded_end[-1] // MOE_TM).astype(jnp.int32)
    block_start = jnp.minimum(jnp.arange(n_blocks, dtype=jnp.int32), n_valid - 1) * MOE_TM
    block_expert = jnp.minimum(jnp.searchsorted(padded_end, block_start, side='right'),
                               n_experts - 1).astype(jnp.int32)
    return slot_tok, slot_gate, dest, block_expert, n_valid.reshape(1)


def _final_kernel(h1_ref, yg_ref, g2_ref, fg_ref, o_ref):
    d = h1_ref.shape[1]
    moe = yg_ref[:, 0:d].astype(F32)
    for k in range(1, TOP_K):
        moe = moe + yg_ref[:, k * d:(k + 1) * d].astype(F32)
    h = h1_ref[...] + g2_ref[...] * moe
    y = h * lax.rsqrt(jnp.mean(h * h, axis=-1, keepdims=True) + NORM_EPS)
    o_ref[...] = (y * fg_ref[...]).astype(o_ref.dtype)


def _final(h1, yg, g2, fg, rows_per_mod):
    n, d = h1.shape
    tm = min(n, 512)
    tiles_per_mod = rows_per_mod // tm
    return pl.pallas_call(
        _final_kernel,
        grid=(n // tm,),
        in_specs=[pl.BlockSpec((tm, d), lambda i: (i, 0)),
                  pl.BlockSpec((tm, TOP_K * d), lambda i: (i, 0)),
                  pl.BlockSpec((None, 1, d), lambda i: (i // tiles_per_mod, 0, 0)),
                  pl.BlockSpec((1, d), lambda i: (0, 0))],
        out_specs=pl.BlockSpec((tm, d), lambda i: (i, 0)),
        out_shape=jax.ShapeDtypeStruct((n, d), F32),
        compiler_params=_params(("parallel",)),
        name="combine_final_norm",
    )(h1, yg, g2, fg)


def kernel(x, c, ctx, c_ctx, w_ada, b_ada, norm1_g, norm2_g, w_in, s5_lam_re, s5_lam_im, s5_log_dt, s5_b_re, s5_b_im, s5_c_re, s5_c_im, s5_d, s5_w_glu, hy_short_w, hy_pos_w1, hy_pos_b1, hy_freq1, hy_pos_w2, hy_pos_b2, hy_freq2, hy_pos_w3, hy_pos_b3, hy_decay, hy_bias, w_branch_s5, w_branch_hy, w_out, router_w, router_b, w_gate_up, b_gate_up, w_down, b_down, final_g):
    bsz, seq, d = x.shape
    ctx_len = ctx.shape[1]
    depth = w_ada.shape[0]
    assert depth == 1, "only the single-layer configuration is implemented"
    n_experts = router_w.shape[2]
    s5_w = s5_d.shape[1]
    hy_w = hy_bias.shape[2]
    n_tok = bsz * seq

    cc = jnp.zeros((8, d), F32).at[:bsz].set(c).at[bsz].set(c_ctx)
    mod = _ada_mod(cc, w_ada[0], b_ada[0][None]).reshape(8, 6, 1, d)
    sh1, sc1, g1, sh2, sc2, g2 = [mod[:bsz, k] for k in range(6)]
    csh1, csc1 = mod[bsz:bsz + 1, 0], mod[bsz:bsz + 1, 1]

    w_in_b = w_in[0].astype(BF16)
    x2d = x.reshape(n_tok, d)
    ua, uh, ug = _norm_mod_proj(x2d, norm1_g[0][None], sh1, sc1, w_in_b,
                                (s5_w, 3 * hy_w, 2 * d), (False, False, True), seq, 1024, 1024)
    (uca,) = _norm_mod_proj(ctx.reshape(bsz * ctx_len, d), norm1_g[0][None], csh1, csc1,
                            w_in_b[:, :s5_w], (s5_w,), (False,), bsz * ctx_len, 512, 1024)

    ya = _s5_branch(ua, uca, bsz, s5_lam_re[0], s5_lam_im[0], s5_log_dt[0], s5_b_re[0], s5_b_im[0],
                    s5_c_re[0], s5_c_im[0], s5_d[0], s5_w_glu[0])

    yh = _from_col_major(_hyena_branch(
        _to_col_major(uh.astype(F32).reshape(bsz, seq, 3 * hy_w)), hy_short_w[0], hy_pos_w1[0], hy_pos_b1[0],
        hy_freq1[0], hy_pos_w2[0], hy_pos_b2[0], hy_freq2[0], hy_pos_w3[0], hy_pos_b3[0], hy_decay[0],
        hy_bias[0]))
    yh = yh.reshape(n_tok, hy_w).astype(BF16)

    m = _merge1(ya, yh, ug, w_branch_s5[0].astype(BF16), w_branch_hy[0].astype(BF16))
    rw = router_w[0]
    rw_hi = rw.astype(BF16)
    rw_lo = (rw - rw_hi.astype(F32)).astype(BF16)
    wr = jnp.zeros((d, 128), BF16).at[:, :n_experts].set(rw_hi).at[:, n_experts:2 * n_experts].set(rw_lo)
    h1, hm, r1, r2 = _merge2(m, x2d, g1, norm2_g[0][None], sh2, sc2, w_out[0].astype(BF16), wr, seq)
    logits = (r1[:, :n_experts] + r1[:, n_experts:2 * n_experts] + r2[:, :n_experts]) + router_b[0]

    slot_tok, slot_gate, dest, block_expert, n_valid = _route(logits, n_experts)
    hm_pad = jnp.concatenate([hm, jnp.zeros((1, d), BF16)], axis=0)
    xg = hm_pad[slot_tok]
    wgu = w_gate_up[0]
    bgu = b_gate_up[0]
    yslots = _moe_blocks(xg, slot_gate[:, None], block_expert, n_valid,
                         wgu[:, :, 0::2].astype(BF16), wgu[:, :, 1::2].astype(BF16),
                         bgu[:, None, 0::2], bgu[:, None, 1::2],
                         w_down[0].astype(BF16), b_down[0][:, None, :])
    yg = yslots[dest].reshape(n_tok, TOP_K * d)
    out = _final(h1, yg, g2, final_g[None], seq)
    return out.reshape(bsz, seq, d)
```

```python
import functools
import math

import jax
import jax.numpy as jnp
from jax import lax
from jax.experimental import pallas as pl
from jax.experimental.pallas import tpu as pltpu

F32 = jnp.float32
BF16 = jnp.bfloat16
HIGHEST = lax.Precision.HIGHEST

LANE = 128
GRID_W = 64
NORM_EPS = 1e-6
S5_GROUP = 16
S5_CHUNK = 32
HY_ORDER = 2
HY_SHORT = 3
HY_NORM_EPS = 1e-6
TOP_K = 4
SWIGLU_LIMIT = 7.0
SWIGLU_ALPHA = 1.702
MOE_TM = 512
VMEM_LIMIT = 56 * 1024 * 1024


def _params(sem, vmem=VMEM_LIMIT):
    return pltpu.CompilerParams(dimension_semantics=sem, vmem_limit_bytes=vmem)


def _dot(a, b):
    return jnp.dot(a, b, preferred_element_type=F32)


def _ada_kernel(c_ref, w_ref, b_ref, o_ref):
    c = c_ref[...]
    a = c * jax.nn.sigmoid(c)
    o_ref[...] = jnp.dot(a, w_ref[...], preferred_element_type=F32, precision=HIGHEST) + b_ref[...]


def _ada_mod(cc, w, b):
    d, n6 = w.shape
    tn = min(512, n6)
    return pl.pallas_call(
        _ada_kernel,
        grid=(n6 // tn,),
        in_specs=[pl.BlockSpec((8, d), lambda j: (0, 0)),
                  pl.BlockSpec((d, tn), lambda j: (0, j)),
                  pl.BlockSpec((1, tn), lambda j: (0, j))],
        out_specs=pl.BlockSpec((8, tn), lambda j: (0, j)),
        out_shape=jax.ShapeDtypeStruct((8, n6), F32),
        compiler_params=_params(("parallel",)),
        name="ada_mod",
    )(cc, w, b)


def _inproj_kernel(x_ref, g_ref, sh_ref, sc_ref, w_ref, *rest, bounds, acts):
    outs = rest[:len(bounds)]
    xn_ref = rest[len(bounds)]
    j = pl.program_id(1)

    @pl.when(j == 0)
    def _():
        x = x_ref[...].astype(F32)
        y = x * lax.rsqrt(jnp.mean(x * x, axis=-1, keepdims=True) + NORM_EPS)
        y = y * g_ref[...]
        y = y * (1.0 + sc_ref[...]) + sh_ref[...]
        xn_ref[...] = y.astype(BF16)

    for (j0, j1), act, o_ref in zip(bounds, acts, outs):
        @pl.when((j >= j0) & (j < j1))
        def _(o_ref=o_ref, act=act):
            r = _dot(xn_ref[...], w_ref[...])
            if act:
                r = jax.nn.sigmoid(r)
            o_ref[...] = r.astype(o_ref.dtype)


def _norm_mod_proj(x2d, g, sh, sc, w, widths, acts, rows_per_mod, tm, tn):
    n, d = x2d.shape
    ncols = w.shape[1]
    tm = min(tm, n)
    tn = min(tn, min(widths))
    bounds, off = [], 0
    for wd in widths:
        bounds.append((off // tn, (off + wd) // tn))
        off += wd
    tiles_per_mod = rows_per_mod // tm

    def out_map(i, j, j0, nj):
        return (i, jnp.clip(j - j0, 0, nj - 1))

    out_specs = [pl.BlockSpec((tm, tn), functools.partial(out_map, j0=j0, nj=j1 - j0))
                 for (j0, j1) in bounds]
    out_shape = [jax.ShapeDtypeStruct((n, wd), BF16) for wd in widths]
    return pl.pallas_call(
        functools.partial(_inproj_kernel, bounds=tuple(bounds), acts=tuple(acts)),
        grid=(n // tm, ncols // tn),
        in_specs=[pl.BlockSpec((tm, d), lambda i, j: (i, 0)),
                  pl.BlockSpec((1, d), lambda i, j: (0, 0)),
                  pl.BlockSpec((None, 1, d), lambda i, j: (i // tiles_per_mod, 0, 0)),
                  pl.BlockSpec((None, 1, d), lambda i, j: (i // tiles_per_mod, 0, 0)),
                  pl.BlockSpec((d, tn), lambda i, j: (0, j))],
        out_specs=out_specs,
        out_shape=out_shape,
        scratch_shapes=[pltpu.VMEM((tm, d), BF16)],
        compiler_params=_params(("parallel", "arbitrary")),
        name="norm_mod_proj",
    )(x2d, g, sh, sc, w)


def _s5_matrices(lam_re, lam_im, log_dt, b_re, b_im, c_re, c_im, d_skip, t):
    lam = lax.complex(lam_re.astype(F32), lam_im.astype(F32))
    dt = jnp.exp(log_dt.astype(F32))[..., None]
    lam_dt = lam * dt
    lam_bar = jnp.exp(lam_dt)
    b_bar = ((lam_bar - 1) / lam)[..., None] * lax.complex(b_re.astype(F32), b_im.astype(F32))
    c_out = lax.complex(c_re.astype(F32), c_im.astype(F32))
    g, p = lam.shape[1], lam.shape[2]
    gs = b_bar.shape[-1]
    k = jnp.arange(t + 1, dtype=F32)
    pw = jnp.exp(lam_dt[..., None] * k)
    kk = jnp.einsum('dgcp,dgpk,dgpe->dgkce', c_out, pw[..., :t], b_bar, precision=HIGHEST).real
    kf, kb = kk[0], kk[1]
    zero_lag = kf[:, :1] + kb[:, :1]
    k_all = jnp.concatenate([kb[:, :0:-1], zero_lag, kf[:, 1:]], axis=1)
    jj = jnp.arange(t)
    lag_idx = jj[None, :] - jj[:, None] + (t - 1)
    m_intra = k_all[:, lag_idx]
    m_intra = m_intra.transpose(0, 1, 4, 2, 3).reshape(g, t * gs, t * gs)
    m_intra = m_intra + jnp.eye(t * gs, dtype=F32)[None] * jnp.tile(
        d_skip.astype(F32).reshape(g, 1, gs), (1, t, 1)).reshape(g, 1, t * gs)
    in_f = pw[0][..., t - 1 - jj][..., None] * b_bar[0][:, :, None, :]
    in_b = pw[1][..., jj][..., None] * b_bar[1][:, :, None, :]

    def to_in(z):
        return z.transpose(0, 2, 3, 1).reshape(g, t * gs, p)

    m_in = jnp.concatenate([to_in(in_f.real), to_in(in_f.imag), to_in(in_b.real), to_in(in_b.imag)], axis=-1)
    out_f = c_out[0][:, :, :, None] * pw[0][:, None, :, 1 + jj]
    out_b = c_out[1][:, :, :, None] * pw[1][:, None, :, t - jj]

    def to_out(z):
        return z.transpose(0, 2, 3, 1).reshape(g, p, t * gs)

    m_out = jnp.concatenate([to_out(out_f.real), -to_out(out_f.imag), to_out(out_b.real), -to_out(out_b.imag)],
                            axis=1)
    mu = pw[..., t]
    mre, mim = mu.real, mu.imag
    coef = jnp.stack([
        jnp.concatenate([mre[0], mre[0], mre[1], mre[1]], axis=-1),
        jnp.concatenate([-mim[0], mim[0], -mim[1], mim[1]], axis=-1),
        jnp.concatenate([mim[0], -mim[0], mim[1], -mim[1]], axis=-1)])
    return m_intra.astype(BF16), m_in.astype(BF16), m_out.astype(BF16), coef


def _s5_in_kernel(a_ref, m_ref, o_ref):
    o_ref[...] = _dot(a_ref[...], m_ref[...])


def _s5_chunk_states(a, m_in):
    g, nc, kdim = a.shape
    s = m_in.shape[2]
    nb = min(nc, 1024)
    return pl.pallas_call(
        _s5_in_kernel,
        grid=(g, nc // nb),
        in_specs=[pl.BlockSpec((None, nb, kdim), lambda gi, i: (gi, i, 0)),
                  pl.BlockSpec((None, kdim, s), lambda gi, i: (gi, 0, 0))],
        out_specs=pl.BlockSpec((None, nb, s), lambda gi, i: (gi, i, 0)),
        out_shape=jax.ShapeDtypeStruct((g, nc, s), F32),
        compiler_params=_params(("parallel", "arbitrary")),
        name="s5_chunk_states",
    )(a, m_in)


def _s5_scan_kernel(sf_ref, sb_ref, init_ref, coef_ref, ef_ref, eb_ref, fin_ref, st_ref, *, nblk, half):
    k = pl.program_id(1)
    nk = pl.num_programs(1)
    lanes = 2 * half

    @pl.when(k == 0)
    def _():
        s0f = init_ref[:, 0:lanes]
        s0b = init_ref[:, lanes:2 * lanes]
        st_ref[0] = s0f
        st_ref[1] = pltpu.roll(s0f, half, 1)
        st_ref[2] = s0b
        st_ref[3] = pltpu.roll(s0b, half, 1)

    def body(r, carry):
        sf, sfw, sb, sbw = carry
        rb = nblk - 1 - r
        xf = sf_ref[r]
        xb = sb_ref[rb]
        ef_ref[r] = sf
        eb_ref[rb] = sb
        xfw = pltpu.roll(xf, half, 1)
        xbw = pltpu.roll(xb, half, 1)
        af, bf, bfw = coef_ref[0, :, 0:lanes], coef_ref[1, :, 0:lanes], coef_ref[2, :, 0:lanes]
        ab, bb, bbw = (coef_ref[0, :, lanes:2 * lanes], coef_ref[1, :, lanes:2 * lanes],
                       coef_ref[2, :, lanes:2 * lanes])
        return (sf * af + sfw * bf + xf, sfw * af + sf * bfw + xfw,
                sb * ab + sbw * bb + xb, sbw * ab + sb * bbw + xbw)

    sf, sfw, sb, sbw = lax.fori_loop(0, nblk, body, (st_ref[0], st_ref[1], st_ref[2], st_ref[3]))
    st_ref[0] = sf
    st_ref[1] = sfw
    st_ref[2] = sb
    st_ref[3] = sbw

    @pl.when(k == nk - 1)
    def _():
        fin_ref[:, 0:lanes] = sf
        fin_ref[:, lanes:2 * lanes] = sb


def _s5_scan(s_t, init, coef):
    bsz, nc, g, s4 = s_t.shape
    lanes = s4 // 2
    nblk = min(nc, 128)
    nk = nc // nblk
    return pl.pallas_call(
        functools.partial(_s5_scan_kernel, nblk=nblk, half=lanes // 2),
        grid=(bsz, nk),
        in_specs=[pl.BlockSpec((None, nblk, g, lanes), lambda b, k: (b, k, 0, 0)),
                  pl.BlockSpec((None, nblk, g, lanes), lambda b, k: (b, nk - 1 - k, 0, 1)),
                  pl.BlockSpec((None, g, s4), lambda b, k: (b, 0, 0)),
                  pl.BlockSpec((3, g, s4), lambda b, k: (0, 0, 0))],
        out_specs=[pl.BlockSpec((None, nblk, g, lanes), lambda b, k: (b, k, 0, 0)),
                   pl.BlockSpec((None, nblk, g, lanes), lambda b, k: (b, nk - 1 - k, 0, 0)),
                   pl.BlockSpec((None, g, s4), lambda b, k: (b, 0, 0))],
        out_shape=[jax.ShapeDtypeStruct((bsz, nc, g, lanes), F32),
                   jax.ShapeDtypeStruct((bsz, nc, g, lanes), F32),
                   jax.ShapeDtypeStruct((bsz, g, s4), F32)],
        scratch_shapes=[pltpu.VMEM((4, g, lanes), F32)],
        compiler_params=_params(("parallel", "arbitrary")),
        name="s5_scan",
    )(s_t, s_t, init, coef)


def _s5_out_kernel(a_ref, e_ref, mi_ref, mo_ref, o_ref):
    y = _dot(a_ref[...], mi_ref[...]) + _dot(e_ref[...], mo_ref[...])
    o_ref[...] = y.astype(o_ref.dtype)


def _s5_chunk_outputs(a, e_in, m_intra, m_out):
    g, nc, kdim = a.shape
    s = e_in.shape[2]
    nb = min(nc, 1024)
    return pl.pallas_call(
        _s5_out_kernel,
        grid=(g, nc // nb),
        in_specs=[pl.BlockSpec((None, nb, kdim), lambda gi, i: (gi, i, 0)),
                  pl.BlockSpec((None, nb, s), lambda gi, i: (gi, i, 0)),
                  pl.BlockSpec((None, kdim, kdim), lambda gi, i: (gi, 0, 0)),
                  pl.BlockSpec((None, s, kdim), lambda gi, i: (gi, 0, 0))],
        out_specs=pl.BlockSpec((None, nb, kdim), lambda gi, i: (gi, i, 0)),
        out_shape=jax.ShapeDtypeStruct((g, nc, kdim), BF16),
        compiler_params=_params(("parallel", "arbitrary")),
        name="s5_chunk_outputs",
    )(a, e_in, m_intra, m_out)


def _s5_readout_kernel(y_ref, wg_ref, o_ref):
    y = jax.nn.gelu(y_ref[...].astype(F32))
    gate = _dot(y.astype(BF16), wg_ref[...])
    o_ref[...] = (y * jax.nn.sigmoid(gate)).astype(o_ref.dtype)


def _s5_readout(y2d, wg):
    n, w = y2d.shape
    tm = min(n, 1024)
    return pl.pallas_call(
        _s5_readout_kernel,
        grid=(n // tm,),
        in_specs=[pl.BlockSpec((tm, w), lambda i: (i, 0)),
                  pl.BlockSpec((w, w), lambda i: (0, 0))],
        out_specs=pl.BlockSpec((tm, w), lambda i: (i, 0)),
        out_shape=jax.ShapeDtypeStruct((n, w), BF16),
        compiler_params=_params(("parallel",)),
        name="s5_readout",
    )(y2d, wg)


def _to_chunks(u2d, g, t):
    n = u2d.shape[0]
    return u2d.reshape(n // t, t, g, S5_GROUP).transpose(2, 0, 1, 3).reshape(g, n // t, t * S5_GROUP)


def _from_chunks(y, g, t):
    nc = y.shape[1]
    return y.reshape(g, nc, t, S5_GROUP).transpose(1, 2, 0, 3).reshape(nc * t, g * S5_GROUP)


def _s5_branch(ua, uca, bsz, lam_re, lam_im, log_dt, b_re, b_im, c_re, c_im, d_skip, w_glu):
    g = lam_re.shape[1]
    t = S5_CHUNK
    m_intra, m_in, m_out, coef = _s5_matrices(lam_re, lam_im, log_dt, b_re, b_im, c_re, c_im, d_skip, t)
    s4 = m_in.shape[2]

    def states(u2d):
        a = _to_chunks(u2d, g, t)
        s = _s5_chunk_states(a, m_in)
        nc = s.shape[1] // bsz
        return a, s.reshape(g, bsz, nc, s4).transpose(1, 2, 0, 3)

    _, s_ctx = states(uca)
    _, _, seed = _s5_scan(s_ctx, jnp.zeros((bsz, g, s4), F32), coef)
    a_lat, s_lat = states(ua)
    e_f, e_b, _ = _s5_scan(s_lat, seed, coef)
    e_in = jnp.concatenate([e_f, e_b], axis=-1).astype(BF16)
    e_in = e_in.transpose(2, 0, 1, 3).reshape(g, -1, s4)
    y = _s5_chunk_outputs(a_lat, e_in, m_intra, m_out)
    return _s5_readout(_from_chunks(y, g, t), w_glu.astype(BF16))


def _to_col_major(z):
    b, n, ch = z.shape
    rows = n // GRID_W
    return z.reshape(b, rows, GRID_W, ch).swapaxes(1, 2).reshape(b, n, ch)


def _from_col_major(z):
    b, n, ch = z.shape
    rows = n // GRID_W
    return z.reshape(b, GRID_W, rows, ch).swapaxes(1, 2).reshape(b, n, ch)


def _hyena_kernel_fft(seqlen, w1, b1, f1, w2, b2, f2, w3, b3, decay, hy_width):
    t_idx = jnp.arange(seqlen, dtype=F32)
    t = t_idx / (seqlen - 1)
    n_bands = (w1.shape[0] - 1) // 2
    bands = jnp.linspace(1e-4, n_bands - 1, n_bands, dtype=F32)
    ang = (2 * jnp.pi / seqlen) * t_idx[:, None] * bands[None, :]
    feats = jnp.concatenate([t[:, None], jnp.cos(ang), -jnp.sin(ang)], axis=-1)
    hid = jnp.sin(f1 * (feats @ w1 + b1))
    hid = jnp.sin(f2 * (hid @ w2 + b2))
    filt = (hid @ w3 + b3) * jnp.exp(-t[:, None] * jnp.abs(decay))
    filt = filt.reshape(seqlen, 2, HY_ORDER, hy_width)
    two_sided = jnp.concatenate([filt[:, 0], jnp.zeros((1, HY_ORDER, hy_width), F32), filt[:0:-1, 1]], axis=0)
    two_sided = two_sided / (jnp.sum(jnp.abs(two_sided), axis=0, keepdims=True) + HY_NORM_EPS)
    return jnp.fft.rfft(two_sided, axis=0)


def _fft_conv(z, k_f, bias):
    seqlen = z.shape[1]
    n = 2 * seqlen
    y = jnp.fft.irfft(jnp.fft.rfft(z, n=n, axis=1) * k_f[None], n=n, axis=1)[:, :seqlen]
    return y + z * bias


def _hyena_branch(z_in, short_w, w1, b1, f1, w2, b2, f2, w3, b3, decay, hy_bias):
    seqlen = z_in.shape[1]
    ch = z_in.shape[-1]
    hy_width = ch // 3
    z = lax.conv_general_dilated(z_in, short_w[:, None, :], window_strides=(1,),
                                 padding=[(HY_SHORT // 2, HY_SHORT // 2)],
                                 dimension_numbers=('NWC', 'WIO', 'NWC'), feature_group_count=ch)
    v, x1, x2 = jnp.split(z, 3, axis=-1)
    k_f = _hyena_kernel_fft(seqlen, w1, b1, f1, w2, b2, f2, w3, b3, decay, hy_width)
    y = x1 * _fft_conv(v, k_f[:, 0], hy_bias[0])
    y = x2 * _fft_conv(y, k_f[:, 1], hy_bias[1])
    return y


def _merge1_kernel(ya_ref, yh_ref, ug_ref, wa_ref, wh_ref, o_ref):
    d = o_ref.shape[1]
    a = _dot(ya_ref[...], wa_ref[...])
    h = _dot(yh_ref[...], wh_ref[...])
    ga = ug_ref[:, :d].astype(F32)
    gh = ug_ref[:, d:].astype(F32)
    o_ref[...] = (ga * a + gh * h).astype(o_ref.dtype)


def _merge1(ya, yh, ug, wa, wh):
    n, ws = ya.shape
    wh_in = yh.shape[1]
    d = wa.shape[1]
    tm = min(n, 512)
    return pl.pallas_call(
        _merge1_kernel,
        grid=(n // tm,),
        in_specs=[pl.BlockSpec((tm, ws), lambda i: (i, 0)),
                  pl.BlockSpec((tm, wh_in), lambda i: (i, 0)),
                  pl.BlockSpec((tm, 2 * d), lambda i: (i, 0)),
                  pl.BlockSpec((ws, d), lambda i: (0, 0)),
                  pl.BlockSpec((wh_in, d), lambda i: (0, 0))],
        out_specs=pl.BlockSpec((tm, d), lambda i: (i, 0)),
        out_shape=jax.ShapeDtypeStruct((n, d), BF16),
        compiler_params=_params(("parallel",)),
        name="merge_branches",
    )(ya, yh, ug, wa, wh)


def _merge2_kernel(m_ref, x_ref, g1_ref, ng_ref, sh_ref, sc_ref, wo_ref, wr_ref,
                   h1_ref, hm_ref, r1_ref, r2_ref):
    mix = _dot(m_ref[...], wo_ref[...])
    h1 = x_ref[...] + g1_ref[...] * mix
    h1_ref[...] = h1
    y = h1 * lax.rsqrt(jnp.mean(h1 * h1, axis=-1, keepdims=True) + NORM_EPS)
    y = y * ng_ref[...]
    y = y * (1.0 + sc_ref[...]) + sh_ref[...]
    hi = y.astype(BF16)
    lo = (y - hi.astype(F32)).astype(BF16)
    hm_ref[...] = hi
    r1_ref[...] = _dot(hi, wr_ref[...])
    r2_ref[...] = _dot(lo, wr_ref[...])


def _merge2(m, x2d, g1, ng, sh2, sc2, wo, wr, rows_per_mod):
    n, d = x2d.shape
    tm = min(n, 512)
    tiles_per_mod = rows_per_mod // tm
    nr = wr.shape[1]
    mod_spec = pl.BlockSpec((None, 1, d), lambda i: (i // tiles_per_mod, 0, 0))
    return pl.pallas_call(
        _merge2_kernel,
        grid=(n // tm,),
        in_specs=[pl.BlockSpec((tm, d), lambda i: (i, 0)),
                  pl.BlockSpec((tm, d), lambda i: (i, 0)),
                  mod_spec,
                  pl.BlockSpec((1, d), lambda i: (0, 0)),
                  mod_spec, mod_spec,
                  pl.BlockSpec((d, d), lambda i: (0, 0)),
                  pl.BlockSpec((d, nr), lambda i: (0, 0))],
        out_specs=[pl.BlockSpec((tm, d), lambda i: (i, 0)),
                   pl.BlockSpec((tm, d), lambda i: (i, 0)),
                   pl.BlockSpec((tm, nr), lambda i: (i, 0)),
                   pl.BlockSpec((tm, nr), lambda i: (i, 0))],
        out_shape=[jax.ShapeDtypeStruct((n, d), F32),
                   jax.ShapeDtypeStruct((n, d), BF16),
                   jax.ShapeDtypeStruct((n, nr), F32),
                   jax.ShapeDtypeStruct((n, nr), F32)],
        compiler_params=_params(("parallel",)),
        name="out_proj_norm_router",
    )(m, x2d, g1, ng, sh2, sc2, wo, wr)


def _moe_kernel(be_ref, nv_ref, x_ref, wgu_ref, bgu_ref, wd_ref, bd_ref, gate_ref, o_ref, acc_ref):
    i = pl.program_id(0)
    j = pl.program_id(1)
    nj = pl.num_programs(1)

    @pl.when(i < nv_ref[0])
    def _():
        gu = _dot(x_ref[...], wgu_ref[...]) + bgu_ref[...]
        acts = []
        for b in range(gu.shape[1] // (2 * LANE)):
            glu = jnp.minimum(gu[:, 2 * b * LANE:(2 * b + 1) * LANE], SWIGLU_LIMIT)
            lin = jnp.clip(gu[:, (2 * b + 1) * LANE:(2 * b + 2) * LANE], -SWIGLU_LIMIT, SWIGLU_LIMIT)
            acts.append((glu * jax.nn.sigmoid(SWIGLU_ALPHA * glu) * (lin + 1.0)).astype(BF16))
        part = _dot(jnp.concatenate(acts, axis=1), wd_ref[...])

        @pl.when(j == 0)
        def _():
            acc_ref[...] = part

        @pl.when(j > 0)
        def _():
            acc_ref[...] += part

        @pl.when(j == nj - 1)
        def _():
            o_ref[...] = ((acc_ref[...] + bd_ref[...]) * gate_ref[...]).astype(o_ref.dtype)


def _deinterleave_kernel(w_ref, p_ref, o_ref):
    for b in range(w_ref.shape[1] // (2 * LANE)):
        sl = slice(2 * b * LANE, (2 * b + 2) * LANE)
        o_ref[:, sl] = _dot(w_ref[:, sl].astype(BF16), p_ref[...]).astype(o_ref.dtype)


def _deinterleave_cast(w):
    e, d, f2 = w.shape
    tr = min(d, 512)
    src = jnp.arange(2 * LANE)
    dst = (src % 2) * LANE + src // 2
    perm = jnp.zeros((2 * LANE, 2 * LANE), BF16).at[src, dst].set(1)
    return pl.pallas_call(
        _deinterleave_kernel,
        grid=(e, d // tr),
        in_specs=[pl.BlockSpec((None, tr, f2), lambda ei, i: (ei, i, 0)),
                  pl.BlockSpec((2 * LANE, 2 * LANE), lambda ei, i: (0, 0))],
        out_specs=pl.BlockSpec((None, tr, f2), lambda ei, i: (ei, i, 0)),
        out_shape=jax.ShapeDtypeStruct((e, d, f2), BF16),
        compiler_params=_params(("parallel", "parallel")),
        name="deinterleave_cast",
    )(w, perm)


def _moe_blocks(xg, gate, block_expert, n_valid, wgu, bgu, wd, bd):
    cap, d = xg.shape
    f = wd.shape[1]
    tm = MOE_TM
    tf = min(f, 512)
    nj = f // tf
    nblk = cap // tm

    def row_map(i, j, be, nv):
        return (jnp.minimum(i, nv[0] - 1), 0)

    def jeff(i, j, nv):
        return jnp.where(i < nv[0], j, nj - 1)

    grid_spec = pltpu.PrefetchScalarGridSpec(
        num_scalar_prefetch=2,
        grid=(nblk, nj),
        in_specs=[pl.BlockSpec((tm, d), row_map),
                  pl.BlockSpec((None, d, 2 * tf), lambda i, j, be, nv: (be[i], 0, jeff(i, j, nv))),
                  pl.BlockSpec((None, 1, 2 * tf), lambda i, j, be, nv: (be[i], 0, jeff(i, j, nv))),
                  pl.BlockSpec((None, tf, d), lambda i, j, be, nv: (be[i], jeff(i, j, nv), 0)),
                  pl.BlockSpec((None, 1, d), lambda i, j, be, nv: (be[i], 0, 0)),
                  pl.BlockSpec((tm, 1), row_map)],
        out_specs=pl.BlockSpec((tm, d), row_map),
        scratch_shapes=[pltpu.VMEM((tm, d), F32)])
    return pl.pallas_call(
        _moe_kernel,
        grid_spec=grid_spec,
        out_shape=jax.ShapeDtypeStruct((cap, d), BF16),
        compiler_params=_params(("arbitrary", "arbitrary")),
        name="moe_experts",
    )(block_expert, n_valid, xg, wgu, bgu, wd, bd, gate)


def _route(logits, n_experts):
    n_tok = logits.shape[0]
    top_val, top_idx = lax.top_k(logits, TOP_K)
    gates = jax.nn.softmax(top_val, axis=-1)
    n_assign = n_tok * TOP_K
    flat_e = top_idx.reshape(-1).astype(jnp.int32)
    order = jnp.argsort(flat_e)
    e_sorted = flat_e[order]
    counts = jnp.bincount(flat_e, length=n_experts).astype(jnp.int32)
    padded = (counts + MOE_TM - 1) // MOE_TM * MOE_TM
    start = jnp.cumsum(counts) - counts
    padded_end = jnp.cumsum(padded)
    padded_start = padded_end - padded
    dest_sorted = padded_start[e_sorted] + jnp.arange(n_assign, dtype=jnp.int32) - start[e_sorted]
    n_blocks = -(-n_assign // MOE_TM) + n_experts
    cap = n_blocks * MOE_TM
    flat_tok = jnp.arange(n_assign, dtype=jnp.int32) // TOP_K
    slot_tok = jnp.full((cap,), n_tok, jnp.int32).at[dest_sorted].set(flat_tok[order])
    slot_gate = jnp.zeros((cap,), F32).at[dest_sorted].set(gates.reshape(-1)[order])
    dest = jnp.zeros((n_assign,), jnp.int32).at[order].set(dest_sorted)
    n_valid = (padded_end[-1] // MOE_TM).astype(jnp.int32)
    block_start = jnp.minimum(jnp.arange(n_blocks, dtype=jnp.int32), n_valid - 1) * MOE_TM
    block_expert = jnp.minimum(jnp.searchsorted(padded_end, block_start, side='right'),
                               n_experts - 1).astype(jnp.int32)
    return slot_tok, slot_gate, dest, block_expert, n_valid.reshape(1)


def _final_kernel(h1_ref, yg_ref, g2_ref, fg_ref, o_ref):
    d = h1_ref.shape[1]
    moe = yg_ref[:, 0:d].astype(F32)
    for k in range(1, TOP_K):
        moe = moe + yg_ref[:, k * d:(k + 1) * d].astype(F32)
    h = h1_ref[...] + g2_ref[...] * moe
    y = h * lax.rsqrt(jnp.mean(h * h, axis=-1, keepdims=True) + NORM_EPS)
    o_ref[...] = (y * fg_ref[...]).astype(o_ref.dtype)


def _final(h1, yg, g2, fg, rows_per_mod):
    n, d = h1.shape
    tm = min(n, 512)
    tiles_per_mod = rows_per_mod // tm
    return pl.pallas_call(
        _final_kernel,
        grid=(n // tm,),
        in_specs=[pl.BlockSpec((tm, d), lambda i: (i, 0)),
                  pl.BlockSpec((tm, TOP_K * d), lambda i: (i, 0)),
                  pl.BlockSpec((None, 1, d), lambda i: (i // tiles_per_mod, 0, 0)),
                  pl.BlockSpec((1, d), lambda i: (0, 0))],
        out_specs=pl.BlockSpec((tm, d), lambda i: (i, 0)),
        out_shape=jax.ShapeDtypeStruct((n, d), F32),
        compiler_params=_params(("parallel",)),
        name="combine_final_norm",
    )(h1, yg, g2, fg)


def kernel(x, c, ctx, c_ctx, w_ada, b_ada, norm1_g, norm2_g, w_in, s5_lam_re, s5_lam_im, s5_log_dt, s5_b_re, s5_b_im, s5_c_re, s5_c_im, s5_d, s5_w_glu, hy_short_w, hy_pos_w1, hy_pos_b1, hy_freq1, hy_pos_w2, hy_pos_b2, hy_freq2, hy_pos_w3, hy_pos_b3, hy_decay, hy_bias, w_branch_s5, w_branch_hy, w_out, router_w, router_b, w_gate_up, b_gate_up, w_down, b_down, final_g):
    bsz, seq, d = x.shape
    ctx_len = ctx.shape[1]
    depth = w_ada.shape[0]
    assert depth == 1, "only the single-layer configuration is implemented"
    n_experts = router_w.shape[2]
    s5_w = s5_d.shape[1]
    hy_w = hy_bias.shape[2]
    n_tok = bsz * seq

    cc = jnp.zeros((8, d), F32).at[:bsz].set(c).at[bsz].set(c_ctx)
    mod = _ada_mod(cc, w_ada[0], b_ada[0][None]).reshape(8, 6, 1, d)
    sh1, sc1, g1, sh2, sc2, g2 = [mod[:bsz, k] for k in range(6)]
    csh1, csc1 = mod[bsz:bsz + 1, 0], mod[bsz:bsz + 1, 1]

    w_in_b = w_in[0].astype(BF16)
    x2d = x.reshape(n_tok, d)
    ua, uh, ug = _norm_mod_proj(x2d, norm1_g[0][None], sh1, sc1, w_in_b,
                                (s5_w, 3 * hy_w, 2 * d), (False, False, True), seq, 1024, 1024)
    (uca,) = _norm_mod_proj(ctx.reshape(bsz * ctx_len, d), norm1_g[0][None], csh1, csc1,
                            w_in_b[:, :s5_w], (s5_w,), (False,), bsz * ctx_len, 512, 1024)

    ya = _s5_branch(ua, uca, bsz, s5_lam_re[0], s5_lam_im[0], s5_log_dt[0], s5_b_re[0], s5_b_im[0],
                    s5_c_re[0], s5_c_im[0], s5_d[0], s5_w_glu[0])

    yh = _from_col_major(_hyena_branch(
        _to_col_major(uh.astype(F32).reshape(bsz, seq, 3 * hy_w)), hy_short_w[0], hy_pos_w1[0], hy_pos_b1[0],
        hy_freq1[0], hy_pos_w2[0], hy_pos_b2[0], hy_freq2[0], hy_pos_w3[0], hy_pos_b3[0], hy_decay[0],
        hy_bias[0]))
    yh = yh.reshape(n_tok, hy_w).astype(BF16)

    m = _merge1(ya, yh, ug, w_branch_s5[0].astype(BF16), w_branch_hy[0].astype(BF16))
    rw = router_w[0]
    rw_hi = rw.astype(BF16)
    rw_lo = (rw - rw_hi.astype(F32)).astype(BF16)
    wr = jnp.zeros((d, 128), BF16).at[:, :n_experts].set(rw_hi).at[:, n_experts:2 * n_experts].set(rw_lo)
    h1, hm, r1, r2 = _merge2(m, x2d, g1, norm2_g[0][None], sh2, sc2, w_out[0].astype(BF16), wr, seq)
    logits = (r1[:, :n_experts] + r1[:, n_experts:2 * n_experts] + r2[:, :n_experts]) + router_b[0]

    slot_tok, slot_gate, dest, block_expert, n_valid = _route(logits, n_experts)
    hm_pad = jnp.concatenate([hm, jnp.zeros((1, d), BF16)], axis=0)
    xg = hm_pad[slot_tok]
    f = w_down.shape[2]
    bgu = b_gate_up[0].reshape(n_experts, f // LANE, LANE, 2).swapaxes(2, 3).reshape(n_experts, 1, 2 * f)
    yslots = _moe_blocks(xg, slot_gate[:, None], block_expert, n_valid, _deinterleave_cast(w_gate_up[0]), bgu,
                         w_down[0].astype(BF16), b_down[0][:, None, :])
    yg = yslots[dest].reshape(n_tok, TOP_K * d)
    out = _final(h1, yg, g2, final_g[None], seq)
    return out.reshape(bsz, seq, d)
```

```python
import functools
import math

import jax
import jax.numpy as jnp
import numpy as np
from jax import lax
from jax.experimental import pallas as pl
from jax.experimental.pallas import tpu as pltpu

F32 = jnp.float32
BF16 = jnp.bfloat16
HIGHEST = lax.Precision.HIGHEST

LANE = 128
GRID_W = 64
NORM_EPS = 1e-6
S5_GROUP = 16
S5_CHUNK = 32
HY_ORDER = 2
HY_NORM_EPS = 1e-6
TOP_K = 4
SWIGLU_LIMIT = 7.0
SWIGLU_ALPHA = 1.702
MOE_TM = 512
VMEM_LIMIT = 56 * 1024 * 1024


def _params(sem, vmem=VMEM_LIMIT):
    return pltpu.CompilerParams(dimension_semantics=sem, vmem_limit_bytes=vmem)


def _dot(a, b):
    return jnp.dot(a, b, preferred_element_type=F32)


def _ada_kernel(c_ref, w_ref, b_ref, o_ref):
    c = c_ref[...]
    a = c * jax.nn.sigmoid(c)
    o_ref[...] = jnp.dot(a, w_ref[...], preferred_element_type=F32, precision=HIGHEST) + b_ref[...]


def _ada_mod(cc, w, b):
    d, n6 = w.shape
    tn = min(512, n6)
    return pl.pallas_call(
        _ada_kernel,
        grid=(n6 // tn,),
        in_specs=[pl.BlockSpec((8, d), lambda j: (0, 0)),
                  pl.BlockSpec((d, tn), lambda j: (0, j)),
                  pl.BlockSpec((1, tn), lambda j: (0, j))],
        out_specs=pl.BlockSpec((8, tn), lambda j: (0, j)),
        out_shape=jax.ShapeDtypeStruct((8, n6), F32),
        compiler_params=_params(("parallel",)),
        name="ada_mod",
    )(cc, w, b)


def _inproj_kernel(x_ref, g_ref, sh_ref, sc_ref, w_ref, *rest, bounds, acts):
    outs = rest[:len(bounds)]
    xn_ref = rest[len(bounds)]
    j = pl.program_id(1)

    @pl.when(j == 0)
    def _():
        x = x_ref[...].astype(F32)
        y = x * lax.rsqrt(jnp.mean(x * x, axis=-1, keepdims=True) + NORM_EPS)
        y = y * g_ref[...]
        y = y * (1.0 + sc_ref[...]) + sh_ref[...]
        xn_ref[...] = y.astype(BF16)

    for (j0, j1), act, o_ref in zip(bounds, acts, outs):
        @pl.when((j >= j0) & (j < j1))
        def _(o_ref=o_ref, act=act):
            r = _dot(xn_ref[...], w_ref[...])
            if act:
                r = jax.nn.sigmoid(r)
            o_ref[...] = r.astype(o_ref.dtype)


def _norm_mod_proj(x2d, g, sh, sc, w, widths, acts, rows_per_mod, tm, tn):
    n, d = x2d.shape
    ncols = w.shape[1]
    tm = min(tm, n)
    tn = min(tn, min(widths))
    bounds, off = [], 0
    for wd in widths:
        bounds.append((off // tn, (off + wd) // tn))
        off += wd
    tiles_per_mod = rows_per_mod // tm

    def out_map(i, j, j0, nj):
        return (i, jnp.clip(j - j0, 0, nj - 1))

    out_specs = [pl.BlockSpec((tm, tn), functools.partial(out_map, j0=j0, nj=j1 - j0))
                 for (j0, j1) in bounds]
    out_shape = [jax.ShapeDtypeStruct((n, wd), BF16) for wd in widths]
    return pl.pallas_call(
        functools.partial(_inproj_kernel, bounds=tuple(bounds), acts=tuple(acts)),
        grid=(n // tm, ncols // tn),
        in_specs=[pl.BlockSpec((tm, d), lambda i, j: (i, 0)),
                  pl.BlockSpec((1, d), lambda i, j: (0, 0)),
                  pl.BlockSpec((None, 1, d), lambda i, j: (i // tiles_per_mod, 0, 0)),
                  pl.BlockSpec((None, 1, d), lambda i, j: (i // tiles_per_mod, 0, 0)),
                  pl.BlockSpec((d, tn), lambda i, j: (0, j))],
        out_specs=out_specs,
        out_shape=out_shape,
        scratch_shapes=[pltpu.VMEM((tm, d), BF16)],
        compiler_params=_params(("parallel", "arbitrary")),
        name="norm_mod_proj",
    )(x2d, g, sh, sc, w)


def _s5_matrices(lam_re, lam_im, log_dt, b_re, b_im, c_re, c_im, d_skip, t):
    lam = lax.complex(lam_re.astype(F32), lam_im.astype(F32))
    dt = jnp.exp(log_dt.astype(F32))[..., None]
    lam_dt = lam * dt
    lam_bar = jnp.exp(lam_dt)
    b_bar = ((lam_bar - 1) / lam)[..., None] * lax.complex(b_re.astype(F32), b_im.astype(F32))
    c_out = lax.complex(c_re.astype(F32), c_im.astype(F32))
    g, p = lam.shape[1], lam.shape[2]
    gs = b_bar.shape[-1]
    k = jnp.arange(t + 1, dtype=F32)
    pw = jnp.exp(lam_dt[..., None] * k)
    kk = jnp.einsum('dgcp,dgpk,dgpe->dgkce', c_out, pw[..., :t], b_bar, precision=HIGHEST).real
    kf, kb = kk[0], kk[1]
    zero_lag = kf[:, :1] + kb[:, :1]
    k_all = jnp.concatenate([kb[:, :0:-1], zero_lag, kf[:, 1:]], axis=1)
    jj = jnp.arange(t)
    lag_idx = jj[None, :] - jj[:, None] + (t - 1)
    m_intra = k_all[:, lag_idx]
    m_intra = m_intra.transpose(0, 1, 4, 2, 3).reshape(g, t * gs, t * gs)
    m_intra = m_intra + jnp.eye(t * gs, dtype=F32)[None] * jnp.tile(
        d_skip.astype(F32).reshape(g, 1, gs), (1, t, 1)).reshape(g, 1, t * gs)
    in_f = pw[0][..., t - 1 - jj][..., None] * b_bar[0][:, :, None, :]
    in_b = pw[1][..., jj][..., None] * b_bar[1][:, :, None, :]

    def to_in(z):
        return z.transpose(0, 2, 3, 1).reshape(g, t * gs, p)

    m_in = jnp.concatenate([to_in(in_f.real), to_in(in_f.imag), to_in(in_b.real), to_in(in_b.imag)], axis=-1)
    out_f = c_out[0][:, :, :, None] * pw[0][:, None, :, 1 + jj]
    out_b = c_out[1][:, :, :, None] * pw[1][:, None, :, t - jj]

    def to_out(z):
        return z.transpose(0, 2, 3, 1).reshape(g, p, t * gs)

    m_out = jnp.concatenate([to_out(out_f.real), -to_out(out_f.imag), to_out(out_b.real), -to_out(out_b.imag)],
                            axis=1)
    mu = pw[..., t]
    mre, mim = mu.real, mu.imag
    coef = jnp.stack([
        jnp.concatenate([mre[0], mre[0], mre[1], mre[1]], axis=-1),
        jnp.concatenate([-mim[0], mim[0], -mim[1], mim[1]], axis=-1),
        jnp.concatenate([mim[0], -mim[0], mim[1], -mim[1]], axis=-1)])
    return m_intra.astype(BF16), m_in.astype(BF16), m_out.astype(BF16), coef


def _s5_in_kernel(a_ref, m_ref, o_ref):
    o_ref[...] = _dot(a_ref[...], m_ref[...])


def _s5_chunk_states(a, m_in):
    g, nc, kdim = a.shape
    s = m_in.shape[2]
    nb = min(nc, 1024)
    return pl.pallas_call(
        _s5_in_kernel,
        grid=(g, nc // nb),
        in_specs=[pl.BlockSpec((None, nb, kdim), lambda gi, i: (gi, i, 0)),
                  pl.BlockSpec((None, kdim, s), lambda gi, i: (gi, 0, 0))],
        out_specs=pl.BlockSpec((None, nb, s), lambda gi, i: (gi, i, 0)),
        out_shape=jax.ShapeDtypeStruct((g, nc, s), F32),
        compiler_params=_params(("parallel", "arbitrary")),
        name="s5_chunk_states",
    )(a, m_in)


def _s5_scan_kernel(sf_ref, sb_ref, init_ref, coef_ref, ef_ref, eb_ref, fin_ref, st_ref, *, nblk, half):
    k = pl.program_id(1)
    nk = pl.num_programs(1)
    lanes = 2 * half

    @pl.when(k == 0)
    def _():
        s0f = init_ref[:, 0:lanes]
        s0b = init_ref[:, lanes:2 * lanes]
        st_ref[0] = s0f
        st_ref[1] = pltpu.roll(s0f, half, 1)
        st_ref[2] = s0b
        st_ref[3] = pltpu.roll(s0b, half, 1)

    def body(r, carry):
        sf, sfw, sb, sbw = carry
        rb = nblk - 1 - r
        xf = sf_ref[r]
        xb = sb_ref[rb]
        ef_ref[r] = sf
        eb_ref[rb] = sb
        xfw = pltpu.roll(xf, half, 1)
        xbw = pltpu.roll(xb, half, 1)
        af, bf, bfw = coef_ref[0, :, 0:lanes], coef_ref[1, :, 0:lanes], coef_ref[2, :, 0:lanes]
        ab, bb, bbw = (coef_ref[0, :, lanes:2 * lanes], coef_ref[1, :, lanes:2 * lanes],
                       coef_ref[2, :, lanes:2 * lanes])
        return (sf * af + sfw * bf + xf, sfw * af + sf * bfw + xfw,
                sb * ab + sbw * bb + xb, sbw * ab + sb * bbw + xbw)

    sf, sfw, sb, sbw = lax.fori_loop(0, nblk, body, (st_ref[0], st_ref[1], st_ref[2], st_ref[3]))
    st_ref[0] = sf
    st_ref[1] = sfw
    st_ref[2] = sb
    st_ref[3] = sbw

    @pl.when(k == nk - 1)
    def _():
        fin_ref[:, 0:lanes] = sf
        fin_ref[:, lanes:2 * lanes] = sb


def _s5_scan(s_t, init, coef):
    bsz, nc, g, s4 = s_t.shape
    lanes = s4 // 2
    nblk = min(nc, 128)
    nk = nc // nblk
    return pl.pallas_call(
        functools.partial(_s5_scan_kernel, nblk=nblk, half=lanes // 2),
        grid=(bsz, nk),
        in_specs=[pl.BlockSpec((None, nblk, g, lanes), lambda b, k: (b, k, 0, 0)),
                  pl.BlockSpec((None, nblk, g, lanes), lambda b, k: (b, nk - 1 - k, 0, 1)),
                  pl.BlockSpec((None, g, s4), lambda b, k: (b, 0, 0)),
                  pl.BlockSpec((3, g, s4), lambda b, k: (0, 0, 0))],
        out_specs=[pl.BlockSpec((None, nblk, g, lanes), lambda b, k: (b, k, 0, 0)),
                   pl.BlockSpec((None, nblk, g, lanes), lambda b, k: (b, nk - 1 - k, 0, 0)),
                   pl.BlockSpec((None, g, s4), lambda b, k: (b, 0, 0))],
        out_shape=[jax.ShapeDtypeStruct((bsz, nc, g, lanes), F32),
                   jax.ShapeDtypeStruct((bsz, nc, g, lanes), F32),
                   jax.ShapeDtypeStruct((bsz, g, s4), F32)],
        scratch_shapes=[pltpu.VMEM((4, g, lanes), F32)],
        compiler_params=_params(("parallel", "arbitrary")),
        name="s5_scan",
    )(s_t, s_t, init, coef)


def _s5_out_kernel(a_ref, e_ref, mi_ref, mo_ref, o_ref):
    y = _dot(a_ref[...], mi_ref[...]) + _dot(e_ref[...], mo_ref[...])
    o_ref[...] = y.astype(o_ref.dtype)


def _s5_chunk_outputs(a, e_in, m_intra, m_out):
    g, nc, kdim = a.shape
    s = e_in.shape[2]
    nb = min(nc, 1024)
    return pl.pallas_call(
        _s5_out_kernel,
        grid=(g, nc // nb),
        in_specs=[pl.BlockSpec((None, nb, kdim), lambda gi, i: (gi, i, 0)),
                  pl.BlockSpec((None, nb, s), lambda gi, i: (gi, i, 0)),
                  pl.BlockSpec((None, kdim, kdim), lambda gi, i: (gi, 0, 0)),
                  pl.BlockSpec((None, s, kdim), lambda gi, i: (gi, 0, 0))],
        out_specs=pl.BlockSpec((None, nb, kdim), lambda gi, i: (gi, i, 0)),
        out_shape=jax.ShapeDtypeStruct((g, nc, kdim), BF16),
        compiler_params=_params(("parallel", "arbitrary")),
        name="s5_chunk_outputs",
    )(a, e_in, m_intra, m_out)


def _s5_readout_kernel(y_ref, wg_ref, o_ref):
    y = jax.nn.gelu(y_ref[...].astype(F32))
    gate = _dot(y.astype(BF16), wg_ref[...])
    o_ref[...] = (y * jax.nn.sigmoid(gate)).astype(o_ref.dtype)


def _s5_readout(y2d, wg):
    n, w = y2d.shape
    tm = min(n, 1024)
    return pl.pallas_call(
        _s5_readout_kernel,
        grid=(n // tm,),
        in_specs=[pl.BlockSpec((tm, w), lambda i: (i, 0)),
                  pl.BlockSpec((w, w), lambda i: (0, 0))],
        out_specs=pl.BlockSpec((tm, w), lambda i: (i, 0)),
        out_shape=jax.ShapeDtypeStruct((n, w), BF16),
        compiler_params=_params(("parallel",)),
        name="s5_readout",
    )(y2d, wg)


def _to_chunks(u2d, g, t):
    n = u2d.shape[0]
    return u2d.reshape(n // t, t, g, S5_GROUP).transpose(2, 0, 1, 3).reshape(g, n // t, t * S5_GROUP)


def _from_chunks(y, g, t):
    nc = y.shape[1]
    return y.reshape(g, nc, t, S5_GROUP).transpose(1, 2, 0, 3).reshape(nc * t, g * S5_GROUP)


def _s5_branch(ua, uca, bsz, lam_re, lam_im, log_dt, b_re, b_im, c_re, c_im, d_skip, w_glu):
    g = lam_re.shape[1]
    t = S5_CHUNK
    m_intra, m_in, m_out, coef = _s5_matrices(lam_re, lam_im, log_dt, b_re, b_im, c_re, c_im, d_skip, t)
    s4 = m_in.shape[2]

    def states(u2d):
        a = _to_chunks(u2d, g, t)
        s = _s5_chunk_states(a, m_in)
        nc = s.shape[1] // bsz
        return a, s.reshape(g, bsz, nc, s4).transpose(1, 2, 0, 3)

    _, s_ctx = states(uca)
    _, _, seed = _s5_scan(s_ctx, jnp.zeros((bsz, g, s4), F32), coef)
    a_lat, s_lat = states(ua)
    e_f, e_b, _ = _s5_scan(s_lat, seed, coef)
    e_in = jnp.concatenate([e_f, e_b], axis=-1).astype(BF16)
    e_in = e_in.transpose(2, 0, 1, 3).reshape(g, -1, s4)
    y = _s5_chunk_outputs(a_lat, e_in, m_intra, m_out)
    return _s5_readout(_from_chunks(y, g, t), w_glu.astype(BF16))


HY_KH = GRID_W + 1
HY_KP = 72
HY_QN = 16


def _hy_phase(w_cols, r_cnt):
    n1_tot = 2 * GRID_W
    n_fft = n1_tot * r_cnt
    kh = GRID_W + 1
    k1 = np.arange(kh, dtype=np.int64)[None, :, None]
    r = np.arange(r_cnt, dtype=np.int64)[:, None, None]
    n1 = np.arange(w_cols, dtype=np.int64)[None, None, :]
    return 2.0 * np.pi * ((k1 * (r_cnt * n1 + r)) % n_fft) / n_fft


def _hy_fwd_table(r_cnt, m):
    th = _hy_phase(GRID_W * m, r_cnt)
    kh = th.shape[1]
    f = np.zeros((r_cnt, 2 * HY_KP, GRID_W * m), np.float32)
    f[:, :kh] = np.cos(th)
    f[:, HY_KP:HY_KP + kh] = -np.sin(th)
    return jnp.asarray(f, BF16)


def _hy_inv_table(r_cnt):
    th = _hy_phase(GRID_W, r_cnt)
    kh = th.shape[1]
    c = np.full((kh,), 2.0)
    c[0] = c[-1] = 1.0
    scale = c[None, :, None] / (2 * GRID_W * r_cnt)
    g = np.zeros((r_cnt, GRID_W, 2 * HY_KP), np.float32)
    g[:, :, :kh] = (scale * np.cos(th)).transpose(0, 2, 1)
    g[:, :, HY_KP:HY_KP + kh] = (-scale * np.sin(th)).transpose(0, 2, 1)
    return jnp.asarray(g, BF16)


def _hy_dft_tables(r_cnt):
    k = np.arange(r_cnt, dtype=np.int64)
    th = 2.0 * np.pi * ((k[:, None] * k[None, :]) % r_cnt) / r_cnt
    fc, fs = np.cos(th), np.sin(th)
    fwd = np.block([[fc, fs], [-fs, fc]]).astype(np.float32)
    inv = np.block([[fc, -fs], [fs, fc]]).astype(np.float32)
    return jnp.asarray(fwd, BF16), jnp.asarray(inv, BF16)


def _hy_short_kernel(x_ref, w_ref, o_ref, *, rows, chunk):
    wk = w_ref[...]
    w0, w1, w2 = wk[0:1], wk[1:2], wk[2:3]
    gw = GRID_W
    col = lax.broadcasted_iota(jnp.int32, (gw, x_ref.shape[1]), 0)

    def piece(a, n):
        return x_ref[pl.ds(a, n), :].astype(F32)

    last = piece(rows - gw, gw)
    prev0 = jnp.where(col == 0, 0.0, pltpu.roll(last, 1, 0))
    o_ref[0:gw, :] = (w0 * prev0 + w1 * piece(0, gw) + w2 * piece(gw, gw)).astype(o_ref.dtype)
    first = piece(0, gw)
    next_l = jnp.where(col == gw - 1, 0.0, pltpu.roll(first, gw - 1, 0))
    o_ref[rows - gw:rows, :] = (w0 * piece(rows - 2 * gw, gw) + w1 * last + w2 * next_l).astype(o_ref.dtype)

    def body(i, carry):
        a = pl.multiple_of(gw + i * chunk, gw)
        o_ref[pl.ds(a, chunk), :] = (w0 * piece(a - gw, chunk) + w1 * piece(a, chunk)
                                     + w2 * piece(a + gw, chunk)).astype(o_ref.dtype)
        return carry

    n_full = (rows - 2 * gw) // chunk
    lax.fori_loop(0, n_full, body, 0)
    rem = rows - 2 * gw - n_full * chunk
    if rem:
        a = gw + n_full * chunk
        o_ref[a:a + rem, :] = (w0 * piece(a - gw, rem) + w1 * piece(a, rem)
                               + w2 * piece(a + gw, rem)).astype(o_ref.dtype)


def _hy_short_conv(uh, short_w, bsz, seq):
    n, ch = uh.shape
    ct = min(ch, LANE)
    chunk = min(512, seq - 2 * GRID_W)
    return pl.pallas_call(
        functools.partial(_hy_short_kernel, rows=seq, chunk=chunk),
        grid=(bsz, ch // ct),
        in_specs=[pl.BlockSpec((seq, ct), lambda b, c: (b, c)),
                  pl.BlockSpec((3, ct), lambda b, c: (0, c))],
        out_specs=pl.BlockSpec((seq, ct), lambda b, c: (b, c)),
        out_shape=jax.ShapeDtypeStruct((n, ch), BF16),
        compiler_params=_params(("parallel", "parallel")),
        name="hy_short_conv",
    )(uh, short_w)


def _hy_fwd1_kernel(*refs, m):
    x_refs, f_ref, o_ref, s_ref = refs[:m], refs[m], refs[m + 1], refs[m + 2]
    gw = GRID_W
    for q in range(HY_QN):
        xq = [x[q * gw:(q + 1) * gw, :] for x in x_refs]
        xq = xq[0] if m == 1 else jnp.concatenate(xq, axis=0)
        s_ref[q] = _dot(f_ref[q], xq)
    t = jnp.swapaxes(s_ref[...], 0, 1).astype(o_ref.dtype)
    o_ref[0] = t[0:HY_KH]
    o_ref[1] = t[HY_KP:HY_KP + HY_KH]


def _hy_fwd1(xs, col_blk0, width, bsz, r_cnt, ct):
    m = len(xs)
    table = _hy_fwd_table(r_cnt, m)
    rb = HY_QN * GRID_W
    nrb = r_cnt // HY_QN
    x_spec = pl.BlockSpec((rb, ct), lambda b, i, c: (b * nrb + i, col_blk0 + c))
    return pl.pallas_call(
        functools.partial(_hy_fwd1_kernel, m=m),
        grid=(bsz, nrb, width // ct),
        in_specs=[x_spec] * m + [pl.BlockSpec((HY_QN, 2 * HY_KP, GRID_W * m), lambda b, i, c: (i, 0, 0))],
        out_specs=pl.BlockSpec((None, 2, HY_KH, HY_QN, ct), lambda b, i, c: (b, 0, 0, i, c)),
        out_shape=jax.ShapeDtypeStruct((bsz, 2, HY_KH, r_cnt, width), BF16),
        scratch_shapes=[pltpu.VMEM((HY_QN, 2 * HY_KP, ct), F32)],
        compiler_params=_params(("parallel", "parallel", "arbitrary")),
        name="hy_fwd_stage1",
    )(*xs, table)


def _hy_spec_kernel(a_ref, f2_ref, o_ref):
    r = a_ref.shape[1]
    x = a_ref[...].reshape(2 * r, a_ref.shape[2])
    o_ref[...] = _dot(f2_ref[...], x).reshape(o_ref.shape)


def _hy_filter_spectrum(a, f2, ct):
    _, _, kh, r_cnt, width = a.shape
    return pl.pallas_call(
        _hy_spec_kernel,
        grid=(kh, width // ct),
        in_specs=[pl.BlockSpec((None, 2, None, r_cnt, ct), lambda k, c: (0, 0, k, 0, c)),
                  pl.BlockSpec((2 * r_cnt, 2 * r_cnt), lambda k, c: (0, 0))],
        out_specs=pl.BlockSpec((2, None, r_cnt, ct), lambda k, c: (0, k, 0, c)),
        out_shape=jax.ShapeDtypeStruct((2, kh, r_cnt, width), F32),
        compiler_params=_params(("parallel", "parallel")),
        name="hy_filter_spectrum",
    )(a, f2)


def _hy_mid_kernel(a_ref, kf_ref, invn_ref, f2_ref, f2i_ref, o_ref):
    r = a_ref.shape[1]
    x = a_ref[...].reshape(2 * r, a_ref.shape[2])
    b = _dot(f2_ref[...], x)
    br, bi = b[:r], b[r:]
    kr, ki = kf_ref[0], kf_ref[1]
    s = invn_ref[...]
    cr = (br * kr - bi * ki) * s
    ci = (br * ki + bi * kr) * s
    c = jnp.concatenate([cr, ci], axis=0).astype(BF16)
    o_ref[...] = _dot(f2i_ref[...], c).reshape(o_ref.shape).astype(o_ref.dtype)


def _hy_mid(a, kf, invn, order, f2, f2i, ct):
    bsz, _, kh, r_cnt, width = a.shape
    nct = width // ct
    return pl.pallas_call(
        _hy_mid_kernel,
        grid=(kh, nct, bsz),
        in_specs=[pl.BlockSpec((None, 2, None, r_cnt, ct), lambda k, c, b: (b, 0, k, 0, c)),
                  pl.BlockSpec((2, None, r_cnt, ct), lambda k, c, b: (0, k, 0, order * nct + c)),
                  pl.BlockSpec((1, ct), lambda k, c, b: (0, order * nct + c)),
                  pl.BlockSpec((2 * r_cnt, 2 * r_cnt), lambda k, c, b: (0, 0)),
                  pl.BlockSpec((2 * r_cnt, 2 * r_cnt), lambda k, c, b: (0, 0))],
        out_specs=pl.BlockSpec((None, 2, None, r_cnt, ct), lambda k, c, b: (b, 0, k, 0, c)),
        out_shape=jax.ShapeDtypeStruct(a.shape, BF16),
        compiler_params=_params(("parallel", "parallel", "arbitrary")),
        name="hy_spectrum_product",
    )(a, kf, invn, f2, f2i)


def _hy_inv1_kernel(d_ref, g_ref, xin_ref, xm_ref, bias_ref, o_ref, t_ref, s_ref):
    gw = GRID_W
    pad = jnp.zeros((HY_KP - HY_KH,) + t_ref.shape[1:], F32)
    t_ref[0:HY_KH] = d_ref[0].astype(F32)
    t_ref[HY_KH:HY_KP] = pad
    t_ref[HY_KP:HY_KP + HY_KH] = d_ref[1].astype(F32)
    t_ref[HY_KP + HY_KH:2 * HY_KP] = pad
    s_ref[...] = jnp.swapaxes(t_ref[...], 0, 1)
    for q in range(HY_QN):
        y = _dot(g_ref[q], s_ref[q].astype(BF16))
        rows = slice(q * gw, (q + 1) * gw)
        xin = xin_ref[rows, :].astype(F32)
        o_ref[rows, :] = (xm_ref[rows, :].astype(F32) * (y + xin * bias_ref[...])).astype(o_ref.dtype)


def _hy_inv1(d, xin, xin_blk0, xm, xm_blk0, bias, ct):
    bsz, _, kh, r_cnt, width = d.shape
    table = _hy_inv_table(r_cnt)
    rb = HY_QN * GRID_W
    nrb = r_cnt // HY_QN
    return pl.pallas_call(
        _hy_inv1_kernel,
        grid=(bsz, nrb, width // ct),
        in_specs=[pl.BlockSpec((None, 2, kh, HY_QN, ct), lambda b, i, c: (b, 0, 0, i, c)),
                  pl.BlockSpec((HY_QN, GRID_W, 2 * HY_KP), lambda b, i, c: (i, 0, 0)),
                  pl.BlockSpec((rb, ct), lambda b, i, c: (b * nrb + i, xin_blk0 + c)),
                  pl.BlockSpec((rb, ct), lambda b, i, c: (b * nrb + i, xm_blk0 + c)),
                  pl.BlockSpec((1, ct), lambda b, i, c: (0, c))],
        out_specs=pl.BlockSpec((rb, ct), lambda b, i, c: (b * nrb + i, c)),
        out_shape=jax.ShapeDtypeStruct((bsz * r_cnt * GRID_W, width), BF16),
        scratch_shapes=[pltpu.VMEM((2 * HY_KP, HY_QN, ct), F32), pltpu.VMEM((HY_QN, 2 * HY_KP, ct), F32)],
        compiler_params=_params(("parallel", "parallel", "arbitrary")),
        name="hy_inv_stage1",
    )(d, table, xin, xm, bias)


def _hy_filter_kernel(w1_ref, b1_ref, f1_ref, w2_ref, b2_ref, f2_ref, w3_ref, b3_ref, dec_ref, om_ref, ph_ref,
                      o_ref, sum_ref, *, direction, seqlen, r_cnt, rows):
    i = pl.program_id(1)
    row = lax.broadcasted_iota(jnp.int32, (rows, LANE), 0) + i * rows
    lane = lax.broadcasted_iota(jnp.int32, (rows, LANE), 1)
    r = jnp.right_shift(row, GRID_W.bit_length() - 1)
    w = jnp.bitwise_and(row, GRID_W - 1)
    n = (w + GRID_W * direction) * r_cnt + r
    p = n if direction == 0 else 2 * seqlen - n
    pf = p.astype(F32)
    t = pf / (seqlen - 1)
    feats = jnp.where(lane == 0, t, jnp.cos(pf * om_ref[...] + ph_ref[...]))
    hid = jnp.sin(f1_ref[...] * (jnp.dot(feats, w1_ref[...], preferred_element_type=F32, precision=HIGHEST)
                                 + b1_ref[...]))
    hid = jnp.sin(f2_ref[...] * (jnp.dot(hid, w2_ref[...], preferred_element_type=F32, precision=HIGHEST)
                                 + b2_ref[...]))
    filt = _dot(hid.astype(BF16), w3_ref[...].astype(BF16)) + b3_ref[...]
    reps = filt.shape[1] // LANE
    t_w = jnp.concatenate([t] * reps, axis=1)
    keep = jnp.concatenate([jnp.where(p < seqlen, 1.0, 0.0)] * reps, axis=1)
    filt = filt * jnp.exp(-t_w * jnp.abs(dec_ref[...])) * keep
    o_ref[...] = filt.astype(o_ref.dtype)

    @pl.when(i == 0)
    def _():
        sum_ref[...] = jnp.zeros_like(sum_ref)

    sum_ref[...] += jnp.sum(jnp.abs(filt), axis=0, keepdims=True)


def _hy_filter_signal(direction, seqlen, w1p, b1, f1, w2, b2, f2, w3, b3, decay, omega, phase, width):
    r_cnt = seqlen // GRID_W
    rows = min(256, seqlen)
    ct = width
    nct = width // ct
    hid = w2.shape[0]
    off = direction * nct
    const = lambda shape: pl.BlockSpec(shape, lambda c, i: (0, 0))
    colv = pl.BlockSpec((1, ct), lambda c, i: (0, off + c))
    return pl.pallas_call(
        functools.partial(_hy_filter_kernel, direction=direction, seqlen=seqlen, r_cnt=r_cnt, rows=rows),
        grid=(nct, seqlen // rows),
        in_specs=[const((LANE, hid)), const((1, hid)), const((1, hid)), const((hid, hid)), const((1, hid)),
                  const((1, hid)), pl.BlockSpec((hid, ct), lambda c, i: (0, off + c)), colv, colv,
                  const((1, LANE)), const((1, LANE))],
        out_specs=[pl.BlockSpec((rows, ct), lambda c, i: (i, c)),
                   pl.BlockSpec((1, ct), lambda c, i: (0, c))],
        out_shape=[jax.ShapeDtypeStruct((seqlen, width), BF16), jax.ShapeDtypeStruct((1, width), F32)],
        compiler_params=_params(("parallel", "arbitrary")),
        name="hy_filter_signal",
    )(w1p, b1, f1, w2, b2, f2, w3, b3, decay, omega, phase)


def _hyena_branch(uh, bsz, seq, short_w, w1, b1, f1, w2, b2, f2, w3, b3, decay, hy_bias):
    width = hy_bias.shape[1]
    r_cnt = seq // GRID_W
    ct = min(width, 256)
    nct = width // ct
    n_emb = w1.shape[0]
    n_bands = (n_emb - 1) // 2
    hid = w1.shape[1]
    bands = jnp.linspace(1e-4, n_bands - 1, n_bands, dtype=F32)
    omega = jnp.zeros((1, LANE), F32).at[0, 1:1 + n_bands].set((2 * jnp.pi / seq) * bands)
    omega = omega.at[0, 1 + n_bands:1 + 2 * n_bands].set((2 * jnp.pi / seq) * bands)
    phase = jnp.zeros((1, LANE), F32).at[0, 1 + n_bands:1 + 2 * n_bands].set(0.5 * jnp.pi)
    w1p = jnp.zeros((LANE, hid), F32).at[:n_emb].set(w1)
    sigs, sums = [], []
    for direction in range(2):
        sig, ssum = _hy_filter_signal(direction, seq, w1p, b1[None], f1[None], w2, b2[None], f2[None], w3,
                                      b3[None], decay[None], omega, phase, HY_ORDER * width)
        sigs.append(sig)
        sums.append(ssum)
    invn = 1.0 / (sums[0] + sums[1] + HY_NORM_EPS)
    f2m, f2i = _hy_dft_tables(r_cnt)
    kf = _hy_filter_spectrum(_hy_fwd1(sigs, 0, HY_ORDER * width, 1, r_cnt, ct), f2m, ct)

    z = _hy_short_conv(uh, short_w.astype(F32), bsz, seq)
    ct3 = min(width, 512)
    a = _hy_fwd1([z], 0, width, bsz, r_cnt, ct)
    d = _hy_mid(a, kf, invn, 0, f2m, f2i, ct3)
    y1 = _hy_inv1(d, z, 0, z, nct, hy_bias[0:1].astype(F32), ct)
    a = _hy_fwd1([y1], 0, width, bsz, r_cnt, ct)
    d = _hy_mid(a, kf, invn, 1, f2m, f2i, ct3)
    return _hy_inv1(d, y1, 0, z, 2 * nct, hy_bias[1:2].astype(F32), ct)


def _merge1_kernel(ya_ref, yh_ref, ug_ref, wa_ref, wh_ref, o_ref):
    d = o_ref.shape[1]
    a = _dot(ya_ref[...], wa_ref[...])
    h = _dot(yh_ref[...], wh_ref[...])
    ga = ug_ref[:, :d].astype(F32)
    gh = ug_ref[:, d:].astype(F32)
    o_ref[...] = (ga * a + gh * h).astype(o_ref.dtype)


def _merge1(ya, yh, ug, wa, wh):
    n, ws = ya.shape
    wh_in = yh.shape[1]
    d = wa.shape[1]
    tm = min(n, 512)
    return pl.pallas_call(
        _merge1_kernel,
        grid=(n // tm,),
        in_specs=[pl.BlockSpec((tm, ws), lambda i: (i, 0)),
                  pl.BlockSpec((tm, wh_in), lambda i: (i, 0)),
                  pl.BlockSpec((tm, 2 * d), lambda i: (i, 0)),
                  pl.BlockSpec((ws, d), lambda i: (0, 0)),
                  pl.BlockSpec((wh_in, d), lambda i: (0, 0))],
        out_specs=pl.BlockSpec((tm, d), lambda i: (i, 0)),
        out_shape=jax.ShapeDtypeStruct((n, d), BF16),
        compiler_params=_params(("parallel",)),
        name="merge_branches",
    )(ya, yh, ug, wa, wh)


def _merge2_kernel(m_ref, x_ref, g1_ref, ng_ref, sh_ref, sc_ref, wo_ref, wr_ref,
                   h1_ref, hm_ref, r1_ref, r2_ref):
    mix = _dot(m_ref[...], wo_ref[...])
    h1 = x_ref[...] + g1_ref[...] * mix
    h1_ref[...] = h1
    y = h1 * lax.rsqrt(jnp.mean(h1 * h1, axis=-1, keepdims=True) + NORM_EPS)
    y = y * ng_ref[...]
    y = y * (1.0 + sc_ref[...]) + sh_ref[...]
    hi = y.astype(BF16)
    lo = (y - hi.astype(F32)).astype(BF16)
    hm_ref[...] = hi
    r1_ref[...] = _dot(hi, wr_ref[...])
    r2_ref[...] = _dot(lo, wr_ref[...])


def _merge2(m, x2d, g1, ng, sh2, sc2, wo, wr, rows_per_mod):
    n, d = x2d.shape
    tm = min(n, 512)
    tiles_per_mod = rows_per_mod // tm
    nr = wr.shape[1]
    mod_spec = pl.BlockSpec((None, 1, d), lambda i: (i // tiles_per_mod, 0, 0))
    return pl.pallas_call(
        _merge2_kernel,
        grid=(n // tm,),
        in_specs=[pl.BlockSpec((tm, d), lambda i: (i, 0)),
                  pl.BlockSpec((tm, d), lambda i: (i, 0)),
                  mod_spec,
                  pl.BlockSpec((1, d), lambda i: (0, 0)),
                  mod_spec, mod_spec,
                  pl.BlockSpec((d, d), lambda i: (0, 0)),
                  pl.BlockSpec((d, nr), lambda i: (0, 0))],
        out_specs=[pl.BlockSpec((tm, d), lambda i: (i, 0)),
                   pl.BlockSpec((tm, d), lambda i: (i, 0)),
                   pl.BlockSpec((tm, nr), lambda i: (i, 0)),
                   pl.BlockSpec((tm, nr), lambda i: (i, 0))],
        out_shape=[jax.ShapeDtypeStruct((n, d), F32),
                   jax.ShapeDtypeStruct((n, d), BF16),
                   jax.ShapeDtypeStruct((n, nr), F32),
                   jax.ShapeDtypeStruct((n, nr), F32)],
        compiler_params=_params(("parallel",)),
        name="out_proj_norm_router",
    )(m, x2d, g1, ng, sh2, sc2, wo, wr)


def _moe_kernel(be_ref, nv_ref, x_ref, wgu_ref, bgu_ref, wd_ref, bd_ref, o_ref, acc_ref):
    i = pl.program_id(0)
    j = pl.program_id(1)
    nj = pl.num_programs(1)

    @pl.when(i < nv_ref[0])
    def _():
        gu = _dot(x_ref[...], wgu_ref[...]) + bgu_ref[...]
        acts = []
        for b in range(gu.shape[1] // (2 * LANE)):
            glu = jnp.minimum(gu[:, 2 * b * LANE:(2 * b + 1) * LANE], SWIGLU_LIMIT)
            lin = jnp.clip(gu[:, (2 * b + 1) * LANE:(2 * b + 2) * LANE], -SWIGLU_LIMIT, SWIGLU_LIMIT)
            acts.append((glu * jax.nn.sigmoid(SWIGLU_ALPHA * glu) * (lin + 1.0)).astype(BF16))
        part = _dot(jnp.concatenate(acts, axis=1), wd_ref[...])

        @pl.when(j == 0)
        def _():
            acc_ref[...] = part

        @pl.when(j > 0)
        def _():
            acc_ref[...] += part

        @pl.when(j == nj - 1)
        def _():
            o_ref[...] = (acc_ref[...] + bd_ref[...]).astype(o_ref.dtype)


def _deinterleave_kernel(w_ref, p_ref, o_ref):
    for b in range(w_ref.shape[1] // (2 * LANE)):
        sl = slice(2 * b * LANE, (2 * b + 2) * LANE)
        o_ref[:, sl] = _dot(w_ref[:, sl].astype(BF16), p_ref[...]).astype(o_ref.dtype)


def _deinterleave_cast(w):
    e, d, f2 = w.shape
    tr = min(d, 512)
    src = jnp.arange(2 * LANE)
    dst = (src % 2) * LANE + src // 2
    perm = jnp.zeros((2 * LANE, 2 * LANE), BF16).at[src, dst].set(1)
    return pl.pallas_call(
        _deinterleave_kernel,
        grid=(e, d // tr),
        in_specs=[pl.BlockSpec((None, tr, f2), lambda ei, i: (ei, i, 0)),
                  pl.BlockSpec((2 * LANE, 2 * LANE), lambda ei, i: (0, 0))],
        out_specs=pl.BlockSpec((None, tr, f2), lambda ei, i: (ei, i, 0)),
        out_shape=jax.ShapeDtypeStruct((e, d, f2), BF16),
        compiler_params=_params(("parallel", "parallel")),
        name="deinterleave_cast",
    )(w, perm)


def _moe_blocks(xg, block_expert, n_valid, wgu, bgu, wd, bd):
    cap, d = xg.shape
    f = wd.shape[1]
    tm = MOE_TM
    tf = min(f, 512)
    nj = f // tf
    nblk = cap // tm

    def row_map(i, j, be, nv):
        return (jnp.minimum(i, nv[0] - 1), 0)

    def jeff(i, j, nv):
        return jnp.where(i < nv[0], j, nj - 1)

    grid_spec = pltpu.PrefetchScalarGridSpec(
        num_scalar_prefetch=2,
        grid=(nblk, nj),
        in_specs=[pl.BlockSpec((tm, d), row_map),
                  pl.BlockSpec((None, d, 2 * tf), lambda i, j, be, nv: (be[i], 0, jeff(i, j, nv))),
                  pl.BlockSpec((None, 1, 2 * tf), lambda i, j, be, nv: (be[i], 0, jeff(i, j, nv))),
                  pl.BlockSpec((None, tf, d), lambda i, j, be, nv: (be[i], jeff(i, j, nv), 0)),
                  pl.BlockSpec((None, 1, d), lambda i, j, be, nv: (be[i], 0, 0))],
        out_specs=pl.BlockSpec((tm, d), row_map),
        scratch_shapes=[pltpu.VMEM((tm, d), F32)])
    return pl.pallas_call(
        _moe_kernel,
        grid_spec=grid_spec,
        out_shape=jax.ShapeDtypeStruct((cap, d), BF16),
        compiler_params=_params(("arbitrary", "arbitrary")),
        name="moe_experts",
    )(block_expert, n_valid, xg, wgu, bgu, wd, bd)


def _route(logits, n_experts):
    n_tok = logits.shape[0]
    top_val, top_idx = lax.top_k(logits, TOP_K)
    gates = jax.nn.softmax(top_val, axis=-1)
    n_assign = n_tok * TOP_K
    flat_e = top_idx.reshape(-1).astype(jnp.int32)
    onehot = (flat_e[:, None] == jnp.arange(n_experts, dtype=jnp.int32)[None, :]).astype(jnp.int32)
    csum = jnp.cumsum(onehot, axis=0)
    rank = jnp.sum(onehot * csum, axis=1) - 1
    counts = csum[-1]
    padded = (counts + MOE_TM - 1) // MOE_TM * MOE_TM
    padded_end = jnp.cumsum(padded)
    padded_start = padded_end - padded
    dest = padded_start[flat_e] + rank
    n_blocks = -(-n_assign // MOE_TM) + n_experts
    cap = n_blocks * MOE_TM
    flat_tok = jnp.arange(n_assign, dtype=jnp.int32) // TOP_K
    slot_tok = jnp.zeros((cap,), jnp.int32).at[dest].set(flat_tok)
    n_valid = (padded_end[-1] // MOE_TM).astype(jnp.int32)
    block_start = jnp.minimum(jnp.arange(n_blocks, dtype=jnp.int32), n_valid - 1) * MOE_TM
    block_expert = jnp.minimum(jnp.searchsorted(padded_end, block_start, side='right'),
                               n_experts - 1).astype(jnp.int32)
    return gates, slot_tok, dest, block_expert, n_valid.reshape(1)


def _final_kernel(h1_ref, yg_ref, gates_ref, g2_ref, fg_ref, o_ref):
    d = h1_ref.shape[1]
    gates = gates_ref[...]
    moe = yg_ref[:, 0:d].astype(F32) * gates[:, 0:1]
    for k in range(1, TOP_K):
        moe = moe + yg_ref[:, k * d:(k + 1) * d].astype(F32) * gates[:, k:k + 1]
    h = h1_ref[...] + g2_ref[...] * moe
    y = h * lax.rsqrt(jnp.mean(h * h, axis=-1, keepdims=True) + NORM_EPS)
    o_ref[...] = (y * fg_ref[...]).astype(o_ref.dtype)


def _final(h1, yg, gates, g2, fg, rows_per_mod):
    n, d = h1.shape
    tm = min(n, 512)
    tiles_per_mod = rows_per_mod // tm
    return pl.pallas_call(
        _final_kernel,
        grid=(n // tm,),
        in_specs=[pl.BlockSpec((tm, d), lambda i: (i, 0)),
                  pl.BlockSpec((tm, TOP_K * d), lambda i: (i, 0)),
                  pl.BlockSpec((tm, TOP_K), lambda i: (i, 0)),
                  pl.BlockSpec((None, 1, d), lambda i: (i // tiles_per_mod, 0, 0)),
                  pl.BlockSpec((1, d), lambda i: (0, 0))],
        out_specs=pl.BlockSpec((tm, d), lambda i: (i, 0)),
        out_shape=jax.ShapeDtypeStruct((n, d), F32),
        compiler_params=_params(("parallel",)),
        name="combine_final_norm",
    )(h1, yg, gates, g2, fg)


def kernel(x, c, ctx, c_ctx, w_ada, b_ada, norm1_g, norm2_g, w_in, s5_lam_re, s5_lam_im, s5_log_dt, s5_b_re, s5_b_im, s5_c_re, s5_c_im, s5_d, s5_w_glu, hy_short_w, hy_pos_w1, hy_pos_b1, hy_freq1, hy_pos_w2, hy_pos_b2, hy_freq2, hy_pos_w3, hy_pos_b3, hy_decay, hy_bias, w_branch_s5, w_branch_hy, w_out, router_w, router_b, w_gate_up, b_gate_up, w_down, b_down, final_g):
    bsz, seq, d = x.shape
    ctx_len = ctx.shape[1]
    depth = w_ada.shape[0]
    assert depth == 1, "only the single-layer configuration is implemented"
    n_experts = router_w.shape[2]
    s5_w = s5_d.shape[1]
    hy_w = hy_bias.shape[2]
    n_tok = bsz * seq

    cc = jnp.zeros((8, d), F32).at[:bsz].set(c).at[bsz].set(c_ctx)
    mod = _ada_mod(cc, w_ada[0], b_ada[0][None]).reshape(8, 6, 1, d)
    sh1, sc1, g1, sh2, sc2, g2 = [mod[:bsz, k] for k in range(6)]
    csh1, csc1 = mod[bsz:bsz + 1, 0], mod[bsz:bsz + 1, 1]

    w_in_b = w_in[0].astype(BF16)
    x2d = x.reshape(n_tok, d)
    ua, uh, ug = _norm_mod_proj(x2d, norm1_g[0][None], sh1, sc1, w_in_b,
                                (s5_w, 3 * hy_w, 2 * d), (False, False, True), seq, 1024, 1024)
    (uca,) = _norm_mod_proj(ctx.reshape(bsz * ctx_len, d), norm1_g[0][None], csh1, csc1,
                            w_in_b[:, :s5_w], (s5_w,), (False,), bsz * ctx_len, 512, 1024)

    ya = _s5_branch(ua, uca, bsz, s5_lam_re[0], s5_lam_im[0], s5_log_dt[0], s5_b_re[0], s5_b_im[0],
                    s5_c_re[0], s5_c_im[0], s5_d[0], s5_w_glu[0])

    yh = _hyena_branch(uh, bsz, seq, hy_short_w[0], hy_pos_w1[0], hy_pos_b1[0], hy_freq1[0], hy_pos_w2[0],
                       hy_pos_b2[0], hy_freq2[0], hy_pos_w3[0], hy_pos_b3[0], hy_decay[0], hy_bias[0])

    m = _merge1(ya, yh, ug, w_branch_s5[0].astype(BF16), w_branch_hy[0].astype(BF16))
    rw = router_w[0]
    rw_hi = rw.astype(BF16)
    rw_lo = (rw - rw_hi.astype(F32)).astype(BF16)
    wr = jnp.zeros((d, 128), BF16).at[:, :n_experts].set(rw_hi).at[:, n_experts:2 * n_experts].set(rw_lo)
    h1, hm, r1, r2 = _merge2(m, x2d, g1, norm2_g[0][None], sh2, sc2, w_out[0].astype(BF16), wr, seq)
    logits = (r1[:, :n_experts] + r1[:, n_experts:2 * n_experts] + r2[:, :n_experts]) + router_b[0]

    gates, slot_tok, dest, block_expert, n_valid = _route(logits, n_experts)
    xg = hm[slot_tok]
    f = w_down.shape[2]
    bgu = b_gate_up[0].reshape(n_experts, f // LANE, LANE, 2).swapaxes(2, 3).reshape(n_experts, 1, 2 * f)
    yslots = _moe_blocks(xg, block_expert, n_valid, _deinterleave_cast(w_gate_up[0]), bgu,
                         w_down[0].astype(BF16), b_down[0][:, None, :])
    yg = yslots[dest].reshape(n_tok, TOP_K * d)
    out = _final(h1, yg, gates, g2, final_g[None], seq)
    return out.reshape(bsz, seq, d)
```

```python
import functools
import math

import jax
import jax.numpy as jnp
import numpy as np
from jax import lax
from jax.experimental import pallas as pl
from jax.experimental.pallas import tpu as pltpu

F32 = jnp.float32
BF16 = jnp.bfloat16
HIGHEST = lax.Precision.HIGHEST

LANE = 128
GRID_W = 64
NORM_EPS = 1e-6
S5_GROUP = 16
S5_CHUNK = 32
HY_ORDER = 2
HY_NORM_EPS = 1e-6
TOP_K = 4
SWIGLU_LIMIT = 7.0
SWIGLU_ALPHA = 1.702
MOE_TM = 512
MOE_TF = 1024
MOE_CHUNKS = 3
VMEM_LIMIT = 56 * 1024 * 1024


def _params(sem, vmem=VMEM_LIMIT):
    return pltpu.CompilerParams(dimension_semantics=sem, vmem_limit_bytes=vmem)


def _dot(a, b):
    return jnp.dot(a, b, preferred_element_type=F32)


def _ada_kernel(c_ref, w_ref, b_ref, o_ref):
    c = c_ref[...]
    a = c * jax.nn.sigmoid(c)
    o_ref[...] = jnp.dot(a, w_ref[...], preferred_element_type=F32, precision=HIGHEST) + b_ref[...]


def _ada_mod(cc, w, b):
    d, n6 = w.shape
    tn = min(512, n6)
    return pl.pallas_call(
        _ada_kernel,
        grid=(n6 // tn,),
        in_specs=[pl.BlockSpec((8, d), lambda j: (0, 0)),
                  pl.BlockSpec((d, tn), lambda j: (0, j)),
                  pl.BlockSpec((1, tn), lambda j: (0, j))],
        out_specs=pl.BlockSpec((8, tn), lambda j: (0, j)),
        out_shape=jax.ShapeDtypeStruct((8, n6), F32),
        compiler_params=_params(("parallel",)),
        name="ada_mod",
    )(cc, w, b)


def _inproj_kernel(x_ref, g_ref, sh_ref, sc_ref, w_ref, *rest, bounds, acts):
    outs = rest[:len(bounds)]
    xn_ref = rest[len(bounds)]
    j = pl.program_id(1)

    @pl.when(j == 0)
    def _():
        x = x_ref[...].astype(F32)
        y = x * lax.rsqrt(jnp.mean(x * x, axis=-1, keepdims=True) + NORM_EPS)
        y = y * g_ref[...]
        y = y * (1.0 + sc_ref[...]) + sh_ref[...]
        xn_ref[...] = y.astype(BF16)

    for (j0, j1), act, o_ref in zip(bounds, acts, outs):
        @pl.when((j >= j0) & (j < j1))
        def _(o_ref=o_ref, act=act):
            r = _dot(xn_ref[...], w_ref[...])
            if act:
                r = jax.nn.sigmoid(r)
            o_ref[...] = r.astype(o_ref.dtype)


def _norm_mod_proj(x2d, g, sh, sc, w, widths, acts, rows_per_mod, tm, tn):
    n, d = x2d.shape
    ncols = w.shape[1]
    tm = min(tm, n)
    tn = min(tn, min(widths))
    bounds, off = [], 0
    for wd in widths:
        bounds.append((off // tn, (off + wd) // tn))
        off += wd
    tiles_per_mod = rows_per_mod // tm

    def out_map(i, j, j0, nj):
        return (i, jnp.clip(j - j0, 0, nj - 1))

    out_specs = [pl.BlockSpec((tm, tn), functools.partial(out_map, j0=j0, nj=j1 - j0))
                 for (j0, j1) in bounds]
    out_shape = [jax.ShapeDtypeStruct((n, wd), BF16) for wd in widths]
    return pl.pallas_call(
        functools.partial(_inproj_kernel, bounds=tuple(bounds), acts=tuple(acts)),
        grid=(n // tm, ncols // tn),
        in_specs=[pl.BlockSpec((tm, d), lambda i, j: (i, 0)),
                  pl.BlockSpec((1, d), lambda i, j: (0, 0)),
                  pl.BlockSpec((None, 1, d), lambda i, j: (i // tiles_per_mod, 0, 0)),
                  pl.BlockSpec((None, 1, d), lambda i, j: (i // tiles_per_mod, 0, 0)),
                  pl.BlockSpec((d, tn), lambda i, j: (0, j))],
        out_specs=out_specs,
        out_shape=out_shape,
        scratch_shapes=[pltpu.VMEM((tm, d), BF16)],
        compiler_params=_params(("parallel", "arbitrary")),
        name="norm_mod_proj",
    )(x2d, g, sh, sc, w)


def _s5_matrices(lam_re, lam_im, log_dt, b_re, b_im, c_re, c_im, d_skip, t):
    lam = lax.complex(lam_re.astype(F32), lam_im.astype(F32))
    dt = jnp.exp(log_dt.astype(F32))[..., None]
    lam_dt = lam * dt
    lam_bar = jnp.exp(lam_dt)
    b_bar = ((lam_bar - 1) / lam)[..., None] * lax.complex(b_re.astype(F32), b_im.astype(F32))
    c_out = lax.complex(c_re.astype(F32), c_im.astype(F32))
    g, p = lam.shape[1], lam.shape[2]
    gs = b_bar.shape[-1]
    k = jnp.arange(t + 1, dtype=F32)
    pw = jnp.exp(lam_dt[..., None] * k)
    kk = jnp.einsum('dgcp,dgpk,dgpe->dgkce', c_out, pw[..., :t], b_bar, precision=HIGHEST).real
    kf, kb = kk[0], kk[1]
    zero_lag = kf[:, :1] + kb[:, :1]
    k_all = jnp.concatenate([kb[:, :0:-1], zero_lag, kf[:, 1:]], axis=1)
    jj = jnp.arange(t)
    lag_idx = jj[None, :] - jj[:, None] + (t - 1)
    m_intra = k_all[:, lag_idx]
    m_intra = m_intra.transpose(0, 1, 4, 2, 3).reshape(g, t * gs, t * gs)
    m_intra = m_intra + jnp.eye(t * gs, dtype=F32)[None] * jnp.tile(
        d_skip.astype(F32).reshape(g, 1, gs), (1, t, 1)).reshape(g, 1, t * gs)
    in_f = pw[0][..., t - 1 - jj][..., None] * b_bar[0][:, :, None, :]
    in_b = pw[1][..., jj][..., None] * b_bar[1][:, :, None, :]

    def to_in(z):
        return z.transpose(0, 2, 3, 1).reshape(g, t * gs, p)

    m_in = jnp.concatenate([to_in(in_f.real), to_in(in_f.imag), to_in(in_b.real), to_in(in_b.imag)], axis=-1)
    out_f = c_out[0][:, :, :, None] * pw[0][:, None, :, 1 + jj]
    out_b = c_out[1][:, :, :, None] * pw[1][:, None, :, t - jj]

    def to_out(z):
        return z.transpose(0, 2, 3, 1).reshape(g, p, t * gs)

    m_out = jnp.concatenate([to_out(out_f.real), -to_out(out_f.imag), to_out(out_b.real), -to_out(out_b.imag)],
                            axis=1)
    mu = pw[..., t]
    mre, mim = mu.real, mu.imag
    coef = jnp.stack([
        jnp.concatenate([mre[0], mre[0], mre[1], mre[1]], axis=-1),
        jnp.concatenate([-mim[0], mim[0], -mim[1], mim[1]], axis=-1),
        jnp.concatenate([mim[0], -mim[0], mim[1], -mim[1]], axis=-1)])
    return m_intra.astype(BF16), m_in.astype(BF16), m_out.astype(BF16), coef


def _s5_in_kernel(a_ref, m_ref, o_ref):
    o_ref[...] = _dot(a_ref[...], m_ref[...])


def _s5_chunk_states(a, m_in):
    g, nc, kdim = a.shape
    s = m_in.shape[2]
    nb = min(nc, 1024)
    return pl.pallas_call(
        _s5_in_kernel,
        grid=(g, nc // nb),
        in_specs=[pl.BlockSpec((None, nb, kdim), lambda gi, i: (gi, i, 0)),
                  pl.BlockSpec((None, kdim, s), lambda gi, i: (gi, 0, 0))],
        out_specs=pl.BlockSpec((None, nb, s), lambda gi, i: (gi, i, 0)),
        out_shape=jax.ShapeDtypeStruct((g, nc, s), F32),
        compiler_params=_params(("parallel", "arbitrary")),
        name="s5_chunk_states",
    )(a, m_in)


def _s5_scan_kernel(sf_ref, sb_ref, init_ref, coef_ref, ef_ref, eb_ref, fin_ref, st_ref, *, nblk, half):
    k = pl.program_id(1)
    nk = pl.num_programs(1)
    lanes = 2 * half

    @pl.when(k == 0)
    def _():
        s0f = init_ref[:, 0:lanes]
        s0b = init_ref[:, lanes:2 * lanes]
        st_ref[0] = s0f
        st_ref[1] = pltpu.roll(s0f, half, 1)
        st_ref[2] = s0b
        st_ref[3] = pltpu.roll(s0b, half, 1)

    def body(r, carry):
        sf, sfw, sb, sbw = carry
        rb = nblk - 1 - r
        xf = sf_ref[r]
        xb = sb_ref[rb]
        ef_ref[r] = sf
        eb_ref[rb] = sb
        xfw = pltpu.roll(xf, half, 1)
        xbw = pltpu.roll(xb, half, 1)
        af, bf, bfw = coef_ref[0, :, 0:lanes], coef_ref[1, :, 0:lanes], coef_ref[2, :, 0:lanes]
        ab, bb, bbw = (coef_ref[0, :, lanes:2 * lanes], coef_ref[1, :, lanes:2 * lanes],
                       coef_ref[2, :, lanes:2 * lanes])
        return (sf * af + sfw * bf + xf, sfw * af + sf * bfw + xfw,
                sb * ab + sbw * bb + xb, sbw * ab + sb * bbw + xbw)

    sf, sfw, sb, sbw = lax.fori_loop(0, nblk, body, (st_ref[0], st_ref[1], st_ref[2], st_ref[3]))
    st_ref[0] = sf
    st_ref[1] = sfw
    st_ref[2] = sb
    st_ref[3] = sbw

    @pl.when(k == nk - 1)
    def _():
        fin_ref[:, 0:lanes] = sf
        fin_ref[:, lanes:2 * lanes] = sb


def _s5_scan(s_t, init, coef):
    bsz, nc, g, s4 = s_t.shape
    lanes = s4 // 2
    nblk = min(nc, 128)
    nk = nc // nblk
    return pl.pallas_call(
        functools.partial(_s5_scan_kernel, nblk=nblk, half=lanes // 2),
        grid=(bsz, nk),
        in_specs=[pl.BlockSpec((None, nblk, g, lanes), lambda b, k: (b, k, 0, 0)),
                  pl.BlockSpec((None, nblk, g, lanes), lambda b, k: (b, nk - 1 - k, 0, 1)),
                  pl.BlockSpec((None, g, s4), lambda b, k: (b, 0, 0)),
                  pl.BlockSpec((3, g, s4), lambda b, k: (0, 0, 0))],
        out_specs=[pl.BlockSpec((None, nblk, g, lanes), lambda b, k: (b, k, 0, 0)),
                   pl.BlockSpec((None, nblk, g, lanes), lambda b, k: (b, nk - 1 - k, 0, 0)),
                   pl.BlockSpec((None, g, s4), lambda b, k: (b, 0, 0))],
        out_shape=[jax.ShapeDtypeStruct((bsz, nc, g, lanes), F32),
                   jax.ShapeDtypeStruct((bsz, nc, g, lanes), F32),
                   jax.ShapeDtypeStruct((bsz, g, s4), F32)],
        scratch_shapes=[pltpu.VMEM((4, g, lanes), F32)],
        compiler_params=_params(("parallel", "arbitrary")),
        name="s5_scan",
    )(s_t, s_t, init, coef)


def _s5_out_kernel(a_ref, e_ref, mi_ref, mo_ref, o_ref):
    y = _dot(a_ref[...], mi_ref[...]) + _dot(e_ref[...], mo_ref[...])
    o_ref[...] = y.astype(o_ref.dtype)


def _s5_chunk_outputs(a, e_in, m_intra, m_out):
    g, nc, kdim = a.shape
    s = e_in.shape[2]
    nb = min(nc, 1024)
    return pl.pallas_call(
        _s5_out_kernel,
        grid=(g, nc // nb),
        in_specs=[pl.BlockSpec((None, nb, kdim), lambda gi, i: (gi, i, 0)),
                  pl.BlockSpec((None, nb, s), lambda gi, i: (gi, i, 0)),
                  pl.BlockSpec((None, kdim, kdim), lambda gi, i: (gi, 0, 0)),
                  pl.BlockSpec((None, s, kdim), lambda gi, i: (gi, 0, 0))],
        out_specs=pl.BlockSpec((None, nb, kdim), lambda gi, i: (gi, i, 0)),
        out_shape=jax.ShapeDtypeStruct((g, nc, kdim), BF16),
        compiler_params=_params(("parallel", "arbitrary")),
        name="s5_chunk_outputs",
    )(a, e_in, m_intra, m_out)


def _s5_readout_kernel(y_ref, wg_ref, o_ref):
    y = jax.nn.gelu(y_ref[...].astype(F32))
    gate = _dot(y.astype(BF16), wg_ref[...])
    o_ref[...] = (y * jax.nn.sigmoid(gate)).astype(o_ref.dtype)


def _s5_readout(y2d, wg):
    n, w = y2d.shape
    tm = min(n, 1024)
    return pl.pallas_call(
        _s5_readout_kernel,
        grid=(n // tm,),
        in_specs=[pl.BlockSpec((tm, w), lambda i: (i, 0)),
                  pl.BlockSpec((w, w), lambda i: (0, 0))],
        out_specs=pl.BlockSpec((tm, w), lambda i: (i, 0)),
        out_shape=jax.ShapeDtypeStruct((n, w), BF16),
        compiler_params=_params(("parallel",)),
        name="s5_readout",
    )(y2d, wg)


def _to_chunks(u2d, g, t):
    n = u2d.shape[0]
    return u2d.reshape(n // t, t, g, S5_GROUP).transpose(2, 0, 1, 3).reshape(g, n // t, t * S5_GROUP)


def _from_chunks(y, g, t):
    nc = y.shape[1]
    return y.reshape(g, nc, t, S5_GROUP).transpose(1, 2, 0, 3).reshape(nc * t, g * S5_GROUP)


def _s5_branch(ua, uca, bsz, lam_re, lam_im, log_dt, b_re, b_im, c_re, c_im, d_skip, w_glu):
    g = lam_re.shape[1]
    t = S5_CHUNK
    m_intra, m_in, m_out, coef = _s5_matrices(lam_re, lam_im, log_dt, b_re, b_im, c_re, c_im, d_skip, t)
    s4 = m_in.shape[2]

    def states(u2d):
        a = _to_chunks(u2d, g, t)
        s = _s5_chunk_states(a, m_in)
        nc = s.shape[1] // bsz
        return a, s.reshape(g, bsz, nc, s4).transpose(1, 2, 0, 3)

    _, s_ctx = states(uca)
    _, _, seed = _s5_scan(s_ctx, jnp.zeros((bsz, g, s4), F32), coef)
    a_lat, s_lat = states(ua)
    e_f, e_b, _ = _s5_scan(s_lat, seed, coef)
    e_in = jnp.concatenate([e_f, e_b], axis=-1).astype(BF16)
    e_in = e_in.transpose(2, 0, 1, 3).reshape(g, -1, s4)
    y = _s5_chunk_outputs(a_lat, e_in, m_intra, m_out)
    return _s5_readout(_from_chunks(y, g, t), w_glu.astype(BF16))


HY_KH = GRID_W + 1
HY_KP = 72
HY_QN = 16


def _hy_phase(w_cols, r_cnt):
    n1_tot = 2 * GRID_W
    n_fft = n1_tot * r_cnt
    kh = GRID_W + 1
    k1 = np.arange(kh, dtype=np.int64)[None, :, None]
    r = np.arange(r_cnt, dtype=np.int64)[:, None, None]
    n1 = np.arange(w_cols, dtype=np.int64)[None, None, :]
    return 2.0 * np.pi * ((k1 * (r_cnt * n1 + r)) % n_fft) / n_fft


def _hy_fwd_table(r_cnt, m):
    th = _hy_phase(GRID_W * m, r_cnt)
    kh = th.shape[1]
    f = np.zeros((r_cnt, 2 * HY_KP, GRID_W * m), np.float32)
    f[:, :kh] = np.cos(th)
    f[:, HY_KP:HY_KP + kh] = -np.sin(th)
    return jnp.asarray(f, BF16)


def _hy_inv_table(r_cnt):
    th = _hy_phase(GRID_W, r_cnt)
    kh = th.shape[1]
    c = np.full((kh,), 2.0)
    c[0] = c[-1] = 1.0
    scale = c[None, :, None] / (2 * GRID_W * r_cnt)
    g = np.zeros((r_cnt, GRID_W, 2 * HY_KP), np.float32)
    g[:, :, :kh] = (scale * np.cos(th)).transpose(0, 2, 1)
    g[:, :, HY_KP:HY_KP + kh] = (-scale * np.sin(th)).transpose(0, 2, 1)
    return jnp.asarray(g, BF16)


def _hy_dft_tables(r_cnt):
    k = np.arange(r_cnt, dtype=np.int64)
    th = 2.0 * np.pi * ((k[:, None] * k[None, :]) % r_cnt) / r_cnt
    fc, fs = np.cos(th), np.sin(th)
    fwd = np.block([[fc, fs], [-fs, fc]]).astype(np.float32)
    inv = np.block([[fc, -fs], [fs, fc]]).astype(np.float32)
    return jnp.asarray(fwd, BF16), jnp.asarray(inv, BF16)


def _hy_short_kernel(x_ref, w_ref, o_ref, *, rows, chunk):
    wk = w_ref[...]
    w0, w1, w2 = wk[0:1], wk[1:2], wk[2:3]
    gw = GRID_W
    col = lax.broadcasted_iota(jnp.int32, (gw, x_ref.shape[1]), 0)

    def piece(a, n):
        return x_ref[pl.ds(a, n), :].astype(F32)

    last = piece(rows - gw, gw)
    prev0 = jnp.where(col == 0, 0.0, pltpu.roll(last, 1, 0))
    o_ref[0:gw, :] = (w0 * prev0 + w1 * piece(0, gw) + w2 * piece(gw, gw)).astype(o_ref.dtype)
    first = piece(0, gw)
    next_l = jnp.where(col == gw - 1, 0.0, pltpu.roll(first, gw - 1, 0))
    o_ref[rows - gw:rows, :] = (w0 * piece(rows - 2 * gw, gw) + w1 * last + w2 * next_l).astype(o_ref.dtype)

    def body(i, carry):
        a = pl.multiple_of(gw + i * chunk, gw)
        o_ref[pl.ds(a, chunk), :] = (w0 * piece(a - gw, chunk) + w1 * piece(a, chunk)
                                     + w2 * piece(a + gw, chunk)).astype(o_ref.dtype)
        return carry

    n_full = (rows - 2 * gw) // chunk
    lax.fori_loop(0, n_full, body, 0)
    rem = rows - 2 * gw - n_full * chunk
    if rem:
        a = gw + n_full * chunk
        o_ref[a:a + rem, :] = (w0 * piece(a - gw, rem) + w1 * piece(a, rem)
                               + w2 * piece(a + gw, rem)).astype(o_ref.dtype)


def _hy_short_conv(uh, short_w, bsz, seq):
    n, ch = uh.shape
    ct = min(ch, LANE)
    chunk = min(512, seq - 2 * GRID_W)
    return pl.pallas_call(
        functools.partial(_hy_short_kernel, rows=seq, chunk=chunk),
        grid=(bsz, ch // ct),
        in_specs=[pl.BlockSpec((seq, ct), lambda b, c: (b, c)),
                  pl.BlockSpec((3, ct), lambda b, c: (0, c))],
        out_specs=pl.BlockSpec((seq, ct), lambda b, c: (b, c)),
        out_shape=jax.ShapeDtypeStruct((n, ch), BF16),
        compiler_params=_params(("parallel", "parallel")),
        name="hy_short_conv",
    )(uh, short_w)


def _hy_fwd1_kernel(*refs, m):
    x_refs, f_ref, o_ref, s_ref = refs[:m], refs[m], refs[m + 1], refs[m + 2]
    gw = GRID_W
    for q in range(HY_QN):
        xq = [x[q * gw:(q + 1) * gw, :] for x in x_refs]
        xq = xq[0] if m == 1 else jnp.concatenate(xq, axis=0)
        s_ref[q] = _dot(f_ref[q], xq)
    t = jnp.swapaxes(s_ref[...], 0, 1).astype(o_ref.dtype)
    o_ref[0] = t[0:HY_KH]
    o_ref[1] = t[HY_KP:HY_KP + HY_KH]


def _hy_fwd1(xs, col_blk0, width, bsz, r_cnt, ct):
    m = len(xs)
    table = _hy_fwd_table(r_cnt, m)
    rb = HY_QN * GRID_W
    nrb = r_cnt // HY_QN
    x_spec = pl.BlockSpec((rb, ct), lambda b, i, c: (b * nrb + i, col_blk0 + c))
    return pl.pallas_call(
        functools.partial(_hy_fwd1_kernel, m=m),
        grid=(bsz, nrb, width // ct),
        in_specs=[x_spec] * m + [pl.BlockSpec((HY_QN, 2 * HY_KP, GRID_W * m), lambda b, i, c: (i, 0, 0))],
        out_specs=pl.BlockSpec((None, 2, HY_KH, HY_QN, ct), lambda b, i, c: (b, 0, 0, i, c)),
        out_shape=jax.ShapeDtypeStruct((bsz, 2, HY_KH, r_cnt, width), BF16),
        scratch_shapes=[pltpu.VMEM((HY_QN, 2 * HY_KP, ct), F32)],
        compiler_params=_params(("parallel", "parallel", "arbitrary")),
        name="hy_fwd_stage1",
    )(*xs, table)


def _hy_spec_kernel(a_ref, f2_ref, o_ref):
    r = a_ref.shape[1]
    x = a_ref[...].reshape(2 * r, a_ref.shape[2])
    o_ref[...] = _dot(f2_ref[...], x).reshape(o_ref.shape)


def _hy_filter_spectrum(a, f2, ct):
    _, _, kh, r_cnt, width = a.shape
    return pl.pallas_call(
        _hy_spec_kernel,
        grid=(kh, width // ct),
        in_specs=[pl.BlockSpec((None, 2, None, r_cnt, ct), lambda k, c: (0, 0, k, 0, c)),
                  pl.BlockSpec((2 * r_cnt, 2 * r_cnt), lambda k, c: (0, 0))],
        out_specs=pl.BlockSpec((2, None, r_cnt, ct), lambda k, c: (0, k, 0, c)),
        out_shape=jax.ShapeDtypeStruct((2, kh, r_cnt, width), F32),
        compiler_params=_params(("parallel", "parallel")),
        name="hy_filter_spectrum",
    )(a, f2)


def _hy_mid_kernel(a_ref, kf_ref, invn_ref, f2_ref, f2i_ref, o_ref):
    r = a_ref.shape[1]
    x = a_ref[...].reshape(2 * r, a_ref.shape[2])
    b = _dot(f2_ref[...], x)
    br, bi = b[:r], b[r:]
    kr, ki = kf_ref[0], kf_ref[1]
    s = invn_ref[...]
    cr = (br * kr - bi * ki) * s
    ci = (br * ki + bi * kr) * s
    c = jnp.concatenate([cr, ci], axis=0).astype(BF16)
    o_ref[...] = _dot(f2i_ref[...], c).reshape(o_ref.shape).astype(o_ref.dtype)


def _hy_mid(a, kf, invn, order, f2, f2i, ct):
    bsz, _, kh, r_cnt, width = a.shape
    nct = width // ct
    return pl.pallas_call(
        _hy_mid_kernel,
        grid=(kh, nct, bsz),
        in_specs=[pl.BlockSpec((None, 2, None, r_cnt, ct), lambda k, c, b: (b, 0, k, 0, c)),
                  pl.BlockSpec((2, None, r_cnt, ct), lambda k, c, b: (0, k, 0, order * nct + c)),
                  pl.BlockSpec((1, ct), lambda k, c, b: (0, order * nct + c)),
                  pl.BlockSpec((2 * r_cnt, 2 * r_cnt), lambda k, c, b: (0, 0)),
                  pl.BlockSpec((2 * r_cnt, 2 * r_cnt), lambda k, c, b: (0, 0))],
        out_specs=pl.BlockSpec((None, 2, None, r_cnt, ct), lambda k, c, b: (b, 0, k, 0, c)),
        out_shape=jax.ShapeDtypeStruct(a.shape, BF16),
        compiler_params=_params(("parallel", "parallel", "arbitrary")),
        name="hy_spectrum_product",
    )(a, kf, invn, f2, f2i)


def _hy_inv1_kernel(d_ref, g_ref, xin_ref, xm_ref, bias_ref, o_ref, t_ref, s_ref):
    gw = GRID_W
    pad = jnp.zeros((HY_KP - HY_KH,) + t_ref.shape[1:], F32)
    t_ref[0:HY_KH] = d_ref[0].astype(F32)
    t_ref[HY_KH:HY_KP] = pad
    t_ref[HY_KP:HY_KP + HY_KH] = d_ref[1].astype(F32)
    t_ref[HY_KP + HY_KH:2 * HY_KP] = pad
    s_ref[...] = jnp.swapaxes(t_ref[...], 0, 1)
    for q in range(HY_QN):
        y = _dot(g_ref[q], s_ref[q].astype(BF16))
        rows = slice(q * gw, (q + 1) * gw)
        xin = xin_ref[rows, :].astype(F32)
        o_ref[rows, :] = (xm_ref[rows, :].astype(F32) * (y + xin * bias_ref[...])).astype(o_ref.dtype)


def _hy_inv1(d, xin, xin_blk0, xm, xm_blk0, bias, ct):
    bsz, _, kh, r_cnt, width = d.shape
    table = _hy_inv_table(r_cnt)
    rb = HY_QN * GRID_W
    nrb = r_cnt // HY_QN
    return pl.pallas_call(
        _hy_inv1_kernel,
        grid=(bsz, nrb, width // ct),
        in_specs=[pl.BlockSpec((None, 2, kh, HY_QN, ct), lambda b, i, c: (b, 0, 0, i, c)),
                  pl.BlockSpec((HY_QN, GRID_W, 2 * HY_KP), lambda b, i, c: (i, 0, 0)),
                  pl.BlockSpec((rb, ct), lambda b, i, c: (b * nrb + i, xin_blk0 + c)),
                  pl.BlockSpec((rb, ct), lambda b, i, c: (b * nrb + i, xm_blk0 + c)),
                  pl.BlockSpec((1, ct), lambda b, i, c: (0, c))],
        out_specs=pl.BlockSpec((rb, ct), lambda b, i, c: (b * nrb + i, c)),
        out_shape=jax.ShapeDtypeStruct((bsz * r_cnt * GRID_W, width), BF16),
        scratch_shapes=[pltpu.VMEM((2 * HY_KP, HY_QN, ct), F32), pltpu.VMEM((HY_QN, 2 * HY_KP, ct), F32)],
        compiler_params=_params(("parallel", "parallel", "arbitrary")),
        name="hy_inv_stage1",
    )(d, table, xin, xm, bias)


def _hy_filter_kernel(w1_ref, b1_ref, f1_ref, w2_ref, b2_ref, f2_ref, w3_ref, b3_ref, dec_ref, om_ref, ph_ref,
                      o_ref, sum_ref, *, direction, seqlen, r_cnt, rows):
    i = pl.program_id(1)
    row = lax.broadcasted_iota(jnp.int32, (rows, LANE), 0) + i * rows
    lane = lax.broadcasted_iota(jnp.int32, (rows, LANE), 1)
    r = jnp.right_shift(row, GRID_W.bit_length() - 1)
    w = jnp.bitwise_and(row, GRID_W - 1)
    n = (w + GRID_W * direction) * r_cnt + r
    p = n if direction == 0 else 2 * seqlen - n
    pf = p.astype(F32)
    t = pf / (seqlen - 1)
    feats = jnp.where(lane == 0, t, jnp.cos(pf * om_ref[...] + ph_ref[...]))
    hid = jnp.sin(f1_ref[...] * (jnp.dot(feats, w1_ref[...], preferred_element_type=F32, precision=HIGHEST)
                                 + b1_ref[...]))
    hid = jnp.sin(f2_ref[...] * (jnp.dot(hid, w2_ref[...], preferred_element_type=F32, precision=HIGHEST)
                                 + b2_ref[...]))
    filt = _dot(hid.astype(BF16), w3_ref[...].astype(BF16)) + b3_ref[...]
    reps = filt.shape[1] // LANE
    t_w = jnp.concatenate([t] * reps, axis=1)
    keep = jnp.concatenate([jnp.where(p < seqlen, 1.0, 0.0)] * reps, axis=1)
    filt = filt * jnp.exp(-t_w * jnp.abs(dec_ref[...])) * keep
    o_ref[...] = filt.astype(o_ref.dtype)

    @pl.when(i == 0)
    def _():
        sum_ref[...] = jnp.zeros_like(sum_ref)

    sum_ref[...] += jnp.sum(jnp.abs(filt), axis=0, keepdims=True)


def _hy_filter_signal(direction, seqlen, w1p, b1, f1, w2, b2, f2, w3, b3, decay, omega, phase, width):
    r_cnt = seqlen // GRID_W
    rows = min(256, seqlen)
    ct = width
    nct = width // ct
    hid = w2.shape[0]
    off = direction * nct
    const = lambda shape: pl.BlockSpec(shape, lambda c, i: (0, 0))
    colv = pl.BlockSpec((1, ct), lambda c, i: (0, off + c))
    return pl.pallas_call(
        functools.partial(_hy_filter_kernel, direction=direction, seqlen=seqlen, r_cnt=r_cnt, rows=rows),
        grid=(nct, seqlen // rows),
        in_specs=[const((LANE, hid)), const((1, hid)), const((1, hid)), const((hid, hid)), const((1, hid)),
                  const((1, hid)), pl.BlockSpec((hid, ct), lambda c, i: (0, off + c)), colv, colv,
                  const((1, LANE)), const((1, LANE))],
        out_specs=[pl.BlockSpec((rows, ct), lambda c, i: (i, c)),
                   pl.BlockSpec((1, ct), lambda c, i: (0, c))],
        out_shape=[jax.ShapeDtypeStruct((seqlen, width), BF16), jax.ShapeDtypeStruct((1, width), F32)],
        compiler_params=_params(("parallel", "arbitrary")),
        name="hy_filter_signal",
    )(w1p, b1, f1, w2, b2, f2, w3, b3, decay, omega, phase)


def _hyena_branch(uh, bsz, seq, short_w, w1, b1, f1, w2, b2, f2, w3, b3, decay, hy_bias):
    width = hy_bias.shape[1]
    r_cnt = seq // GRID_W
    ct = min(width, 256)
    nct = width // ct
    n_emb = w1.shape[0]
    n_bands = (n_emb - 1) // 2
    hid = w1.shape[1]
    bands = jnp.linspace(1e-4, n_bands - 1, n_bands, dtype=F32)
    omega = jnp.zeros((1, LANE), F32).at[0, 1:1 + n_bands].set((2 * jnp.pi / seq) * bands)
    omega = omega.at[0, 1 + n_bands:1 + 2 * n_bands].set((2 * jnp.pi / seq) * bands)
    phase = jnp.zeros((1, LANE), F32).at[0, 1 + n_bands:1 + 2 * n_bands].set(0.5 * jnp.pi)
    w1p = jnp.zeros((LANE, hid), F32).at[:n_emb].set(w1)
    sigs, sums = [], []
    for direction in range(2):
        sig, ssum = _hy_filter_signal(direction, seq, w1p, b1[None], f1[None], w2, b2[None], f2[None], w3,
                                      b3[None], decay[None], omega, phase, HY_ORDER * width)
        sigs.append(sig)
        sums.append(ssum)
    invn = 1.0 / (sums[0] + sums[1] + HY_NORM_EPS)
    f2m, f2i = _hy_dft_tables(r_cnt)
    ct3 = min(width, 1024)
    kf = _hy_filter_spectrum(_hy_fwd1(sigs, 0, HY_ORDER * width, 1, r_cnt, ct), f2m, ct3)

    z = _hy_short_conv(uh, short_w.astype(F32), bsz, seq)
    a = _hy_fwd1([z], 0, width, bsz, r_cnt, ct)
    d = _hy_mid(a, kf, invn, 0, f2m, f2i, ct3)
    y1 = _hy_inv1(d, z, 0, z, nct, hy_bias[0:1].astype(F32), ct)
    a = _hy_fwd1([y1], 0, width, bsz, r_cnt, ct)
    d = _hy_mid(a, kf, invn, 1, f2m, f2i, ct3)
    return _hy_inv1(d, y1, 0, z, 2 * nct, hy_bias[1:2].astype(F32), ct)


def _merge1_kernel(ya_ref, yh_ref, ug_ref, wa_ref, wh_ref, o_ref):
    d = o_ref.shape[1]
    a = _dot(ya_ref[...], wa_ref[...])
    h = _dot(yh_ref[...], wh_ref[...])
    ga = ug_ref[:, :d].astype(F32)
    gh = ug_ref[:, d:].astype(F32)
    o_ref[...] = (ga * a + gh * h).astype(o_ref.dtype)


def _merge1(ya, yh, ug, wa, wh):
    n, ws = ya.shape
    wh_in = yh.shape[1]
    d = wa.shape[1]
    tm = min(n, 512)
    return pl.pallas_call(
        _merge1_kernel,
        grid=(n // tm,),
        in_specs=[pl.BlockSpec((tm, ws), lambda i: (i, 0)),
                  pl.BlockSpec((tm, wh_in), lambda i: (i, 0)),
                  pl.BlockSpec((tm, 2 * d), lambda i: (i, 0)),
                  pl.BlockSpec((ws, d), lambda i: (0, 0)),
                  pl.BlockSpec((wh_in, d), lambda i: (0, 0))],
        out_specs=pl.BlockSpec((tm, d), lambda i: (i, 0)),
        out_shape=jax.ShapeDtypeStruct((n, d), BF16),
        compiler_params=_params(("parallel",)),
        name="merge_branches",
    )(ya, yh, ug, wa, wh)


def _merge2_kernel(m_ref, x_ref, g1_ref, ng_ref, sh_ref, sc_ref, wo_ref, wr_ref,
                   h1_ref, hm_ref, r1_ref, r2_ref):
    mix = _dot(m_ref[...], wo_ref[...])
    h1 = x_ref[...] + g1_ref[...] * mix
    h1_ref[...] = h1
    y = h1 * lax.rsqrt(jnp.mean(h1 * h1, axis=-1, keepdims=True) + NORM_EPS)
    y = y * ng_ref[...]
    y = y * (1.0 + sc_ref[...]) + sh_ref[...]
    hi = y.astype(BF16)
    lo = (y - hi.astype(F32)).astype(BF16)
    hm_ref[...] = hi
    r1_ref[...] = _dot(hi, wr_ref[...])
    r2_ref[...] = _dot(lo, wr_ref[...])


def _merge2(m, x2d, g1, ng, sh2, sc2, wo, wr, rows_per_mod):
    n, d = x2d.shape
    tm = min(n, 512)
    tiles_per_mod = rows_per_mod // tm
    nr = wr.shape[1]
    mod_spec = pl.BlockSpec((None, 1, d), lambda i: (i // tiles_per_mod, 0, 0))
    return pl.pallas_call(
        _merge2_kernel,
        grid=(n // tm,),
        in_specs=[pl.BlockSpec((tm, d), lambda i: (i, 0)),
                  pl.BlockSpec((tm, d), lambda i: (i, 0)),
                  mod_spec,
                  pl.BlockSpec((1, d), lambda i: (0, 0)),
                  mod_spec, mod_spec,
                  pl.BlockSpec((d, d), lambda i: (0, 0)),
                  pl.BlockSpec((d, nr), lambda i: (0, 0))],
        out_specs=[pl.BlockSpec((tm, d), lambda i: (i, 0)),
                   pl.BlockSpec((tm, d), lambda i: (i, 0)),
                   pl.BlockSpec((tm, nr), lambda i: (i, 0)),
                   pl.BlockSpec((tm, nr), lambda i: (i, 0))],
        out_shape=[jax.ShapeDtypeStruct((n, d), F32),
                   jax.ShapeDtypeStruct((n, d), BF16),
                   jax.ShapeDtypeStruct((n, nr), F32),
                   jax.ShapeDtypeStruct((n, nr), F32)],
        compiler_params=_params(("parallel",)),
        name="out_proj_norm_router",
    )(m, x2d, g1, ng, sh2, sc2, wo, wr)


def _moe_kernel(be_ref, nv_ref, x_ref, wgu_ref, bgu_ref, wd_ref, bd_ref, *rest):
    o_ref, acc_ref = rest[-2:]
    i = pl.program_id(0)
    j = pl.program_id(1)
    nj = pl.num_programs(1)

    @pl.when((i < nv_ref[0]) & (j == 0))
    def _():
        acc_ref[...] = jnp.zeros_like(acc_ref)

    @pl.when(i < nv_ref[0])
    def _():
        gu = _dot(x_ref[...], wgu_ref[...]) + bgu_ref[...]
        acts = []
        for b in range(gu.shape[1] // (2 * LANE)):
            glu = jnp.minimum(gu[:, 2 * b * LANE:(2 * b + 1) * LANE], SWIGLU_LIMIT)
            lin = jnp.clip(gu[:, (2 * b + 1) * LANE:(2 * b + 2) * LANE], -SWIGLU_LIMIT, SWIGLU_LIMIT)
            acts.append((glu * jax.nn.sigmoid(SWIGLU_ALPHA * glu) * (lin + 1.0)).astype(BF16))
        acc_ref[...] += _dot(jnp.concatenate(acts, axis=1), wd_ref[...])

        @pl.when(j == nj - 1)
        def _():
            o_ref[...] = (acc_ref[...] + bd_ref[...]).astype(o_ref.dtype)


def _deinterleave_kernel(w_ref, p_ref, o_ref):
    for b in range(w_ref.shape[1] // (2 * LANE)):
        sl = slice(2 * b * LANE, (2 * b + 2) * LANE)
        o_ref[:, sl] = _dot(w_ref[:, sl].astype(BF16), p_ref[...]).astype(o_ref.dtype)


def _deinterleave_cast(w):
    e, d, f2 = w.shape
    tr = min(d, 512)
    src = jnp.arange(2 * LANE)
    dst = (src % 2) * LANE + src // 2
    perm = jnp.zeros((2 * LANE, 2 * LANE), BF16).at[src, dst].set(1)
    return pl.pallas_call(
        _deinterleave_kernel,
        grid=(e, d // tr),
        in_specs=[pl.BlockSpec((None, tr, f2), lambda ei, i: (ei, i, 0)),
                  pl.BlockSpec((2 * LANE, 2 * LANE), lambda ei, i: (0, 0))],
        out_specs=pl.BlockSpec((None, tr, f2), lambda ei, i: (ei, i, 0)),
        out_shape=jax.ShapeDtypeStruct((e, d, f2), BF16),
        compiler_params=_params(("parallel", "parallel")),
        name="deinterleave_cast",
    )(w, perm)


def _moe_blocks(xg, y_prev, blk0, cap, block_expert, n_valid, wgu, bgu, wd, bd):
    rows, d = xg.shape
    f = wd.shape[1]
    tm = MOE_TM
    tf = min(f, MOE_TF)
    nj = f // tf
    nblk = rows // tm

    def last(nv):
        return jnp.maximum(nv[0] - 1, 0)

    def jeff(i, j, nv):
        return jnp.where(i < nv[0], j, nj - 1)

    in_specs = [pl.BlockSpec((tm, d), lambda i, j, be, nv: (jnp.minimum(i, last(nv)), 0)),
                pl.BlockSpec((None, d, 2 * tf), lambda i, j, be, nv: (be[i], 0, jeff(i, j, nv))),
                pl.BlockSpec((None, 1, 2 * tf), lambda i, j, be, nv: (be[i], 0, jeff(i, j, nv))),
                pl.BlockSpec((None, tf, d), lambda i, j, be, nv: (be[i], jeff(i, j, nv), 0)),
                pl.BlockSpec((None, 1, d), lambda i, j, be, nv: (be[i], 0, 0))]
    args = [block_expert, n_valid, xg, wgu, bgu, wd, bd]
    aliases = {}
    if y_prev is not None:
        in_specs.append(pl.BlockSpec(memory_space=pl.ANY))
        args.append(y_prev)
        aliases = {len(args) - 1: 0}
    grid_spec = pltpu.PrefetchScalarGridSpec(
        num_scalar_prefetch=2,
        grid=(nblk, nj),
        in_specs=in_specs,
        out_specs=pl.BlockSpec((tm, d), lambda i, j, be, nv: (blk0 + jnp.minimum(i, last(nv)), 0)),
        scratch_shapes=[pltpu.VMEM((tm, d), F32)])
    return pl.pallas_call(
        _moe_kernel,
        grid_spec=grid_spec,
        out_shape=jax.ShapeDtypeStruct((cap, d), BF16),
        input_output_aliases=aliases,
        compiler_params=_params(("arbitrary", "arbitrary")),
        name="moe_experts",
    )(*args)


def _route(logits, n_experts):
    n_tok = logits.shape[0]
    top_val, top_idx = lax.top_k(logits, TOP_K)
    gates = jax.nn.softmax(top_val, axis=-1)
    n_assign = n_tok * TOP_K
    flat_e = top_idx.reshape(-1).astype(jnp.int32)
    experts = jnp.arange(n_experts, dtype=jnp.int32)
    onehot = (flat_e[:, None] == experts[None, :]).astype(jnp.int32)
    csum = jnp.cumsum(onehot, axis=0)
    rank = jnp.sum(onehot * csum, axis=1) - 1
    counts = csum[-1]
    padded = (counts + MOE_TM - 1) // MOE_TM * MOE_TM
    start = jnp.cumsum(counts) - counts
    padded_end = jnp.cumsum(padded)
    padded_start = padded_end - padded
    dest = padded_start[flat_e] + rank
    n_blocks = -(-n_assign // MOE_TM) + n_experts
    cap = n_blocks * MOE_TM
    n_valid = (padded_end[-1] // MOE_TM).astype(jnp.int32)
    order = jnp.argsort(flat_e)
    slots = jnp.arange(cap, dtype=jnp.int32)
    slot_e = jnp.minimum(jnp.sum((padded_end[None, :] <= slots[:, None]).astype(jnp.int32), axis=1), n_experts - 1)
    slot_rank = slots - padded_start[slot_e]
    src = jnp.minimum(start[slot_e] + slot_rank, n_assign - 1)
    slot_tok = jnp.where(slot_rank < counts[slot_e], order[src] // TOP_K, 0).astype(jnp.int32)
    block_start = jnp.minimum(jnp.arange(n_blocks, dtype=jnp.int32), n_valid - 1) * MOE_TM
    block_expert = jnp.minimum(
        jnp.sum((padded_end[None, :] <= block_start[:, None]).astype(jnp.int32), axis=1), n_experts - 1)
    return gates, slot_tok, dest, block_expert, n_valid


def _final_kernel(h1_ref, *rest):
    yg_refs = rest[:TOP_K]
    gates_ref, g2_ref, fg_ref, o_ref = rest[TOP_K:]
    gates = gates_ref[...]
    moe = yg_refs[0][...].astype(F32) * gates[:, 0:1]
    for k in range(1, TOP_K):
        moe = moe + yg_refs[k][...].astype(F32) * gates[:, k:k + 1]
    h = h1_ref[...] + g2_ref[...] * moe
    y = h * lax.rsqrt(jnp.mean(h * h, axis=-1, keepdims=True) + NORM_EPS)
    o_ref[...] = (y * fg_ref[...]).astype(o_ref.dtype)


def _final(h1, yg, gates, g2, fg, rows_per_mod):
    n, d = h1.shape
    tm = min(n, 512)
    tiles_per_mod = rows_per_mod // tm
    nt = n // tm
    yg_specs = [pl.BlockSpec((tm, d), functools.partial(lambda i, k: (k * nt + i, 0), k=k)) for k in range(TOP_K)]
    return pl.pallas_call(
        _final_kernel,
        grid=(nt,),
        in_specs=[pl.BlockSpec((tm, d), lambda i: (i, 0))] + yg_specs + [
                  pl.BlockSpec((tm, TOP_K), lambda i: (i, 0)),
                  pl.BlockSpec((None, 1, d), lambda i: (i // tiles_per_mod, 0, 0)),
                  pl.BlockSpec((1, d), lambda i: (0, 0))],
        out_specs=pl.BlockSpec((tm, d), lambda i: (i, 0)),
        out_shape=jax.ShapeDtypeStruct((n, d), F32),
        compiler_params=_params(("parallel",)),
        name="combine_final_norm",
    )(h1, *([yg] * TOP_K), gates, g2, fg)


def kernel(x, c, ctx, c_ctx, w_ada, b_ada, norm1_g, norm2_g, w_in, s5_lam_re, s5_lam_im, s5_log_dt, s5_b_re, s5_b_im, s5_c_re, s5_c_im, s5_d, s5_w_glu, hy_short_w, hy_pos_w1, hy_pos_b1, hy_freq1, hy_pos_w2, hy_pos_b2, hy_freq2, hy_pos_w3, hy_pos_b3, hy_decay, hy_bias, w_branch_s5, w_branch_hy, w_out, router_w, router_b, w_gate_up, b_gate_up, w_down, b_down, final_g):
    bsz, seq, d = x.shape
    ctx_len = ctx.shape[1]
    depth = w_ada.shape[0]
    assert depth == 1, "only the single-layer configuration is implemented"
    n_experts = router_w.shape[2]
    s5_w = s5_d.shape[1]
    hy_w = hy_bias.shape[2]
    n_tok = bsz * seq

    cc = jnp.zeros((8, d), F32).at[:bsz].set(c).at[bsz].set(c_ctx)
    mod = _ada_mod(cc, w_ada[0], b_ada[0][None]).reshape(8, 6, 1, d)
    sh1, sc1, g1, sh2, sc2, g2 = [mod[:bsz, k] for k in range(6)]
    csh1, csc1 = mod[bsz:bsz + 1, 0], mod[bsz:bsz + 1, 1]

    w_in_b = w_in[0].astype(BF16)
    x2d = x.reshape(n_tok, d)
    ua, uh, ug = _norm_mod_proj(x2d, norm1_g[0][None], sh1, sc1, w_in_b,
                                (s5_w, 3 * hy_w, 2 * d), (False, False, True), seq, 1024, 1024)
    (uca,) = _norm_mod_proj(ctx.reshape(bsz * ctx_len, d), norm1_g[0][None], csh1, csc1,
                            w_in_b[:, :s5_w], (s5_w,), (False,), bsz * ctx_len, 512, 1024)

    ya = _s5_branch(ua, uca, bsz, s5_lam_re[0], s5_lam_im[0], s5_log_dt[0], s5_b_re[0], s5_b_im[0],
                    s5_c_re[0], s5_c_im[0], s5_d[0], s5_w_glu[0])

    yh = _hyena_branch(uh, bsz, seq, hy_short_w[0], hy_pos_w1[0], hy_pos_b1[0], hy_freq1[0], hy_pos_w2[0],
                       hy_pos_b2[0], hy_freq2[0], hy_pos_w3[0], hy_pos_b3[0], hy_decay[0], hy_bias[0])

    m = _merge1(ya, yh, ug, w_branch_s5[0].astype(BF16), w_branch_hy[0].astype(BF16))
    rw = router_w[0]
    rw_hi = rw.astype(BF16)
    rw_lo = (rw - rw_hi.astype(F32)).astype(BF16)
    wr = jnp.zeros((d, 128), BF16).at[:, :n_experts].set(rw_hi).at[:, n_experts:2 * n_experts].set(rw_lo)
    h1, hm, r1, r2 = _merge2(m, x2d, g1, norm2_g[0][None], sh2, sc2, w_out[0].astype(BF16), wr, seq)
    logits = (r1[:, :n_experts] + r1[:, n_experts:2 * n_experts] + r2[:, :n_experts]) + router_b[0]

    gates, slot_tok, dest, block_expert, n_valid = _route(logits, n_experts)
    f = w_down.shape[2]
    bgu = b_gate_up[0].reshape(n_experts, f // LANE, LANE, 2).swapaxes(2, 3).reshape(n_experts, 1, 2 * f)
    wgu_b = _deinterleave_cast(w_gate_up[0])
    wd_b = w_down[0].astype(BF16)
    n_blocks = block_expert.shape[0]
    cap = n_blocks * MOE_TM
    chunk_blocks = -(-n_blocks // MOE_CHUNKS)
    yslots = None
    for blk0 in range(0, n_blocks, chunk_blocks):
        nb = min(chunk_blocks, n_blocks - blk0)
        xg = hm[slot_tok[blk0 * MOE_TM:(blk0 + nb) * MOE_TM]]
        nv = jnp.clip(n_valid - blk0, 0, nb).astype(jnp.int32).reshape(1)
        yslots = _moe_blocks(xg, yslots, blk0, cap, block_expert[blk0:blk0 + nb], nv, wgu_b, bgu, wd_b,
                             b_down[0][:, None, :])
    yg = yslots[dest.reshape(n_tok, TOP_K).T.reshape(-1)]
    out = _final(h1, yg, gates, g2, final_g[None], seq)
    return out.reshape(bsz, seq, d)
```

```python
import functools
import math

import jax
import jax.numpy as jnp
import numpy as np
from jax import lax
from jax.experimental import pallas as pl
from jax.experimental.pallas import tpu as pltpu

F32 = jnp.float32
BF16 = jnp.bfloat16
HIGHEST = lax.Precision.HIGHEST

LANE = 128
GRID_W = 64
NORM_EPS = 1e-6
S5_GROUP = 16
S5_CHUNK = 32
HY_ORDER = 2
HY_NORM_EPS = 1e-6
TOP_K = 4
SWIGLU_LIMIT = 7.0
SWIGLU_ALPHA = 1.702
MOE_TM = 512
MOE_TF = 1024
VMEM_LIMIT = 56 * 1024 * 1024


def _params(sem, vmem=VMEM_LIMIT):
    return pltpu.CompilerParams(dimension_semantics=sem, vmem_limit_bytes=vmem)


def _dot(a, b):
    return jnp.dot(a, b, preferred_element_type=F32)


def _ada_kernel(c_ref, w_ref, b_ref, o_ref):
    c = c_ref[...]
    a = c * jax.nn.sigmoid(c)
    o_ref[...] = jnp.dot(a, w_ref[...], preferred_element_type=F32, precision=HIGHEST) + b_ref[...]


def _ada_mod(cc, w, b):
    d, n6 = w.shape
    tn = min(512, n6)
    return pl.pallas_call(
        _ada_kernel,
        grid=(n6 // tn,),
        in_specs=[pl.BlockSpec((8, d), lambda j: (0, 0)),
                  pl.BlockSpec((d, tn), lambda j: (0, j)),
                  pl.BlockSpec((1, tn), lambda j: (0, j))],
        out_specs=pl.BlockSpec((8, tn), lambda j: (0, j)),
        out_shape=jax.ShapeDtypeStruct((8, n6), F32),
        compiler_params=_params(("parallel",)),
        name="ada_mod",
    )(cc, w, b)


def _inproj_kernel(x_ref, g_ref, sh_ref, sc_ref, w_ref, *rest, bounds, acts, riders):
    nr = len(riders)
    n_in = nr + (1 if any(riders) else 0)
    rider_in, rest = rest[:n_in], rest[n_in:]
    outs, rider_out, xn_ref = rest[:len(bounds)], rest[len(bounds):len(bounds) + nr], rest[-1]
    j = pl.program_id(1)

    @pl.when(j == 0)
    def _():
        x = x_ref[...].astype(F32)
        y = x * lax.rsqrt(jnp.mean(x * x, axis=-1, keepdims=True) + NORM_EPS)
        y = y * g_ref[...]
        y = y * (1.0 + sc_ref[...]) + sh_ref[...]
        xn_ref[...] = y.astype(BF16)

    for (j0, j1), act, o_ref in zip(bounds, acts, outs):
        @pl.when((j >= j0) & (j < j1))
        def _(o_ref=o_ref, act=act):
            r = _dot(xn_ref[...], w_ref[...])
            if act:
                r = jax.nn.sigmoid(r)
            o_ref[...] = r.astype(o_ref.dtype)

    for r_in, r_out, deint in zip(rider_in, rider_out, riders):
        if deint:
            for b in range(r_in.shape[1] // (2 * LANE)):
                sl = slice(2 * b * LANE, (2 * b + 2) * LANE)
                r_out[:, sl] = _dot(r_in[:, sl].astype(BF16), rider_in[-1][...]).astype(r_out.dtype)
        else:
            r_out[...] = r_in[...].astype(r_out.dtype)


def _deinterleave_perm():
    src = jnp.arange(2 * LANE)
    dst = (src % 2) * LANE + src // 2
    return jnp.zeros((2 * LANE, 2 * LANE), BF16).at[src, dst].set(1)


def _norm_mod_proj(x2d, g, sh, sc, w, widths, acts, rows_per_mod, tm, tn, riders=()):
    n, d = x2d.shape
    ncols = w.shape[1]
    tm = min(tm, n)
    tn = min(tn, min(widths))
    bounds, off = [], 0
    for wd in widths:
        bounds.append((off // tn, (off + wd) // tn))
        off += wd
    tiles_per_mod = rows_per_mod // tm
    nj = ncols // tn
    nsteps = (n // tm) * nj

    def out_map(i, j, j0, nj):
        return (i, jnp.clip(j - j0, 0, nj - 1))

    out_specs = [pl.BlockSpec((tm, tn), functools.partial(out_map, j0=j0, nj=j1 - j0))
                 for (j0, j1) in bounds]
    out_shape = [jax.ShapeDtypeStruct((n, wd), BF16) for wd in widths]
    rider_args, rider_specs = [], []
    for arr, _ in riders:
        spec = pl.BlockSpec((arr.shape[0] // nsteps, arr.shape[1]), lambda i, j: (i * nj + j, 0))
        rider_args.append(arr)
        rider_specs.append(spec)
        out_specs.append(spec)
        out_shape.append(jax.ShapeDtypeStruct(arr.shape, BF16))
    if any(flag for _, flag in riders):
        rider_args.append(_deinterleave_perm())
        rider_specs.append(pl.BlockSpec((2 * LANE, 2 * LANE), lambda i, j: (0, 0)))
    return pl.pallas_call(
        functools.partial(_inproj_kernel, bounds=tuple(bounds), acts=tuple(acts),
                          riders=tuple(flag for _, flag in riders)),
        grid=(n // tm, nj),
        in_specs=[pl.BlockSpec((tm, d), lambda i, j: (i, 0)),
                  pl.BlockSpec((1, d), lambda i, j: (0, 0)),
                  pl.BlockSpec((None, 1, d), lambda i, j: (i // tiles_per_mod, 0, 0)),
                  pl.BlockSpec((None, 1, d), lambda i, j: (i // tiles_per_mod, 0, 0)),
                  pl.BlockSpec((d, tn), lambda i, j: (0, j))] + rider_specs,
        out_specs=out_specs,
        out_shape=out_shape,
        scratch_shapes=[pltpu.VMEM((tm, d), BF16)],
        compiler_params=_params(("parallel", "arbitrary")),
        name="norm_mod_proj",
    )(x2d, g, sh, sc, w, *rider_args)


def _s5_matrices(lam_re, lam_im, log_dt, b_re, b_im, c_re, c_im, d_skip, t):
    lam = lax.complex(lam_re.astype(F32), lam_im.astype(F32))
    dt = jnp.exp(log_dt.astype(F32))[..., None]
    lam_dt = lam * dt
    lam_bar = jnp.exp(lam_dt)
    b_bar = ((lam_bar - 1) / lam)[..., None] * lax.complex(b_re.astype(F32), b_im.astype(F32))
    c_out = lax.complex(c_re.astype(F32), c_im.astype(F32))
    g, p = lam.shape[1], lam.shape[2]
    gs = b_bar.shape[-1]
    k = jnp.arange(t + 1, dtype=F32)
    pw = jnp.exp(lam_dt[..., None] * k)
    kk = jnp.einsum('dgcp,dgpk,dgpe->dgkce', c_out, pw[..., :t], b_bar, precision=HIGHEST).real
    kf, kb = kk[0], kk[1]
    zero_lag = kf[:, :1] + kb[:, :1]
    k_all = jnp.concatenate([kb[:, :0:-1], zero_lag, kf[:, 1:]], axis=1)
    jj = jnp.arange(t)
    lag_idx = jj[None, :] - jj[:, None] + (t - 1)
    m_intra = k_all[:, lag_idx]
    m_intra = m_intra.transpose(0, 1, 4, 2, 3).reshape(g, t * gs, t * gs)
    m_intra = m_intra + jnp.eye(t * gs, dtype=F32)[None] * jnp.tile(
        d_skip.astype(F32).reshape(g, 1, gs), (1, t, 1)).reshape(g, 1, t * gs)
    in_f = pw[0][..., t - 1 - jj][..., None] * b_bar[0][:, :, None, :]
    in_b = pw[1][..., jj][..., None] * b_bar[1][:, :, None, :]

    def to_in(z):
        return z.transpose(0, 2, 3, 1).reshape(g, t * gs, p)

    m_in = jnp.concatenate([to_in(in_f.real), to_in(in_f.imag), to_in(in_b.real), to_in(in_b.imag)], axis=-1)
    out_f = c_out[0][:, :, :, None] * pw[0][:, None, :, 1 + jj]
    out_b = c_out[1][:, :, :, None] * pw[1][:, None, :, t - jj]

    def to_out(z):
        return z.transpose(0, 2, 3, 1).reshape(g, p, t * gs)

    m_out = jnp.concatenate([to_out(out_f.real), -to_out(out_f.imag), to_out(out_b.real), -to_out(out_b.imag)],
                            axis=1)
    mu = pw[..., t]
    mre, mim = mu.real, mu.imag
    coef = jnp.stack([
        jnp.concatenate([mre[0], mre[0], mre[1], mre[1]], axis=-1),
        jnp.concatenate([-mim[0], mim[0], -mim[1], mim[1]], axis=-1),
        jnp.concatenate([mim[0], -mim[0], mim[1], -mim[1]], axis=-1)])
    return m_intra.astype(BF16), m_in.astype(BF16), m_out.astype(BF16), coef


def _s5_in_kernel(a_ref, m_ref, o_ref):
    o_ref[...] = _dot(a_ref[...], m_ref[...])


def _s5_chunk_states(a, m_in):
    g, nc, kdim = a.shape
    s = m_in.shape[2]
    nb = min(nc, 1024)
    return pl.pallas_call(
        _s5_in_kernel,
        grid=(g, nc // nb),
        in_specs=[pl.BlockSpec((None, nb, kdim), lambda gi, i: (gi, i, 0)),
                  pl.BlockSpec((None, kdim, s), lambda gi, i: (gi, 0, 0))],
        out_specs=pl.BlockSpec((None, nb, s), lambda gi, i: (gi, i, 0)),
        out_shape=jax.ShapeDtypeStruct((g, nc, s), F32),
        compiler_params=_params(("parallel", "arbitrary")),
        name="s5_chunk_states",
    )(a, m_in)


def _s5_scan_kernel(sf_ref, sb_ref, init_ref, coef_ref, ef_ref, eb_ref, fin_ref, st_ref, *, nblk, half):
    k = pl.program_id(1)
    nk = pl.num_programs(1)
    lanes = 2 * half

    @pl.when(k == 0)
    def _():
        s0f = init_ref[:, 0:lanes]
        s0b = init_ref[:, lanes:2 * lanes]
        st_ref[0] = s0f
        st_ref[1] = pltpu.roll(s0f, half, 1)
        st_ref[2] = s0b
        st_ref[3] = pltpu.roll(s0b, half, 1)

    def body(r, carry):
        sf, sfw, sb, sbw = carry
        rb = nblk - 1 - r
        xf = sf_ref[r]
        xb = sb_ref[rb]
        ef_ref[r] = sf
        eb_ref[rb] = sb
        xfw = pltpu.roll(xf, half, 1)
        xbw = pltpu.roll(xb, half, 1)
        af, bf, bfw = coef_ref[0, :, 0:lanes], coef_ref[1, :, 0:lanes], coef_ref[2, :, 0:lanes]
        ab, bb, bbw = (coef_ref[0, :, lanes:2 * lanes], coef_ref[1, :, lanes:2 * lanes],
                       coef_ref[2, :, lanes:2 * lanes])
        return (sf * af + sfw * bf + xf, sfw * af + sf * bfw + xfw,
                sb * ab + sbw * bb + xb, sbw * ab + sb * bbw + xbw)

    sf, sfw, sb, sbw = lax.fori_loop(0, nblk, body, (st_ref[0], st_ref[1], st_ref[2], st_ref[3]))
    st_ref[0] = sf
    st_ref[1] = sfw
    st_ref[2] = sb
    st_ref[3] = sbw

    @pl.when(k == nk - 1)
    def _():
        fin_ref[:, 0:lanes] = sf
        fin_ref[:, lanes:2 * lanes] = sb


def _s5_scan(s_t, init, coef):
    bsz, nc, g, s4 = s_t.shape
    lanes = s4 // 2
    nblk = min(nc, 128)
    nk = nc // nblk
    return pl.pallas_call(
        functools.partial(_s5_scan_kernel, nblk=nblk, half=lanes // 2),
        grid=(bsz, nk),
        in_specs=[pl.BlockSpec((None, nblk, g, lanes), lambda b, k: (b, k, 0, 0)),
                  pl.BlockSpec((None, nblk, g, lanes), lambda b, k: (b, nk - 1 - k, 0, 1)),
                  pl.BlockSpec((None, g, s4), lambda b, k: (b, 0, 0)),
                  pl.BlockSpec((3, g, s4), lambda b, k: (0, 0, 0))],
        out_specs=[pl.BlockSpec((None, nblk, g, lanes), lambda b, k: (b, k, 0, 0)),
                   pl.BlockSpec((None, nblk, g, lanes), lambda b, k: (b, nk - 1 - k, 0, 0)),
                   pl.BlockSpec((None, g, s4), lambda b, k: (b, 0, 0))],
        out_shape=[jax.ShapeDtypeStruct((bsz, nc, g, lanes), F32),
                   jax.ShapeDtypeStruct((bsz, nc, g, lanes), F32),
                   jax.ShapeDtypeStruct((bsz, g, s4), F32)],
        scratch_shapes=[pltpu.VMEM((4, g, lanes), F32)],
        compiler_params=_params(("parallel", "arbitrary")),
        name="s5_scan",
    )(s_t, s_t, init, coef)


def _s5_out_kernel(a_ref, e_ref, mi_ref, mo_ref, o_ref):
    y = _dot(a_ref[...], mi_ref[...]) + _dot(e_ref[...], mo_ref[...])
    o_ref[...] = y.astype(o_ref.dtype)


def _s5_chunk_outputs(a, e_in, m_intra, m_out):
    g, nc, kdim = a.shape
    s = e_in.shape[2]
    nb = min(nc, 1024)
    return pl.pallas_call(
        _s5_out_kernel,
        grid=(g, nc // nb),
        in_specs=[pl.BlockSpec((None, nb, kdim), lambda gi, i: (gi, i, 0)),
                  pl.BlockSpec((None, nb, s), lambda gi, i: (gi, i, 0)),
                  pl.BlockSpec((None, kdim, kdim), lambda gi, i: (gi, 0, 0)),
                  pl.BlockSpec((None, s, kdim), lambda gi, i: (gi, 0, 0))],
        out_specs=pl.BlockSpec((None, nb, kdim), lambda gi, i: (gi, i, 0)),
        out_shape=jax.ShapeDtypeStruct((g, nc, kdim), BF16),
        compiler_params=_params(("parallel", "arbitrary")),
        name="s5_chunk_outputs",
    )(a, e_in, m_intra, m_out)


def _s5_readout_kernel(y_ref, wg_ref, o_ref):
    y = jax.nn.gelu(y_ref[...].astype(F32))
    gate = _dot(y.astype(BF16), wg_ref[...])
    o_ref[...] = (y * jax.nn.sigmoid(gate)).astype(o_ref.dtype)


def _s5_readout(y2d, wg):
    n, w = y2d.shape
    tm = min(n, 1024)
    return pl.pallas_call(
        _s5_readout_kernel,
        grid=(n // tm,),
        in_specs=[pl.BlockSpec((tm, w), lambda i: (i, 0)),
                  pl.BlockSpec((w, w), lambda i: (0, 0))],
        out_specs=pl.BlockSpec((tm, w), lambda i: (i, 0)),
        out_shape=jax.ShapeDtypeStruct((n, w), BF16),
        compiler_params=_params(("parallel",)),
        name="s5_readout",
    )(y2d, wg)


def _to_chunks(u2d, g, t):
    n = u2d.shape[0]
    return u2d.reshape(n // t, t, g, S5_GROUP).transpose(2, 0, 1, 3).reshape(g, n // t, t * S5_GROUP)


def _from_chunks(y, g, t):
    nc = y.shape[1]
    return y.reshape(g, nc, t, S5_GROUP).transpose(1, 2, 0, 3).reshape(nc * t, g * S5_GROUP)


def _s5_branch(ua, uca, bsz, lam_re, lam_im, log_dt, b_re, b_im, c_re, c_im, d_skip, w_glu):
    g = lam_re.shape[1]
    t = S5_CHUNK
    m_intra, m_in, m_out, coef = _s5_matrices(lam_re, lam_im, log_dt, b_re, b_im, c_re, c_im, d_skip, t)
    s4 = m_in.shape[2]

    def states(u2d):
        a = _to_chunks(u2d, g, t)
        s = _s5_chunk_states(a, m_in)
        nc = s.shape[1] // bsz
        return a, s.reshape(g, bsz, nc, s4).transpose(1, 2, 0, 3)

    _, s_ctx = states(uca)
    _, _, seed = _s5_scan(s_ctx, jnp.zeros((bsz, g, s4), F32), coef)
    a_lat, s_lat = states(ua)
    e_f, e_b, _ = _s5_scan(s_lat, seed, coef)
    e_in = jnp.concatenate([e_f, e_b], axis=-1).astype(BF16)
    e_in = e_in.transpose(2, 0, 1, 3).reshape(g, -1, s4)
    y = _s5_chunk_outputs(a_lat, e_in, m_intra, m_out)
    return _s5_readout(_from_chunks(y, g, t), w_glu.astype(BF16))


HY_KH = GRID_W + 1
HY_KP = 72
HY_QN = 16


def _hy_phase(w_cols, r_cnt):
    n1_tot = 2 * GRID_W
    n_fft = n1_tot * r_cnt
    kh = GRID_W + 1
    k1 = np.arange(kh, dtype=np.int64)[None, :, None]
    r = np.arange(r_cnt, dtype=np.int64)[:, None, None]
    n1 = np.arange(w_cols, dtype=np.int64)[None, None, :]
    return 2.0 * np.pi * ((k1 * (r_cnt * n1 + r)) % n_fft) / n_fft


def _hy_fwd_table(r_cnt, m):
    th = _hy_phase(GRID_W * m, r_cnt)
    kh = th.shape[1]
    f = np.zeros((r_cnt, 2 * HY_KP, GRID_W * m), np.float32)
    f[:, :kh] = np.cos(th)
    f[:, HY_KP:HY_KP + kh] = -np.sin(th)
    return jnp.asarray(f, BF16)


def _hy_inv_table(r_cnt):
    th = _hy_phase(GRID_W, r_cnt)
    kh = th.shape[1]
    c = np.full((kh,), 2.0)
    c[0] = c[-1] = 1.0
    scale = c[None, :, None] / (2 * GRID_W * r_cnt)
    g = np.zeros((r_cnt, GRID_W, 2 * HY_KP), np.float32)
    g[:, :, :kh] = (scale * np.cos(th)).transpose(0, 2, 1)
    g[:, :, HY_KP:HY_KP + kh] = (-scale * np.sin(th)).transpose(0, 2, 1)
    return jnp.asarray(g, BF16)


def _hy_dft_tables(r_cnt):
    k = np.arange(r_cnt, dtype=np.int64)
    th = 2.0 * np.pi * ((k[:, None] * k[None, :]) % r_cnt) / r_cnt
    fc, fs = np.cos(th), np.sin(th)
    fwd = np.block([[fc, fs], [-fs, fc]]).astype(np.float32)
    inv = np.block([[fc, -fs], [fs, fc]]).astype(np.float32)
    return jnp.asarray(fwd, BF16), jnp.asarray(inv, BF16)


def _hy_short_kernel(x_ref, w_ref, o_ref, *, rows, chunk):
    wk = w_ref[...]
    w0, w1, w2 = wk[0:1], wk[1:2], wk[2:3]
    gw = GRID_W
    col = lax.broadcasted_iota(jnp.int32, (gw, x_ref.shape[1]), 0)

    def piece(a, n):
        return x_ref[pl.ds(a, n), :].astype(F32)

    last = piece(rows - gw, gw)
    prev0 = jnp.where(col == 0, 0.0, pltpu.roll(last, 1, 0))
    o_ref[0:gw, :] = (w0 * prev0 + w1 * piece(0, gw) + w2 * piece(gw, gw)).astype(o_ref.dtype)
    first = piece(0, gw)
    next_l = jnp.where(col == gw - 1, 0.0, pltpu.roll(first, gw - 1, 0))
    o_ref[rows - gw:rows, :] = (w0 * piece(rows - 2 * gw, gw) + w1 * last + w2 * next_l).astype(o_ref.dtype)

    def body(i, carry):
        a = pl.multiple_of(gw + i * chunk, gw)
        o_ref[pl.ds(a, chunk), :] = (w0 * piece(a - gw, chunk) + w1 * piece(a, chunk)
                                     + w2 * piece(a + gw, chunk)).astype(o_ref.dtype)
        return carry

    n_full = (rows - 2 * gw) // chunk
    lax.fori_loop(0, n_full, body, 0)
    rem = rows - 2 * gw - n_full * chunk
    if rem:
        a = gw + n_full * chunk
        o_ref[a:a + rem, :] = (w0 * piece(a - gw, rem) + w1 * piece(a, rem)
                               + w2 * piece(a + gw, rem)).astype(o_ref.dtype)


def _hy_short_conv(uh, short_w, bsz, seq):
    n, ch = uh.shape
    ct = min(ch, LANE)
    chunk = min(512, seq - 2 * GRID_W)
    return pl.pallas_call(
        functools.partial(_hy_short_kernel, rows=seq, chunk=chunk),
        grid=(bsz, ch // ct),
        in_specs=[pl.BlockSpec((seq, ct), lambda b, c: (b, c)),
                  pl.BlockSpec((3, ct), lambda b, c: (0, c))],
        out_specs=pl.BlockSpec((seq, ct), lambda b, c: (b, c)),
        out_shape=jax.ShapeDtypeStruct((n, ch), BF16),
        compiler_params=_params(("parallel", "parallel")),
        name="hy_short_conv",
    )(uh, short_w)


def _hy_fwd1_kernel(*refs, m):
    x_refs, f_ref, o_ref, s_ref = refs[:m], refs[m], refs[m + 1], refs[m + 2]
    gw = GRID_W
    for q in range(HY_QN):
        xq = [x[q * gw:(q + 1) * gw, :] for x in x_refs]
        xq = xq[0] if m == 1 else jnp.concatenate(xq, axis=0)
        s_ref[q] = _dot(f_ref[q], xq)
    t = jnp.swapaxes(s_ref[...], 0, 1).astype(o_ref.dtype)
    o_ref[0] = t[0:HY_KH]
    o_ref[1] = t[HY_KP:HY_KP + HY_KH]


def _hy_fwd1(xs, col_blk0, width, bsz, r_cnt, ct):
    m = len(xs)
    table = _hy_fwd_table(r_cnt, m)
    rb = HY_QN * GRID_W
    nrb = r_cnt // HY_QN
    x_spec = pl.BlockSpec((rb, ct), lambda b, i, c: (b * nrb + i, col_blk0 + c))
    return pl.pallas_call(
        functools.partial(_hy_fwd1_kernel, m=m),
        grid=(bsz, nrb, width // ct),
        in_specs=[x_spec] * m + [pl.BlockSpec((HY_QN, 2 * HY_KP, GRID_W * m), lambda b, i, c: (i, 0, 0))],
        out_specs=pl.BlockSpec((None, 2, HY_KH, HY_QN, ct), lambda b, i, c: (b, 0, 0, i, c)),
        out_shape=jax.ShapeDtypeStruct((bsz, 2, HY_KH, r_cnt, width), BF16),
        scratch_shapes=[pltpu.VMEM((HY_QN, 2 * HY_KP, ct), F32)],
        compiler_params=_params(("parallel", "parallel", "arbitrary")),
        name="hy_fwd_stage1",
    )(*xs, table)


def _hy_spec_kernel(a_ref, f2_ref, o_ref):
    r = a_ref.shape[1]
    x = a_ref[...].reshape(2 * r, a_ref.shape[2])
    o_ref[...] = _dot(f2_ref[...], x).reshape(o_ref.shape)


def _hy_filter_spectrum(a, f2, ct):
    _, _, kh, r_cnt, width = a.shape
    return pl.pallas_call(
        _hy_spec_kernel,
        grid=(kh, width // ct),
        in_specs=[pl.BlockSpec((None, 2, None, r_cnt, ct), lambda k, c: (0, 0, k, 0, c)),
                  pl.BlockSpec((2 * r_cnt, 2 * r_cnt), lambda k, c: (0, 0))],
        out_specs=pl.BlockSpec((2, None, r_cnt, ct), lambda k, c: (0, k, 0, c)),
        out_shape=jax.ShapeDtypeStruct((2, kh, r_cnt, width), F32),
        compiler_params=_params(("parallel", "parallel")),
        name="hy_filter_spectrum",
    )(a, f2)


def _hy_mid_kernel(a_ref, kf_ref, invn_ref, f2_ref, f2i_ref, o_ref):
    r = a_ref.shape[1]
    x = a_ref[...].reshape(2 * r, a_ref.shape[2])
    b = _dot(f2_ref[...], x)
    br, bi = b[:r], b[r:]
    kr, ki = kf_ref[0], kf_ref[1]
    s = invn_ref[...]
    cr = (br * kr - bi * ki) * s
    ci = (br * ki + bi * kr) * s
    c = jnp.concatenate([cr, ci], axis=0).astype(BF16)
    o_ref[...] = _dot(f2i_ref[...], c).reshape(o_ref.shape).astype(o_ref.dtype)


def _hy_mid(a, kf, invn, order, f2, f2i, ct):
    bsz, _, kh, r_cnt, width = a.shape
    nct = width // ct
    return pl.pallas_call(
        _hy_mid_kernel,
        grid=(kh, nct, bsz),
        in_specs=[pl.BlockSpec((None, 2, None, r_cnt, ct), lambda k, c, b: (b, 0, k, 0, c)),
                  pl.BlockSpec((2, None, r_cnt, ct), lambda k, c, b: (0, k, 0, order * nct + c)),
                  pl.BlockSpec((1, ct), lambda k, c, b: (0, order * nct + c)),
                  pl.BlockSpec((2 * r_cnt, 2 * r_cnt), lambda k, c, b: (0, 0)),
                  pl.BlockSpec((2 * r_cnt, 2 * r_cnt), lambda k, c, b: (0, 0))],
        out_specs=pl.BlockSpec((None, 2, None, r_cnt, ct), lambda k, c, b: (b, 0, k, 0, c)),
        out_shape=jax.ShapeDtypeStruct(a.shape, BF16),
        compiler_params=_params(("parallel", "parallel", "arbitrary")),
        name="hy_spectrum_product",
    )(a, kf, invn, f2, f2i)


def _hy_inv1_kernel(d_ref, g_ref, xin_ref, xm_ref, bias_ref, o_ref, t_ref, s_ref):
    gw = GRID_W
    pad = jnp.zeros((HY_KP - HY_KH,) + t_ref.shape[1:], F32)
    t_ref[0:HY_KH] = d_ref[0].astype(F32)
    t_ref[HY_KH:HY_KP] = pad
    t_ref[HY_KP:HY_KP + HY_KH] = d_ref[1].astype(F32)
    t_ref[HY_KP + HY_KH:2 * HY_KP] = pad
    s_ref[...] = jnp.swapaxes(t_ref[...], 0, 1)
    for q in range(HY_QN):
        y = _dot(g_ref[q], s_ref[q].astype(BF16))
        rows = slice(q * gw, (q + 1) * gw)
        xin = xin_ref[rows, :].astype(F32)
        o_ref[rows, :] = (xm_ref[rows, :].astype(F32) * (y + xin * bias_ref[...])).astype(o_ref.dtype)


def _hy_inv1(d, xin, xin_blk0, xm, xm_blk0, bias, ct):
    bsz, _, kh, r_cnt, width = d.shape
    table = _hy_inv_table(r_cnt)
    rb = HY_QN * GRID_W
    nrb = r_cnt // HY_QN
    return pl.pallas_call(
        _hy_inv1_kernel,
        grid=(bsz, nrb, width // ct),
        in_specs=[pl.BlockSpec((None, 2, kh, HY_QN, ct), lambda b, i, c: (b, 0, 0, i, c)),
                  pl.BlockSpec((HY_QN, GRID_W, 2 * HY_KP), lambda b, i, c: (i, 0, 0)),
                  pl.BlockSpec((rb, ct), lambda b, i, c: (b * nrb + i, xin_blk0 + c)),
                  pl.BlockSpec((rb, ct), lambda b, i, c: (b * nrb + i, xm_blk0 + c)),
                  pl.BlockSpec((1, ct), lambda b, i, c: (0, c))],
        out_specs=pl.BlockSpec((rb, ct), lambda b, i, c: (b * nrb + i, c)),
        out_shape=jax.ShapeDtypeStruct((bsz * r_cnt * GRID_W, width), BF16),
        scratch_shapes=[pltpu.VMEM((2 * HY_KP, HY_QN, ct), F32), pltpu.VMEM((HY_QN, 2 * HY_KP, ct), F32)],
        compiler_params=_params(("parallel", "parallel", "arbitrary")),
        name="hy_inv_stage1",
    )(d, table, xin, xm, bias)


def _hy_filter_kernel(w1_ref, b1_ref, f1_ref, w2_ref, b2_ref, f2_ref, w3_ref, b3_ref, dec_ref, om_ref, ph_ref,
                      o_ref, sum_ref, *, direction, seqlen, r_cnt, rows):
    i = pl.program_id(1)
    row = lax.broadcasted_iota(jnp.int32, (rows, LANE), 0) + i * rows
    lane = lax.broadcasted_iota(jnp.int32, (rows, LANE), 1)
    r = jnp.right_shift(row, GRID_W.bit_length() - 1)
    w = jnp.bitwise_and(row, GRID_W - 1)
    n = (w + GRID_W * direction) * r_cnt + r
    p = n if direction == 0 else 2 * seqlen - n
    pf = p.astype(F32)
    t = pf / (seqlen - 1)
    feats = jnp.where(lane == 0, t, jnp.cos(pf * om_ref[...] + ph_ref[...]))
    hid = jnp.sin(f1_ref[...] * (jnp.dot(feats, w1_ref[...], preferred_element_type=F32, precision=HIGHEST)
                                 + b1_ref[...]))
    hid = jnp.sin(f2_ref[...] * (jnp.dot(hid, w2_ref[...], preferred_element_type=F32, precision=HIGHEST)
                                 + b2_ref[...]))
    filt = _dot(hid.astype(BF16), w3_ref[...].astype(BF16)) + b3_ref[...]
    reps = filt.shape[1] // LANE
    t_w = jnp.concatenate([t] * reps, axis=1)
    keep = jnp.concatenate([jnp.where(p < seqlen, 1.0, 0.0)] * reps, axis=1)
    filt = filt * jnp.exp(-t_w * jnp.abs(dec_ref[...])) * keep
    o_ref[...] = filt.astype(o_ref.dtype)

    @pl.when(i == 0)
    def _():
        sum_ref[...] = jnp.zeros_like(sum_ref)

    sum_ref[...] += jnp.sum(jnp.abs(filt), axis=0, keepdims=True)


def _hy_filter_signal(direction, seqlen, w1p, b1, f1, w2, b2, f2, w3, b3, decay, omega, phase, width):
    r_cnt = seqlen // GRID_W
    rows = min(256, seqlen)
    ct = width
    nct = width // ct
    hid = w2.shape[0]
    off = direction * nct
    const = lambda shape: pl.BlockSpec(shape, lambda c, i: (0, 0))
    colv = pl.BlockSpec((1, ct), lambda c, i: (0, off + c))
    return pl.pallas_call(
        functools.partial(_hy_filter_kernel, direction=direction, seqlen=seqlen, r_cnt=r_cnt, rows=rows),
        grid=(nct, seqlen // rows),
        in_specs=[const((LANE, hid)), const((1, hid)), const((1, hid)), const((hid, hid)), const((1, hid)),
                  const((1, hid)), pl.BlockSpec((hid, ct), lambda c, i: (0, off + c)), colv, colv,
                  const((1, LANE)), const((1, LANE))],
        out_specs=[pl.BlockSpec((rows, ct), lambda c, i: (i, c)),
                   pl.BlockSpec((1, ct), lambda c, i: (0, c))],
        out_shape=[jax.ShapeDtypeStruct((seqlen, width), BF16), jax.ShapeDtypeStruct((1, width), F32)],
        compiler_params=_params(("parallel", "arbitrary")),
        name="hy_filter_signal",
    )(w1p, b1, f1, w2, b2, f2, w3, b3, decay, omega, phase)


def _hyena_branch(uh, bsz, seq, short_w, w1, b1, f1, w2, b2, f2, w3, b3, decay, hy_bias):
    width = hy_bias.shape[1]
    r_cnt = seq // GRID_W
    ct = min(width, 256)
    nct = width // ct
    n_emb = w1.shape[0]
    n_bands = (n_emb - 1) // 2
    hid = w1.shape[1]
    bands = jnp.linspace(1e-4, n_bands - 1, n_bands, dtype=F32)
    omega = jnp.zeros((1, LANE), F32).at[0, 1:1 + n_bands].set((2 * jnp.pi / seq) * bands)
    omega = omega.at[0, 1 + n_bands:1 + 2 * n_bands].set((2 * jnp.pi / seq) * bands)
    phase = jnp.zeros((1, LANE), F32).at[0, 1 + n_bands:1 + 2 * n_bands].set(0.5 * jnp.pi)
    w1p = jnp.zeros((LANE, hid), F32).at[:n_emb].set(w1)
    sigs, sums = [], []
    for direction in range(2):
        sig, ssum = _hy_filter_signal(direction, seq, w1p, b1[None], f1[None], w2, b2[None], f2[None], w3,
                                      b3[None], decay[None], omega, phase, HY_ORDER * width)
        sigs.append(sig)
        sums.append(ssum)
    invn = 1.0 / (sums[0] + sums[1] + HY_NORM_EPS)
    f2m, f2i = _hy_dft_tables(r_cnt)
    ct3 = min(width, 1024)
    kf = _hy_filter_spectrum(_hy_fwd1(sigs, 0, HY_ORDER * width, 1, r_cnt, ct), f2m, ct3)

    z = _hy_short_conv(uh, short_w.astype(F32), bsz, seq)
    a = _hy_fwd1([z], 0, width, bsz, r_cnt, ct)
    d = _hy_mid(a, kf, invn, 0, f2m, f2i, ct3)
    y1 = _hy_inv1(d, z, 0, z, nct, hy_bias[0:1].astype(F32), ct)
    a = _hy_fwd1([y1], 0, width, bsz, r_cnt, ct)
    d = _hy_mid(a, kf, invn, 1, f2m, f2i, ct3)
    return _hy_inv1(d, y1, 0, z, 2 * nct, hy_bias[1:2].astype(F32), ct)


def _merge1_kernel(ya_ref, yh_ref, ug_ref, wa_ref, wh_ref, o_ref):
    d = o_ref.shape[1]
    a = _dot(ya_ref[...], wa_ref[...])
    h = _dot(yh_ref[...], wh_ref[...])
    ga = ug_ref[:, :d].astype(F32)
    gh = ug_ref[:, d:].astype(F32)
    o_ref[...] = (ga * a + gh * h).astype(o_ref.dtype)


def _merge1(ya, yh, ug, wa, wh):
    n, ws = ya.shape
    wh_in = yh.shape[1]
    d = wa.shape[1]
    tm = min(n, 512)
    return pl.pallas_call(
        _merge1_kernel,
        grid=(n // tm,),
        in_specs=[pl.BlockSpec((tm, ws), lambda i: (i, 0)),
                  pl.BlockSpec((tm, wh_in), lambda i: (i, 0)),
                  pl.BlockSpec((tm, 2 * d), lambda i: (i, 0)),
                  pl.BlockSpec((ws, d), lambda i: (0, 0)),
                  pl.BlockSpec((wh_in, d), lambda i: (0, 0))],
        out_specs=pl.BlockSpec((tm, d), lambda i: (i, 0)),
        out_shape=jax.ShapeDtypeStruct((n, d), BF16),
        compiler_params=_params(("parallel",)),
        name="merge_branches",
    )(ya, yh, ug, wa, wh)


def _merge2_kernel(m_ref, x_ref, g1_ref, ng_ref, sh_ref, sc_ref, wo_ref, wr_ref,
                   h1_ref, hm_ref, r1_ref, r2_ref):
    mix = _dot(m_ref[...], wo_ref[...])
    h1 = x_ref[...] + g1_ref[...] * mix
    h1_ref[...] = h1
    y = h1 * lax.rsqrt(jnp.mean(h1 * h1, axis=-1, keepdims=True) + NORM_EPS)
    y = y * ng_ref[...]
    y = y * (1.0 + sc_ref[...]) + sh_ref[...]
    hi = y.astype(BF16)
    lo = (y - hi.astype(F32)).astype(BF16)
    hm_ref[...] = y
    r1_ref[...] = _dot(hi, wr_ref[...])
    r2_ref[...] = _dot(lo, wr_ref[...])


def _merge2(m, x2d, g1, ng, sh2, sc2, wo, wr, rows_per_mod):
    n, d = x2d.shape
    tm = min(n, 512)
    tiles_per_mod = rows_per_mod // tm
    nr = wr.shape[1]
    mod_spec = pl.BlockSpec((None, 1, d), lambda i: (i // tiles_per_mod, 0, 0))
    return pl.pallas_call(
        _merge2_kernel,
        grid=(n // tm,),
        in_specs=[pl.BlockSpec((tm, d), lambda i: (i, 0)),
                  pl.BlockSpec((tm, d), lambda i: (i, 0)),
                  mod_spec,
                  pl.BlockSpec((1, d), lambda i: (0, 0)),
                  mod_spec, mod_spec,
                  pl.BlockSpec((d, d), lambda i: (0, 0)),
                  pl.BlockSpec((d, nr), lambda i: (0, 0))],
        out_specs=[pl.BlockSpec((tm, d), lambda i: (i, 0)),
                   pl.BlockSpec((tm, d), lambda i: (i, 0)),
                   pl.BlockSpec((tm, nr), lambda i: (i, 0)),
                   pl.BlockSpec((tm, nr), lambda i: (i, 0))],
        out_shape=[jax.ShapeDtypeStruct((n, d), F32),
                   jax.ShapeDtypeStruct((n, d), F32),
                   jax.ShapeDtypeStruct((n, nr), F32),
                   jax.ShapeDtypeStruct((n, nr), F32)],
        compiler_params=_params(("parallel",)),
        name="out_proj_norm_router",
    )(m, x2d, g1, ng, sh2, sc2, wo, wr)


def _moe_kernel(be_ref, nv_ref, tok0_ref, tokn_ref, hm_ref, wgu_ref, bgu_ref, wd_ref, bd_ref, o_ref,
                xbuf_ref, xb_ref, acc_ref, sem):
    i = pl.program_id(0)
    j = pl.program_id(1)
    nj = pl.num_programs(1)
    nv = nv_ref[0]
    tm = xb_ref.shape[0]
    per_step = tm // nj

    def row_copy(tok_ref, r, slot):
        return pltpu.make_async_copy(hm_ref.at[pl.ds(tok_ref[0, r], 1), :],
                                     xbuf_ref.at[slot, pl.ds(r, 1), :], sem.at[slot])

    def wait_block(slot):
        pltpu.make_async_copy(hm_ref.at[pl.ds(0, tm), :], xbuf_ref.at[slot], sem.at[slot]).wait()

    @pl.when((i == 0) & (j == 0) & (nv > 0))
    def _():
        for r in range(tm):
            row_copy(tok0_ref, r, 0).start()

    @pl.when((i < nv) & (j == 0))
    def _():
        slot = i % 2
        wait_block(slot)
        xb_ref[...] = xbuf_ref[slot].astype(BF16)
        acc_ref[...] = jnp.zeros_like(acc_ref)

    @pl.when(i < nv)
    def _():
        nslot = (i + 1) % 2
        for r in range(per_step):
            row_copy(tokn_ref, j * per_step + r, nslot).start()
        gu = _dot(xb_ref[...], wgu_ref[...]) + bgu_ref[...]
        acts = []
        for b in range(gu.shape[1] // (2 * LANE)):
            glu = jnp.minimum(gu[:, 2 * b * LANE:(2 * b + 1) * LANE], SWIGLU_LIMIT)
            lin = jnp.clip(gu[:, (2 * b + 1) * LANE:(2 * b + 2) * LANE], -SWIGLU_LIMIT, SWIGLU_LIMIT)
            acts.append((glu * jax.nn.sigmoid(SWIGLU_ALPHA * glu) * (lin + 1.0)).astype(BF16))
        acc_ref[...] += _dot(jnp.concatenate(acts, axis=1), wd_ref[...])

        @pl.when(j == nj - 1)
        def _():
            o_ref[...] = (acc_ref[...] + bd_ref[...]).astype(o_ref.dtype)

    @pl.when((i == nv - 1) & (j == nj - 1))
    def _():
        wait_block((i + 1) % 2)

    @pl.when((i >= nv) & (j == nj - 1))
    def _():
        o_ref[...] = jnp.zeros_like(o_ref)


def _moe_blocks(hm, slot_tok, block_expert, n_valid, wgu, bgu, wd, bd):
    _, d = hm.shape
    f = wd.shape[1]
    tm = MOE_TM
    tf = min(f, MOE_TF)
    nj = f // tf
    nblk = block_expert.shape[0]

    def jeff(i, j, nv):
        return jnp.where(i < nv[0], j, nj - 1)

    grid_spec = pltpu.PrefetchScalarGridSpec(
        num_scalar_prefetch=2,
        grid=(nblk, nj),
        in_specs=[pl.BlockSpec((None, 1, tm), lambda i, j, be, nv: (0, 0, 0), memory_space=pltpu.SMEM),
                  pl.BlockSpec((None, 1, tm), lambda i, j, be, nv: (jnp.minimum(i + 1, nblk - 1), 0, 0),
                               memory_space=pltpu.SMEM),
                  pl.BlockSpec(memory_space=pl.ANY),
                  pl.BlockSpec((None, d, 2 * tf), lambda i, j, be, nv: (be[i], 0, jeff(i, j, nv))),
                  pl.BlockSpec((None, 1, 2 * tf), lambda i, j, be, nv: (be[i], 0, jeff(i, j, nv))),
                  pl.BlockSpec((None, tf, d), lambda i, j, be, nv: (be[i], jeff(i, j, nv), 0)),
                  pl.BlockSpec((None, 1, d), lambda i, j, be, nv: (be[i], 0, 0))],
        out_specs=pl.BlockSpec((tm, d), lambda i, j, be, nv: (i, 0)),
        scratch_shapes=[pltpu.VMEM((2, tm, d), F32), pltpu.VMEM((tm, d), BF16), pltpu.VMEM((tm, d), F32),
                        pltpu.SemaphoreType.DMA((2,))])
    return pl.pallas_call(
        _moe_kernel,
        grid_spec=grid_spec,
        out_shape=jax.ShapeDtypeStruct((nblk * tm, d), BF16),
        compiler_params=_params(("arbitrary", "arbitrary")),
        name="moe_experts",
    )(block_expert, n_valid, slot_tok, slot_tok, hm, wgu, bgu, wd, bd)


def _route(logits, n_experts):
    n_tok = logits.shape[0]
    top_val, top_idx = lax.top_k(logits, TOP_K)
    gates = jax.nn.softmax(top_val, axis=-1)
    n_assign = n_tok * TOP_K
    flat_e = top_idx.reshape(-1).astype(jnp.int32)
    experts = jnp.arange(n_experts, dtype=jnp.int32)
    onehot = (flat_e[:, None] == experts[None, :]).astype(jnp.int32)
    csum = jnp.cumsum(onehot, axis=0)
    rank = jnp.sum(onehot * csum, axis=1) - 1
    counts = csum[-1]
    padded = (counts + MOE_TM - 1) // MOE_TM * MOE_TM
    start = jnp.cumsum(counts) - counts
    padded_end = jnp.cumsum(padded)
    padded_start = padded_end - padded
    dest = padded_start[flat_e] + rank
    n_blocks = -(-n_assign // MOE_TM) + n_experts
    cap = n_blocks * MOE_TM
    n_valid = (padded_end[-1] // MOE_TM).astype(jnp.int32)
    order = jnp.argsort(flat_e)
    slots = jnp.arange(cap, dtype=jnp.int32)
    slot_e = jnp.minimum(jnp.sum((padded_end[None, :] <= slots[:, None]).astype(jnp.int32), axis=1), n_experts - 1)
    slot_rank = slots - padded_start[slot_e]
    src = jnp.minimum(start[slot_e] + slot_rank, n_assign - 1)
    slot_tok = jnp.where(slot_rank < counts[slot_e], order[src] // TOP_K, 0).astype(jnp.int32)
    block_start = jnp.minimum(jnp.arange(n_blocks, dtype=jnp.int32), n_valid - 1) * MOE_TM
    block_expert = jnp.minimum(
        jnp.sum((padded_end[None, :] <= block_start[:, None]).astype(jnp.int32), axis=1), n_experts - 1)
    return gates, slot_tok, dest, block_expert, n_valid


def _final_kernel(h1_ref, *rest):
    yg_refs = rest[:TOP_K]
    gates_ref, g2_ref, fg_ref, o_ref = rest[TOP_K:]
    gates = gates_ref[...]
    moe = yg_refs[0][...].astype(F32) * gates[:, 0:1]
    for k in range(1, TOP_K):
        moe = moe + yg_refs[k][...].astype(F32) * gates[:, k:k + 1]
    h = h1_ref[...] + g2_ref[...] * moe
    y = h * lax.rsqrt(jnp.mean(h * h, axis=-1, keepdims=True) + NORM_EPS)
    o_ref[...] = (y * fg_ref[...]).astype(o_ref.dtype)


def _final(h1, yg, gates, g2, fg, rows_per_mod):
    n, d = h1.shape
    tm = min(n, 512)
    tiles_per_mod = rows_per_mod // tm
    nt = n // tm
    yg_specs = [pl.BlockSpec((tm, d), functools.partial(lambda i, k: (k * nt + i, 0), k=k)) for k in range(TOP_K)]
    return pl.pallas_call(
        _final_kernel,
        grid=(nt,),
        in_specs=[pl.BlockSpec((tm, d), lambda i: (i, 0))] + yg_specs + [
                  pl.BlockSpec((tm, TOP_K), lambda i: (i, 0)),
                  pl.BlockSpec((None, 1, d), lambda i: (i // tiles_per_mod, 0, 0)),
                  pl.BlockSpec((1, d), lambda i: (0, 0))],
        out_specs=pl.BlockSpec((tm, d), lambda i: (i, 0)),
        out_shape=jax.ShapeDtypeStruct((n, d), F32),
        compiler_params=_params(("parallel",)),
        name="combine_final_norm",
    )(h1, *([yg] * TOP_K), gates, g2, fg)


def kernel(x, c, ctx, c_ctx, w_ada, b_ada, norm1_g, norm2_g, w_in, s5_lam_re, s5_lam_im, s5_log_dt, s5_b_re, s5_b_im, s5_c_re, s5_c_im, s5_d, s5_w_glu, hy_short_w, hy_pos_w1, hy_pos_b1, hy_freq1, hy_pos_w2, hy_pos_b2, hy_freq2, hy_pos_w3, hy_pos_b3, hy_decay, hy_bias, w_branch_s5, w_branch_hy, w_out, router_w, router_b, w_gate_up, b_gate_up, w_down, b_down, final_g):
    bsz, seq, d = x.shape
    ctx_len = ctx.shape[1]
    depth = w_ada.shape[0]
    assert depth == 1, "only the single-layer configuration is implemented"
    n_experts = router_w.shape[2]
    s5_w = s5_d.shape[1]
    hy_w = hy_bias.shape[2]
    n_tok = bsz * seq

    cc = jnp.zeros((8, d), F32).at[:bsz].set(c).at[bsz].set(c_ctx)
    mod = _ada_mod(cc, w_ada[0], b_ada[0][None]).reshape(8, 6, 1, d)
    sh1, sc1, g1, sh2, sc2, g2 = [mod[:bsz, k] for k in range(6)]
    csh1, csc1 = mod[bsz:bsz + 1, 0], mod[bsz:bsz + 1, 1]

    w_in_b = w_in[0].astype(BF16)
    x2d = x.reshape(n_tok, d)
    riders = ((w_gate_up[0].reshape(-1, w_gate_up.shape[3]), True), (w_down[0].reshape(-1, d), False))
    ua, uh, ug, wgu_b, wd_b = _norm_mod_proj(x2d, norm1_g[0][None], sh1, sc1, w_in_b,
                                             (s5_w, 3 * hy_w, 2 * d), (False, False, True), seq, 512, 1024,
                                             riders=riders)
    (uca,) = _norm_mod_proj(ctx.reshape(bsz * ctx_len, d), norm1_g[0][None], csh1, csc1,
                            w_in_b[:, :s5_w], (s5_w,), (False,), bsz * ctx_len, 512, 1024)

    ya = _s5_branch(ua, uca, bsz, s5_lam_re[0], s5_lam_im[0], s5_log_dt[0], s5_b_re[0], s5_b_im[0],
                    s5_c_re[0], s5_c_im[0], s5_d[0], s5_w_glu[0])

    yh = _hyena_branch(uh, bsz, seq, hy_short_w[0], hy_pos_w1[0], hy_pos_b1[0], hy_freq1[0], hy_pos_w2[0],
                       hy_pos_b2[0], hy_freq2[0], hy_pos_w3[0], hy_pos_b3[0], hy_decay[0], hy_bias[0])

    m = _merge1(ya, yh, ug, w_branch_s5[0].astype(BF16), w_branch_hy[0].astype(BF16))
    rw = router_w[0]
    rw_hi = rw.astype(BF16)
    rw_lo = (rw - rw_hi.astype(F32)).astype(BF16)
    wr = jnp.zeros((d, 128), BF16).at[:, :n_experts].set(rw_hi).at[:, n_experts:2 * n_experts].set(rw_lo)
    h1, hm, r1, r2 = _merge2(m, x2d, g1, norm2_g[0][None], sh2, sc2, w_out[0].astype(BF16), wr, seq)
    logits = (r1[:, :n_experts] + r1[:, n_experts:2 * n_experts] + r2[:, :n_experts]) + router_b[0]

    gates, slot_tok, dest, block_expert, n_valid = _route(logits, n_experts)
    f = w_down.shape[2]
    bgu = b_gate_up[0].reshape(n_experts, f // LANE, LANE, 2).swapaxes(2, 3).reshape(n_experts, 1, 2 * f)
    yslots = _moe_blocks(hm, slot_tok.reshape(-1, 1, MOE_TM), block_expert, n_valid.reshape(1),
                         wgu_b.reshape(w_gate_up.shape[1:]), bgu, wd_b.reshape(w_down.shape[1:]),
                         b_down[0][:, None, :])
    yg = yslots[dest.reshape(n_tok, TOP_K).T.reshape(-1)]
    out = _final(h1, yg, gates, g2, final_g[None], seq)
    return out.reshape(bsz, seq, d)
```

```python
import functools
import math

import jax
import jax.numpy as jnp
import numpy as np
from jax import lax
from jax.experimental import pallas as pl
from jax.experimental.pallas import tpu as pltpu

F32 = jnp.float32
BF16 = jnp.bfloat16
HIGHEST = lax.Precision.HIGHEST

LANE = 128
GRID_W = 64
NORM_EPS = 1e-6
S5_GROUP = 16
S5_CHUNK = 32
HY_ORDER = 2
HY_NORM_EPS = 1e-6
TOP_K = 4
SWIGLU_LIMIT = 7.0
SWIGLU_ALPHA = 1.702
MOE_TM = 512
MOE_TF = 1024
MOE_PITCH = 20
VMEM_LIMIT = 56 * 1024 * 1024


def _params(sem, vmem=VMEM_LIMIT):
    return pltpu.CompilerParams(dimension_semantics=sem, vmem_limit_bytes=vmem)


def _dot(a, b):
    return jnp.dot(a, b, preferred_element_type=F32)


def _ada_kernel(c_ref, w_ref, b_ref, o_ref):
    c = c_ref[...]
    a = c * jax.nn.sigmoid(c)
    o_ref[...] = jnp.dot(a, w_ref[...], preferred_element_type=F32, precision=HIGHEST) + b_ref[...]


def _ada_mod(cc, w, b):
    d, n6 = w.shape
    tn = min(512, n6)
    return pl.pallas_call(
        _ada_kernel,
        grid=(n6 // tn,),
        in_specs=[pl.BlockSpec((8, d), lambda j: (0, 0)),
                  pl.BlockSpec((d, tn), lambda j: (0, j)),
                  pl.BlockSpec((1, tn), lambda j: (0, j))],
        out_specs=pl.BlockSpec((8, tn), lambda j: (0, j)),
        out_shape=jax.ShapeDtypeStruct((8, n6), F32),
        compiler_params=_params(("parallel",)),
        name="ada_mod",
    )(cc, w, b)


def _inproj_kernel(x_ref, g_ref, sh_ref, sc_ref, w_ref, *rest, bounds, acts, riders):
    nr = len(riders)
    n_in = nr + (1 if any(riders) else 0)
    rider_in, rest = rest[:n_in], rest[n_in:]
    outs, rider_out, xn_ref = rest[:len(bounds)], rest[len(bounds):len(bounds) + nr], rest[-1]
    j = pl.program_id(1)

    @pl.when(j == 0)
    def _():
        x = x_ref[...].astype(F32)
        y = x * lax.rsqrt(jnp.mean(x * x, axis=-1, keepdims=True) + NORM_EPS)
        y = y * g_ref[...]
        y = y * (1.0 + sc_ref[...]) + sh_ref[...]
        xn_ref[...] = y.astype(BF16)

    for (j0, j1), act, o_ref in zip(bounds, acts, outs):
        @pl.when((j >= j0) & (j < j1))
        def _(o_ref=o_ref, act=act):
            r = _dot(xn_ref[...], w_ref[...])
            if act:
                r = jax.nn.sigmoid(r)
            o_ref[...] = r.astype(o_ref.dtype)

    for r_in, r_out, deint in zip(rider_in, rider_out, riders):
        if deint:
            for b in range(r_in.shape[1] // (2 * LANE)):
                sl = slice(2 * b * LANE, (2 * b + 2) * LANE)
                r_out[:, sl] = _dot(r_in[:, sl].astype(BF16), rider_in[-1][...]).astype(r_out.dtype)
        else:
            r_out[...] = r_in[...].astype(r_out.dtype)


def _deinterleave_perm():
    src = jnp.arange(2 * LANE)
    dst = (src % 2) * LANE + src // 2
    return jnp.zeros((2 * LANE, 2 * LANE), BF16).at[src, dst].set(1)


def _norm_mod_proj(x2d, g, sh, sc, w, widths, acts, rows_per_mod, tm, tn, riders=()):
    n, d = x2d.shape
    ncols = w.shape[1]
    tm = min(tm, n)
    tn = min(tn, min(widths))
    bounds, off = [], 0
    for wd in widths:
        bounds.append((off // tn, (off + wd) // tn))
        off += wd
    tiles_per_mod = rows_per_mod // tm
    nj = ncols // tn
    nsteps = (n // tm) * nj

    def out_map(i, j, j0, nj):
        return (i, jnp.clip(j - j0, 0, nj - 1))

    out_specs = [pl.BlockSpec((tm, tn), functools.partial(out_map, j0=j0, nj=j1 - j0))
                 for (j0, j1) in bounds]
    out_shape = [jax.ShapeDtypeStruct((n, wd), BF16) for wd in widths]
    rider_args, rider_specs = [], []
    for arr, _ in riders:
        spec = pl.BlockSpec((arr.shape[0] // nsteps, arr.shape[1]), lambda i, j: (i * nj + j, 0))
        rider_args.append(arr)
        rider_specs.append(spec)
        out_specs.append(spec)
        out_shape.append(jax.ShapeDtypeStruct(arr.shape, BF16))
    if any(flag for _, flag in riders):
        rider_args.append(_deinterleave_perm())
        rider_specs.append(pl.BlockSpec((2 * LANE, 2 * LANE), lambda i, j: (0, 0)))
    return pl.pallas_call(
        functools.partial(_inproj_kernel, bounds=tuple(bounds), acts=tuple(acts),
                          riders=tuple(flag for _, flag in riders)),
        grid=(n // tm, nj),
        in_specs=[pl.BlockSpec((tm, d), lambda i, j: (i, 0)),
                  pl.BlockSpec((1, d), lambda i, j: (0, 0)),
                  pl.BlockSpec((None, 1, d), lambda i, j: (i // tiles_per_mod, 0, 0)),
                  pl.BlockSpec((None, 1, d), lambda i, j: (i // tiles_per_mod, 0, 0)),
                  pl.BlockSpec((d, tn), lambda i, j: (0, j))] + rider_specs,
        out_specs=out_specs,
        out_shape=out_shape,
        scratch_shapes=[pltpu.VMEM((tm, d), BF16)],
        compiler_params=_params(("parallel", "arbitrary")),
        name="norm_mod_proj",
    )(x2d, g, sh, sc, w, *rider_args)


def _s5_matrices(lam_re, lam_im, log_dt, b_re, b_im, c_re, c_im, d_skip, t):
    lam = lax.complex(lam_re.astype(F32), lam_im.astype(F32))
    dt = jnp.exp(log_dt.astype(F32))[..., None]
    lam_dt = lam * dt
    lam_bar = jnp.exp(lam_dt)
    b_bar = ((lam_bar - 1) / lam)[..., None] * lax.complex(b_re.astype(F32), b_im.astype(F32))
    c_out = lax.complex(c_re.astype(F32), c_im.astype(F32))
    g, p = lam.shape[1], lam.shape[2]
    gs = b_bar.shape[-1]
    k = jnp.arange(t + 1, dtype=F32)
    pw = jnp.exp(lam_dt[..., None] * k)
    kk = jnp.einsum('dgcp,dgpk,dgpe->dgkce', c_out, pw[..., :t], b_bar, precision=HIGHEST).real
    kf, kb = kk[0], kk[1]
    zero_lag = kf[:, :1] + kb[:, :1]
    k_all = jnp.concatenate([kb[:, :0:-1], zero_lag, kf[:, 1:]], axis=1)
    jj = jnp.arange(t)
    lag_idx = jj[None, :] - jj[:, None] + (t - 1)
    m_intra = k_all[:, lag_idx]
    m_intra = m_intra.transpose(0, 1, 4, 2, 3).reshape(g, t * gs, t * gs)
    m_intra = m_intra + jnp.eye(t * gs, dtype=F32)[None] * jnp.tile(
        d_skip.astype(F32).reshape(g, 1, gs), (1, t, 1)).reshape(g, 1, t * gs)
    in_f = pw[0][..., t - 1 - jj][..., None] * b_bar[0][:, :, None, :]
    in_b = pw[1][..., jj][..., None] * b_bar[1][:, :, None, :]

    def to_in(z):
        return z.transpose(0, 2, 3, 1).reshape(g, t * gs, p)

    m_in = jnp.concatenate([to_in(in_f.real), to_in(in_f.imag), to_in(in_b.real), to_in(in_b.imag)], axis=-1)
    out_f = c_out[0][:, :, :, None] * pw[0][:, None, :, 1 + jj]
    out_b = c_out[1][:, :, :, None] * pw[1][:, None, :, t - jj]

    def to_out(z):
        return z.transpose(0, 2, 3, 1).reshape(g, p, t * gs)

    m_out = jnp.concatenate([to_out(out_f.real), -to_out(out_f.imag), to_out(out_b.real), -to_out(out_b.imag)],
                            axis=1)
    mu = pw[..., t]
    mre, mim = mu.real, mu.imag
    coef = jnp.stack([
        jnp.concatenate([mre[0], mre[0], mre[1], mre[1]], axis=-1),
        jnp.concatenate([-mim[0], mim[0], -mim[1], mim[1]], axis=-1),
        jnp.concatenate([mim[0], -mim[0], mim[1], -mim[1]], axis=-1)])
    return m_intra.astype(BF16), m_in.astype(BF16), m_out.astype(BF16), coef


def _s5_in_kernel(a_ref, m_ref, o_ref):
    o_ref[...] = _dot(a_ref[...], m_ref[...])


def _s5_chunk_states(a, m_in):
    g, nc, kdim = a.shape
    s = m_in.shape[2]
    nb = min(nc, 1024)
    return pl.pallas_call(
        _s5_in_kernel,
        grid=(g, nc // nb),
        in_specs=[pl.BlockSpec((None, nb, kdim), lambda gi, i: (gi, i, 0)),
                  pl.BlockSpec((None, kdim, s), lambda gi, i: (gi, 0, 0))],
        out_specs=pl.BlockSpec((None, nb, s), lambda gi, i: (gi, i, 0)),
        out_shape=jax.ShapeDtypeStruct((g, nc, s), F32),
        compiler_params=_params(("parallel", "arbitrary")),
        name="s5_chunk_states",
    )(a, m_in)


def _s5_scan_kernel(sf_ref, sb_ref, init_ref, coef_ref, ef_ref, eb_ref, fin_ref, st_ref, *, nblk, half):
    k = pl.program_id(1)
    nk = pl.num_programs(1)
    lanes = 2 * half

    @pl.when(k == 0)
    def _():
        s0f = init_ref[:, 0:lanes]
        s0b = init_ref[:, lanes:2 * lanes]
        st_ref[0] = s0f
        st_ref[1] = pltpu.roll(s0f, half, 1)
        st_ref[2] = s0b
        st_ref[3] = pltpu.roll(s0b, half, 1)

    def body(r, carry):
        sf, sfw, sb, sbw = carry
        rb = nblk - 1 - r
        xf = sf_ref[r]
        xb = sb_ref[rb]
        ef_ref[r] = sf
        eb_ref[rb] = sb
        xfw = pltpu.roll(xf, half, 1)
        xbw = pltpu.roll(xb, half, 1)
        af, bf, bfw = coef_ref[0, :, 0:lanes], coef_ref[1, :, 0:lanes], coef_ref[2, :, 0:lanes]
        ab, bb, bbw = (coef_ref[0, :, lanes:2 * lanes], coef_ref[1, :, lanes:2 * lanes],
                       coef_ref[2, :, lanes:2 * lanes])
        return (sf * af + sfw * bf + xf, sfw * af + sf * bfw + xfw,
                sb * ab + sbw * bb + xb, sbw * ab + sb * bbw + xbw)

    sf, sfw, sb, sbw = lax.fori_loop(0, nblk, body, (st_ref[0], st_ref[1], st_ref[2], st_ref[3]))
    st_ref[0] = sf
    st_ref[1] = sfw
    st_ref[2] = sb
    st_ref[3] = sbw

    @pl.when(k == nk - 1)
    def _():
        fin_ref[:, 0:lanes] = sf
        fin_ref[:, lanes:2 * lanes] = sb


def _s5_scan(s_t, init, coef):
    bsz, nc, g, s4 = s_t.shape
    lanes = s4 // 2
    nblk = min(nc, 128)
    nk = nc // nblk
    return pl.pallas_call(
        functools.partial(_s5_scan_kernel, nblk=nblk, half=lanes // 2),
        grid=(bsz, nk),
        in_specs=[pl.BlockSpec((None, nblk, g, lanes), lambda b, k: (b, k, 0, 0)),
                  pl.BlockSpec((None, nblk, g, lanes), lambda b, k: (b, nk - 1 - k, 0, 1)),
                  pl.BlockSpec((None, g, s4), lambda b, k: (b, 0, 0)),
                  pl.BlockSpec((3, g, s4), lambda b, k: (0, 0, 0))],
        out_specs=[pl.BlockSpec((None, nblk, g, lanes), lambda b, k: (b, k, 0, 0)),
                   pl.BlockSpec((None, nblk, g, lanes), lambda b, k: (b, nk - 1 - k, 0, 0)),
                   pl.BlockSpec((None, g, s4), lambda b, k: (b, 0, 0))],
        out_shape=[jax.ShapeDtypeStruct((bsz, nc, g, lanes), F32),
                   jax.ShapeDtypeStruct((bsz, nc, g, lanes), F32),
                   jax.ShapeDtypeStruct((bsz, g, s4), F32)],
        scratch_shapes=[pltpu.VMEM((4, g, lanes), F32)],
        compiler_params=_params(("parallel", "arbitrary")),
        name="s5_scan",
    )(s_t, s_t, init, coef)


def _s5_out_kernel(a_ref, e_ref, mi_ref, mo_ref, o_ref):
    y = _dot(a_ref[...], mi_ref[...]) + _dot(e_ref[...], mo_ref[...])
    o_ref[...] = y.astype(o_ref.dtype)


def _s5_chunk_outputs(a, e_in, m_intra, m_out):
    g, nc, kdim = a.shape
    s = e_in.shape[2]
    nb = min(nc, 1024)
    return pl.pallas_call(
        _s5_out_kernel,
        grid=(g, nc // nb),
        in_specs=[pl.BlockSpec((None, nb, kdim), lambda gi, i: (gi, i, 0)),
                  pl.BlockSpec((None, nb, s), lambda gi, i: (gi, i, 0)),
                  pl.BlockSpec((None, kdim, kdim), lambda gi, i: (gi, 0, 0)),
                  pl.BlockSpec((None, s, kdim), lambda gi, i: (gi, 0, 0))],
        out_specs=pl.BlockSpec((None, nb, kdim), lambda gi, i: (gi, i, 0)),
        out_shape=jax.ShapeDtypeStruct((g, nc, kdim), BF16),
        compiler_params=_params(("parallel", "arbitrary")),
        name="s5_chunk_outputs",
    )(a, e_in, m_intra, m_out)


def _s5_readout_kernel(y_ref, wg_ref, o_ref):
    y = jax.nn.gelu(y_ref[...].astype(F32))
    gate = _dot(y.astype(BF16), wg_ref[...])
    o_ref[...] = (y * jax.nn.sigmoid(gate)).astype(o_ref.dtype)


def _s5_readout(y2d, wg):
    n, w = y2d.shape
    tm = min(n, 1024)
    return pl.pallas_call(
        _s5_readout_kernel,
        grid=(n // tm,),
        in_specs=[pl.BlockSpec((tm, w), lambda i: (i, 0)),
                  pl.BlockSpec((w, w), lambda i: (0, 0))],
        out_specs=pl.BlockSpec((tm, w), lambda i: (i, 0)),
        out_shape=jax.ShapeDtypeStruct((n, w), BF16),
        compiler_params=_params(("parallel",)),
        name="s5_readout",
    )(y2d, wg)


def _to_chunks(u2d, g, t):
    n = u2d.shape[0]
    return u2d.reshape(n // t, t, g, S5_GROUP).transpose(2, 0, 1, 3).reshape(g, n // t, t * S5_GROUP)


def _from_chunks(y, g, t):
    nc = y.shape[1]
    return y.reshape(g, nc, t, S5_GROUP).transpose(1, 2, 0, 3).reshape(nc * t, g * S5_GROUP)


def _s5_branch(ua, uca, bsz, lam_re, lam_im, log_dt, b_re, b_im, c_re, c_im, d_skip, w_glu):
    g = lam_re.shape[1]
    t = S5_CHUNK
    m_intra, m_in, m_out, coef = _s5_matrices(lam_re, lam_im, log_dt, b_re, b_im, c_re, c_im, d_skip, t)
    s4 = m_in.shape[2]

    def states(u2d):
        a = _to_chunks(u2d, g, t)
        s = _s5_chunk_states(a, m_in)
        nc = s.shape[1] // bsz
        return a, s.reshape(g, bsz, nc, s4).transpose(1, 2, 0, 3)

    _, s_ctx = states(uca)
    _, _, seed = _s5_scan(s_ctx, jnp.zeros((bsz, g, s4), F32), coef)
    a_lat, s_lat = states(ua)
    e_f, e_b, _ = _s5_scan(s_lat, seed, coef)
    e_in = jnp.concatenate([e_f, e_b], axis=-1).astype(BF16)
    e_in = e_in.transpose(2, 0, 1, 3).reshape(g, -1, s4)
    y = _s5_chunk_outputs(a_lat, e_in, m_intra, m_out)
    return _s5_readout(_from_chunks(y, g, t), w_glu.astype(BF16))


HY_KH = GRID_W + 1
HY_KP = 72
HY_QN = 16


def _hy_phase(w_cols, r_cnt):
    n1_tot = 2 * GRID_W
    n_fft = n1_tot * r_cnt
    kh = GRID_W + 1
    k1 = np.arange(kh, dtype=np.int64)[None, :, None]
    r = np.arange(r_cnt, dtype=np.int64)[:, None, None]
    n1 = np.arange(w_cols, dtype=np.int64)[None, None, :]
    return 2.0 * np.pi * ((k1 * (r_cnt * n1 + r)) % n_fft) / n_fft


def _hy_fwd_table(r_cnt, m):
    th = _hy_phase(GRID_W * m, r_cnt)
    kh = th.shape[1]
    f = np.zeros((r_cnt, 2 * HY_KP, GRID_W * m), np.float32)
    f[:, :kh] = np.cos(th)
    f[:, HY_KP:HY_KP + kh] = -np.sin(th)
    return jnp.asarray(f, BF16)


def _hy_inv_table(r_cnt):
    th = _hy_phase(GRID_W, r_cnt)
    kh = th.shape[1]
    c = np.full((kh,), 2.0)
    c[0] = c[-1] = 1.0
    scale = c[None, :, None] / (2 * GRID_W * r_cnt)
    g = np.zeros((r_cnt, GRID_W, 2 * HY_KP), np.float32)
    g[:, :, :kh] = (scale * np.cos(th)).transpose(0, 2, 1)
    g[:, :, HY_KP:HY_KP + kh] = (-scale * np.sin(th)).transpose(0, 2, 1)
    return jnp.asarray(g, BF16)


def _hy_dft_tables(r_cnt):
    k = np.arange(r_cnt, dtype=np.int64)
    th = 2.0 * np.pi * ((k[:, None] * k[None, :]) % r_cnt) / r_cnt
    fc, fs = np.cos(th), np.sin(th)
    fwd = np.block([[fc, fs], [-fs, fc]]).astype(np.float32)
    inv = np.block([[fc, -fs], [fs, fc]]).astype(np.float32)
    return jnp.asarray(fwd, BF16), jnp.asarray(inv, BF16)


def _hy_short_kernel(x_ref, w_ref, o_ref, *, rows, chunk):
    wk = w_ref[...]
    w0, w1, w2 = wk[0:1], wk[1:2], wk[2:3]
    gw = GRID_W
    col = lax.broadcasted_iota(jnp.int32, (gw, x_ref.shape[1]), 0)

    def piece(a, n):
        return x_ref[pl.ds(a, n), :].astype(F32)

    last = piece(rows - gw, gw)
    prev0 = jnp.where(col == 0, 0.0, pltpu.roll(last, 1, 0))
    o_ref[0:gw, :] = (w0 * prev0 + w1 * piece(0, gw) + w2 * piece(gw, gw)).astype(o_ref.dtype)
    first = piece(0, gw)
    next_l = jnp.where(col == gw - 1, 0.0, pltpu.roll(first, gw - 1, 0))
    o_ref[rows - gw:rows, :] = (w0 * piece(rows - 2 * gw, gw) + w1 * last + w2 * next_l).astype(o_ref.dtype)

    def body(i, carry):
        a = pl.multiple_of(gw + i * chunk, gw)
        o_ref[pl.ds(a, chunk), :] = (w0 * piece(a - gw, chunk) + w1 * piece(a, chunk)
                                     + w2 * piece(a + gw, chunk)).astype(o_ref.dtype)
        return carry

    n_full = (rows - 2 * gw) // chunk
    lax.fori_loop(0, n_full, body, 0)
    rem = rows - 2 * gw - n_full * chunk
    if rem:
        a = gw + n_full * chunk
        o_ref[a:a + rem, :] = (w0 * piece(a - gw, rem) + w1 * piece(a, rem)
                               + w2 * piece(a + gw, rem)).astype(o_ref.dtype)


def _hy_short_conv(uh, short_w, bsz, seq):
    n, ch = uh.shape
    ct = min(ch, LANE)
    chunk = min(512, seq - 2 * GRID_W)
    return pl.pallas_call(
        functools.partial(_hy_short_kernel, rows=seq, chunk=chunk),
        grid=(bsz, ch // ct),
        in_specs=[pl.BlockSpec((seq, ct), lambda b, c: (b, c)),
                  pl.BlockSpec((3, ct), lambda b, c: (0, c))],
        out_specs=pl.BlockSpec((seq, ct), lambda b, c: (b, c)),
        out_shape=jax.ShapeDtypeStruct((n, ch), BF16),
        compiler_params=_params(("parallel", "parallel")),
        name="hy_short_conv",
    )(uh, short_w)


def _hy_fwd1_kernel(*refs, m):
    x_refs, f_ref, o_ref, s_ref = refs[:m], refs[m], refs[m + 1], refs[m + 2]
    gw = GRID_W
    for q in range(HY_QN):
        xq = [x[q * gw:(q + 1) * gw, :] for x in x_refs]
        xq = xq[0] if m == 1 else jnp.concatenate(xq, axis=0)
        s_ref[q] = _dot(f_ref[q], xq)
    t = jnp.swapaxes(s_ref[...], 0, 1).astype(o_ref.dtype)
    o_ref[0] = t[0:HY_KH]
    o_ref[1] = t[HY_KP:HY_KP + HY_KH]


def _hy_fwd1(xs, col_blk0, width, bsz, r_cnt, ct):
    m = len(xs)
    table = _hy_fwd_table(r_cnt, m)
    rb = HY_QN * GRID_W
    nrb = r_cnt // HY_QN
    x_spec = pl.BlockSpec((rb, ct), lambda b, i, c: (b * nrb + i, col_blk0 + c))
    return pl.pallas_call(
        functools.partial(_hy_fwd1_kernel, m=m),
        grid=(bsz, nrb, width // ct),
        in_specs=[x_spec] * m + [pl.BlockSpec((HY_QN, 2 * HY_KP, GRID_W * m), lambda b, i, c: (i, 0, 0))],
        out_specs=pl.BlockSpec((None, 2, HY_KH, HY_QN, ct), lambda b, i, c: (b, 0, 0, i, c)),
        out_shape=jax.ShapeDtypeStruct((bsz, 2, HY_KH, r_cnt, width), BF16),
        scratch_shapes=[pltpu.VMEM((HY_QN, 2 * HY_KP, ct), F32)],
        compiler_params=_params(("parallel", "parallel", "arbitrary")),
        name="hy_fwd_stage1",
    )(*xs, table)


def _hy_spec_kernel(a_ref, f2_ref, o_ref):
    r = a_ref.shape[1]
    x = a_ref[...].reshape(2 * r, a_ref.shape[2])
    o_ref[...] = _dot(f2_ref[...], x).reshape(o_ref.shape)


def _hy_filter_spectrum(a, f2, ct):
    _, _, kh, r_cnt, width = a.shape
    return pl.pallas_call(
        _hy_spec_kernel,
        grid=(kh, width // ct),
        in_specs=[pl.BlockSpec((None, 2, None, r_cnt, ct), lambda k, c: (0, 0, k, 0, c)),
                  pl.BlockSpec((2 * r_cnt, 2 * r_cnt), lambda k, c: (0, 0))],
        out_specs=pl.BlockSpec((2, None, r_cnt, ct), lambda k, c: (0, k, 0, c)),
        out_shape=jax.ShapeDtypeStruct((2, kh, r_cnt, width), F32),
        compiler_params=_params(("parallel", "parallel")),
        name="hy_filter_spectrum",
    )(a, f2)


def _hy_mid_kernel(a_ref, kf_ref, invn_ref, f2_ref, f2i_ref, o_ref):
    r = a_ref.shape[1]
    x = a_ref[...].reshape(2 * r, a_ref.shape[2])
    b = _dot(f2_ref[...], x)
    br, bi = b[:r], b[r:]
    kr, ki = kf_ref[0], kf_ref[1]
    s = invn_ref[...]
    cr = (br * kr - bi * ki) * s
    ci = (br * ki + bi * kr) * s
    c = jnp.concatenate([cr, ci], axis=0).astype(BF16)
    o_ref[...] = _dot(f2i_ref[...], c).reshape(o_ref.shape).astype(o_ref.dtype)


def _hy_mid(a, kf, invn, order, f2, f2i, ct):
    bsz, _, kh, r_cnt, width = a.shape
    nct = width // ct
    return pl.pallas_call(
        _hy_mid_kernel,
        grid=(kh, nct, bsz),
        in_specs=[pl.BlockSpec((None, 2, None, r_cnt, ct), lambda k, c, b: (b, 0, k, 0, c)),
                  pl.BlockSpec((2, None, r_cnt, ct), lambda k, c, b: (0, k, 0, order * nct + c)),
                  pl.BlockSpec((1, ct), lambda k, c, b: (0, order * nct + c)),
                  pl.BlockSpec((2 * r_cnt, 2 * r_cnt), lambda k, c, b: (0, 0)),
                  pl.BlockSpec((2 * r_cnt, 2 * r_cnt), lambda k, c, b: (0, 0))],
        out_specs=pl.BlockSpec((None, 2, None, r_cnt, ct), lambda k, c, b: (b, 0, k, 0, c)),
        out_shape=jax.ShapeDtypeStruct(a.shape, BF16),
        compiler_params=_params(("parallel", "parallel", "arbitrary")),
        name="hy_spectrum_product",
    )(a, kf, invn, f2, f2i)


def _hy_inv1_kernel(d_ref, g_ref, xin_ref, xm_ref, bias_ref, o_ref, t_ref, s_ref):
    gw = GRID_W
    pad = jnp.zeros((HY_KP - HY_KH,) + t_ref.shape[1:], F32)
    t_ref[0:HY_KH] = d_ref[0].astype(F32)
    t_ref[HY_KH:HY_KP] = pad
    t_ref[HY_KP:HY_KP + HY_KH] = d_ref[1].astype(F32)
    t_ref[HY_KP + HY_KH:2 * HY_KP] = pad
    s_ref[...] = jnp.swapaxes(t_ref[...], 0, 1)
    for q in range(HY_QN):
        y = _dot(g_ref[q], s_ref[q].astype(BF16))
        rows = slice(q * gw, (q + 1) * gw)
        xin = xin_ref[rows, :].astype(F32)
        o_ref[rows, :] = (xm_ref[rows, :].astype(F32) * (y + xin * bias_ref[...])).astype(o_ref.dtype)


def _hy_inv1(d, xin, xin_blk0, xm, xm_blk0, bias, ct):
    bsz, _, kh, r_cnt, width = d.shape
    table = _hy_inv_table(r_cnt)
    rb = HY_QN * GRID_W
    nrb = r_cnt // HY_QN
    return pl.pallas_call(
        _hy_inv1_kernel,
        grid=(bsz, nrb, width // ct),
        in_specs=[pl.BlockSpec((None, 2, kh, HY_QN, ct), lambda b, i, c: (b, 0, 0, i, c)),
                  pl.BlockSpec((HY_QN, GRID_W, 2 * HY_KP), lambda b, i, c: (i, 0, 0)),
                  pl.BlockSpec((rb, ct), lambda b, i, c: (b * nrb + i, xin_blk0 + c)),
                  pl.BlockSpec((rb, ct), lambda b, i, c: (b * nrb + i, xm_blk0 + c)),
                  pl.BlockSpec((1, ct), lambda b, i, c: (0, c))],
        out_specs=pl.BlockSpec((rb, ct), lambda b, i, c: (b * nrb + i, c)),
        out_shape=jax.ShapeDtypeStruct((bsz * r_cnt * GRID_W, width), BF16),
        scratch_shapes=[pltpu.VMEM((2 * HY_KP, HY_QN, ct), F32), pltpu.VMEM((HY_QN, 2 * HY_KP, ct), F32)],
        compiler_params=_params(("parallel", "parallel", "arbitrary")),
        name="hy_inv_stage1",
    )(d, table, xin, xm, bias)


def _hy_filter_kernel(w1_ref, b1_ref, f1_ref, w2_ref, b2_ref, f2_ref, w3_ref, b3_ref, dec_ref, om_ref, ph_ref,
                      o_ref, sum_ref, *, direction, seqlen, r_cnt, rows):
    i = pl.program_id(1)
    row = lax.broadcasted_iota(jnp.int32, (rows, LANE), 0) + i * rows
    lane = lax.broadcasted_iota(jnp.int32, (rows, LANE), 1)
    r = jnp.right_shift(row, GRID_W.bit_length() - 1)
    w = jnp.bitwise_and(row, GRID_W - 1)
    n = (w + GRID_W * direction) * r_cnt + r
    p = n if direction == 0 else 2 * seqlen - n
    pf = p.astype(F32)
    t = pf / (seqlen - 1)
    feats = jnp.where(lane == 0, t, jnp.cos(pf * om_ref[...] + ph_ref[...]))
    hid = jnp.sin(f1_ref[...] * (jnp.dot(feats, w1_ref[...], preferred_element_type=F32, precision=HIGHEST)
                                 + b1_ref[...]))
    hid = jnp.sin(f2_ref[...] * (jnp.dot(hid, w2_ref[...], preferred_element_type=F32, precision=HIGHEST)
                                 + b2_ref[...]))
    filt = _dot(hid.astype(BF16), w3_ref[...].astype(BF16)) + b3_ref[...]
    reps = filt.shape[1] // LANE
    t_w = jnp.concatenate([t] * reps, axis=1)
    keep = jnp.concatenate([jnp.where(p < seqlen, 1.0, 0.0)] * reps, axis=1)
    filt = filt * jnp.exp(-t_w * jnp.abs(dec_ref[...])) * keep
    o_ref[...] = filt.astype(o_ref.dtype)

    @pl.when(i == 0)
    def _():
        sum_ref[...] = jnp.zeros_like(sum_ref)

    sum_ref[...] += jnp.sum(jnp.abs(filt), axis=0, keepdims=True)


def _hy_filter_signal(direction, seqlen, w1p, b1, f1, w2, b2, f2, w3, b3, decay, omega, phase, width):
    r_cnt = seqlen // GRID_W
    rows = min(256, seqlen)
    ct = width
    nct = width // ct
    hid = w2.shape[0]
    off = direction * nct
    const = lambda shape: pl.BlockSpec(shape, lambda c, i: (0, 0))
    colv = pl.BlockSpec((1, ct), lambda c, i: (0, off + c))
    return pl.pallas_call(
        functools.partial(_hy_filter_kernel, direction=direction, seqlen=seqlen, r_cnt=r_cnt, rows=rows),
        grid=(nct, seqlen // rows),
        in_specs=[const((LANE, hid)), const((1, hid)), const((1, hid)), const((hid, hid)), const((1, hid)),
                  const((1, hid)), pl.BlockSpec((hid, ct), lambda c, i: (0, off + c)), colv, colv,
                  const((1, LANE)), const((1, LANE))],
        out_specs=[pl.BlockSpec((rows, ct), lambda c, i: (i, c)),
                   pl.BlockSpec((1, ct), lambda c, i: (0, c))],
        out_shape=[jax.ShapeDtypeStruct((seqlen, width), BF16), jax.ShapeDtypeStruct((1, width), F32)],
        compiler_params=_params(("parallel", "arbitrary")),
        name="hy_filter_signal",
    )(w1p, b1, f1, w2, b2, f2, w3, b3, decay, omega, phase)


def _hyena_branch(uh, bsz, seq, short_w, w1, b1, f1, w2, b2, f2, w3, b3, decay, hy_bias):
    width = hy_bias.shape[1]
    r_cnt = seq // GRID_W
    ct = min(width, 256)
    nct = width // ct
    n_emb = w1.shape[0]
    n_bands = (n_emb - 1) // 2
    hid = w1.shape[1]
    bands = jnp.linspace(1e-4, n_bands - 1, n_bands, dtype=F32)
    omega = jnp.zeros((1, LANE), F32).at[0, 1:1 + n_bands].set((2 * jnp.pi / seq) * bands)
    omega = omega.at[0, 1 + n_bands:1 + 2 * n_bands].set((2 * jnp.pi / seq) * bands)
    phase = jnp.zeros((1, LANE), F32).at[0, 1 + n_bands:1 + 2 * n_bands].set(0.5 * jnp.pi)
    w1p = jnp.zeros((LANE, hid), F32).at[:n_emb].set(w1)
    sigs, sums = [], []
    for direction in range(2):
        sig, ssum = _hy_filter_signal(direction, seq, w1p, b1[None], f1[None], w2, b2[None], f2[None], w3,
                                      b3[None], decay[None], omega, phase, HY_ORDER * width)
        sigs.append(sig)
        sums.append(ssum)
    invn = 1.0 / (sums[0] + sums[1] + HY_NORM_EPS)
    f2m, f2i = _hy_dft_tables(r_cnt)
    ct3 = min(width, 1024)
    kf = _hy_filter_spectrum(_hy_fwd1(sigs, 0, HY_ORDER * width, 1, r_cnt, ct), f2m, ct3)

    z = _hy_short_conv(uh, short_w.astype(F32), bsz, seq)
    a = _hy_fwd1([z], 0, width, bsz, r_cnt, ct)
    d = _hy_mid(a, kf, invn, 0, f2m, f2i, ct3)
    y1 = _hy_inv1(d, z, 0, z, nct, hy_bias[0:1].astype(F32), ct)
    a = _hy_fwd1([y1], 0, width, bsz, r_cnt, ct)
    d = _hy_mid(a, kf, invn, 1, f2m, f2i, ct3)
    return _hy_inv1(d, y1, 0, z, 2 * nct, hy_bias[1:2].astype(F32), ct)


def _merge1_kernel(ya_ref, yh_ref, ug_ref, wa_ref, wh_ref, o_ref):
    d = o_ref.shape[1]
    a = _dot(ya_ref[...], wa_ref[...])
    h = _dot(yh_ref[...], wh_ref[...])
    ga = ug_ref[:, :d].astype(F32)
    gh = ug_ref[:, d:].astype(F32)
    o_ref[...] = (ga * a + gh * h).astype(o_ref.dtype)


def _merge1(ya, yh, ug, wa, wh):
    n, ws = ya.shape
    wh_in = yh.shape[1]
    d = wa.shape[1]
    tm = min(n, 512)
    return pl.pallas_call(
        _merge1_kernel,
        grid=(n // tm,),
        in_specs=[pl.BlockSpec((tm, ws), lambda i: (i, 0)),
                  pl.BlockSpec((tm, wh_in), lambda i: (i, 0)),
                  pl.BlockSpec((tm, 2 * d), lambda i: (i, 0)),
                  pl.BlockSpec((ws, d), lambda i: (0, 0)),
                  pl.BlockSpec((wh_in, d), lambda i: (0, 0))],
        out_specs=pl.BlockSpec((tm, d), lambda i: (i, 0)),
        out_shape=jax.ShapeDtypeStruct((n, d), BF16),
        compiler_params=_params(("parallel",)),
        name="merge_branches",
    )(ya, yh, ug, wa, wh)


def _merge2_kernel(m_ref, x_ref, g1_ref, ng_ref, sh_ref, sc_ref, wo_ref, wr_ref,
                   h1_ref, hm_ref, r1_ref, r2_ref):
    mix = _dot(m_ref[...], wo_ref[...])
    h1 = x_ref[...] + g1_ref[...] * mix
    h1_ref[...] = h1
    y = h1 * lax.rsqrt(jnp.mean(h1 * h1, axis=-1, keepdims=True) + NORM_EPS)
    y = y * ng_ref[...]
    y = y * (1.0 + sc_ref[...]) + sh_ref[...]
    hi = y.astype(BF16)
    lo = (y - hi.astype(F32)).astype(BF16)
    n_sub = y.shape[1] // LANE
    for s in range(n_sub):
        hm_ref[pl.ds(s, y.shape[0], stride=n_sub), :] = y[:, s * LANE:(s + 1) * LANE]
    r1_ref[...] = _dot(hi, wr_ref[...])
    r2_ref[...] = _dot(lo, wr_ref[...])


def _merge2(m, x2d, g1, ng, sh2, sc2, wo, wr, rows_per_mod):
    n, d = x2d.shape
    tm = min(n, 512)
    tiles_per_mod = rows_per_mod // tm
    nr = wr.shape[1]
    mod_spec = pl.BlockSpec((None, 1, d), lambda i: (i // tiles_per_mod, 0, 0))
    return pl.pallas_call(
        _merge2_kernel,
        grid=(n // tm,),
        in_specs=[pl.BlockSpec((tm, d), lambda i: (i, 0)),
                  pl.BlockSpec((tm, d), lambda i: (i, 0)),
                  mod_spec,
                  pl.BlockSpec((1, d), lambda i: (0, 0)),
                  mod_spec, mod_spec,
                  pl.BlockSpec((d, d), lambda i: (0, 0)),
                  pl.BlockSpec((d, nr), lambda i: (0, 0))],
        out_specs=[pl.BlockSpec((tm, d), lambda i: (i, 0)),
                   pl.BlockSpec((tm * (d // LANE), LANE), lambda i: (i, 0)),
                   pl.BlockSpec((tm, nr), lambda i: (i, 0)),
                   pl.BlockSpec((tm, nr), lambda i: (i, 0))],
        out_shape=[jax.ShapeDtypeStruct((n, d), F32),
                   jax.ShapeDtypeStruct((n * (d // LANE), LANE), F32),
                   jax.ShapeDtypeStruct((n, nr), F32),
                   jax.ShapeDtypeStruct((n, nr), F32)],
        compiler_params=_params(("parallel",)),
        name="out_proj_norm_router",
    )(m, x2d, g1, ng, sh2, sc2, wo, wr)


def _moe_kernel(be_ref, nv_ref, tok0_ref, tokn_ref, hm_ref, wgu_ref, bgu_ref, wd_ref, bd_ref, o_ref,
                xbuf_ref, xb_ref, acc_ref, sem):
    i = pl.program_id(0)
    j = pl.program_id(1)
    nj = pl.num_programs(1)
    nv = nv_ref[0]
    tm = xb_ref.shape[0]
    per_step = tm // nj

    n_sub = hm_ref.shape[1]

    def row_copy(tok, r, slot):
        return pltpu.make_async_copy(hm_ref.at[tok], xbuf_ref.at[slot, pl.ds(r * MOE_PITCH, n_sub), :],
                                     sem.at[slot])

    def wait_block(slot):
        for r in range(tm):
            row_copy(0, r, slot).wait()

    @pl.when((i == 0) & (j == 0) & (nv > 0))
    def _():
        for r in range(tm):
            row_copy(tok0_ref[0, r], r, 0).start()

    @pl.when((i < nv) & (j == 0))
    def _():
        slot = i % 2
        wait_block(slot)
        for s in range(n_sub):
            xb_ref[:, s * LANE:(s + 1) * LANE] = xbuf_ref[slot, pl.ds(s, tm, stride=MOE_PITCH), :].astype(BF16)
        acc_ref[...] = jnp.zeros_like(acc_ref)

    @pl.when(i < nv)
    def _():
        nslot = (i + 1) % 2
        for r in range(per_step):
            rr = j * per_step + r
            row_copy(tokn_ref[0, rr], rr, nslot).start()
        gu = _dot(xb_ref[...], wgu_ref[...]) + bgu_ref[...]
        acts = []
        for b in range(gu.shape[1] // (2 * LANE)):
            glu = jnp.minimum(gu[:, 2 * b * LANE:(2 * b + 1) * LANE], SWIGLU_LIMIT)
            lin = jnp.clip(gu[:, (2 * b + 1) * LANE:(2 * b + 2) * LANE], -SWIGLU_LIMIT, SWIGLU_LIMIT)
            acts.append((glu * jax.nn.sigmoid(SWIGLU_ALPHA * glu) * (lin + 1.0)).astype(BF16))
        acc_ref[...] += _dot(jnp.concatenate(acts, axis=1), wd_ref[...])

        @pl.when(j == nj - 1)
        def _():
            o_ref[...] = (acc_ref[...] + bd_ref[...]).astype(o_ref.dtype)

    @pl.when((i == nv - 1) & (j == nj - 1))
    def _():
        wait_block((i + 1) % 2)

    @pl.when((i >= nv) & (j == nj - 1))
    def _():
        o_ref[...] = jnp.zeros_like(o_ref)


def _moe_blocks(hm, slot_tok, block_expert, n_valid, wgu, bgu, wd, bd):
    d = wd.shape[2]
    assert hm.shape[1] <= MOE_PITCH and hm.shape[1] * hm.shape[2] == d
    f = wd.shape[1]
    tm = MOE_TM
    tf = min(f, MOE_TF)
    nj = f // tf
    nblk = block_expert.shape[0]

    def jeff(i, j, nv):
        return jnp.where(i < nv[0], j, nj - 1)

    grid_spec = pltpu.PrefetchScalarGridSpec(
        num_scalar_prefetch=2,
        grid=(nblk, nj),
        in_specs=[pl.BlockSpec((None, 1, tm), lambda i, j, be, nv: (0, 0, 0), memory_space=pltpu.SMEM),
                  pl.BlockSpec((None, 1, tm), lambda i, j, be, nv: (jnp.minimum(i + 1, nblk - 1), 0, 0),
                               memory_space=pltpu.SMEM),
                  pl.BlockSpec(memory_space=pl.ANY),
                  pl.BlockSpec((None, d, 2 * tf), lambda i, j, be, nv: (be[i], 0, jeff(i, j, nv))),
                  pl.BlockSpec((None, 1, 2 * tf), lambda i, j, be, nv: (be[i], 0, jeff(i, j, nv))),
                  pl.BlockSpec((None, tf, d), lambda i, j, be, nv: (be[i], jeff(i, j, nv), 0)),
                  pl.BlockSpec((None, 1, d), lambda i, j, be, nv: (be[i], 0, 0))],
        out_specs=pl.BlockSpec((tm, d), lambda i, j, be, nv: (i, 0)),
        scratch_shapes=[pltpu.VMEM((2, tm * MOE_PITCH, LANE), F32), pltpu.VMEM((tm, d), BF16),
                        pltpu.VMEM((tm, d), F32),
                        pltpu.SemaphoreType.DMA((2,))])
    return pl.pallas_call(
        _moe_kernel,
        grid_spec=grid_spec,
        out_shape=jax.ShapeDtypeStruct((nblk * tm, d), BF16),
        compiler_params=_params(("arbitrary", "arbitrary")),
        name="moe_experts",
    )(block_expert, n_valid, slot_tok, slot_tok, hm, wgu, bgu, wd, bd)


def _route(logits, n_experts):
    n_tok = logits.shape[0]
    top_val, top_idx = lax.top_k(logits, TOP_K)
    gates = jax.nn.softmax(top_val, axis=-1)
    n_assign = n_tok * TOP_K
    flat_e = top_idx.reshape(-1).astype(jnp.int32)
    experts = jnp.arange(n_experts, dtype=jnp.int32)
    onehot = (flat_e[:, None] == experts[None, :]).astype(jnp.int32)
    csum = jnp.cumsum(onehot, axis=0)
    rank = jnp.sum(onehot * csum, axis=1) - 1
    counts = csum[-1]
    padded = (counts + MOE_TM - 1) // MOE_TM * MOE_TM
    start = jnp.cumsum(counts) - counts
    padded_end = jnp.cumsum(padded)
    padded_start = padded_end - padded
    dest = padded_start[flat_e] + rank
    n_blocks = -(-n_assign // MOE_TM) + n_experts
    cap = n_blocks * MOE_TM
    n_valid = (padded_end[-1] // MOE_TM).astype(jnp.int32)
    order = jnp.argsort(flat_e)
    slots = jnp.arange(cap, dtype=jnp.int32)
    slot_e = jnp.minimum(jnp.sum((padded_end[None, :] <= slots[:, None]).astype(jnp.int32), axis=1), n_experts - 1)
    slot_rank = slots - padded_start[slot_e]
    src = jnp.minimum(start[slot_e] + slot_rank, n_assign - 1)
    slot_tok = jnp.where(slot_rank < counts[slot_e], order[src] // TOP_K, 0).astype(jnp.int32)
    block_start = jnp.minimum(jnp.arange(n_blocks, dtype=jnp.int32), n_valid - 1) * MOE_TM
    block_expert = jnp.minimum(
        jnp.sum((padded_end[None, :] <= block_start[:, None]).astype(jnp.int32), axis=1), n_experts - 1)
    return gates, slot_tok, dest, block_expert, n_valid


def _final_kernel(h1_ref, *rest):
    yg_refs = rest[:TOP_K]
    gates_ref, g2_ref, fg_ref, o_ref = rest[TOP_K:]
    gates = gates_ref[...]
    moe = yg_refs[0][...].astype(F32) * gates[:, 0:1]
    for k in range(1, TOP_K):
        moe = moe + yg_refs[k][...].astype(F32) * gates[:, k:k + 1]
    h = h1_ref[...] + g2_ref[...] * moe
    y = h * lax.rsqrt(jnp.mean(h * h, axis=-1, keepdims=True) + NORM_EPS)
    o_ref[...] = (y * fg_ref[...]).astype(o_ref.dtype)


def _final(h1, yg, gates, g2, fg, rows_per_mod):
    n, d = h1.shape
    tm = min(n, 512)
    tiles_per_mod = rows_per_mod // tm
    nt = n // tm
    yg_specs = [pl.BlockSpec((tm, d), functools.partial(lambda i, k: (k * nt + i, 0), k=k)) for k in range(TOP_K)]
    return pl.pallas_call(
        _final_kernel,
        grid=(nt,),
        in_specs=[pl.BlockSpec((tm, d), lambda i: (i, 0))] + yg_specs + [
                  pl.BlockSpec((tm, TOP_K), lambda i: (i, 0)),
                  pl.BlockSpec((None, 1, d), lambda i: (i // tiles_per_mod, 0, 0)),
                  pl.BlockSpec((1, d), lambda i: (0, 0))],
        out_specs=pl.BlockSpec((tm, d), lambda i: (i, 0)),
        out_shape=jax.ShapeDtypeStruct((n, d), F32),
        compiler_params=_params(("parallel",)),
        name="combine_final_norm",
    )(h1, *([yg] * TOP_K), gates, g2, fg)


def kernel(x, c, ctx, c_ctx, w_ada, b_ada, norm1_g, norm2_g, w_in, s5_lam_re, s5_lam_im, s5_log_dt, s5_b_re, s5_b_im, s5_c_re, s5_c_im, s5_d, s5_w_glu, hy_short_w, hy_pos_w1, hy_pos_b1, hy_freq1, hy_pos_w2, hy_pos_b2, hy_freq2, hy_pos_w3, hy_pos_b3, hy_decay, hy_bias, w_branch_s5, w_branch_hy, w_out, router_w, router_b, w_gate_up, b_gate_up, w_down, b_down, final_g):
    bsz, seq, d = x.shape
    ctx_len = ctx.shape[1]
    depth = w_ada.shape[0]
    assert depth == 1, "only the single-layer configuration is implemented"
    n_experts = router_w.shape[2]
    s5_w = s5_d.shape[1]
    hy_w = hy_bias.shape[2]
    n_tok = bsz * seq

    cc = jnp.zeros((8, d), F32).at[:bsz].set(c).at[bsz].set(c_ctx)
    mod = _ada_mod(cc, w_ada[0], b_ada[0][None]).reshape(8, 6, 1, d)
    sh1, sc1, g1, sh2, sc2, g2 = [mod[:bsz, k] for k in range(6)]
    csh1, csc1 = mod[bsz:bsz + 1, 0], mod[bsz:bsz + 1, 1]

    w_in_b = w_in[0].astype(BF16)
    x2d = x.reshape(n_tok, d)
    riders = ((w_gate_up[0].reshape(-1, w_gate_up.shape[3]), True), (w_down[0].reshape(-1, d), False))
    ua, uh, ug, wgu_b, wd_b = _norm_mod_proj(x2d, norm1_g[0][None], sh1, sc1, w_in_b,
                                             (s5_w, 3 * hy_w, 2 * d), (False, False, True), seq, 1024, 512,
                                             riders=riders)
    (uca,) = _norm_mod_proj(ctx.reshape(bsz * ctx_len, d), norm1_g[0][None], csh1, csc1,
                            w_in_b[:, :s5_w], (s5_w,), (False,), bsz * ctx_len, 512, 1024)

    ya = _s5_branch(ua, uca, bsz, s5_lam_re[0], s5_lam_im[0], s5_log_dt[0], s5_b_re[0], s5_b_im[0],
                    s5_c_re[0], s5_c_im[0], s5_d[0], s5_w_glu[0])

    yh = _hyena_branch(uh, bsz, seq, hy_short_w[0], hy_pos_w1[0], hy_pos_b1[0], hy_freq1[0], hy_pos_w2[0],
                       hy_pos_b2[0], hy_freq2[0], hy_pos_w3[0], hy_pos_b3[0], hy_decay[0], hy_bias[0])

    m = _merge1(ya, yh, ug, w_branch_s5[0].astype(BF16), w_branch_hy[0].astype(BF16))
    rw = router_w[0]
    rw_hi = rw.astype(BF16)
    rw_lo = (rw - rw_hi.astype(F32)).astype(BF16)
    wr = jnp.zeros((d, 128), BF16).at[:, :n_experts].set(rw_hi).at[:, n_experts:2 * n_experts].set(rw_lo)
    h1, hm, r1, r2 = _merge2(m, x2d, g1, norm2_g[0][None], sh2, sc2, w_out[0].astype(BF16), wr, seq)
    logits = (r1[:, :n_experts] + r1[:, n_experts:2 * n_experts] + r2[:, :n_experts]) + router_b[0]

    gates, slot_tok, dest, block_expert, n_valid = _route(logits, n_experts)
    f = w_down.shape[2]
    bgu = b_gate_up[0].reshape(n_experts, f // LANE, LANE, 2).swapaxes(2, 3).reshape(n_experts, 1, 2 * f)
    yslots = _moe_blocks(hm.reshape(n_tok, d // LANE, LANE), slot_tok.reshape(-1, 1, MOE_TM), block_expert,
                         n_valid.reshape(1),
                         wgu_b.reshape(w_gate_up.shape[1:]), bgu, wd_b.reshape(w_down.shape[1:]),
                         b_down[0][:, None, :])
    yg = yslots[dest.reshape(n_tok, TOP_K).T.reshape(-1)]
    out = _final(h1, yg, gates, g2, final_g[None], seq)
    return out.reshape(bsz, seq, d)
```

```python
import functools
import math

import jax
import jax.numpy as jnp
import numpy as np
from jax import lax
from jax.experimental import pallas as pl
from jax.experimental.pallas import tpu as pltpu

F32 = jnp.float32
BF16 = jnp.bfloat16
HIGHEST = lax.Precision.HIGHEST

LANE = 128
GRID_W = 64
NORM_EPS = 1e-6
S5_GROUP = 16
S5_CHUNK = 32
HY_ORDER = 2
HY_NORM_EPS = 1e-6
TOP_K = 4
SWIGLU_LIMIT = 7.0
SWIGLU_ALPHA = 1.702
MOE_TM = 512
MOE_TF = 1024
MOE_PITCH = 20
VMEM_LIMIT = 56 * 1024 * 1024


def _params(sem, vmem=VMEM_LIMIT):
    return pltpu.CompilerParams(dimension_semantics=sem, vmem_limit_bytes=vmem)


def _dot(a, b):
    return jnp.dot(a, b, preferred_element_type=F32)


def _ada_kernel(c_ref, w_ref, b_ref, o_ref):
    c = c_ref[...]
    a = c * jax.nn.sigmoid(c)
    o_ref[...] = jnp.dot(a, w_ref[...], preferred_element_type=F32, precision=HIGHEST) + b_ref[...]


def _ada_mod(cc, w, b):
    d, n6 = w.shape
    tn = min(512, n6)
    return pl.pallas_call(
        _ada_kernel,
        grid=(n6 // tn,),
        in_specs=[pl.BlockSpec((8, d), lambda j: (0, 0)),
                  pl.BlockSpec((d, tn), lambda j: (0, j)),
                  pl.BlockSpec((1, tn), lambda j: (0, j))],
        out_specs=pl.BlockSpec((8, tn), lambda j: (0, j)),
        out_shape=jax.ShapeDtypeStruct((8, n6), F32),
        compiler_params=_params(("parallel",)),
        name="ada_mod",
    )(cc, w, b)


def _inproj_kernel(x_ref, g_ref, sh_ref, sc_ref, w_ref, *rest, bounds, acts, riders):
    nr = len(riders)
    n_in = nr + (1 if any(riders) else 0)
    rider_in, rest = rest[:n_in], rest[n_in:]
    outs, rider_out, xn_ref = rest[:len(bounds)], rest[len(bounds):len(bounds) + nr], rest[-1]
    j = pl.program_id(1)

    @pl.when(j == 0)
    def _():
        x = x_ref[...].astype(F32)
        y = x * lax.rsqrt(jnp.mean(x * x, axis=-1, keepdims=True) + NORM_EPS)
        y = y * g_ref[...]
        y = y * (1.0 + sc_ref[...]) + sh_ref[...]
        xn_ref[...] = y.astype(BF16)

    for (j0, j1), act, o_ref in zip(bounds, acts, outs):
        @pl.when((j >= j0) & (j < j1))
        def _(o_ref=o_ref, act=act):
            r = _dot(xn_ref[...], w_ref[...])
            if act:
                r = jax.nn.sigmoid(r)
            o_ref[...] = r.astype(o_ref.dtype)

    for r_in, r_out, deint in zip(rider_in, rider_out, riders):
        if deint:
            for b in range(r_in.shape[1] // (2 * LANE)):
                sl = slice(2 * b * LANE, (2 * b + 2) * LANE)
                r_out[:, sl] = _dot(r_in[:, sl].astype(BF16), rider_in[-1][...]).astype(r_out.dtype)
        else:
            r_out[...] = r_in[...].astype(r_out.dtype)


def _deinterleave_perm():
    src = jnp.arange(2 * LANE)
    dst = (src % 2) * LANE + src // 2
    return jnp.zeros((2 * LANE, 2 * LANE), BF16).at[src, dst].set(1)


def _norm_mod_proj(x2d, g, sh, sc, w, widths, acts, rows_per_mod, tm, tn, riders=()):
    n, d = x2d.shape
    ncols = w.shape[1]
    tm = min(tm, n)
    tn = min(tn, min(widths))
    bounds, off = [], 0
    for wd in widths:
        bounds.append((off // tn, (off + wd) // tn))
        off += wd
    tiles_per_mod = rows_per_mod // tm
    nj = ncols // tn
    nsteps = (n // tm) * nj

    def out_map(i, j, j0, nj):
        return (i, jnp.clip(j - j0, 0, nj - 1))

    out_specs = [pl.BlockSpec((tm, tn), functools.partial(out_map, j0=j0, nj=j1 - j0))
                 for (j0, j1) in bounds]
    out_shape = [jax.ShapeDtypeStruct((n, wd), BF16) for wd in widths]
    rider_args, rider_specs = [], []
    for arr, _ in riders:
        spec = pl.BlockSpec((arr.shape[0] // nsteps, arr.shape[1]), lambda i, j: (i * nj + j, 0))
        rider_args.append(arr)
        rider_specs.append(spec)
        out_specs.append(spec)
        out_shape.append(jax.ShapeDtypeStruct(arr.shape, BF16))
    if any(flag for _, flag in riders):
        rider_args.append(_deinterleave_perm())
        rider_specs.append(pl.BlockSpec((2 * LANE, 2 * LANE), lambda i, j: (0, 0)))
    return pl.pallas_call(
        functools.partial(_inproj_kernel, bounds=tuple(bounds), acts=tuple(acts),
                          riders=tuple(flag for _, flag in riders)),
        grid=(n // tm, nj),
        in_specs=[pl.BlockSpec((tm, d), lambda i, j: (i, 0)),
                  pl.BlockSpec((1, d), lambda i, j: (0, 0)),
                  pl.BlockSpec((None, 1, d), lambda i, j: (i // tiles_per_mod, 0, 0)),
                  pl.BlockSpec((None, 1, d), lambda i, j: (i // tiles_per_mod, 0, 0)),
                  pl.BlockSpec((d, tn), lambda i, j: (0, j))] + rider_specs,
        out_specs=out_specs,
        out_shape=out_shape,
        scratch_shapes=[pltpu.VMEM((tm, d), BF16)],
        compiler_params=_params(("parallel", "arbitrary")),
        name="norm_mod_proj",
    )(x2d, g, sh, sc, w, *rider_args)


def _s5_matrices(lam_re, lam_im, log_dt, b_re, b_im, c_re, c_im, d_skip, t):
    lam = lax.complex(lam_re.astype(F32), lam_im.astype(F32))
    dt = jnp.exp(log_dt.astype(F32))[..., None]
    lam_dt = lam * dt
    lam_bar = jnp.exp(lam_dt)
    b_bar = ((lam_bar - 1) / lam)[..., None] * lax.complex(b_re.astype(F32), b_im.astype(F32))
    c_out = lax.complex(c_re.astype(F32), c_im.astype(F32))
    g, p = lam.shape[1], lam.shape[2]
    gs = b_bar.shape[-1]
    k = jnp.arange(t + 1, dtype=F32)
    pw = jnp.exp(lam_dt[..., None] * k)
    kk = jnp.einsum('dgcp,dgpk,dgpe->dgkce', c_out, pw[..., :t], b_bar, precision=HIGHEST).real
    kf, kb = kk[0], kk[1]
    zero_lag = kf[:, :1] + kb[:, :1]
    k_all = jnp.concatenate([kb[:, :0:-1], zero_lag, kf[:, 1:]], axis=1)
    jj = jnp.arange(t)
    lag_idx = jj[None, :] - jj[:, None] + (t - 1)
    m_intra = k_all[:, lag_idx]
    m_intra = m_intra.transpose(0, 1, 4, 2, 3).reshape(g, t * gs, t * gs)
    m_intra = m_intra + jnp.eye(t * gs, dtype=F32)[None] * jnp.tile(
        d_skip.astype(F32).reshape(g, 1, gs), (1, t, 1)).reshape(g, 1, t * gs)
    in_f = pw[0][..., t - 1 - jj][..., None] * b_bar[0][:, :, None, :]
    in_b = pw[1][..., jj][..., None] * b_bar[1][:, :, None, :]

    def to_in(z):
        return z.transpose(0, 2, 3, 1).reshape(g, t * gs, p)

    m_in = jnp.concatenate([to_in(in_f.real), to_in(in_f.imag), to_in(in_b.real), to_in(in_b.imag)], axis=-1)
    out_f = c_out[0][:, :, :, None] * pw[0][:, None, :, 1 + jj]
    out_b = c_out[1][:, :, :, None] * pw[1][:, None, :, t - jj]

    def to_out(z):
        return z.transpose(0, 2, 3, 1).reshape(g, p, t * gs)

    m_out = jnp.concatenate([to_out(out_f.real), -to_out(out_f.imag), to_out(out_b.real), -to_out(out_b.imag)],
                            axis=1)
    mu = pw[..., t]
    mre, mim = mu.real, mu.imag
    coef = jnp.stack([
        jnp.concatenate([mre[0], mre[0], mre[1], mre[1]], axis=-1),
        jnp.concatenate([-mim[0], mim[0], -mim[1], mim[1]], axis=-1),
        jnp.concatenate([mim[0], -mim[0], mim[1], -mim[1]], axis=-1)])
    return m_intra.astype(BF16), m_in.astype(BF16), m_out.astype(BF16), coef


def _s5_in_kernel(a_ref, m_ref, o_ref):
    o_ref[...] = _dot(a_ref[...], m_ref[...])


def _s5_chunk_states(a, m_in):
    g, nc, kdim = a.shape
    s = m_in.shape[2]
    nb = min(nc, 1024)
    return pl.pallas_call(
        _s5_in_kernel,
        grid=(g, nc // nb),
        in_specs=[pl.BlockSpec((None, nb, kdim), lambda gi, i: (gi, i, 0)),
                  pl.BlockSpec((None, kdim, s), lambda gi, i: (gi, 0, 0))],
        out_specs=pl.BlockSpec((None, nb, s), lambda gi, i: (gi, i, 0)),
        out_shape=jax.ShapeDtypeStruct((g, nc, s), F32),
        compiler_params=_params(("parallel", "arbitrary")),
        name="s5_chunk_states",
    )(a, m_in)


def _s5_scan_kernel(sf_ref, sb_ref, init_ref, coef_ref, ef_ref, eb_ref, fin_ref, st_ref, *, nblk, half):
    k = pl.program_id(1)
    nk = pl.num_programs(1)
    lanes = 2 * half

    @pl.when(k == 0)
    def _():
        s0f = init_ref[:, 0:lanes]
        s0b = init_ref[:, lanes:2 * lanes]
        st_ref[0] = s0f
        st_ref[1] = pltpu.roll(s0f, half, 1)
        st_ref[2] = s0b
        st_ref[3] = pltpu.roll(s0b, half, 1)

    def body(r, carry):
        sf, sfw, sb, sbw = carry
        rb = nblk - 1 - r
        xf = sf_ref[r]
        xb = sb_ref[rb]
        ef_ref[r] = sf
        eb_ref[rb] = sb
        xfw = pltpu.roll(xf, half, 1)
        xbw = pltpu.roll(xb, half, 1)
        af, bf, bfw = coef_ref[0, :, 0:lanes], coef_ref[1, :, 0:lanes], coef_ref[2, :, 0:lanes]
        ab, bb, bbw = (coef_ref[0, :, lanes:2 * lanes], coef_ref[1, :, lanes:2 * lanes],
                       coef_ref[2, :, lanes:2 * lanes])
        return (sf * af + sfw * bf + xf, sfw * af + sf * bfw + xfw,
                sb * ab + sbw * bb + xb, sbw * ab + sb * bbw + xbw)

    sf, sfw, sb, sbw = lax.fori_loop(0, nblk, body, (st_ref[0], st_ref[1], st_ref[2], st_ref[3]))
    st_ref[0] = sf
    st_ref[1] = sfw
    st_ref[2] = sb
    st_ref[3] = sbw

    @pl.when(k == nk - 1)
    def _():
        fin_ref[:, 0:lanes] = sf
        fin_ref[:, lanes:2 * lanes] = sb


def _s5_scan(s_t, init, coef):
    bsz, nc, g, s4 = s_t.shape
    lanes = s4 // 2
    nblk = min(nc, 128)
    nk = nc // nblk
    return pl.pallas_call(
        functools.partial(_s5_scan_kernel, nblk=nblk, half=lanes // 2),
        grid=(bsz, nk),
        in_specs=[pl.BlockSpec((None, nblk, g, lanes), lambda b, k: (b, k, 0, 0)),
                  pl.BlockSpec((None, nblk, g, lanes), lambda b, k: (b, nk - 1 - k, 0, 1)),
                  pl.BlockSpec((None, g, s4), lambda b, k: (b, 0, 0)),
                  pl.BlockSpec((3, g, s4), lambda b, k: (0, 0, 0))],
        out_specs=[pl.BlockSpec((None, nblk, g, lanes), lambda b, k: (b, k, 0, 0)),
                   pl.BlockSpec((None, nblk, g, lanes), lambda b, k: (b, nk - 1 - k, 0, 0)),
                   pl.BlockSpec((None, g, s4), lambda b, k: (b, 0, 0))],
        out_shape=[jax.ShapeDtypeStruct((bsz, nc, g, lanes), F32),
                   jax.ShapeDtypeStruct((bsz, nc, g, lanes), F32),
                   jax.ShapeDtypeStruct((bsz, g, s4), F32)],
        scratch_shapes=[pltpu.VMEM((4, g, lanes), F32)],
        compiler_params=_params(("parallel", "arbitrary")),
        name="s5_scan",
    )(s_t, s_t, init, coef)


def _s5_out_kernel(a_ref, e_ref, mi_ref, mo_ref, o_ref):
    y = _dot(a_ref[...], mi_ref[...]) + _dot(e_ref[...], mo_ref[...])
    o_ref[...] = y.astype(o_ref.dtype)


def _s5_chunk_outputs(a, e_in, m_intra, m_out):
    g, nc, kdim = a.shape
    s = e_in.shape[2]
    nb = min(nc, 1024)
    return pl.pallas_call(
        _s5_out_kernel,
        grid=(g, nc // nb),
        in_specs=[pl.BlockSpec((None, nb, kdim), lambda gi, i: (gi, i, 0)),
                  pl.BlockSpec((None, nb, s), lambda gi, i: (gi, i, 0)),
                  pl.BlockSpec((None, kdim, kdim), lambda gi, i: (gi, 0, 0)),
                  pl.BlockSpec((None, s, kdim), lambda gi, i: (gi, 0, 0))],
        out_specs=pl.BlockSpec((None, nb, kdim), lambda gi, i: (gi, i, 0)),
        out_shape=jax.ShapeDtypeStruct((g, nc, kdim), BF16),
        compiler_params=_params(("parallel", "arbitrary")),
        name="s5_chunk_outputs",
    )(a, e_in, m_intra, m_out)


def _s5_readout_kernel(y_ref, wg_ref, o_ref):
    y = jax.nn.gelu(y_ref[...].astype(F32))
    gate = _dot(y.astype(BF16), wg_ref[...])
    o_ref[...] = (y * jax.nn.sigmoid(gate)).astype(o_ref.dtype)


def _s5_readout(y2d, wg):
    n, w = y2d.shape
    tm = min(n, 1024)
    return pl.pallas_call(
        _s5_readout_kernel,
        grid=(n // tm,),
        in_specs=[pl.BlockSpec((tm, w), lambda i: (i, 0)),
                  pl.BlockSpec((w, w), lambda i: (0, 0))],
        out_specs=pl.BlockSpec((tm, w), lambda i: (i, 0)),
        out_shape=jax.ShapeDtypeStruct((n, w), BF16),
        compiler_params=_params(("parallel",)),
        name="s5_readout",
    )(y2d, wg)


def _to_chunks(u2d, g, t):
    n = u2d.shape[0]
    return u2d.reshape(n // t, t, g, S5_GROUP).transpose(2, 0, 1, 3).reshape(g, n // t, t * S5_GROUP)


def _from_chunks(y, g, t):
    nc = y.shape[1]
    return y.reshape(g, nc, t, S5_GROUP).transpose(1, 2, 0, 3).reshape(nc * t, g * S5_GROUP)


def _s5_branch(ua, uca, bsz, lam_re, lam_im, log_dt, b_re, b_im, c_re, c_im, d_skip, w_glu):
    g = lam_re.shape[1]
    t = S5_CHUNK
    m_intra, m_in, m_out, coef = _s5_matrices(lam_re, lam_im, log_dt, b_re, b_im, c_re, c_im, d_skip, t)
    s4 = m_in.shape[2]

    def states(u2d):
        a = _to_chunks(u2d, g, t)
        s = _s5_chunk_states(a, m_in)
        nc = s.shape[1] // bsz
        return a, s.reshape(g, bsz, nc, s4).transpose(1, 2, 0, 3)

    _, s_ctx = states(uca)
    _, _, seed = _s5_scan(s_ctx, jnp.zeros((bsz, g, s4), F32), coef)
    a_lat, s_lat = states(ua)
    e_f, e_b, _ = _s5_scan(s_lat, seed, coef)
    e_in = jnp.concatenate([e_f, e_b], axis=-1).astype(BF16)
    e_in = e_in.transpose(2, 0, 1, 3).reshape(g, -1, s4)
    y = _s5_chunk_outputs(a_lat, e_in, m_intra, m_out)
    return _s5_readout(_from_chunks(y, g, t), w_glu.astype(BF16))


HY_KH = GRID_W + 1
HY_KP = 72
HY_QN = 16


def _hy_phase(w_cols, r_cnt):
    n1_tot = 2 * GRID_W
    n_fft = n1_tot * r_cnt
    kh = GRID_W + 1
    k1 = np.arange(kh, dtype=np.int64)[None, :, None]
    r = np.arange(r_cnt, dtype=np.int64)[:, None, None]
    n1 = np.arange(w_cols, dtype=np.int64)[None, None, :]
    return 2.0 * np.pi * ((k1 * (r_cnt * n1 + r)) % n_fft) / n_fft


def _hy_fwd_table(r_cnt, m):
    th = _hy_phase(GRID_W * m, r_cnt)
    kh = th.shape[1]
    f = np.zeros((r_cnt, 2 * HY_KP, GRID_W * m), np.float32)
    f[:, :kh] = np.cos(th)
    f[:, HY_KP:HY_KP + kh] = -np.sin(th)
    return jnp.asarray(f, BF16)


def _hy_inv_table(r_cnt):
    th = _hy_phase(GRID_W, r_cnt)
    kh = th.shape[1]
    c = np.full((kh,), 2.0)
    c[0] = c[-1] = 1.0
    scale = c[None, :, None] / (2 * GRID_W * r_cnt)
    g = np.zeros((r_cnt, GRID_W, 2 * HY_KP), np.float32)
    g[:, :, :kh] = (scale * np.cos(th)).transpose(0, 2, 1)
    g[:, :, HY_KP:HY_KP + kh] = (-scale * np.sin(th)).transpose(0, 2, 1)
    return jnp.asarray(g, BF16)


def _hy_dft_tables(r_cnt):
    k = np.arange(r_cnt, dtype=np.int64)
    th = 2.0 * np.pi * ((k[:, None] * k[None, :]) % r_cnt) / r_cnt
    fc, fs = np.cos(th), np.sin(th)
    fwd = np.block([[fc, fs], [-fs, fc]]).astype(np.float32)
    inv = np.block([[fc, -fs], [fs, fc]]).astype(np.float32)
    return jnp.asarray(fwd, BF16), jnp.asarray(inv, BF16)


def _hy_short_kernel(x_ref, w_ref, o_ref, *, rows, chunk):
    wk = w_ref[...]
    w0, w1, w2 = wk[0:1], wk[1:2], wk[2:3]
    gw = GRID_W
    col = lax.broadcasted_iota(jnp.int32, (gw, x_ref.shape[1]), 0)

    def piece(a, n):
        return x_ref[pl.ds(a, n), :].astype(F32)

    last = piece(rows - gw, gw)
    prev0 = jnp.where(col == 0, 0.0, pltpu.roll(last, 1, 0))
    o_ref[0:gw, :] = (w0 * prev0 + w1 * piece(0, gw) + w2 * piece(gw, gw)).astype(o_ref.dtype)
    first = piece(0, gw)
    next_l = jnp.where(col == gw - 1, 0.0, pltpu.roll(first, gw - 1, 0))
    o_ref[rows - gw:rows, :] = (w0 * piece(rows - 2 * gw, gw) + w1 * last + w2 * next_l).astype(o_ref.dtype)

    def body(i, carry):
        a = pl.multiple_of(gw + i * chunk, gw)
        o_ref[pl.ds(a, chunk), :] = (w0 * piece(a - gw, chunk) + w1 * piece(a, chunk)
                                     + w2 * piece(a + gw, chunk)).astype(o_ref.dtype)
        return carry

    n_full = (rows - 2 * gw) // chunk
    lax.fori_loop(0, n_full, body, 0)
    rem = rows - 2 * gw - n_full * chunk
    if rem:
        a = gw + n_full * chunk
        o_ref[a:a + rem, :] = (w0 * piece(a - gw, rem) + w1 * piece(a, rem)
                               + w2 * piece(a + gw, rem)).astype(o_ref.dtype)


def _hy_short_conv(uh, short_w, bsz, seq):
    n, ch = uh.shape
    ct = min(ch, LANE)
    chunk = min(512, seq - 2 * GRID_W)
    return pl.pallas_call(
        functools.partial(_hy_short_kernel, rows=seq, chunk=chunk),
        grid=(bsz, ch // ct),
        in_specs=[pl.BlockSpec((seq, ct), lambda b, c: (b, c)),
                  pl.BlockSpec((3, ct), lambda b, c: (0, c))],
        out_specs=pl.BlockSpec((seq, ct), lambda b, c: (b, c)),
        out_shape=jax.ShapeDtypeStruct((n, ch), BF16),
        compiler_params=_params(("parallel", "parallel")),
        name="hy_short_conv",
    )(uh, short_w)


def _hy_fwd1_kernel(*refs, m):
    x_refs, f_ref, o_ref, s_ref = refs[:m], refs[m], refs[m + 1], refs[m + 2]
    gw = GRID_W
    for q in range(HY_QN):
        xq = [x[q * gw:(q + 1) * gw, :] for x in x_refs]
        xq = xq[0] if m == 1 else jnp.concatenate(xq, axis=0)
        s_ref[q] = _dot(f_ref[q], xq)
    t = jnp.swapaxes(s_ref[...], 0, 1).astype(o_ref.dtype)
    o_ref[0] = t[0:HY_KH]
    o_ref[1] = t[HY_KP:HY_KP + HY_KH]


def _hy_fwd1(xs, col_blk0, width, bsz, r_cnt, ct):
    m = len(xs)
    table = _hy_fwd_table(r_cnt, m)
    rb = HY_QN * GRID_W
    nrb = r_cnt // HY_QN
    x_spec = pl.BlockSpec((rb, ct), lambda b, i, c: (b * nrb + i, col_blk0 + c))
    return pl.pallas_call(
        functools.partial(_hy_fwd1_kernel, m=m),
        grid=(bsz, nrb, width // ct),
        in_specs=[x_spec] * m + [pl.BlockSpec((HY_QN, 2 * HY_KP, GRID_W * m), lambda b, i, c: (i, 0, 0))],
        out_specs=pl.BlockSpec((None, 2, HY_KH, HY_QN, ct), lambda b, i, c: (b, 0, 0, i, c)),
        out_shape=jax.ShapeDtypeStruct((bsz, 2, HY_KH, r_cnt, width), BF16),
        scratch_shapes=[pltpu.VMEM((HY_QN, 2 * HY_KP, ct), F32)],
        compiler_params=_params(("parallel", "parallel", "arbitrary")),
        name="hy_fwd_stage1",
    )(*xs, table)


def _hy_spec_kernel(a_ref, f2_ref, o_ref):
    r = a_ref.shape[1]
    x = a_ref[...].reshape(2 * r, a_ref.shape[2])
    o_ref[...] = _dot(f2_ref[...], x).reshape(o_ref.shape)


def _hy_filter_spectrum(a, f2, ct):
    _, _, kh, r_cnt, width = a.shape
    return pl.pallas_call(
        _hy_spec_kernel,
        grid=(kh, width // ct),
        in_specs=[pl.BlockSpec((None, 2, None, r_cnt, ct), lambda k, c: (0, 0, k, 0, c)),
                  pl.BlockSpec((2 * r_cnt, 2 * r_cnt), lambda k, c: (0, 0))],
        out_specs=pl.BlockSpec((2, None, r_cnt, ct), lambda k, c: (0, k, 0, c)),
        out_shape=jax.ShapeDtypeStruct((2, kh, r_cnt, width), F32),
        compiler_params=_params(("parallel", "parallel")),
        name="hy_filter_spectrum",
    )(a, f2)


def _hy_mid_kernel(a_ref, kf_ref, invn_ref, f2_ref, f2i_ref, o_ref):
    r = a_ref.shape[1]
    x = a_ref[...].reshape(2 * r, a_ref.shape[2])
    b = _dot(f2_ref[...], x)
    br, bi = b[:r], b[r:]
    kr, ki = kf_ref[0], kf_ref[1]
    s = invn_ref[...]
    cr = (br * kr - bi * ki) * s
    ci = (br * ki + bi * kr) * s
    c = jnp.concatenate([cr, ci], axis=0).astype(BF16)
    o_ref[...] = _dot(f2i_ref[...], c).reshape(o_ref.shape).astype(o_ref.dtype)


def _hy_mid(a, kf, invn, order, f2, f2i, ct):
    bsz, _, kh, r_cnt, width = a.shape
    nct = width // ct
    return pl.pallas_call(
        _hy_mid_kernel,
        grid=(kh, nct, bsz),
        in_specs=[pl.BlockSpec((None, 2, None, r_cnt, ct), lambda k, c, b: (b, 0, k, 0, c)),
                  pl.BlockSpec((2, None, r_cnt, ct), lambda k, c, b: (0, k, 0, order * nct + c)),
                  pl.BlockSpec((1, ct), lambda k, c, b: (0, order * nct + c)),
                  pl.BlockSpec((2 * r_cnt, 2 * r_cnt), lambda k, c, b: (0, 0)),
                  pl.BlockSpec((2 * r_cnt, 2 * r_cnt), lambda k, c, b: (0, 0))],
        out_specs=pl.BlockSpec((None, 2, None, r_cnt, ct), lambda k, c, b: (b, 0, k, 0, c)),
        out_shape=jax.ShapeDtypeStruct(a.shape, BF16),
        compiler_params=_params(("parallel", "parallel", "arbitrary")),
        name="hy_spectrum_product",
    )(a, kf, invn, f2, f2i)


def _hy_inv1_kernel(d_ref, g_ref, xin_ref, xm_ref, bias_ref, o_ref, t_ref, s_ref):
    gw = GRID_W
    pad = jnp.zeros((HY_KP - HY_KH,) + t_ref.shape[1:], F32)
    t_ref[0:HY_KH] = d_ref[0].astype(F32)
    t_ref[HY_KH:HY_KP] = pad
    t_ref[HY_KP:HY_KP + HY_KH] = d_ref[1].astype(F32)
    t_ref[HY_KP + HY_KH:2 * HY_KP] = pad
    s_ref[...] = jnp.swapaxes(t_ref[...], 0, 1)
    for q in range(HY_QN):
        y = _dot(g_ref[q], s_ref[q].astype(BF16))
        rows = slice(q * gw, (q + 1) * gw)
        xin = xin_ref[rows, :].astype(F32)
        o_ref[rows, :] = (xm_ref[rows, :].astype(F32) * (y + xin * bias_ref[...])).astype(o_ref.dtype)


def _hy_inv1(d, xin, xin_blk0, xm, xm_blk0, bias, ct):
    bsz, _, kh, r_cnt, width = d.shape
    table = _hy_inv_table(r_cnt)
    rb = HY_QN * GRID_W
    nrb = r_cnt // HY_QN
    return pl.pallas_call(
        _hy_inv1_kernel,
        grid=(bsz, nrb, width // ct),
        in_specs=[pl.BlockSpec((None, 2, kh, HY_QN, ct), lambda b, i, c: (b, 0, 0, i, c)),
                  pl.BlockSpec((HY_QN, GRID_W, 2 * HY_KP), lambda b, i, c: (i, 0, 0)),
                  pl.BlockSpec((rb, ct), lambda b, i, c: (b * nrb + i, xin_blk0 + c)),
                  pl.BlockSpec((rb, ct), lambda b, i, c: (b * nrb + i, xm_blk0 + c)),
                  pl.BlockSpec((1, ct), lambda b, i, c: (0, c))],
        out_specs=pl.BlockSpec((rb, ct), lambda b, i, c: (b * nrb + i, c)),
        out_shape=jax.ShapeDtypeStruct((bsz * r_cnt * GRID_W, width), BF16),
        scratch_shapes=[pltpu.VMEM((2 * HY_KP, HY_QN, ct), F32), pltpu.VMEM((HY_QN, 2 * HY_KP, ct), F32)],
        compiler_params=_params(("parallel", "parallel", "arbitrary")),
        name="hy_inv_stage1",
    )(d, table, xin, xm, bias)


def _hy_filter_kernel(w1_ref, b1_ref, f1_ref, w2_ref, b2_ref, f2_ref, w3_ref, b3_ref, dec_ref, om_ref, ph_ref,
                      o_ref, sum_ref, *, direction, seqlen, r_cnt, rows):
    i = pl.program_id(1)
    row = lax.broadcasted_iota(jnp.int32, (rows, LANE), 0) + i * rows
    lane = lax.broadcasted_iota(jnp.int32, (rows, LANE), 1)
    r = jnp.right_shift(row, GRID_W.bit_length() - 1)
    w = jnp.bitwise_and(row, GRID_W - 1)
    n = (w + GRID_W * direction) * r_cnt + r
    p = n if direction == 0 else 2 * seqlen - n
    pf = p.astype(F32)
    t = pf / (seqlen - 1)
    feats = jnp.where(lane == 0, t, jnp.cos(pf * om_ref[...] + ph_ref[...]))
    hid = jnp.sin(f1_ref[...] * (jnp.dot(feats, w1_ref[...], preferred_element_type=F32, precision=HIGHEST)
                                 + b1_ref[...]))
    hid = jnp.sin(f2_ref[...] * (jnp.dot(hid, w2_ref[...], preferred_element_type=F32, precision=HIGHEST)
                                 + b2_ref[...]))
    filt = _dot(hid.astype(BF16), w3_ref[...].astype(BF16)) + b3_ref[...]
    reps = filt.shape[1] // LANE
    t_w = jnp.concatenate([t] * reps, axis=1)
    keep = jnp.concatenate([jnp.where(p < seqlen, 1.0, 0.0)] * reps, axis=1)
    filt = filt * jnp.exp(-t_w * jnp.abs(dec_ref[...])) * keep
    o_ref[...] = filt.astype(o_ref.dtype)

    @pl.when(i == 0)
    def _():
        sum_ref[...] = jnp.zeros_like(sum_ref)

    sum_ref[...] += jnp.sum(jnp.abs(filt), axis=0, keepdims=True)


def _hy_filter_signal(direction, seqlen, w1p, b1, f1, w2, b2, f2, w3, b3, decay, omega, phase, width):
    r_cnt = seqlen // GRID_W
    rows = min(256, seqlen)
    ct = width
    nct = width // ct
    hid = w2.shape[0]
    off = direction * nct
    const = lambda shape: pl.BlockSpec(shape, lambda c, i: (0, 0))
    colv = pl.BlockSpec((1, ct), lambda c, i: (0, off + c))
    return pl.pallas_call(
        functools.partial(_hy_filter_kernel, direction=direction, seqlen=seqlen, r_cnt=r_cnt, rows=rows),
        grid=(nct, seqlen // rows),
        in_specs=[const((LANE, hid)), const((1, hid)), const((1, hid)), const((hid, hid)), const((1, hid)),
                  const((1, hid)), pl.BlockSpec((hid, ct), lambda c, i: (0, off + c)), colv, colv,
                  const((1, LANE)), const((1, LANE))],
        out_specs=[pl.BlockSpec((rows, ct), lambda c, i: (i, c)),
                   pl.BlockSpec((1, ct), lambda c, i: (0, c))],
        out_shape=[jax.ShapeDtypeStruct((seqlen, width), BF16), jax.ShapeDtypeStruct((1, width), F32)],
        compiler_params=_params(("parallel", "arbitrary")),
        name="hy_filter_signal",
    )(w1p, b1, f1, w2, b2, f2, w3, b3, decay, omega, phase)


def _hyena_branch(uh, bsz, seq, short_w, w1, b1, f1, w2, b2, f2, w3, b3, decay, hy_bias):
    width = hy_bias.shape[1]
    r_cnt = seq // GRID_W
    ct = min(width, 256)
    nct = width // ct
    n_emb = w1.shape[0]
    n_bands = (n_emb - 1) // 2
    hid = w1.shape[1]
    bands = jnp.linspace(1e-4, n_bands - 1, n_bands, dtype=F32)
    omega = jnp.zeros((1, LANE), F32).at[0, 1:1 + n_bands].set((2 * jnp.pi / seq) * bands)
    omega = omega.at[0, 1 + n_bands:1 + 2 * n_bands].set((2 * jnp.pi / seq) * bands)
    phase = jnp.zeros((1, LANE), F32).at[0, 1 + n_bands:1 + 2 * n_bands].set(0.5 * jnp.pi)
    w1p = jnp.zeros((LANE, hid), F32).at[:n_emb].set(w1)
    sigs, sums = [], []
    for direction in range(2):
        sig, ssum = _hy_filter_signal(direction, seq, w1p, b1[None], f1[None], w2, b2[None], f2[None], w3,
                                      b3[None], decay[None], omega, phase, HY_ORDER * width)
        sigs.append(sig)
        sums.append(ssum)
    invn = 1.0 / (sums[0] + sums[1] + HY_NORM_EPS)
    f2m, f2i = _hy_dft_tables(r_cnt)
    ct3 = min(width, 1024)
    kf = _hy_filter_spectrum(_hy_fwd1(sigs, 0, HY_ORDER * width, 1, r_cnt, ct), f2m, ct3)

    z = _hy_short_conv(uh, short_w.astype(F32), bsz, seq)
    a = _hy_fwd1([z], 0, width, bsz, r_cnt, ct)
    d = _hy_mid(a, kf, invn, 0, f2m, f2i, ct3)
    y1 = _hy_inv1(d, z, 0, z, nct, hy_bias[0:1].astype(F32), ct)
    a = _hy_fwd1([y1], 0, width, bsz, r_cnt, ct)
    d = _hy_mid(a, kf, invn, 1, f2m, f2i, ct3)
    return _hy_inv1(d, y1, 0, z, 2 * nct, hy_bias[1:2].astype(F32), ct)


def _merge1_kernel(ya_ref, yh_ref, ug_ref, wa_ref, wh_ref, o_ref):
    d = o_ref.shape[1]
    a = _dot(ya_ref[...], wa_ref[...])
    h = _dot(yh_ref[...], wh_ref[...])
    ga = ug_ref[:, :d].astype(F32)
    gh = ug_ref[:, d:].astype(F32)
    o_ref[...] = (ga * a + gh * h).astype(o_ref.dtype)


def _merge1(ya, yh, ug, wa, wh):
    n, ws = ya.shape
    wh_in = yh.shape[1]
    d = wa.shape[1]
    tm = min(n, 512)
    return pl.pallas_call(
        _merge1_kernel,
        grid=(n // tm,),
        in_specs=[pl.BlockSpec((tm, ws), lambda i: (i, 0)),
                  pl.BlockSpec((tm, wh_in), lambda i: (i, 0)),
                  pl.BlockSpec((tm, 2 * d), lambda i: (i, 0)),
                  pl.BlockSpec((ws, d), lambda i: (0, 0)),
                  pl.BlockSpec((wh_in, d), lambda i: (0, 0))],
        out_specs=pl.BlockSpec((tm, d), lambda i: (i, 0)),
        out_shape=jax.ShapeDtypeStruct((n, d), BF16),
        compiler_params=_params(("parallel",)),
        name="merge_branches",
    )(ya, yh, ug, wa, wh)


def _merge2_kernel(m_ref, x_ref, g1_ref, ng_ref, sh_ref, sc_ref, wo_ref, wr_ref,
                   h1_ref, hm_ref, r1_ref, r2_ref):
    mix = _dot(m_ref[...], wo_ref[...])
    h1 = x_ref[...] + g1_ref[...] * mix
    h1_ref[...] = h1
    y = h1 * lax.rsqrt(jnp.mean(h1 * h1, axis=-1, keepdims=True) + NORM_EPS)
    y = y * ng_ref[...]
    y = y * (1.0 + sc_ref[...]) + sh_ref[...]
    hi = y.astype(BF16)
    lo = (y - hi.astype(F32)).astype(BF16)
    n_sub = y.shape[1] // LANE
    for s in range(n_sub):
        hm_ref[pl.ds(s, y.shape[0], stride=n_sub), :] = y[:, s * LANE:(s + 1) * LANE]
    r1_ref[...] = _dot(hi, wr_ref[...])
    r2_ref[...] = _dot(lo, wr_ref[...])


def _merge2(m, x2d, g1, ng, sh2, sc2, wo, wr, rows_per_mod):
    n, d = x2d.shape
    tm = min(n, 512)
    tiles_per_mod = rows_per_mod // tm
    nr = wr.shape[1]
    mod_spec = pl.BlockSpec((None, 1, d), lambda i: (i // tiles_per_mod, 0, 0))
    return pl.pallas_call(
        _merge2_kernel,
        grid=(n // tm,),
        in_specs=[pl.BlockSpec((tm, d), lambda i: (i, 0)),
                  pl.BlockSpec((tm, d), lambda i: (i, 0)),
                  mod_spec,
                  pl.BlockSpec((1, d), lambda i: (0, 0)),
                  mod_spec, mod_spec,
                  pl.BlockSpec((d, d), lambda i: (0, 0)),
                  pl.BlockSpec((d, nr), lambda i: (0, 0))],
        out_specs=[pl.BlockSpec((tm, d), lambda i: (i, 0)),
                   pl.BlockSpec((tm * (d // LANE), LANE), lambda i: (i, 0)),
                   pl.BlockSpec((tm, nr), lambda i: (i, 0)),
                   pl.BlockSpec((tm, nr), lambda i: (i, 0))],
        out_shape=[jax.ShapeDtypeStruct((n, d), F32),
                   jax.ShapeDtypeStruct((n * (d // LANE), LANE), F32),
                   jax.ShapeDtypeStruct((n, nr), F32),
                   jax.ShapeDtypeStruct((n, nr), F32)],
        compiler_params=_params(("parallel",)),
        name="out_proj_norm_router",
    )(m, x2d, g1, ng, sh2, sc2, wo, wr)


def _moe_kernel(be_ref, nv_ref, tok0_ref, tokn_ref, hm_ref, wgu_ref, bgu_ref, wd_ref, bd_ref, o_ref,
                xbuf_ref, xb_ref, acc_ref, sem):
    i = pl.program_id(0)
    j = pl.program_id(1)
    nj = pl.num_programs(1)
    nv = nv_ref[0]
    tm = xb_ref.shape[0]

    n_sub = hm_ref.shape[1]

    def row_copy(tok, r, slot):
        return pltpu.make_async_copy(hm_ref.at[tok], xbuf_ref.at[slot, pl.ds(r * MOE_PITCH, n_sub), :],
                                     sem.at[slot])

    def wait_block(slot):
        for r in range(tm):
            row_copy(0, r, slot).wait()

    @pl.when((i == 0) & (j == 0) & (nv > 0))
    def _():
        for r in range(tm):
            row_copy(tok0_ref[0, r], r, 0).start()

    @pl.when((i < nv) & (j == 0))
    def _():
        for r in range(tm):
            row_copy(tokn_ref[0, r], r, (i + 1) % 2).start()
        slot = i % 2
        wait_block(slot)
        for s in range(n_sub):
            xb_ref[:, s * LANE:(s + 1) * LANE] = xbuf_ref[slot, pl.ds(s, tm, stride=MOE_PITCH), :].astype(BF16)
        acc_ref[...] = jnp.zeros_like(acc_ref)

    @pl.when(i < nv)
    def _():
        gu = _dot(xb_ref[...], wgu_ref[...]) + bgu_ref[...]
        acts = []
        for b in range(gu.shape[1] // (2 * LANE)):
            glu = jnp.minimum(gu[:, 2 * b * LANE:(2 * b + 1) * LANE], SWIGLU_LIMIT)
            lin = jnp.clip(gu[:, (2 * b + 1) * LANE:(2 * b + 2) * LANE], -SWIGLU_LIMIT, SWIGLU_LIMIT)
            acts.append((glu * jax.nn.sigmoid(SWIGLU_ALPHA * glu) * (lin + 1.0)).astype(BF16))
        acc_ref[...] += _dot(jnp.concatenate(acts, axis=1), wd_ref[...])

        @pl.when(j == nj - 1)
        def _():
            o_ref[...] = (acc_ref[...] + bd_ref[...]).astype(o_ref.dtype)

    @pl.when((i == nv - 1) & (j == nj - 1))
    def _():
        wait_block((i + 1) % 2)

    @pl.when((i >= nv) & (j == nj - 1))
    def _():
        o_ref[...] = jnp.zeros_like(o_ref)


def _moe_blocks(hm, slot_tok, block_expert, n_valid, wgu, bgu, wd, bd):
    d = wd.shape[2]
    assert hm.shape[1] <= MOE_PITCH and hm.shape[1] * hm.shape[2] == d
    f = wd.shape[1]
    tm = MOE_TM
    tf = min(f, MOE_TF)
    nj = f // tf
    nblk = block_expert.shape[0]

    def jeff(i, j, nv):
        return jnp.where(i < nv[0], j, nj - 1)

    grid_spec = pltpu.PrefetchScalarGridSpec(
        num_scalar_prefetch=2,
        grid=(nblk, nj),
        in_specs=[pl.BlockSpec((None, 1, tm), lambda i, j, be, nv: (0, 0, 0), memory_space=pltpu.SMEM),
                  pl.BlockSpec((None, 1, tm), lambda i, j, be, nv: (jnp.minimum(i + 1, nblk - 1), 0, 0),
                               memory_space=pltpu.SMEM),
                  pl.BlockSpec(memory_space=pl.ANY),
                  pl.BlockSpec((None, d, 2 * tf), lambda i, j, be, nv: (be[i], 0, jeff(i, j, nv))),
                  pl.BlockSpec((None, 1, 2 * tf), lambda i, j, be, nv: (be[i], 0, jeff(i, j, nv))),
                  pl.BlockSpec((None, tf, d), lambda i, j, be, nv: (be[i], jeff(i, j, nv), 0)),
                  pl.BlockSpec((None, 1, d), lambda i, j, be, nv: (be[i], 0, 0))],
        out_specs=pl.BlockSpec((tm, d), lambda i, j, be, nv: (i, 0)),
        scratch_shapes=[pltpu.VMEM((2, tm * MOE_PITCH, LANE), F32), pltpu.VMEM((tm, d), BF16),
                        pltpu.VMEM((tm, d), F32),
                        pltpu.SemaphoreType.DMA((2,))])
    return pl.pallas_call(
        _moe_kernel,
        grid_spec=grid_spec,
        out_shape=jax.ShapeDtypeStruct((nblk * tm, d), BF16),
        compiler_params=_params(("arbitrary", "arbitrary")),
        name="moe_experts",
    )(block_expert, n_valid, slot_tok, slot_tok, hm, wgu, bgu, wd, bd)


def _route(logits, n_experts):
    n_tok = logits.shape[0]
    top_val, top_idx = lax.top_k(logits, TOP_K)
    gates = jax.nn.softmax(top_val, axis=-1)
    n_assign = n_tok * TOP_K
    flat_e = top_idx.reshape(-1).astype(jnp.int32)
    experts = jnp.arange(n_experts, dtype=jnp.int32)
    onehot = (flat_e[:, None] == experts[None, :]).astype(jnp.int32)
    csum = jnp.cumsum(onehot, axis=0)
    rank = jnp.sum(onehot * csum, axis=1) - 1
    counts = csum[-1]
    padded = (counts + MOE_TM - 1) // MOE_TM * MOE_TM
    start = jnp.cumsum(counts) - counts
    padded_end = jnp.cumsum(padded)
    padded_start = padded_end - padded
    dest = padded_start[flat_e] + rank
    n_blocks = -(-n_assign // MOE_TM) + n_experts
    cap = n_blocks * MOE_TM
    n_valid = (padded_end[-1] // MOE_TM).astype(jnp.int32)
    order = jnp.argsort(flat_e)
    slots = jnp.arange(cap, dtype=jnp.int32)
    slot_e = jnp.minimum(jnp.sum((padded_end[None, :] <= slots[:, None]).astype(jnp.int32), axis=1), n_experts - 1)
    slot_rank = slots - padded_start[slot_e]
    src = jnp.minimum(start[slot_e] + slot_rank, n_assign - 1)
    slot_tok = jnp.where(slot_rank < counts[slot_e], order[src] // TOP_K, 0).astype(jnp.int32)
    block_start = jnp.minimum(jnp.arange(n_blocks, dtype=jnp.int32), n_valid - 1) * MOE_TM
    block_expert = jnp.minimum(
        jnp.sum((padded_end[None, :] <= block_start[:, None]).astype(jnp.int32), axis=1), n_experts - 1)
    return gates, slot_tok, dest, block_expert, n_valid


def _final_kernel(h1_ref, *rest):
    yg_refs = rest[:TOP_K]
    gates_ref, g2_ref, fg_ref, o_ref = rest[TOP_K:]
    gates = gates_ref[...]
    moe = yg_refs[0][...].astype(F32) * gates[:, 0:1]
    for k in range(1, TOP_K):
        moe = moe + yg_refs[k][...].astype(F32) * gates[:, k:k + 1]
    h = h1_ref[...] + g2_ref[...] * moe
    y = h * lax.rsqrt(jnp.mean(h * h, axis=-1, keepdims=True) + NORM_EPS)
    o_ref[...] = (y * fg_ref[...]).astype(o_ref.dtype)


def _final(h1, yg, gates, g2, fg, rows_per_mod):
    n, d = h1.shape
    tm = min(n, 512)
    tiles_per_mod = rows_per_mod // tm
    nt = n // tm
    yg_specs = [pl.BlockSpec((tm, d), functools.partial(lambda i, k: (k * nt + i, 0), k=k)) for k in range(TOP_K)]
    return pl.pallas_call(
        _final_kernel,
        grid=(nt,),
        in_specs=[pl.BlockSpec((tm, d), lambda i: (i, 0))] + yg_specs + [
                  pl.BlockSpec((tm, TOP_K), lambda i: (i, 0)),
                  pl.BlockSpec((None, 1, d), lambda i: (i // tiles_per_mod, 0, 0)),
                  pl.BlockSpec((1, d), lambda i: (0, 0))],
        out_specs=pl.BlockSpec((tm, d), lambda i: (i, 0)),
        out_shape=jax.ShapeDtypeStruct((n, d), F32),
        compiler_params=_params(("parallel",)),
        name="combine_final_norm",
    )(h1, *([yg] * TOP_K), gates, g2, fg)


def kernel(x, c, ctx, c_ctx, w_ada, b_ada, norm1_g, norm2_g, w_in, s5_lam_re, s5_lam_im, s5_log_dt, s5_b_re, s5_b_im, s5_c_re, s5_c_im, s5_d, s5_w_glu, hy_short_w, hy_pos_w1, hy_pos_b1, hy_freq1, hy_pos_w2, hy_pos_b2, hy_freq2, hy_pos_w3, hy_pos_b3, hy_decay, hy_bias, w_branch_s5, w_branch_hy, w_out, router_w, router_b, w_gate_up, b_gate_up, w_down, b_down, final_g):
    bsz, seq, d = x.shape
    ctx_len = ctx.shape[1]
    depth = w_ada.shape[0]
    assert depth == 1, "only the single-layer configuration is implemented"
    n_experts = router_w.shape[2]
    s5_w = s5_d.shape[1]
    hy_w = hy_bias.shape[2]
    n_tok = bsz * seq

    cc = jnp.zeros((8, d), F32).at[:bsz].set(c).at[bsz].set(c_ctx)
    mod = _ada_mod(cc, w_ada[0], b_ada[0][None]).reshape(8, 6, 1, d)
    sh1, sc1, g1, sh2, sc2, g2 = [mod[:bsz, k] for k in range(6)]
    csh1, csc1 = mod[bsz:bsz + 1, 0], mod[bsz:bsz + 1, 1]

    w_in_b = w_in[0].astype(BF16)
    x2d = x.reshape(n_tok, d)
    riders = ((w_gate_up[0].reshape(-1, w_gate_up.shape[3]), True), (w_down[0].reshape(-1, d), False))
    ua, uh, ug, wgu_b, wd_b = _norm_mod_proj(x2d, norm1_g[0][None], sh1, sc1, w_in_b,
                                             (s5_w, 3 * hy_w, 2 * d), (False, False, True), seq, 1024, 512,
                                             riders=riders)
    (uca,) = _norm_mod_proj(ctx.reshape(bsz * ctx_len, d), norm1_g[0][None], csh1, csc1,
                            w_in_b[:, :s5_w], (s5_w,), (False,), bsz * ctx_len, 512, 1024)

    ya = _s5_branch(ua, uca, bsz, s5_lam_re[0], s5_lam_im[0], s5_log_dt[0], s5_b_re[0], s5_b_im[0],
                    s5_c_re[0], s5_c_im[0], s5_d[0], s5_w_glu[0])

    yh = _hyena_branch(uh, bsz, seq, hy_short_w[0], hy_pos_w1[0], hy_pos_b1[0], hy_freq1[0], hy_pos_w2[0],
                       hy_pos_b2[0], hy_freq2[0], hy_pos_w3[0], hy_pos_b3[0], hy_decay[0], hy_bias[0])

    m = _merge1(ya, yh, ug, w_branch_s5[0].astype(BF16), w_branch_hy[0].astype(BF16))
    rw = router_w[0]
    rw_hi = rw.astype(BF16)
    rw_lo = (rw - rw_hi.astype(F32)).astype(BF16)
    wr = jnp.zeros((d, 128), BF16).at[:, :n_experts].set(rw_hi).at[:, n_experts:2 * n_experts].set(rw_lo)
    h1, hm, r1, r2 = _merge2(m, x2d, g1, norm2_g[0][None], sh2, sc2, w_out[0].astype(BF16), wr, seq)
    logits = (r1[:, :n_experts] + r1[:, n_experts:2 * n_experts] + r2[:, :n_experts]) + router_b[0]

    gates, slot_tok, dest, block_expert, n_valid = _route(logits, n_experts)
    f = w_down.shape[2]
    bgu = b_gate_up[0].reshape(n_experts, f // LANE, LANE, 2).swapaxes(2, 3).reshape(n_experts, 1, 2 * f)
    yslots = _moe_blocks(hm.reshape(n_tok, d // LANE, LANE), slot_tok.reshape(-1, 1, MOE_TM), block_expert,
                         n_valid.reshape(1),
                         wgu_b.reshape(w_gate_up.shape[1:]), bgu, wd_b.reshape(w_down.shape[1:]),
                         b_down[0][:, None, :])
    yg = yslots[dest.reshape(n_tok, TOP_K).T.reshape(-1)]
    out = _final(h1, yg, gates, g2, final_g[None], seq)
    return out.reshape(bsz, seq, d)
```

```python
import functools
import math

import jax
import jax.numpy as jnp
import numpy as np
from jax import lax
from jax.experimental import pallas as pl
from jax.experimental.pallas import tpu as pltpu

F32 = jnp.float32
BF16 = jnp.bfloat16
HIGHEST = lax.Precision.HIGHEST

LANE = 128
GRID_W = 64
NORM_EPS = 1e-6
S5_GROUP = 16
S5_CHUNK = 32
HY_ORDER = 2
HY_NORM_EPS = 1e-6
TOP_K = 4
SWIGLU_LIMIT = 7.0
SWIGLU_ALPHA = 1.702
MOE_TM = 512
MOE_TF = 1024
MOE_PITCH = 20
VMEM_LIMIT = 56 * 1024 * 1024


def _params(sem, vmem=VMEM_LIMIT):
    return pltpu.CompilerParams(dimension_semantics=sem, vmem_limit_bytes=vmem)


def _dot(a, b):
    return jnp.dot(a, b, preferred_element_type=F32)


def _ada_kernel(c_ref, w_ref, b_ref, o_ref):
    c = c_ref[...]
    a = c * jax.nn.sigmoid(c)
    o_ref[...] = jnp.dot(a, w_ref[...], preferred_element_type=F32, precision=HIGHEST) + b_ref[...]


def _ada_mod(cc, w, b):
    d, n6 = w.shape
    tn = min(512, n6)
    return pl.pallas_call(
        _ada_kernel,
        grid=(n6 // tn,),
        in_specs=[pl.BlockSpec((8, d), lambda j: (0, 0)),
                  pl.BlockSpec((d, tn), lambda j: (0, j)),
                  pl.BlockSpec((1, tn), lambda j: (0, j))],
        out_specs=pl.BlockSpec((8, tn), lambda j: (0, j)),
        out_shape=jax.ShapeDtypeStruct((8, n6), F32),
        compiler_params=_params(("parallel",)),
        name="ada_mod",
    )(cc, w, b)


def _inproj_kernel(x_ref, g_ref, sh_ref, sc_ref, w_ref, *rest, bounds, acts, riders):
    nr = len(riders)
    n_in = nr + (1 if any(riders) else 0)
    rider_in, rest = rest[:n_in], rest[n_in:]
    outs, rider_out, xn_ref = rest[:len(bounds)], rest[len(bounds):len(bounds) + nr], rest[-1]
    j = pl.program_id(1)

    @pl.when(j == 0)
    def _():
        x = x_ref[...].astype(F32)
        y = x * lax.rsqrt(jnp.mean(x * x, axis=-1, keepdims=True) + NORM_EPS)
        y = y * g_ref[...]
        y = y * (1.0 + sc_ref[...]) + sh_ref[...]
        xn_ref[...] = y.astype(BF16)

    for (j0, j1), act, o_ref in zip(bounds, acts, outs):
        @pl.when((j >= j0) & (j < j1))
        def _(o_ref=o_ref, act=act):
            r = _dot(xn_ref[...], w_ref[...])
            if act:
                r = jax.nn.sigmoid(r)
            o_ref[...] = r.astype(o_ref.dtype)

    for r_in, r_out, deint in zip(rider_in, rider_out, riders):
        if deint:
            for b in range(r_in.shape[1] // (2 * LANE)):
                sl = slice(2 * b * LANE, (2 * b + 2) * LANE)
                r_out[:, sl] = _dot(r_in[:, sl].astype(BF16), rider_in[-1][...]).astype(r_out.dtype)
        else:
            r_out[...] = r_in[...].astype(r_out.dtype)


def _deinterleave_perm():
    src = jnp.arange(2 * LANE)
    dst = (src % 2) * LANE + src // 2
    return jnp.zeros((2 * LANE, 2 * LANE), BF16).at[src, dst].set(1)


def _norm_mod_proj(x2d, g, sh, sc, w, widths, acts, rows_per_mod, tm, tn, riders=()):
    n, d = x2d.shape
    ncols = w.shape[1]
    tm = min(tm, n)
    tn = min(tn, min(widths))
    bounds, off = [], 0
    for wd in widths:
        bounds.append((off // tn, (off + wd) // tn))
        off += wd
    tiles_per_mod = rows_per_mod // tm
    nj = ncols // tn
    nsteps = (n // tm) * nj

    def out_map(i, j, j0, nj):
        return (i, jnp.clip(j - j0, 0, nj - 1))

    out_specs = [pl.BlockSpec((tm, tn), functools.partial(out_map, j0=j0, nj=j1 - j0))
                 for (j0, j1) in bounds]
    out_shape = [jax.ShapeDtypeStruct((n, wd), BF16) for wd in widths]
    rider_args, rider_specs = [], []
    for arr, _ in riders:
        spec = pl.BlockSpec((arr.shape[0] // nsteps, arr.shape[1]), lambda i, j: (i * nj + j, 0))
        rider_args.append(arr)
        rider_specs.append(spec)
        out_specs.append(spec)
        out_shape.append(jax.ShapeDtypeStruct(arr.shape, BF16))
    if any(flag for _, flag in riders):
        rider_args.append(_deinterleave_perm())
        rider_specs.append(pl.BlockSpec((2 * LANE, 2 * LANE), lambda i, j: (0, 0)))
    return pl.pallas_call(
        functools.partial(_inproj_kernel, bounds=tuple(bounds), acts=tuple(acts),
                          riders=tuple(flag for _, flag in riders)),
        grid=(n // tm, nj),
        in_specs=[pl.BlockSpec((tm, d), lambda i, j: (i, 0)),
                  pl.BlockSpec((1, d), lambda i, j: (0, 0)),
                  pl.BlockSpec((None, 1, d), lambda i, j: (i // tiles_per_mod, 0, 0)),
                  pl.BlockSpec((None, 1, d), lambda i, j: (i // tiles_per_mod, 0, 0)),
                  pl.BlockSpec((d, tn), lambda i, j: (0, j))] + rider_specs,
        out_specs=out_specs,
        out_shape=out_shape,
        scratch_shapes=[pltpu.VMEM((tm, d), BF16)],
        compiler_params=_params(("parallel", "arbitrary")),
        name="norm_mod_proj",
    )(x2d, g, sh, sc, w, *rider_args)


def _s5_matrices(lam_re, lam_im, log_dt, b_re, b_im, c_re, c_im, d_skip, t):
    lam = lax.complex(lam_re.astype(F32), lam_im.astype(F32))
    dt = jnp.exp(log_dt.astype(F32))[..., None]
    lam_dt = lam * dt
    lam_bar = jnp.exp(lam_dt)
    b_bar = ((lam_bar - 1) / lam)[..., None] * lax.complex(b_re.astype(F32), b_im.astype(F32))
    c_out = lax.complex(c_re.astype(F32), c_im.astype(F32))
    g, p = lam.shape[1], lam.shape[2]
    gs = b_bar.shape[-1]
    k = jnp.arange(t + 1, dtype=F32)
    pw = jnp.exp(lam_dt[..., None] * k)
    kk = jnp.einsum('dgcp,dgpk,dgpe->dgkce', c_out, pw[..., :t], b_bar, precision=HIGHEST).real
    kf, kb = kk[0], kk[1]
    zero_lag = kf[:, :1] + kb[:, :1]
    k_all = jnp.concatenate([kb[:, :0:-1], zero_lag, kf[:, 1:]], axis=1)
    jj = jnp.arange(t)
    lag_idx = jj[None, :] - jj[:, None] + (t - 1)
    m_intra = k_all[:, lag_idx]
    m_intra = m_intra.transpose(0, 1, 4, 2, 3).reshape(g, t * gs, t * gs)
    m_intra = m_intra + jnp.eye(t * gs, dtype=F32)[None] * jnp.tile(
        d_skip.astype(F32).reshape(g, 1, gs), (1, t, 1)).reshape(g, 1, t * gs)
    in_f = pw[0][..., t - 1 - jj][..., None] * b_bar[0][:, :, None, :]
    in_b = pw[1][..., jj][..., None] * b_bar[1][:, :, None, :]

    def to_in(z):
        return z.transpose(0, 2, 3, 1).reshape(g, t * gs, p)

    m_in = jnp.concatenate([to_in(in_f.real), to_in(in_f.imag), to_in(in_b.real), to_in(in_b.imag)], axis=-1)
    out_f = c_out[0][:, :, :, None] * pw[0][:, None, :, 1 + jj]
    out_b = c_out[1][:, :, :, None] * pw[1][:, None, :, t - jj]

    def to_out(z):
        return z.transpose(0, 2, 3, 1).reshape(g, p, t * gs)

    m_out = jnp.concatenate([to_out(out_f.real), -to_out(out_f.imag), to_out(out_b.real), -to_out(out_b.imag)],
                            axis=1)
    mu = pw[..., t]
    mre, mim = mu.real, mu.imag
    coef = jnp.stack([
        jnp.concatenate([mre[0], mre[0], mre[1], mre[1]], axis=-1),
        jnp.concatenate([-mim[0], mim[0], -mim[1], mim[1]], axis=-1),
        jnp.concatenate([mim[0], -mim[0], mim[1], -mim[1]], axis=-1)])
    return m_intra.astype(BF16), m_in.astype(BF16), m_out.astype(BF16), coef


def _s5_in_kernel(a_ref, m_ref, o_ref):
    o_ref[...] = _dot(a_ref[...], m_ref[...])


def _s5_chunk_states(a, m_in):
    g, nc, kdim = a.shape
    s = m_in.shape[2]
    nb = min(nc, 1024)
    return pl.pallas_call(
        _s5_in_kernel,
        grid=(g, nc // nb),
        in_specs=[pl.BlockSpec((None, nb, kdim), lambda gi, i: (gi, i, 0)),
                  pl.BlockSpec((None, kdim, s), lambda gi, i: (gi, 0, 0))],
        out_specs=pl.BlockSpec((None, nb, s), lambda gi, i: (gi, i, 0)),
        out_shape=jax.ShapeDtypeStruct((g, nc, s), F32),
        compiler_params=_params(("parallel", "arbitrary")),
        name="s5_chunk_states",
    )(a, m_in)


def _s5_scan_kernel(sf_ref, sb_ref, init_ref, coef_ref, ef_ref, eb_ref, fin_ref, st_ref, *, nblk, half):
    k = pl.program_id(1)
    nk = pl.num_programs(1)
    lanes = 2 * half

    @pl.when(k == 0)
    def _():
        s0f = init_ref[:, 0:lanes]
        s0b = init_ref[:, lanes:2 * lanes]
        st_ref[0] = s0f
        st_ref[1] = pltpu.roll(s0f, half, 1)
        st_ref[2] = s0b
        st_ref[3] = pltpu.roll(s0b, half, 1)

    def body(r, carry):
        sf, sfw, sb, sbw = carry
        rb = nblk - 1 - r
        xf = sf_ref[r]
        xb = sb_ref[rb]
        ef_ref[r] = sf
        eb_ref[rb] = sb
        xfw = pltpu.roll(xf, half, 1)
        xbw = pltpu.roll(xb, half, 1)
        af, bf, bfw = coef_ref[0, :, 0:lanes], coef_ref[1, :, 0:lanes], coef_ref[2, :, 0:lanes]
        ab, bb, bbw = (coef_ref[0, :, lanes:2 * lanes], coef_ref[1, :, lanes:2 * lanes],
                       coef_ref[2, :, lanes:2 * lanes])
        return (sf * af + sfw * bf + xf, sfw * af + sf * bfw + xfw,
                sb * ab + sbw * bb + xb, sbw * ab + sb * bbw + xbw)

    sf, sfw, sb, sbw = lax.fori_loop(0, nblk, body, (st_ref[0], st_ref[1], st_ref[2], st_ref[3]))
    st_ref[0] = sf
    st_ref[1] = sfw
    st_ref[2] = sb
    st_ref[3] = sbw

    @pl.when(k == nk - 1)
    def _():
        fin_ref[:, 0:lanes] = sf
        fin_ref[:, lanes:2 * lanes] = sb


def _s5_scan(s_t, init, coef):
    bsz, nc, g, s4 = s_t.shape
    lanes = s4 // 2
    nblk = min(nc, 128)
    nk = nc // nblk
    return pl.pallas_call(
        functools.partial(_s5_scan_kernel, nblk=nblk, half=lanes // 2),
        grid=(bsz, nk),
        in_specs=[pl.BlockSpec((None, nblk, g, lanes), lambda b, k: (b, k, 0, 0)),
                  pl.BlockSpec((None, nblk, g, lanes), lambda b, k: (b, nk - 1 - k, 0, 1)),
                  pl.BlockSpec((None, g, s4), lambda b, k: (b, 0, 0)),
                  pl.BlockSpec((3, g, s4), lambda b, k: (0, 0, 0))],
        out_specs=[pl.BlockSpec((None, nblk, g, lanes), lambda b, k: (b, k, 0, 0)),
                   pl.BlockSpec((None, nblk, g, lanes), lambda b, k: (b, nk - 1 - k, 0, 0)),
                   pl.BlockSpec((None, g, s4), lambda b, k: (b, 0, 0))],
        out_shape=[jax.ShapeDtypeStruct((bsz, nc, g, lanes), F32),
                   jax.ShapeDtypeStruct((bsz, nc, g, lanes), F32),
                   jax.ShapeDtypeStruct((bsz, g, s4), F32)],
        scratch_shapes=[pltpu.VMEM((4, g, lanes), F32)],
        compiler_params=_params(("parallel", "arbitrary")),
        name="s5_scan",
    )(s_t, s_t, init, coef)


def _s5_out_kernel(a_ref, e_ref, mi_ref, mo_ref, o_ref):
    y = _dot(a_ref[...], mi_ref[...]) + _dot(e_ref[...], mo_ref[...])
    o_ref[...] = y.astype(o_ref.dtype)


def _s5_chunk_outputs(a, e_in, m_intra, m_out):
    g, nc, kdim = a.shape
    s = e_in.shape[2]
    nb = min(nc, 1024)
    return pl.pallas_call(
        _s5_out_kernel,
        grid=(g, nc // nb),
        in_specs=[pl.BlockSpec((None, nb, kdim), lambda gi, i: (gi, i, 0)),
                  pl.BlockSpec((None, nb, s), lambda gi, i: (gi, i, 0)),
                  pl.BlockSpec((None, kdim, kdim), lambda gi, i: (gi, 0, 0)),
                  pl.BlockSpec((None, s, kdim), lambda gi, i: (gi, 0, 0))],
        out_specs=pl.BlockSpec((None, nb, kdim), lambda gi, i: (gi, i, 0)),
        out_shape=jax.ShapeDtypeStruct((g, nc, kdim), BF16),
        compiler_params=_params(("parallel", "arbitrary")),
        name="s5_chunk_outputs",
    )(a, e_in, m_intra, m_out)


def _s5_readout_kernel(y_ref, wg_ref, o_ref):
    y = jax.nn.gelu(y_ref[...].astype(F32))
    gate = _dot(y.astype(BF16), wg_ref[...])
    o_ref[...] = (y * jax.nn.sigmoid(gate)).astype(o_ref.dtype)


def _s5_readout(y2d, wg):
    n, w = y2d.shape
    tm = min(n, 1024)
    return pl.pallas_call(
        _s5_readout_kernel,
        grid=(n // tm,),
        in_specs=[pl.BlockSpec((tm, w), lambda i: (i, 0)),
                  pl.BlockSpec((w, w), lambda i: (0, 0))],
        out_specs=pl.BlockSpec((tm, w), lambda i: (i, 0)),
        out_shape=jax.ShapeDtypeStruct((n, w), BF16),
        compiler_params=_params(("parallel",)),
        name="s5_readout",
    )(y2d, wg)


def _to_chunks(u2d, g, t):
    n = u2d.shape[0]
    return u2d.reshape(n // t, t, g, S5_GROUP).transpose(2, 0, 1, 3).reshape(g, n // t, t * S5_GROUP)


def _from_chunks(y, g, t):
    nc = y.shape[1]
    return y.reshape(g, nc, t, S5_GROUP).transpose(1, 2, 0, 3).reshape(nc * t, g * S5_GROUP)


def _s5_branch(ua, uca, bsz, lam_re, lam_im, log_dt, b_re, b_im, c_re, c_im, d_skip, w_glu):
    g = lam_re.shape[1]
    t = S5_CHUNK
    m_intra, m_in, m_out, coef = _s5_matrices(lam_re, lam_im, log_dt, b_re, b_im, c_re, c_im, d_skip, t)
    s4 = m_in.shape[2]

    def states(u2d):
        a = _to_chunks(u2d, g, t)
        s = _s5_chunk_states(a, m_in)
        nc = s.shape[1] // bsz
        return a, s.reshape(g, bsz, nc, s4).transpose(1, 2, 0, 3)

    _, s_ctx = states(uca)
    _, _, seed = _s5_scan(s_ctx, jnp.zeros((bsz, g, s4), F32), coef)
    a_lat, s_lat = states(ua)
    e_f, e_b, _ = _s5_scan(s_lat, seed, coef)
    e_in = jnp.concatenate([e_f, e_b], axis=-1).astype(BF16)
    e_in = e_in.transpose(2, 0, 1, 3).reshape(g, -1, s4)
    y = _s5_chunk_outputs(a_lat, e_in, m_intra, m_out)
    return _s5_readout(_from_chunks(y, g, t), w_glu.astype(BF16))


HY_KH = GRID_W + 1
HY_KP = 72
HY_QN = 16


def _hy_phase(w_cols, r_cnt):
    n1_tot = 2 * GRID_W
    n_fft = n1_tot * r_cnt
    kh = GRID_W + 1
    k1 = np.arange(kh, dtype=np.int64)[None, :, None]
    r = np.arange(r_cnt, dtype=np.int64)[:, None, None]
    n1 = np.arange(w_cols, dtype=np.int64)[None, None, :]
    return 2.0 * np.pi * ((k1 * (r_cnt * n1 + r)) % n_fft) / n_fft


def _hy_fwd_table(r_cnt, m):
    th = _hy_phase(GRID_W * m, r_cnt)
    kh = th.shape[1]
    f = np.zeros((r_cnt, 2 * HY_KP, GRID_W * m), np.float32)
    f[:, :kh] = np.cos(th)
    f[:, HY_KP:HY_KP + kh] = -np.sin(th)
    return jnp.asarray(f, BF16)


def _hy_inv_table(r_cnt):
    th = _hy_phase(GRID_W, r_cnt)
    kh = th.shape[1]
    c = np.full((kh,), 2.0)
    c[0] = c[-1] = 1.0
    scale = c[None, :, None] / (2 * GRID_W * r_cnt)
    g = np.zeros((r_cnt, GRID_W, 2 * HY_KP), np.float32)
    g[:, :, :kh] = (scale * np.cos(th)).transpose(0, 2, 1)
    g[:, :, HY_KP:HY_KP + kh] = (-scale * np.sin(th)).transpose(0, 2, 1)
    return jnp.asarray(g, BF16)


def _hy_dft_tables(r_cnt):
    k = np.arange(r_cnt, dtype=np.int64)
    th = 2.0 * np.pi * ((k[:, None] * k[None, :]) % r_cnt) / r_cnt
    fc, fs = np.cos(th), np.sin(th)
    fwd = np.block([[fc, fs], [-fs, fc]]).astype(np.float32)
    inv = np.block([[fc, -fs], [fs, fc]]).astype(np.float32)
    return jnp.asarray(fwd, BF16), jnp.asarray(inv, BF16)


def _hy_short_kernel(x_ref, w_ref, o_ref, *, rows, chunk):
    wk = w_ref[...]
    w0, w1, w2 = wk[0:1], wk[1:2], wk[2:3]
    gw = GRID_W
    col = lax.broadcasted_iota(jnp.int32, (gw, x_ref.shape[1]), 0)

    def piece(a, n):
        return x_ref[pl.ds(a, n), :].astype(F32)

    last = piece(rows - gw, gw)
    prev0 = jnp.where(col == 0, 0.0, pltpu.roll(last, 1, 0))
    o_ref[0:gw, :] = (w0 * prev0 + w1 * piece(0, gw) + w2 * piece(gw, gw)).astype(o_ref.dtype)
    first = piece(0, gw)
    next_l = jnp.where(col == gw - 1, 0.0, pltpu.roll(first, gw - 1, 0))
    o_ref[rows - gw:rows, :] = (w0 * piece(rows - 2 * gw, gw) + w1 * last + w2 * next_l).astype(o_ref.dtype)

    def body(i, carry):
        a = pl.multiple_of(gw + i * chunk, gw)
        o_ref[pl.ds(a, chunk), :] = (w0 * piece(a - gw, chunk) + w1 * piece(a, chunk)
                                     + w2 * piece(a + gw, chunk)).astype(o_ref.dtype)
        return carry

    n_full = (rows - 2 * gw) // chunk
    lax.fori_loop(0, n_full, body, 0)
    rem = rows - 2 * gw - n_full * chunk
    if rem:
        a = gw + n_full * chunk
        o_ref[a:a + rem, :] = (w0 * piece(a - gw, rem) + w1 * piece(a, rem)
                               + w2 * piece(a + gw, rem)).astype(o_ref.dtype)


def _hy_short_conv(uh, short_w, bsz, seq):
    n, ch = uh.shape
    ct = min(ch, LANE)
    chunk = min(512, seq - 2 * GRID_W)
    return pl.pallas_call(
        functools.partial(_hy_short_kernel, rows=seq, chunk=chunk),
        grid=(bsz, ch // ct),
        in_specs=[pl.BlockSpec((seq, ct), lambda b, c: (b, c)),
                  pl.BlockSpec((3, ct), lambda b, c: (0, c))],
        out_specs=pl.BlockSpec((seq, ct), lambda b, c: (b, c)),
        out_shape=jax.ShapeDtypeStruct((n, ch), BF16),
        compiler_params=_params(("parallel", "parallel")),
        name="hy_short_conv",
    )(uh, short_w)


def _hy_fwd1_kernel(*refs, m):
    x_refs, f_ref, o_ref, s_ref = refs[:m], refs[m], refs[m + 1], refs[m + 2]
    gw = GRID_W
    for q in range(HY_QN):
        xq = [x[q * gw:(q + 1) * gw, :] for x in x_refs]
        xq = xq[0] if m == 1 else jnp.concatenate(xq, axis=0)
        s_ref[q] = _dot(f_ref[q], xq)
    t = jnp.swapaxes(s_ref[...], 0, 1).astype(o_ref.dtype)
    o_ref[0] = t[0:HY_KH]
    o_ref[1] = t[HY_KP:HY_KP + HY_KH]


def _hy_fwd1(xs, col_blk0, width, bsz, r_cnt, ct):
    m = len(xs)
    table = _hy_fwd_table(r_cnt, m)
    rb = HY_QN * GRID_W
    nrb = r_cnt // HY_QN
    x_spec = pl.BlockSpec((rb, ct), lambda b, i, c: (b * nrb + i, col_blk0 + c))
    return pl.pallas_call(
        functools.partial(_hy_fwd1_kernel, m=m),
        grid=(bsz, nrb, width // ct),
        in_specs=[x_spec] * m + [pl.BlockSpec((HY_QN, 2 * HY_KP, GRID_W * m), lambda b, i, c: (i, 0, 0))],
        out_specs=pl.BlockSpec((None, 2, HY_KH, HY_QN, ct), lambda b, i, c: (b, 0, 0, i, c)),
        out_shape=jax.ShapeDtypeStruct((bsz, 2, HY_KH, r_cnt, width), BF16),
        scratch_shapes=[pltpu.VMEM((HY_QN, 2 * HY_KP, ct), F32)],
        compiler_params=_params(("parallel", "parallel", "arbitrary")),
        name="hy_fwd_stage1",
    )(*xs, table)


def _hy_spec_kernel(a_ref, f2_ref, o_ref):
    r = a_ref.shape[1]
    x = a_ref[...].reshape(2 * r, a_ref.shape[2])
    o_ref[...] = _dot(f2_ref[...], x).reshape(o_ref.shape)


def _hy_filter_spectrum(a, f2, ct):
    _, _, kh, r_cnt, width = a.shape
    return pl.pallas_call(
        _hy_spec_kernel,
        grid=(kh, width // ct),
        in_specs=[pl.BlockSpec((None, 2, None, r_cnt, ct), lambda k, c: (0, 0, k, 0, c)),
                  pl.BlockSpec((2 * r_cnt, 2 * r_cnt), lambda k, c: (0, 0))],
        out_specs=pl.BlockSpec((2, None, r_cnt, ct), lambda k, c: (0, k, 0, c)),
        out_shape=jax.ShapeDtypeStruct((2, kh, r_cnt, width), F32),
        compiler_params=_params(("parallel", "parallel")),
        name="hy_filter_spectrum",
    )(a, f2)


def _hy_mid_kernel(a_ref, kf_ref, invn_ref, f2_ref, f2i_ref, o_ref):
    r = a_ref.shape[1]
    x = a_ref[...].reshape(2 * r, a_ref.shape[2])
    b = _dot(f2_ref[...], x)
    br, bi = b[:r], b[r:]
    kr, ki = kf_ref[0], kf_ref[1]
    s = invn_ref[...]
    cr = (br * kr - bi * ki) * s
    ci = (br * ki + bi * kr) * s
    c = jnp.concatenate([cr, ci], axis=0).astype(BF16)
    o_ref[...] = _dot(f2i_ref[...], c).reshape(o_ref.shape).astype(o_ref.dtype)


def _hy_mid(a, kf, invn, order, f2, f2i, ct):
    bsz, _, kh, r_cnt, width = a.shape
    nct = width // ct
    return pl.pallas_call(
        _hy_mid_kernel,
        grid=(kh, nct, bsz),
        in_specs=[pl.BlockSpec((None, 2, None, r_cnt, ct), lambda k, c, b: (b, 0, k, 0, c)),
                  pl.BlockSpec((2, None, r_cnt, ct), lambda k, c, b: (0, k, 0, order * nct + c)),
                  pl.BlockSpec((1, ct), lambda k, c, b: (0, order * nct + c)),
                  pl.BlockSpec((2 * r_cnt, 2 * r_cnt), lambda k, c, b: (0, 0)),
                  pl.BlockSpec((2 * r_cnt, 2 * r_cnt), lambda k, c, b: (0, 0))],
        out_specs=pl.BlockSpec((None, 2, None, r_cnt, ct), lambda k, c, b: (b, 0, k, 0, c)),
        out_shape=jax.ShapeDtypeStruct(a.shape, BF16),
        compiler_params=_params(("parallel", "parallel", "arbitrary")),
        name="hy_spectrum_product",
    )(a, kf, invn, f2, f2i)


def _hy_inv1_kernel(d_ref, g_ref, xin_ref, xm_ref, bias_ref, o_ref, t_ref, s_ref):
    gw = GRID_W
    pad = jnp.zeros((HY_KP - HY_KH,) + t_ref.shape[1:], F32)
    t_ref[0:HY_KH] = d_ref[0].astype(F32)
    t_ref[HY_KH:HY_KP] = pad
    t_ref[HY_KP:HY_KP + HY_KH] = d_ref[1].astype(F32)
    t_ref[HY_KP + HY_KH:2 * HY_KP] = pad
    s_ref[...] = jnp.swapaxes(t_ref[...], 0, 1)
    for q in range(HY_QN):
        y = _dot(g_ref[q], s_ref[q].astype(BF16))
        rows = slice(q * gw, (q + 1) * gw)
        xin = xin_ref[rows, :].astype(F32)
        o_ref[rows, :] = (xm_ref[rows, :].astype(F32) * (y + xin * bias_ref[...])).astype(o_ref.dtype)


def _hy_inv1(d, xin, xin_blk0, xm, xm_blk0, bias, ct):
    bsz, _, kh, r_cnt, width = d.shape
    table = _hy_inv_table(r_cnt)
    rb = HY_QN * GRID_W
    nrb = r_cnt // HY_QN
    return pl.pallas_call(
        _hy_inv1_kernel,
        grid=(bsz, nrb, width // ct),
        in_specs=[pl.BlockSpec((None, 2, kh, HY_QN, ct), lambda b, i, c: (b, 0, 0, i, c)),
                  pl.BlockSpec((HY_QN, GRID_W, 2 * HY_KP), lambda b, i, c: (i, 0, 0)),
                  pl.BlockSpec((rb, ct), lambda b, i, c: (b * nrb + i, xin_blk0 + c)),
                  pl.BlockSpec((rb, ct), lambda b, i, c: (b * nrb + i, xm_blk0 + c)),
                  pl.BlockSpec((1, ct), lambda b, i, c: (0, c))],
        out_specs=pl.BlockSpec((rb, ct), lambda b, i, c: (b * nrb + i, c)),
        out_shape=jax.ShapeDtypeStruct((bsz * r_cnt * GRID_W, width), BF16),
        scratch_shapes=[pltpu.VMEM((2 * HY_KP, HY_QN, ct), F32), pltpu.VMEM((HY_QN, 2 * HY_KP, ct), F32)],
        compiler_params=_params(("parallel", "parallel", "arbitrary")),
        name="hy_inv_stage1",
    )(d, table, xin, xm, bias)


def _hy_filter_kernel(w1_ref, b1_ref, f1_ref, w2_ref, b2_ref, f2_ref, w3_ref, b3_ref, dec_ref, om_ref, ph_ref,
                      o_ref, sum_ref, *, direction, seqlen, r_cnt, rows):
    i = pl.program_id(1)
    row = lax.broadcasted_iota(jnp.int32, (rows, LANE), 0) + i * rows
    lane = lax.broadcasted_iota(jnp.int32, (rows, LANE), 1)
    r = jnp.right_shift(row, GRID_W.bit_length() - 1)
    w = jnp.bitwise_and(row, GRID_W - 1)
    n = (w + GRID_W * direction) * r_cnt + r
    p = n if direction == 0 else 2 * seqlen - n
    pf = p.astype(F32)
    t = pf / (seqlen - 1)
    feats = jnp.where(lane == 0, t, jnp.cos(pf * om_ref[...] + ph_ref[...]))
    hid = jnp.sin(f1_ref[...] * (jnp.dot(feats, w1_ref[...], preferred_element_type=F32, precision=HIGHEST)
                                 + b1_ref[...]))
    hid = jnp.sin(f2_ref[...] * (jnp.dot(hid, w2_ref[...], preferred_element_type=F32, precision=HIGHEST)
                                 + b2_ref[...]))
    filt = _dot(hid.astype(BF16), w3_ref[...].astype(BF16)) + b3_ref[...]
    reps = filt.shape[1] // LANE
    t_w = jnp.concatenate([t] * reps, axis=1)
    keep = jnp.concatenate([jnp.where(p < seqlen, 1.0, 0.0)] * reps, axis=1)
    filt = filt * jnp.exp(-t_w * jnp.abs(dec_ref[...])) * keep
    o_ref[...] = filt.astype(o_ref.dtype)

    @pl.when(i == 0)
    def _():
        sum_ref[...] = jnp.zeros_like(sum_ref)

    sum_ref[...] += jnp.sum(jnp.abs(filt), axis=0, keepdims=True)


def _hy_filter_signal(direction, seqlen, w1p, b1, f1, w2, b2, f2, w3, b3, decay, omega, phase, width):
    r_cnt = seqlen // GRID_W
    rows = min(256, seqlen)
    ct = width
    nct = width // ct
    hid = w2.shape[0]
    off = direction * nct
    const = lambda shape: pl.BlockSpec(shape, lambda c, i: (0, 0))
    colv = pl.BlockSpec((1, ct), lambda c, i: (0, off + c))
    return pl.pallas_call(
        functools.partial(_hy_filter_kernel, direction=direction, seqlen=seqlen, r_cnt=r_cnt, rows=rows),
        grid=(nct, seqlen // rows),
        in_specs=[const((LANE, hid)), const((1, hid)), const((1, hid)), const((hid, hid)), const((1, hid)),
                  const((1, hid)), pl.BlockSpec((hid, ct), lambda c, i: (0, off + c)), colv, colv,
                  const((1, LANE)), const((1, LANE))],
        out_specs=[pl.BlockSpec((rows, ct), lambda c, i: (i, c)),
                   pl.BlockSpec((1, ct), lambda c, i: (0, c))],
        out_shape=[jax.ShapeDtypeStruct((seqlen, width), BF16), jax.ShapeDtypeStruct((1, width), F32)],
        compiler_params=_params(("parallel", "arbitrary")),
        name="hy_filter_signal",
    )(w1p, b1, f1, w2, b2, f2, w3, b3, decay, omega, phase)


def _hyena_branch(uh, bsz, seq, short_w, w1, b1, f1, w2, b2, f2, w3, b3, decay, hy_bias):
    width = hy_bias.shape[1]
    r_cnt = seq // GRID_W
    ct = min(width, 256)
    nct = width // ct
    n_emb = w1.shape[0]
    n_bands = (n_emb - 1) // 2
    hid = w1.shape[1]
    bands = jnp.linspace(1e-4, n_bands - 1, n_bands, dtype=F32)
    omega = jnp.zeros((1, LANE), F32).at[0, 1:1 + n_bands].set((2 * jnp.pi / seq) * bands)
    omega = omega.at[0, 1 + n_bands:1 + 2 * n_bands].set((2 * jnp.pi / seq) * bands)
    phase = jnp.zeros((1, LANE), F32).at[0, 1 + n_bands:1 + 2 * n_bands].set(0.5 * jnp.pi)
    w1p = jnp.zeros((LANE, hid), F32).at[:n_emb].set(w1)
    sigs, sums = [], []
    for direction in range(2):
        sig, ssum = _hy_filter_signal(direction, seq, w1p, b1[None], f1[None], w2, b2[None], f2[None], w3,
                                      b3[None], decay[None], omega, phase, HY_ORDER * width)
        sigs.append(sig)
        sums.append(ssum)
    invn = 1.0 / (sums[0] + sums[1] + HY_NORM_EPS)
    f2m, f2i = _hy_dft_tables(r_cnt)
    ct3 = min(width, 1024)
    kf = _hy_filter_spectrum(_hy_fwd1(sigs, 0, HY_ORDER * width, 1, r_cnt, ct), f2m, ct3)

    z = _hy_short_conv(uh, short_w.astype(F32), bsz, seq)
    a = _hy_fwd1([z], 0, width, bsz, r_cnt, ct)
    d = _hy_mid(a, kf, invn, 0, f2m, f2i, ct3)
    y1 = _hy_inv1(d, z, 0, z, nct, hy_bias[0:1].astype(F32), ct)
    a = _hy_fwd1([y1], 0, width, bsz, r_cnt, ct)
    d = _hy_mid(a, kf, invn, 1, f2m, f2i, ct3)
    return _hy_inv1(d, y1, 0, z, 2 * nct, hy_bias[1:2].astype(F32), ct)


def _merge1_kernel(ya_ref, yh_ref, ug_ref, wa_ref, wh_ref, o_ref):
    d = o_ref.shape[1]
    a = _dot(ya_ref[...], wa_ref[...])
    h = _dot(yh_ref[...], wh_ref[...])
    ga = ug_ref[:, :d].astype(F32)
    gh = ug_ref[:, d:].astype(F32)
    o_ref[...] = (ga * a + gh * h).astype(o_ref.dtype)


def _merge1(ya, yh, ug, wa, wh):
    n, ws = ya.shape
    wh_in = yh.shape[1]
    d = wa.shape[1]
    tm = min(n, 512)
    return pl.pallas_call(
        _merge1_kernel,
        grid=(n // tm,),
        in_specs=[pl.BlockSpec((tm, ws), lambda i: (i, 0)),
                  pl.BlockSpec((tm, wh_in), lambda i: (i, 0)),
                  pl.BlockSpec((tm, 2 * d), lambda i: (i, 0)),
                  pl.BlockSpec((ws, d), lambda i: (0, 0)),
                  pl.BlockSpec((wh_in, d), lambda i: (0, 0))],
        out_specs=pl.BlockSpec((tm, d), lambda i: (i, 0)),
        out_shape=jax.ShapeDtypeStruct((n, d), BF16),
        compiler_params=_params(("parallel",)),
        name="merge_branches",
    )(ya, yh, ug, wa, wh)


def _merge2_kernel(m_ref, x_ref, g1_ref, ng_ref, sh_ref, sc_ref, wo_ref, wr_ref,
                   h1_ref, hm_ref, r1_ref, r2_ref):
    mix = _dot(m_ref[...], wo_ref[...])
    h1 = x_ref[...] + g1_ref[...] * mix
    h1_ref[...] = h1
    y = h1 * lax.rsqrt(jnp.mean(h1 * h1, axis=-1, keepdims=True) + NORM_EPS)
    y = y * ng_ref[...]
    y = y * (1.0 + sc_ref[...]) + sh_ref[...]
    hi = y.astype(BF16)
    lo = (y - hi.astype(F32)).astype(BF16)
    n_sub = y.shape[1] // LANE
    for s in range(n_sub):
        hm_ref[pl.ds(s, y.shape[0], stride=n_sub), :] = y[:, s * LANE:(s + 1) * LANE]
    r1_ref[...] = _dot(hi, wr_ref[...])
    r2_ref[...] = _dot(lo, wr_ref[...])


def _merge2(m, x2d, g1, ng, sh2, sc2, wo, wr, rows_per_mod):
    n, d = x2d.shape
    tm = min(n, 512)
    tiles_per_mod = rows_per_mod // tm
    nr = wr.shape[1]
    mod_spec = pl.BlockSpec((None, 1, d), lambda i: (i // tiles_per_mod, 0, 0))
    return pl.pallas_call(
        _merge2_kernel,
        grid=(n // tm,),
        in_specs=[pl.BlockSpec((tm, d), lambda i: (i, 0)),
                  pl.BlockSpec((tm, d), lambda i: (i, 0)),
                  mod_spec,
                  pl.BlockSpec((1, d), lambda i: (0, 0)),
                  mod_spec, mod_spec,
                  pl.BlockSpec((d, d), lambda i: (0, 0)),
                  pl.BlockSpec((d, nr), lambda i: (0, 0))],
        out_specs=[pl.BlockSpec((tm, d), lambda i: (i, 0)),
                   pl.BlockSpec((tm * (d // LANE), LANE), lambda i: (i, 0)),
                   pl.BlockSpec((tm, nr), lambda i: (i, 0)),
                   pl.BlockSpec((tm, nr), lambda i: (i, 0))],
        out_shape=[jax.ShapeDtypeStruct((n, d), F32),
                   jax.ShapeDtypeStruct((n * (d // LANE), LANE), F32),
                   jax.ShapeDtypeStruct((n, nr), F32),
                   jax.ShapeDtypeStruct((n, nr), F32)],
        compiler_params=_params(("parallel",)),
        name="out_proj_norm_router",
    )(m, x2d, g1, ng, sh2, sc2, wo, wr)


def _moe_kernel(be_ref, nv_ref, tok0_ref, tokn_ref, hm_ref, wgu_ref, bgu_ref, wd_ref, bd_ref, o_ref,
                xbuf_ref, acc_ref, sem):
    i = pl.program_id(0)
    j = pl.program_id(1)
    nj = pl.num_programs(1)
    nv = nv_ref[0]
    tm = acc_ref.shape[0]

    n_sub = hm_ref.shape[1]

    def row_copy(tok, r, slot):
        return pltpu.make_async_copy(hm_ref.at[tok], xbuf_ref.at[slot, pl.ds(r * MOE_PITCH, n_sub), :],
                                     sem.at[slot])

    def wait_block(slot):
        for r in range(tm):
            row_copy(0, r, slot).wait()

    @pl.when((i == 0) & (j == 0) & (nv > 0))
    def _():
        for r in range(tm):
            row_copy(tok0_ref[0, r], r, 0).start()

    @pl.when((i < nv) & (j == 0))
    def _():
        wait_block(i % 2)
        acc_ref[...] = jnp.zeros_like(acc_ref)

    def expert_step(request_next):
        if request_next:
            for r in range(tm):
                row_copy(tokn_ref[0, r], r, (i + 1) % 2).start()
        slot = i % 2
        x = jnp.concatenate([xbuf_ref[slot, pl.ds(s, tm, stride=MOE_PITCH), :].astype(BF16)
                             for s in range(n_sub)], axis=1)
        gu = _dot(x, wgu_ref[...]) + bgu_ref[...]
        acts = []
        for b in range(gu.shape[1] // (2 * LANE)):
            glu = jnp.minimum(gu[:, 2 * b * LANE:(2 * b + 1) * LANE], SWIGLU_LIMIT)
            lin = jnp.clip(gu[:, (2 * b + 1) * LANE:(2 * b + 2) * LANE], -SWIGLU_LIMIT, SWIGLU_LIMIT)
            acts.append((glu * jax.nn.sigmoid(SWIGLU_ALPHA * glu) * (lin + 1.0)).astype(BF16))
        acc_ref[...] += _dot(jnp.concatenate(acts, axis=1), wd_ref[...])

    @pl.when((i < nv) & (j == 0))
    def _():
        expert_step(True)

    @pl.when((i < nv) & (j > 0))
    def _():
        expert_step(False)

    @pl.when((i < nv) & (j == nj - 1))
    def _():
        o_ref[...] = (acc_ref[...] + bd_ref[...]).astype(o_ref.dtype)

    @pl.when((i == nv - 1) & (j == nj - 1))
    def _():
        wait_block((i + 1) % 2)

    @pl.when((i >= nv) & (j == nj - 1))
    def _():
        o_ref[...] = jnp.zeros_like(o_ref)


def _moe_blocks(hm, slot_tok, block_expert, n_valid, wgu, bgu, wd, bd):
    d = wd.shape[2]
    assert hm.shape[1] <= MOE_PITCH and hm.shape[1] * hm.shape[2] == d
    f = wd.shape[1]
    tm = MOE_TM
    tf = min(f, MOE_TF)
    nj = f // tf
    nblk = block_expert.shape[0]

    def jeff(i, j, nv):
        return jnp.where(i < nv[0], j, nj - 1)

    grid_spec = pltpu.PrefetchScalarGridSpec(
        num_scalar_prefetch=2,
        grid=(nblk, nj),
        in_specs=[pl.BlockSpec((None, 1, tm), lambda i, j, be, nv: (0, 0, 0), memory_space=pltpu.SMEM),
                  pl.BlockSpec((None, 1, tm), lambda i, j, be, nv: (jnp.minimum(i + 1, nblk - 1), 0, 0),
                               memory_space=pltpu.SMEM),
                  pl.BlockSpec(memory_space=pl.ANY),
                  pl.BlockSpec((None, d, 2 * tf), lambda i, j, be, nv: (be[i], 0, jeff(i, j, nv))),
                  pl.BlockSpec((None, 1, 2 * tf), lambda i, j, be, nv: (be[i], 0, jeff(i, j, nv))),
                  pl.BlockSpec((None, tf, d), lambda i, j, be, nv: (be[i], jeff(i, j, nv), 0)),
                  pl.BlockSpec((None, 1, d), lambda i, j, be, nv: (be[i], 0, 0))],
        out_specs=pl.BlockSpec((tm, d), lambda i, j, be, nv: (i, 0)),
        scratch_shapes=[pltpu.VMEM((2, tm * MOE_PITCH, LANE), F32), pltpu.VMEM((tm, d), F32),
                        pltpu.SemaphoreType.DMA((2,))])
    return pl.pallas_call(
        _moe_kernel,
        grid_spec=grid_spec,
        out_shape=jax.ShapeDtypeStruct((nblk * tm, d), BF16),
        compiler_params=_params(("arbitrary", "arbitrary")),
        name="moe_experts",
    )(block_expert, n_valid, slot_tok, slot_tok, hm, wgu, bgu, wd, bd)


def _route(logits, n_experts):
    n_tok = logits.shape[0]
    top_val, top_idx = lax.top_k(logits, TOP_K)
    gates = jax.nn.softmax(top_val, axis=-1)
    n_assign = n_tok * TOP_K
    flat_e = top_idx.reshape(-1).astype(jnp.int32)
    experts = jnp.arange(n_experts, dtype=jnp.int32)
    onehot = (flat_e[:, None] == experts[None, :]).astype(jnp.int32)
    csum = jnp.cumsum(onehot, axis=0)
    rank = jnp.sum(onehot * csum, axis=1) - 1
    counts = csum[-1]
    padded = (counts + MOE_TM - 1) // MOE_TM * MOE_TM
    start = jnp.cumsum(counts) - counts
    padded_end = jnp.cumsum(padded)
    padded_start = padded_end - padded
    dest = padded_start[flat_e] + rank
    n_blocks = -(-n_assign // MOE_TM) + n_experts
    cap = n_blocks * MOE_TM
    n_valid = (padded_end[-1] // MOE_TM).astype(jnp.int32)
    order = jnp.argsort(flat_e)
    slots = jnp.arange(cap, dtype=jnp.int32)
    slot_e = jnp.minimum(jnp.sum((padded_end[None, :] <= slots[:, None]).astype(jnp.int32), axis=1), n_experts - 1)
    slot_rank = slots - padded_start[slot_e]
    src = jnp.minimum(start[slot_e] + slot_rank, n_assign - 1)
    slot_tok = jnp.where(slot_rank < counts[slot_e], order[src] // TOP_K, 0).astype(jnp.int32)
    block_start = jnp.minimum(jnp.arange(n_blocks, dtype=jnp.int32), n_valid - 1) * MOE_TM
    block_expert = jnp.minimum(
        jnp.sum((padded_end[None, :] <= block_start[:, None]).astype(jnp.int32), axis=1), n_experts - 1)
    return gates, slot_tok, dest, block_expert, n_valid


def _final_kernel(h1_ref, *rest):
    yg_refs = rest[:TOP_K]
    gates_ref, g2_ref, fg_ref, o_ref = rest[TOP_K:]
    gates = gates_ref[...]
    moe = yg_refs[0][...].astype(F32) * gates[:, 0:1]
    for k in range(1, TOP_K):
        moe = moe + yg_refs[k][...].astype(F32) * gates[:, k:k + 1]
    h = h1_ref[...] + g2_ref[...] * moe
    y = h * lax.rsqrt(jnp.mean(h * h, axis=-1, keepdims=True) + NORM_EPS)
    o_ref[...] = (y * fg_ref[...]).astype(o_ref.dtype)


def _final(h1, yg, gates, g2, fg, rows_per_mod):
    n, d = h1.shape
    tm = min(n, 512)
    tiles_per_mod = rows_per_mod // tm
    nt = n // tm
    yg_specs = [pl.BlockSpec((tm, d), functools.partial(lambda i, k: (k * nt + i, 0), k=k)) for k in range(TOP_K)]
    return pl.pallas_call(
        _final_kernel,
        grid=(nt,),
        in_specs=[pl.BlockSpec((tm, d), lambda i: (i, 0))] + yg_specs + [
                  pl.BlockSpec((tm, TOP_K), lambda i: (i, 0)),
                  pl.BlockSpec((None, 1, d), lambda i: (i // tiles_per_mod, 0, 0)),
                  pl.BlockSpec((1, d), lambda i: (0, 0))],
        out_specs=pl.BlockSpec((tm, d), lambda i: (i, 0)),
        out_shape=jax.ShapeDtypeStruct((n, d), F32),
        compiler_params=_params(("parallel",)),
        name="combine_final_norm",
    )(h1, *([yg] * TOP_K), gates, g2, fg)


def kernel(x, c, ctx, c_ctx, w_ada, b_ada, norm1_g, norm2_g, w_in, s5_lam_re, s5_lam_im, s5_log_dt, s5_b_re, s5_b_im, s5_c_re, s5_c_im, s5_d, s5_w_glu, hy_short_w, hy_pos_w1, hy_pos_b1, hy_freq1, hy_pos_w2, hy_pos_b2, hy_freq2, hy_pos_w3, hy_pos_b3, hy_decay, hy_bias, w_branch_s5, w_branch_hy, w_out, router_w, router_b, w_gate_up, b_gate_up, w_down, b_down, final_g):
    bsz, seq, d = x.shape
    ctx_len = ctx.shape[1]
    depth = w_ada.shape[0]
    assert depth == 1, "only the single-layer configuration is implemented"
    n_experts = router_w.shape[2]
    s5_w = s5_d.shape[1]
    hy_w = hy_bias.shape[2]
    n_tok = bsz * seq

    cc = jnp.zeros((8, d), F32).at[:bsz].set(c).at[bsz].set(c_ctx)
    mod = _ada_mod(cc, w_ada[0], b_ada[0][None]).reshape(8, 6, 1, d)
    sh1, sc1, g1, sh2, sc2, g2 = [mod[:bsz, k] for k in range(6)]
    csh1, csc1 = mod[bsz:bsz + 1, 0], mod[bsz:bsz + 1, 1]

    w_in_b = w_in[0].astype(BF16)
    x2d = x.reshape(n_tok, d)
    riders = ((w_gate_up[0].reshape(-1, w_gate_up.shape[3]), True), (w_down[0].reshape(-1, d), False))
    ua, uh, ug, wgu_b, wd_b = _norm_mod_proj(x2d, norm1_g[0][None], sh1, sc1, w_in_b,
                                             (s5_w, 3 * hy_w, 2 * d), (False, False, True), seq, 1024, 512,
                                             riders=riders)
    (uca,) = _norm_mod_proj(ctx.reshape(bsz * ctx_len, d), norm1_g[0][None], csh1, csc1,
                            w_in_b[:, :s5_w], (s5_w,), (False,), bsz * ctx_len, 512, 1024)

    ya = _s5_branch(ua, uca, bsz, s5_lam_re[0], s5_lam_im[0], s5_log_dt[0], s5_b_re[0], s5_b_im[0],
                    s5_c_re[0], s5_c_im[0], s5_d[0], s5_w_glu[0])

    yh = _hyena_branch(uh, bsz, seq, hy_short_w[0], hy_pos_w1[0], hy_pos_b1[0], hy_freq1[0], hy_pos_w2[0],
                       hy_pos_b2[0], hy_freq2[0], hy_pos_w3[0], hy_pos_b3[0], hy_decay[0], hy_bias[0])

    m = _merge1(ya, yh, ug, w_branch_s5[0].astype(BF16), w_branch_hy[0].astype(BF16))
    rw = router_w[0]
    rw_hi = rw.astype(BF16)
    rw_lo = (rw - rw_hi.astype(F32)).astype(BF16)
    wr = jnp.zeros((d, 128), BF16).at[:, :n_experts].set(rw_hi).at[:, n_experts:2 * n_experts].set(rw_lo)
    h1, hm, r1, r2 = _merge2(m, x2d, g1, norm2_g[0][None], sh2, sc2, w_out[0].astype(BF16), wr, seq)
    logits = (r1[:, :n_experts] + r1[:, n_experts:2 * n_experts] + r2[:, :n_experts]) + router_b[0]

    gates, slot_tok, dest, block_expert, n_valid = _route(logits, n_experts)
    f = w_down.shape[2]
    bgu = b_gate_up[0].reshape(n_experts, f // LANE, LANE, 2).swapaxes(2, 3).reshape(n_experts, 1, 2 * f)
    yslots = _moe_blocks(hm.reshape(n_tok, d // LANE, LANE), slot_tok.reshape(-1, 1, MOE_TM), block_expert,
                         n_valid.reshape(1),
                         wgu_b.reshape(w_gate_up.shape[1:]), bgu, wd_b.reshape(w_down.shape[1:]),
                         b_down[0][:, None, :])
    yg = yslots[dest.reshape(n_tok, TOP_K).T.reshape(-1)]
    out = _final(h1, yg, gates, g2, final_g[None], seq)
    return out.reshape(bsz, seq, d)
```

```python
import functools
import math

import jax
import jax.numpy as jnp
import numpy as np
from jax import lax
from jax.experimental import pallas as pl
from jax.experimental.pallas import tpu as pltpu

F32 = jnp.float32
BF16 = jnp.bfloat16
HIGHEST = lax.Precision.HIGHEST

LANE = 128
GRID_W = 64
NORM_EPS = 1e-6
S5_GROUP = 16
S5_CHUNK = 32
HY_ORDER = 2
HY_NORM_EPS = 1e-6
TOP_K = 4
SWIGLU_LIMIT = 7.0
SWIGLU_ALPHA = 1.702
MOE_TM = 512
MOE_TF = 1024
COMBINE_CHUNKS = 4
MOE_PITCH = 20
VMEM_LIMIT = 56 * 1024 * 1024


def _params(sem, vmem=VMEM_LIMIT):
    return pltpu.CompilerParams(dimension_semantics=sem, vmem_limit_bytes=vmem)


def _dot(a, b):
    return jnp.dot(a, b, preferred_element_type=F32)


def _ada_kernel(c_ref, w_ref, b_ref, o_ref):
    c = c_ref[...]
    a = c * jax.nn.sigmoid(c)
    o_ref[...] = jnp.dot(a, w_ref[...], preferred_element_type=F32, precision=HIGHEST) + b_ref[...]


def _ada_mod(cc, w, b):
    d, n6 = w.shape
    tn = min(512, n6)
    return pl.pallas_call(
        _ada_kernel,
        grid=(n6 // tn,),
        in_specs=[pl.BlockSpec((8, d), lambda j: (0, 0)),
                  pl.BlockSpec((d, tn), lambda j: (0, j)),
                  pl.BlockSpec((1, tn), lambda j: (0, j))],
        out_specs=pl.BlockSpec((8, tn), lambda j: (0, j)),
        out_shape=jax.ShapeDtypeStruct((8, n6), F32),
        compiler_params=_params(("parallel",)),
        name="ada_mod",
    )(cc, w, b)


def _inproj_kernel(x_ref, g_ref, sh_ref, sc_ref, w_ref, *rest, bounds, acts, riders):
    nr = len(riders)
    n_in = nr + (1 if any(riders) else 0)
    rider_in, rest = rest[:n_in], rest[n_in:]
    outs, rider_out, xn_ref = rest[:len(bounds)], rest[len(bounds):len(bounds) + nr], rest[-1]
    j = pl.program_id(1)

    @pl.when(j == 0)
    def _():
        x = x_ref[...].astype(F32)
        y = x * lax.rsqrt(jnp.mean(x * x, axis=-1, keepdims=True) + NORM_EPS)
        y = y * g_ref[...]
        y = y * (1.0 + sc_ref[...]) + sh_ref[...]
        xn_ref[...] = y.astype(BF16)

    for (j0, j1), act, o_ref in zip(bounds, acts, outs):
        @pl.when((j >= j0) & (j < j1))
        def _(o_ref=o_ref, act=act):
            r = _dot(xn_ref[...], w_ref[...])
            if act:
                r = jax.nn.sigmoid(r)
            o_ref[...] = r.astype(o_ref.dtype)

    for r_in, r_out, deint in zip(rider_in, rider_out, riders):
        if deint:
            for b in range(r_in.shape[1] // (2 * LANE)):
                sl = slice(2 * b * LANE, (2 * b + 2) * LANE)
                r_out[:, sl] = _dot(r_in[:, sl].astype(BF16), rider_in[-1][...]).astype(r_out.dtype)
        else:
            r_out[...] = r_in[...].astype(r_out.dtype)


def _deinterleave_perm():
    src = jnp.arange(2 * LANE)
    dst = (src % 2) * LANE + src // 2
    return jnp.zeros((2 * LANE, 2 * LANE), BF16).at[src, dst].set(1)


def _norm_mod_proj(x2d, g, sh, sc, w, widths, acts, rows_per_mod, tm, tn, riders=()):
    n, d = x2d.shape
    ncols = w.shape[1]
    tm = min(tm, n)
    tn = min(tn, min(widths))
    bounds, off = [], 0
    for wd in widths:
        bounds.append((off // tn, (off + wd) // tn))
        off += wd
    tiles_per_mod = rows_per_mod // tm
    nj = ncols // tn
    nsteps = (n // tm) * nj

    def out_map(i, j, j0, nj):
        return (i, jnp.clip(j - j0, 0, nj - 1))

    out_specs = [pl.BlockSpec((tm, tn), functools.partial(out_map, j0=j0, nj=j1 - j0))
                 for (j0, j1) in bounds]
    out_shape = [jax.ShapeDtypeStruct((n, wd), BF16) for wd in widths]
    rider_args, rider_specs = [], []
    for arr, _ in riders:
        spec = pl.BlockSpec((arr.shape[0] // nsteps, arr.shape[1]), lambda i, j: (i * nj + j, 0))
        rider_args.append(arr)
        rider_specs.append(spec)
        out_specs.append(spec)
        out_shape.append(jax.ShapeDtypeStruct(arr.shape, BF16))
    if any(flag for _, flag in riders):
        rider_args.append(_deinterleave_perm())
        rider_specs.append(pl.BlockSpec((2 * LANE, 2 * LANE), lambda i, j: (0, 0)))
    return pl.pallas_call(
        functools.partial(_inproj_kernel, bounds=tuple(bounds), acts=tuple(acts),
                          riders=tuple(flag for _, flag in riders)),
        grid=(n // tm, nj),
        in_specs=[pl.BlockSpec((tm, d), lambda i, j: (i, 0)),
                  pl.BlockSpec((1, d), lambda i, j: (0, 0)),
                  pl.BlockSpec((None, 1, d), lambda i, j: (i // tiles_per_mod, 0, 0)),
                  pl.BlockSpec((None, 1, d), lambda i, j: (i // tiles_per_mod, 0, 0)),
                  pl.BlockSpec((d, tn), lambda i, j: (0, j))] + rider_specs,
        out_specs=out_specs,
        out_shape=out_shape,
        scratch_shapes=[pltpu.VMEM((tm, d), BF16)],
        compiler_params=_params(("parallel", "arbitrary")),
        name="norm_mod_proj",
    )(x2d, g, sh, sc, w, *rider_args)


def _s5_matrices(lam_re, lam_im, log_dt, b_re, b_im, c_re, c_im, d_skip, t):
    lam = lax.complex(lam_re.astype(F32), lam_im.astype(F32))
    dt = jnp.exp(log_dt.astype(F32))[..., None]
    lam_dt = lam * dt
    lam_bar = jnp.exp(lam_dt)
    b_bar = ((lam_bar - 1) / lam)[..., None] * lax.complex(b_re.astype(F32), b_im.astype(F32))
    c_out = lax.complex(c_re.astype(F32), c_im.astype(F32))
    g, p = lam.shape[1], lam.shape[2]
    gs = b_bar.shape[-1]
    k = jnp.arange(t + 1, dtype=F32)
    pw = jnp.exp(lam_dt[..., None] * k)
    kk = jnp.einsum('dgcp,dgpk,dgpe->dgkce', c_out, pw[..., :t], b_bar, precision=HIGHEST).real
    kf, kb = kk[0], kk[1]
    zero_lag = kf[:, :1] + kb[:, :1]
    k_all = jnp.concatenate([kb[:, :0:-1], zero_lag, kf[:, 1:]], axis=1)
    jj = jnp.arange(t)
    lag_idx = jj[None, :] - jj[:, None] + (t - 1)
    m_intra = k_all[:, lag_idx]
    m_intra = m_intra.transpose(0, 1, 4, 2, 3).reshape(g, t * gs, t * gs)
    m_intra = m_intra + jnp.eye(t * gs, dtype=F32)[None] * jnp.tile(
        d_skip.astype(F32).reshape(g, 1, gs), (1, t, 1)).reshape(g, 1, t * gs)
    in_f = pw[0][..., t - 1 - jj][..., None] * b_bar[0][:, :, None, :]
    in_b = pw[1][..., jj][..., None] * b_bar[1][:, :, None, :]

    def to_in(z):
        return z.transpose(0, 2, 3, 1).reshape(g, t * gs, p)

    m_in = jnp.concatenate([to_in(in_f.real), to_in(in_f.imag), to_in(in_b.real), to_in(in_b.imag)], axis=-1)
    out_f = c_out[0][:, :, :, None] * pw[0][:, None, :, 1 + jj]
    out_b = c_out[1][:, :, :, None] * pw[1][:, None, :, t - jj]

    def to_out(z):
        return z.transpose(0, 2, 3, 1).reshape(g, p, t * gs)

    m_out = jnp.concatenate([to_out(out_f.real), -to_out(out_f.imag), to_out(out_b.real), -to_out(out_b.imag)],
                            axis=1)
    mu = pw[..., t]
    mre, mim = mu.real, mu.imag
    coef = jnp.stack([
        jnp.concatenate([mre[0], mre[0], mre[1], mre[1]], axis=-1),
        jnp.concatenate([-mim[0], mim[0], -mim[1], mim[1]], axis=-1),
        jnp.concatenate([mim[0], -mim[0], mim[1], -mim[1]], axis=-1)])
    return m_intra.astype(BF16), m_in.astype(BF16), m_out.astype(BF16), coef


def _s5_in_kernel(a_ref, m_ref, o_ref):
    o_ref[...] = _dot(a_ref[...], m_ref[...])


def _s5_chunk_states(a, m_in):
    g, nc, kdim = a.shape
    s = m_in.shape[2]
    nb = min(nc, 1024)
    return pl.pallas_call(
        _s5_in_kernel,
        grid=(g, nc // nb),
        in_specs=[pl.BlockSpec((None, nb, kdim), lambda gi, i: (gi, i, 0)),
                  pl.BlockSpec((None, kdim, s), lambda gi, i: (gi, 0, 0))],
        out_specs=pl.BlockSpec((None, nb, s), lambda gi, i: (gi, i, 0)),
        out_shape=jax.ShapeDtypeStruct((g, nc, s), F32),
        compiler_params=_params(("parallel", "arbitrary")),
        name="s5_chunk_states",
    )(a, m_in)


def _s5_scan_kernel(sf_ref, sb_ref, init_ref, coef_ref, ef_ref, eb_ref, fin_ref, st_ref, *, nblk, half):
    k = pl.program_id(1)
    nk = pl.num_programs(1)
    lanes = 2 * half

    @pl.when(k == 0)
    def _():
        s0f = init_ref[:, 0:lanes]
        s0b = init_ref[:, lanes:2 * lanes]
        st_ref[0] = s0f
        st_ref[1] = pltpu.roll(s0f, half, 1)
        st_ref[2] = s0b
        st_ref[3] = pltpu.roll(s0b, half, 1)

    def body(r, carry):
        sf, sfw, sb, sbw = carry
        rb = nblk - 1 - r
        xf = sf_ref[r]
        xb = sb_ref[rb]
        ef_ref[r] = sf
        eb_ref[rb] = sb
        xfw = pltpu.roll(xf, half, 1)
        xbw = pltpu.roll(xb, half, 1)
        af, bf, bfw = coef_ref[0, :, 0:lanes], coef_ref[1, :, 0:lanes], coef_ref[2, :, 0:lanes]
        ab, bb, bbw = (coef_ref[0, :, lanes:2 * lanes], coef_ref[1, :, lanes:2 * lanes],
                       coef_ref[2, :, lanes:2 * lanes])
        return (sf * af + sfw * bf + xf, sfw * af + sf * bfw + xfw,
                sb * ab + sbw * bb + xb, sbw * ab + sb * bbw + xbw)

    sf, sfw, sb, sbw = lax.fori_loop(0, nblk, body, (st_ref[0], st_ref[1], st_ref[2], st_ref[3]))
    st_ref[0] = sf
    st_ref[1] = sfw
    st_ref[2] = sb
    st_ref[3] = sbw

    @pl.when(k == nk - 1)
    def _():
        fin_ref[:, 0:lanes] = sf
        fin_ref[:, lanes:2 * lanes] = sb


def _s5_scan(s_t, init, coef):
    bsz, nc, g, s4 = s_t.shape
    lanes = s4 // 2
    nblk = min(nc, 128)
    nk = nc // nblk
    return pl.pallas_call(
        functools.partial(_s5_scan_kernel, nblk=nblk, half=lanes // 2),
        grid=(bsz, nk),
        in_specs=[pl.BlockSpec((None, nblk, g, lanes), lambda b, k: (b, k, 0, 0)),
                  pl.BlockSpec((None, nblk, g, lanes), lambda b, k: (b, nk - 1 - k, 0, 1)),
                  pl.BlockSpec((None, g, s4), lambda b, k: (b, 0, 0)),
                  pl.BlockSpec((3, g, s4), lambda b, k: (0, 0, 0))],
        out_specs=[pl.BlockSpec((None, nblk, g, lanes), lambda b, k: (b, k, 0, 0)),
                   pl.BlockSpec((None, nblk, g, lanes), lambda b, k: (b, nk - 1 - k, 0, 0)),
                   pl.BlockSpec((None, g, s4), lambda b, k: (b, 0, 0))],
        out_shape=[jax.ShapeDtypeStruct((bsz, nc, g, lanes), F32),
                   jax.ShapeDtypeStruct((bsz, nc, g, lanes), F32),
                   jax.ShapeDtypeStruct((bsz, g, s4), F32)],
        scratch_shapes=[pltpu.VMEM((4, g, lanes), F32)],
        compiler_params=_params(("parallel", "arbitrary")),
        name="s5_scan",
    )(s_t, s_t, init, coef)


def _s5_out_kernel(a_ref, e_ref, mi_ref, mo_ref, o_ref):
    y = _dot(a_ref[...], mi_ref[...]) + _dot(e_ref[...], mo_ref[...])
    o_ref[...] = y.astype(o_ref.dtype)


def _s5_chunk_outputs(a, e_in, m_intra, m_out):
    g, nc, kdim = a.shape
    s = e_in.shape[2]
    nb = min(nc, 1024)
    return pl.pallas_call(
        _s5_out_kernel,
        grid=(g, nc // nb),
        in_specs=[pl.BlockSpec((None, nb, kdim), lambda gi, i: (gi, i, 0)),
                  pl.BlockSpec((None, nb, s), lambda gi, i: (gi, i, 0)),
                  pl.BlockSpec((None, kdim, kdim), lambda gi, i: (gi, 0, 0)),
                  pl.BlockSpec((None, s, kdim), lambda gi, i: (gi, 0, 0))],
        out_specs=pl.BlockSpec((None, nb, kdim), lambda gi, i: (gi, i, 0)),
        out_shape=jax.ShapeDtypeStruct((g, nc, kdim), BF16),
        compiler_params=_params(("parallel", "arbitrary")),
        name="s5_chunk_outputs",
    )(a, e_in, m_intra, m_out)


def _s5_readout_kernel(y_ref, wg_ref, o_ref):
    y = jax.nn.gelu(y_ref[...].astype(F32))
    gate = _dot(y.astype(BF16), wg_ref[...])
    o_ref[...] = (y * jax.nn.sigmoid(gate)).astype(o_ref.dtype)


def _s5_readout(y2d, wg):
    n, w = y2d.shape
    tm = min(n, 1024)
    return pl.pallas_call(
        _s5_readout_kernel,
        grid=(n // tm,),
        in_specs=[pl.BlockSpec((tm, w), lambda i: (i, 0)),
                  pl.BlockSpec((w, w), lambda i: (0, 0))],
        out_specs=pl.BlockSpec((tm, w), lambda i: (i, 0)),
        out_shape=jax.ShapeDtypeStruct((n, w), BF16),
        compiler_params=_params(("parallel",)),
        name="s5_readout",
    )(y2d, wg)


def _to_chunks(u2d, g, t):
    n = u2d.shape[0]
    return u2d.reshape(n // t, t, g, S5_GROUP).transpose(2, 0, 1, 3).reshape(g, n // t, t * S5_GROUP)


def _from_chunks(y, g, t):
    nc = y.shape[1]
    return y.reshape(g, nc, t, S5_GROUP).transpose(1, 2, 0, 3).reshape(nc * t, g * S5_GROUP)


def _s5_branch(ua, uca, bsz, lam_re, lam_im, log_dt, b_re, b_im, c_re, c_im, d_skip, w_glu):
    g = lam_re.shape[1]
    t = S5_CHUNK
    m_intra, m_in, m_out, coef = _s5_matrices(lam_re, lam_im, log_dt, b_re, b_im, c_re, c_im, d_skip, t)
    s4 = m_in.shape[2]

    def states(u2d):
        a = _to_chunks(u2d, g, t)
        s = _s5_chunk_states(a, m_in)
        nc = s.shape[1] // bsz
        return a, s.reshape(g, bsz, nc, s4).transpose(1, 2, 0, 3)

    _, s_ctx = states(uca)
    _, _, seed = _s5_scan(s_ctx, jnp.zeros((bsz, g, s4), F32), coef)
    a_lat, s_lat = states(ua)
    e_f, e_b, _ = _s5_scan(s_lat, seed, coef)
    e_in = jnp.concatenate([e_f, e_b], axis=-1).astype(BF16)
    e_in = e_in.transpose(2, 0, 1, 3).reshape(g, -1, s4)
    y = _s5_chunk_outputs(a_lat, e_in, m_intra, m_out)
    return _s5_readout(_from_chunks(y, g, t), w_glu.astype(BF16))


HY_KH = GRID_W + 1
HY_KP = 72
HY_QN = 16


def _hy_phase(w_cols, r_cnt):
    n1_tot = 2 * GRID_W
    n_fft = n1_tot * r_cnt
    kh = GRID_W + 1
    k1 = np.arange(kh, dtype=np.int64)[None, :, None]
    r = np.arange(r_cnt, dtype=np.int64)[:, None, None]
    n1 = np.arange(w_cols, dtype=np.int64)[None, None, :]
    return 2.0 * np.pi * ((k1 * (r_cnt * n1 + r)) % n_fft) / n_fft


def _hy_fwd_table(r_cnt, m):
    th = _hy_phase(GRID_W * m, r_cnt)
    kh = th.shape[1]
    f = np.zeros((r_cnt, 2 * HY_KP, GRID_W * m), np.float32)
    f[:, :kh] = np.cos(th)
    f[:, HY_KP:HY_KP + kh] = -np.sin(th)
    return jnp.asarray(f, BF16)


def _hy_inv_table(r_cnt):
    th = _hy_phase(GRID_W, r_cnt)
    kh = th.shape[1]
    c = np.full((kh,), 2.0)
    c[0] = c[-1] = 1.0
    scale = c[None, :, None] / (2 * GRID_W * r_cnt)
    g = np.zeros((r_cnt, GRID_W, 2 * HY_KP), np.float32)
    g[:, :, :kh] = (scale * np.cos(th)).transpose(0, 2, 1)
    g[:, :, HY_KP:HY_KP + kh] = (-scale * np.sin(th)).transpose(0, 2, 1)
    return jnp.asarray(g, BF16)


def _hy_dft_tables(r_cnt):
    k = np.arange(r_cnt, dtype=np.int64)
    th = 2.0 * np.pi * ((k[:, None] * k[None, :]) % r_cnt) / r_cnt
    fc, fs = np.cos(th), np.sin(th)
    fwd = np.block([[fc, fs], [-fs, fc]]).astype(np.float32)
    inv = np.block([[fc, -fs], [fs, fc]]).astype(np.float32)
    return jnp.asarray(fwd, BF16), jnp.asarray(inv, BF16)


def _hy_short_kernel(x_ref, w_ref, o_ref, *, rows, chunk):
    wk = w_ref[...]
    w0, w1, w2 = wk[0:1], wk[1:2], wk[2:3]
    gw = GRID_W
    col = lax.broadcasted_iota(jnp.int32, (gw, x_ref.shape[1]), 0)

    def piece(a, n):
        return x_ref[pl.ds(a, n), :].astype(F32)

    last = piece(rows - gw, gw)
    prev0 = jnp.where(col == 0, 0.0, pltpu.roll(last, 1, 0))
    o_ref[0:gw, :] = (w0 * prev0 + w1 * piece(0, gw) + w2 * piece(gw, gw)).astype(o_ref.dtype)
    first = piece(0, gw)
    next_l = jnp.where(col == gw - 1, 0.0, pltpu.roll(first, gw - 1, 0))
    o_ref[rows - gw:rows, :] = (w0 * piece(rows - 2 * gw, gw) + w1 * last + w2 * next_l).astype(o_ref.dtype)

    def body(i, carry):
        a = pl.multiple_of(gw + i * chunk, gw)
        o_ref[pl.ds(a, chunk), :] = (w0 * piece(a - gw, chunk) + w1 * piece(a, chunk)
                                     + w2 * piece(a + gw, chunk)).astype(o_ref.dtype)
        return carry

    n_full = (rows - 2 * gw) // chunk
    lax.fori_loop(0, n_full, body, 0)
    rem = rows - 2 * gw - n_full * chunk
    if rem:
        a = gw + n_full * chunk
        o_ref[a:a + rem, :] = (w0 * piece(a - gw, rem) + w1 * piece(a, rem)
                               + w2 * piece(a + gw, rem)).astype(o_ref.dtype)


def _hy_short_conv(uh, short_w, bsz, seq):
    n, ch = uh.shape
    ct = min(ch, LANE)
    chunk = min(512, seq - 2 * GRID_W)
    return pl.pallas_call(
        functools.partial(_hy_short_kernel, rows=seq, chunk=chunk),
        grid=(bsz, ch // ct),
        in_specs=[pl.BlockSpec((seq, ct), lambda b, c: (b, c)),
                  pl.BlockSpec((3, ct), lambda b, c: (0, c))],
        out_specs=pl.BlockSpec((seq, ct), lambda b, c: (b, c)),
        out_shape=jax.ShapeDtypeStruct((n, ch), BF16),
        compiler_params=_params(("parallel", "parallel")),
        name="hy_short_conv",
    )(uh, short_w)


def _hy_fwd1_kernel(*refs, m):
    x_refs, f_ref, o_ref, s_ref = refs[:m], refs[m], refs[m + 1], refs[m + 2]
    gw = GRID_W
    for q in range(HY_QN):
        xq = [x[q * gw:(q + 1) * gw, :] for x in x_refs]
        xq = xq[0] if m == 1 else jnp.concatenate(xq, axis=0)
        s_ref[q] = _dot(f_ref[q], xq)
    t = jnp.swapaxes(s_ref[...], 0, 1).astype(o_ref.dtype)
    o_ref[0] = t[0:HY_KH]
    o_ref[1] = t[HY_KP:HY_KP + HY_KH]


def _hy_fwd1(xs, col_blk0, width, bsz, r_cnt, ct):
    m = len(xs)
    table = _hy_fwd_table(r_cnt, m)
    rb = HY_QN * GRID_W
    nrb = r_cnt // HY_QN
    x_spec = pl.BlockSpec((rb, ct), lambda b, i, c: (b * nrb + i, col_blk0 + c))
    return pl.pallas_call(
        functools.partial(_hy_fwd1_kernel, m=m),
        grid=(bsz, nrb, width // ct),
        in_specs=[x_spec] * m + [pl.BlockSpec((HY_QN, 2 * HY_KP, GRID_W * m), lambda b, i, c: (i, 0, 0))],
        out_specs=pl.BlockSpec((None, 2, HY_KH, HY_QN, ct), lambda b, i, c: (b, 0, 0, i, c)),
        out_shape=jax.ShapeDtypeStruct((bsz, 2, HY_KH, r_cnt, width), BF16),
        scratch_shapes=[pltpu.VMEM((HY_QN, 2 * HY_KP, ct), F32)],
        compiler_params=_params(("parallel", "parallel", "arbitrary")),
        name="hy_fwd_stage1",
    )(*xs, table)


def _hy_spec_kernel(a_ref, f2_ref, o_ref):
    r = a_ref.shape[1]
    x = a_ref[...].reshape(2 * r, a_ref.shape[2])
    o_ref[...] = _dot(f2_ref[...], x).reshape(o_ref.shape)


def _hy_filter_spectrum(a, f2, ct):
    _, _, kh, r_cnt, width = a.shape
    return pl.pallas_call(
        _hy_spec_kernel,
        grid=(kh, width // ct),
        in_specs=[pl.BlockSpec((None, 2, None, r_cnt, ct), lambda k, c: (0, 0, k, 0, c)),
                  pl.BlockSpec((2 * r_cnt, 2 * r_cnt), lambda k, c: (0, 0))],
        out_specs=pl.BlockSpec((2, None, r_cnt, ct), lambda k, c: (0, k, 0, c)),
        out_shape=jax.ShapeDtypeStruct((2, kh, r_cnt, width), F32),
        compiler_params=_params(("parallel", "parallel")),
        name="hy_filter_spectrum",
    )(a, f2)


def _hy_mid_kernel(a_ref, kf_ref, invn_ref, f2_ref, f2i_ref, o_ref):
    r = a_ref.shape[1]
    x = a_ref[...].reshape(2 * r, a_ref.shape[2])
    b = _dot(f2_ref[...], x)
    br, bi = b[:r], b[r:]
    kr, ki = kf_ref[0], kf_ref[1]
    s = invn_ref[...]
    cr = (br * kr - bi * ki) * s
    ci = (br * ki + bi * kr) * s
    c = jnp.concatenate([cr, ci], axis=0).astype(BF16)
    o_ref[...] = _dot(f2i_ref[...], c).reshape(o_ref.shape).astype(o_ref.dtype)


def _hy_mid(a, kf, invn, order, f2, f2i, ct):
    bsz, _, kh, r_cnt, width = a.shape
    nct = width // ct
    return pl.pallas_call(
        _hy_mid_kernel,
        grid=(kh, nct, bsz),
        in_specs=[pl.BlockSpec((None, 2, None, r_cnt, ct), lambda k, c, b: (b, 0, k, 0, c)),
                  pl.BlockSpec((2, None, r_cnt, ct), lambda k, c, b: (0, k, 0, order * nct + c)),
                  pl.BlockSpec((1, ct), lambda k, c, b: (0, order * nct + c)),
                  pl.BlockSpec((2 * r_cnt, 2 * r_cnt), lambda k, c, b: (0, 0)),
                  pl.BlockSpec((2 * r_cnt, 2 * r_cnt), lambda k, c, b: (0, 0))],
        out_specs=pl.BlockSpec((None, 2, None, r_cnt, ct), lambda k, c, b: (b, 0, k, 0, c)),
        out_shape=jax.ShapeDtypeStruct(a.shape, BF16),
        compiler_params=_params(("parallel", "parallel", "arbitrary")),
        name="hy_spectrum_product",
    )(a, kf, invn, f2, f2i)


def _hy_inv1_kernel(d_ref, g_ref, xin_ref, xm_ref, bias_ref, o_ref, t_ref, s_ref):
    gw = GRID_W
    pad = jnp.zeros((HY_KP - HY_KH,) + t_ref.shape[1:], F32)
    t_ref[0:HY_KH] = d_ref[0].astype(F32)
    t_ref[HY_KH:HY_KP] = pad
    t_ref[HY_KP:HY_KP + HY_KH] = d_ref[1].astype(F32)
    t_ref[HY_KP + HY_KH:2 * HY_KP] = pad
    s_ref[...] = jnp.swapaxes(t_ref[...], 0, 1)
    for q in range(HY_QN):
        y = _dot(g_ref[q], s_ref[q].astype(BF16))
        rows = slice(q * gw, (q + 1) * gw)
        xin = xin_ref[rows, :].astype(F32)
        o_ref[rows, :] = (xm_ref[rows, :].astype(F32) * (y + xin * bias_ref[...])).astype(o_ref.dtype)


def _hy_inv1(d, xin, xin_blk0, xm, xm_blk0, bias, ct):
    bsz, _, kh, r_cnt, width = d.shape
    table = _hy_inv_table(r_cnt)
    rb = HY_QN * GRID_W
    nrb = r_cnt // HY_QN
    return pl.pallas_call(
        _hy_inv1_kernel,
        grid=(bsz, nrb, width // ct),
        in_specs=[pl.BlockSpec((None, 2, kh, HY_QN, ct), lambda b, i, c: (b, 0, 0, i, c)),
                  pl.BlockSpec((HY_QN, GRID_W, 2 * HY_KP), lambda b, i, c: (i, 0, 0)),
                  pl.BlockSpec((rb, ct), lambda b, i, c: (b * nrb + i, xin_blk0 + c)),
                  pl.BlockSpec((rb, ct), lambda b, i, c: (b * nrb + i, xm_blk0 + c)),
                  pl.BlockSpec((1, ct), lambda b, i, c: (0, c))],
        out_specs=pl.BlockSpec((rb, ct), lambda b, i, c: (b * nrb + i, c)),
        out_shape=jax.ShapeDtypeStruct((bsz * r_cnt * GRID_W, width), BF16),
        scratch_shapes=[pltpu.VMEM((2 * HY_KP, HY_QN, ct), F32), pltpu.VMEM((HY_QN, 2 * HY_KP, ct), F32)],
        compiler_params=_params(("parallel", "parallel", "arbitrary")),
        name="hy_inv_stage1",
    )(d, table, xin, xm, bias)


def _hy_filter_kernel(w1_ref, b1_ref, f1_ref, w2_ref, b2_ref, f2_ref, w3_ref, b3_ref, dec_ref, om_ref, ph_ref,
                      o_ref, sum_ref, *, direction, seqlen, r_cnt, rows):
    i = pl.program_id(1)
    row = lax.broadcasted_iota(jnp.int32, (rows, LANE), 0) + i * rows
    lane = lax.broadcasted_iota(jnp.int32, (rows, LANE), 1)
    r = jnp.right_shift(row, GRID_W.bit_length() - 1)
    w = jnp.bitwise_and(row, GRID_W - 1)
    n = (w + GRID_W * direction) * r_cnt + r
    p = n if direction == 0 else 2 * seqlen - n
    pf = p.astype(F32)
    t = pf / (seqlen - 1)
    feats = jnp.where(lane == 0, t, jnp.cos(pf * om_ref[...] + ph_ref[...]))
    hid = jnp.sin(f1_ref[...] * (jnp.dot(feats, w1_ref[...], preferred_element_type=F32, precision=HIGHEST)
                                 + b1_ref[...]))
    hid = jnp.sin(f2_ref[...] * (jnp.dot(hid, w2_ref[...], preferred_element_type=F32, precision=HIGHEST)
                                 + b2_ref[...]))
    filt = _dot(hid.astype(BF16), w3_ref[...].astype(BF16)) + b3_ref[...]
    reps = filt.shape[1] // LANE
    t_w = jnp.concatenate([t] * reps, axis=1)
    keep = jnp.concatenate([jnp.where(p < seqlen, 1.0, 0.0)] * reps, axis=1)
    filt = filt * jnp.exp(-t_w * jnp.abs(dec_ref[...])) * keep
    o_ref[...] = filt.astype(o_ref.dtype)

    @pl.when(i == 0)
    def _():
        sum_ref[...] = jnp.zeros_like(sum_ref)

    sum_ref[...] += jnp.sum(jnp.abs(filt), axis=0, keepdims=True)


def _hy_filter_signal(direction, seqlen, w1p, b1, f1, w2, b2, f2, w3, b3, decay, omega, phase, width):
    r_cnt = seqlen // GRID_W
    rows = min(256, seqlen)
    ct = width
    nct = width // ct
    hid = w2.shape[0]
    off = direction * nct
    const = lambda shape: pl.BlockSpec(shape, lambda c, i: (0, 0))
    colv = pl.BlockSpec((1, ct), lambda c, i: (0, off + c))
    return pl.pallas_call(
        functools.partial(_hy_filter_kernel, direction=direction, seqlen=seqlen, r_cnt=r_cnt, rows=rows),
        grid=(nct, seqlen // rows),
        in_specs=[const((LANE, hid)), const((1, hid)), const((1, hid)), const((hid, hid)), const((1, hid)),
                  const((1, hid)), pl.BlockSpec((hid, ct), lambda c, i: (0, off + c)), colv, colv,
                  const((1, LANE)), const((1, LANE))],
        out_specs=[pl.BlockSpec((rows, ct), lambda c, i: (i, c)),
                   pl.BlockSpec((1, ct), lambda c, i: (0, c))],
        out_shape=[jax.ShapeDtypeStruct((seqlen, width), BF16), jax.ShapeDtypeStruct((1, width), F32)],
        compiler_params=_params(("parallel", "arbitrary")),
        name="hy_filter_signal",
    )(w1p, b1, f1, w2, b2, f2, w3, b3, decay, omega, phase)


def _hyena_branch(uh, bsz, seq, short_w, w1, b1, f1, w2, b2, f2, w3, b3, decay, hy_bias):
    width = hy_bias.shape[1]
    r_cnt = seq // GRID_W
    ct = min(width, 512)
    nct = width // ct
    n_emb = w1.shape[0]
    n_bands = (n_emb - 1) // 2
    hid = w1.shape[1]
    bands = jnp.linspace(1e-4, n_bands - 1, n_bands, dtype=F32)
    omega = jnp.zeros((1, LANE), F32).at[0, 1:1 + n_bands].set((2 * jnp.pi / seq) * bands)
    omega = omega.at[0, 1 + n_bands:1 + 2 * n_bands].set((2 * jnp.pi / seq) * bands)
    phase = jnp.zeros((1, LANE), F32).at[0, 1 + n_bands:1 + 2 * n_bands].set(0.5 * jnp.pi)
    w1p = jnp.zeros((LANE, hid), F32).at[:n_emb].set(w1)
    sigs, sums = [], []
    for direction in range(2):
        sig, ssum = _hy_filter_signal(direction, seq, w1p, b1[None], f1[None], w2, b2[None], f2[None], w3,
                                      b3[None], decay[None], omega, phase, HY_ORDER * width)
        sigs.append(sig)
        sums.append(ssum)
    invn = 1.0 / (sums[0] + sums[1] + HY_NORM_EPS)
    f2m, f2i = _hy_dft_tables(r_cnt)
    ct3 = min(width, 1024)
    kf = _hy_filter_spectrum(_hy_fwd1(sigs, 0, HY_ORDER * width, 1, r_cnt, ct), f2m, ct3)

    z = _hy_short_conv(uh, short_w.astype(F32), bsz, seq)
    a = _hy_fwd1([z], 0, width, bsz, r_cnt, ct)
    d = _hy_mid(a, kf, invn, 0, f2m, f2i, ct3)
    y1 = _hy_inv1(d, z, 0, z, nct, hy_bias[0:1].astype(F32), ct)
    a = _hy_fwd1([y1], 0, width, bsz, r_cnt, ct)
    d = _hy_mid(a, kf, invn, 1, f2m, f2i, ct3)
    return _hy_inv1(d, y1, 0, z, 2 * nct, hy_bias[1:2].astype(F32), ct)


def _merge1_kernel(ya_ref, yh_ref, ug_ref, wa_ref, wh_ref, o_ref):
    d = o_ref.shape[1]
    a = _dot(ya_ref[...], wa_ref[...])
    h = _dot(yh_ref[...], wh_ref[...])
    ga = ug_ref[:, :d].astype(F32)
    gh = ug_ref[:, d:].astype(F32)
    o_ref[...] = (ga * a + gh * h).astype(o_ref.dtype)


def _merge1(ya, yh, ug, wa, wh):
    n, ws = ya.shape
    wh_in = yh.shape[1]
    d = wa.shape[1]
    tm = min(n, 512)
    return pl.pallas_call(
        _merge1_kernel,
        grid=(n // tm,),
        in_specs=[pl.BlockSpec((tm, ws), lambda i: (i, 0)),
                  pl.BlockSpec((tm, wh_in), lambda i: (i, 0)),
                  pl.BlockSpec((tm, 2 * d), lambda i: (i, 0)),
                  pl.BlockSpec((ws, d), lambda i: (0, 0)),
                  pl.BlockSpec((wh_in, d), lambda i: (0, 0))],
        out_specs=pl.BlockSpec((tm, d), lambda i: (i, 0)),
        out_shape=jax.ShapeDtypeStruct((n, d), BF16),
        compiler_params=_params(("parallel",)),
        name="merge_branches",
    )(ya, yh, ug, wa, wh)


def _merge2_kernel(m_ref, x_ref, g1_ref, ng_ref, sh_ref, sc_ref, wo_ref, wr_ref,
                   h1_ref, hm_ref, r1_ref, r2_ref):
    mix = _dot(m_ref[...], wo_ref[...])
    h1 = x_ref[...] + g1_ref[...] * mix
    h1_ref[...] = h1
    y = h1 * lax.rsqrt(jnp.mean(h1 * h1, axis=-1, keepdims=True) + NORM_EPS)
    y = y * ng_ref[...]
    y = y * (1.0 + sc_ref[...]) + sh_ref[...]
    hi = y.astype(BF16)
    lo = (y - hi.astype(F32)).astype(BF16)
    n_sub = y.shape[1] // LANE
    for s in range(n_sub):
        hm_ref[pl.ds(s, y.shape[0], stride=n_sub), :] = y[:, s * LANE:(s + 1) * LANE]
    r1_ref[...] = _dot(hi, wr_ref[...])
    r2_ref[...] = _dot(lo, wr_ref[...])


def _merge2(m, x2d, g1, ng, sh2, sc2, wo, wr, rows_per_mod):
    n, d = x2d.shape
    tm = min(n, 512)
    tiles_per_mod = rows_per_mod // tm
    nr = wr.shape[1]
    mod_spec = pl.BlockSpec((None, 1, d), lambda i: (i // tiles_per_mod, 0, 0))
    return pl.pallas_call(
        _merge2_kernel,
        grid=(n // tm,),
        in_specs=[pl.BlockSpec((tm, d), lambda i: (i, 0)),
                  pl.BlockSpec((tm, d), lambda i: (i, 0)),
                  mod_spec,
                  pl.BlockSpec((1, d), lambda i: (0, 0)),
                  mod_spec, mod_spec,
                  pl.BlockSpec((d, d), lambda i: (0, 0)),
                  pl.BlockSpec((d, nr), lambda i: (0, 0))],
        out_specs=[pl.BlockSpec((tm, d), lambda i: (i, 0)),
                   pl.BlockSpec((tm * (d // LANE), LANE), lambda i: (i, 0)),
                   pl.BlockSpec((tm, nr), lambda i: (i, 0)),
                   pl.BlockSpec((tm, nr), lambda i: (i, 0))],
        out_shape=[jax.ShapeDtypeStruct((n, d), F32),
                   jax.ShapeDtypeStruct((n * (d // LANE), LANE), F32),
                   jax.ShapeDtypeStruct((n, nr), F32),
                   jax.ShapeDtypeStruct((n, nr), F32)],
        compiler_params=_params(("parallel",)),
        name="out_proj_norm_router",
    )(m, x2d, g1, ng, sh2, sc2, wo, wr)


def _moe_kernel(be_ref, nv_ref, tok0_ref, tokn_ref, hm_ref, wgu_ref, bgu_ref, wd_ref, bd_ref, o_ref,
                xbuf_ref, acc_ref, sem):
    i = pl.program_id(0)
    j = pl.program_id(1)
    nj = pl.num_programs(1)
    nv = nv_ref[0]
    tm = acc_ref.shape[0]

    n_sub = hm_ref.shape[1]

    def row_copy(tok, r, slot):
        return pltpu.make_async_copy(hm_ref.at[tok], xbuf_ref.at[slot, pl.ds(r * MOE_PITCH, n_sub), :],
                                     sem.at[slot])

    def wait_block(slot):
        for r in range(tm):
            row_copy(0, r, slot).wait()

    @pl.when((i == 0) & (j == 0) & (nv > 0))
    def _():
        for r in range(tm):
            row_copy(tok0_ref[0, r], r, 0).start()

    @pl.when((i < nv) & (j == 0))
    def _():
        wait_block(i % 2)
        acc_ref[...] = jnp.zeros_like(acc_ref)

    def expert_step(request_next):
        if request_next:
            for r in range(tm):
                row_copy(tokn_ref[0, r], r, (i + 1) % 2).start()
        slot = i % 2
        x = jnp.concatenate([xbuf_ref[slot, pl.ds(s, tm, stride=MOE_PITCH), :].astype(BF16)
                             for s in range(n_sub)], axis=1)
        gu = _dot(x, wgu_ref[...]) + bgu_ref[...]
        acts = []
        for b in range(gu.shape[1] // (2 * LANE)):
            glu = jnp.minimum(gu[:, 2 * b * LANE:(2 * b + 1) * LANE], SWIGLU_LIMIT)
            lin = jnp.clip(gu[:, (2 * b + 1) * LANE:(2 * b + 2) * LANE], -SWIGLU_LIMIT, SWIGLU_LIMIT)
            acts.append((glu * jax.nn.sigmoid(SWIGLU_ALPHA * glu) * (lin + 1.0)).astype(BF16))
        acc_ref[...] += _dot(jnp.concatenate(acts, axis=1), wd_ref[...])

    @pl.when((i < nv) & (j == 0))
    def _():
        expert_step(True)

    @pl.when((i < nv) & (j > 0))
    def _():
        expert_step(False)

    @pl.when((i < nv) & (j == nj - 1))
    def _():
        o_ref[...] = (acc_ref[...] + bd_ref[...]).astype(o_ref.dtype)

    @pl.when((i == nv - 1) & (j == nj - 1))
    def _():
        wait_block((i + 1) % 2)

    @pl.when((i >= nv) & (j == nj - 1))
    def _():
        o_ref[...] = jnp.zeros_like(o_ref)


def _moe_blocks(hm, slot_tok, block_expert, n_valid, wgu, bgu, wd, bd):
    d = wd.shape[2]
    assert hm.shape[1] <= MOE_PITCH and hm.shape[1] * hm.shape[2] == d
    f = wd.shape[1]
    tm = MOE_TM
    tf = min(f, MOE_TF)
    nj = f // tf
    nblk = block_expert.shape[0]

    def jeff(i, j, nv):
        return jnp.where(i < nv[0], j, nj - 1)

    grid_spec = pltpu.PrefetchScalarGridSpec(
        num_scalar_prefetch=2,
        grid=(nblk, nj),
        in_specs=[pl.BlockSpec((None, 1, tm), lambda i, j, be, nv: (0, 0, 0), memory_space=pltpu.SMEM),
                  pl.BlockSpec((None, 1, tm), lambda i, j, be, nv: (jnp.minimum(i + 1, nblk - 1), 0, 0),
                               memory_space=pltpu.SMEM),
                  pl.BlockSpec(memory_space=pl.ANY),
                  pl.BlockSpec((None, d, 2 * tf), lambda i, j, be, nv: (be[i], 0, jeff(i, j, nv))),
                  pl.BlockSpec((None, 1, 2 * tf), lambda i, j, be, nv: (be[i], 0, jeff(i, j, nv))),
                  pl.BlockSpec((None, tf, d), lambda i, j, be, nv: (be[i], jeff(i, j, nv), 0)),
                  pl.BlockSpec((None, 1, d), lambda i, j, be, nv: (be[i], 0, 0))],
        out_specs=pl.BlockSpec((tm, d), lambda i, j, be, nv: (i, 0)),
        scratch_shapes=[pltpu.VMEM((2, tm * MOE_PITCH, LANE), F32), pltpu.VMEM((tm, d), F32),
                        pltpu.SemaphoreType.DMA((2,))])
    return pl.pallas_call(
        _moe_kernel,
        grid_spec=grid_spec,
        out_shape=jax.ShapeDtypeStruct((nblk * tm, d), BF16),
        compiler_params=_params(("arbitrary", "arbitrary")),
        name="moe_experts",
    )(block_expert, n_valid, slot_tok, slot_tok, hm, wgu, bgu, wd, bd)


def _route(logits, n_experts):
    n_tok = logits.shape[0]
    top_val, top_idx = lax.top_k(logits, TOP_K)
    gates = jax.nn.softmax(top_val, axis=-1)
    n_assign = n_tok * TOP_K
    flat_e = top_idx.reshape(-1).astype(jnp.int32)
    experts = jnp.arange(n_experts, dtype=jnp.int32)
    onehot = (flat_e[:, None] == experts[None, :]).astype(jnp.int32)
    csum = jnp.cumsum(onehot, axis=0)
    rank = jnp.sum(onehot * csum, axis=1) - 1
    counts = csum[-1]
    padded = (counts + MOE_TM - 1) // MOE_TM * MOE_TM
    start = jnp.cumsum(counts) - counts
    padded_end = jnp.cumsum(padded)
    padded_start = padded_end - padded
    dest = padded_start[flat_e] + rank
    n_blocks = -(-n_assign // MOE_TM) + n_experts
    cap = n_blocks * MOE_TM
    n_valid = (padded_end[-1] // MOE_TM).astype(jnp.int32)
    order = jnp.argsort(flat_e)
    slots = jnp.arange(cap, dtype=jnp.int32)
    slot_e = jnp.minimum(jnp.sum((padded_end[None, :] <= slots[:, None]).astype(jnp.int32), axis=1), n_experts - 1)
    slot_rank = slots - padded_start[slot_e]
    src = jnp.minimum(start[slot_e] + slot_rank, n_assign - 1)
    slot_tok = jnp.where(slot_rank < counts[slot_e], order[src] // TOP_K, 0).astype(jnp.int32)
    block_start = jnp.minimum(jnp.arange(n_blocks, dtype=jnp.int32), n_valid - 1) * MOE_TM
    block_expert = jnp.minimum(
        jnp.sum((padded_end[None, :] <= block_start[:, None]).astype(jnp.int32), axis=1), n_experts - 1)
    return gates, slot_tok, dest, block_expert, n_valid


def _final_kernel(h1_ref, *rest):
    yg_refs = rest[:TOP_K]
    gates_ref, g2_ref, fg_ref = rest[TOP_K:TOP_K + 3]
    o_ref = rest[-1]
    gates = gates_ref[...]
    moe = yg_refs[0][...].astype(F32) * gates[:, 0:1]
    for k in range(1, TOP_K):
        moe = moe + yg_refs[k][...].astype(F32) * gates[:, k:k + 1]
    h = h1_ref[...] + g2_ref[...] * moe
    y = h * lax.rsqrt(jnp.mean(h * h, axis=-1, keepdims=True) + NORM_EPS)
    o_ref[...] = (y * fg_ref[...]).astype(o_ref.dtype)


def _final(h1, yg, gates, g2, fg, rows_per_mod, tok0, out_prev):
    n, d = h1.shape
    nc = yg.shape[0] // TOP_K
    tm = min(nc, 512)
    tiles_per_mod = rows_per_mod // tm
    nt = nc // tm
    t0 = tok0 // tm
    yg_specs = [pl.BlockSpec((tm, d), functools.partial(lambda i, k: (k * nt + i, 0), k=k)) for k in range(TOP_K)]
    in_specs = [pl.BlockSpec((tm, d), lambda i: (t0 + i, 0))] + yg_specs + [
        pl.BlockSpec((tm, TOP_K), lambda i: (t0 + i, 0)),
        pl.BlockSpec((None, 1, d), lambda i: ((t0 + i) // tiles_per_mod, 0, 0)),
        pl.BlockSpec((1, d), lambda i: (0, 0))]
    args = [h1] + [yg] * TOP_K + [gates, g2, fg]
    aliases = {}
    if out_prev is not None:
        in_specs.append(pl.BlockSpec(memory_space=pl.ANY))
        args.append(out_prev)
        aliases = {len(args) - 1: 0}
    return pl.pallas_call(
        _final_kernel,
        grid=(nt,),
        in_specs=in_specs,
        out_specs=pl.BlockSpec((tm, d), lambda i: (t0 + i, 0)),
        out_shape=jax.ShapeDtypeStruct((n, d), F32),
        input_output_aliases=aliases,
        compiler_params=_params(("parallel",)),
        name="combine_final_norm",
    )(*args)


def kernel(x, c, ctx, c_ctx, w_ada, b_ada, norm1_g, norm2_g, w_in, s5_lam_re, s5_lam_im, s5_log_dt, s5_b_re, s5_b_im, s5_c_re, s5_c_im, s5_d, s5_w_glu, hy_short_w, hy_pos_w1, hy_pos_b1, hy_freq1, hy_pos_w2, hy_pos_b2, hy_freq2, hy_pos_w3, hy_pos_b3, hy_decay, hy_bias, w_branch_s5, w_branch_hy, w_out, router_w, router_b, w_gate_up, b_gate_up, w_down, b_down, final_g):
    bsz, seq, d = x.shape
    ctx_len = ctx.shape[1]
    depth = w_ada.shape[0]
    assert depth == 1, "only the single-layer configuration is implemented"
    n_experts = router_w.shape[2]
    s5_w = s5_d.shape[1]
    hy_w = hy_bias.shape[2]
    n_tok = bsz * seq

    cc = jnp.zeros((8, d), F32).at[:bsz].set(c).at[bsz].set(c_ctx)
    mod = _ada_mod(cc, w_ada[0], b_ada[0][None]).reshape(8, 6, 1, d)
    sh1, sc1, g1, sh2, sc2, g2 = [mod[:bsz, k] for k in range(6)]
    csh1, csc1 = mod[bsz:bsz + 1, 0], mod[bsz:bsz + 1, 1]

    w_in_b = w_in[0].astype(BF16)
    x2d = x.reshape(n_tok, d)
    riders = ((w_gate_up[0].reshape(-1, w_gate_up.shape[3]), True), (w_down[0].reshape(-1, d), False))
    ua, uh, ug, wgu_b, wd_b = _norm_mod_proj(x2d, norm1_g[0][None], sh1, sc1, w_in_b,
                                             (s5_w, 3 * hy_w, 2 * d), (False, False, True), seq, 1024, 512,
                                             riders=riders)
    (uca,) = _norm_mod_proj(ctx.reshape(bsz * ctx_len, d), norm1_g[0][None], csh1, csc1,
                            w_in_b[:, :s5_w], (s5_w,), (False,), bsz * ctx_len, 512, 1024)

    ya = _s5_branch(ua, uca, bsz, s5_lam_re[0], s5_lam_im[0], s5_log_dt[0], s5_b_re[0], s5_b_im[0],
                    s5_c_re[0], s5_c_im[0], s5_d[0], s5_w_glu[0])

    yh = _hyena_branch(uh, bsz, seq, hy_short_w[0], hy_pos_w1[0], hy_pos_b1[0], hy_freq1[0], hy_pos_w2[0],
                       hy_pos_b2[0], hy_freq2[0], hy_pos_w3[0], hy_pos_b3[0], hy_decay[0], hy_bias[0])

    m = _merge1(ya, yh, ug, w_branch_s5[0].astype(BF16), w_branch_hy[0].astype(BF16))
    rw = router_w[0]
    rw_hi = rw.astype(BF16)
    rw_lo = (rw - rw_hi.astype(F32)).astype(BF16)
    wr = jnp.zeros((d, 128), BF16).at[:, :n_experts].set(rw_hi).at[:, n_experts:2 * n_experts].set(rw_lo)
    h1, hm, r1, r2 = _merge2(m, x2d, g1, norm2_g[0][None], sh2, sc2, w_out[0].astype(BF16), wr, seq)
    logits = (r1[:, :n_experts] + r1[:, n_experts:2 * n_experts] + r2[:, :n_experts]) + router_b[0]

    gates, slot_tok, dest, block_expert, n_valid = _route(logits, n_experts)
    f = w_down.shape[2]
    bgu = b_gate_up[0].reshape(n_experts, f // LANE, LANE, 2).swapaxes(2, 3).reshape(n_experts, 1, 2 * f)
    yslots = _moe_blocks(hm.reshape(n_tok, d // LANE, LANE), slot_tok.reshape(-1, 1, MOE_TM), block_expert,
                         n_valid.reshape(1),
                         wgu_b.reshape(w_gate_up.shape[1:]), bgu, wd_b.reshape(w_down.shape[1:]),
                         b_down[0][:, None, :])
    dest2 = dest.reshape(n_tok, TOP_K)
    nc = n_tok // COMBINE_CHUNKS
    out = None
    for ck in range(COMBINE_CHUNKS):
        yg = yslots[dest2[ck * nc:(ck + 1) * nc].T.reshape(-1)]
        out = _final(h1, yg, gates, g2, final_g[None], seq, ck * nc, out)
    return out.reshape(bsz, seq, d)
```

```python
import functools
import math

import jax
import jax.numpy as jnp
import numpy as np
from jax import lax
from jax.experimental import pallas as pl
from jax.experimental.pallas import tpu as pltpu

F32 = jnp.float32
BF16 = jnp.bfloat16
HIGHEST = lax.Precision.HIGHEST

LANE = 128
GRID_W = 64
NORM_EPS = 1e-6
S5_GROUP = 16
S5_CHUNK = 32
HY_ORDER = 2
HY_NORM_EPS = 1e-6
TOP_K = 4
SWIGLU_LIMIT = 7.0
SWIGLU_ALPHA = 1.702
MOE_TM = 512
MOE_TF = 1024
MOE_PITCH = 20
VMEM_LIMIT = 56 * 1024 * 1024


def _params(sem, vmem=VMEM_LIMIT):
    return pltpu.CompilerParams(dimension_semantics=sem, vmem_limit_bytes=vmem)


def _dot(a, b):
    return jnp.dot(a, b, preferred_element_type=F32)


def _ada_kernel(c_ref, w_ref, b_ref, o_ref):
    c = c_ref[...]
    a = c * jax.nn.sigmoid(c)
    o_ref[...] = jnp.dot(a, w_ref[...], preferred_element_type=F32, precision=HIGHEST) + b_ref[...]


def _ada_mod(cc, w, b):
    d, n6 = w.shape
    tn = min(512, n6)
    return pl.pallas_call(
        _ada_kernel,
        grid=(n6 // tn,),
        in_specs=[pl.BlockSpec((8, d), lambda j: (0, 0)),
                  pl.BlockSpec((d, tn), lambda j: (0, j)),
                  pl.BlockSpec((1, tn), lambda j: (0, j))],
        out_specs=pl.BlockSpec((8, tn), lambda j: (0, j)),
        out_shape=jax.ShapeDtypeStruct((8, n6), F32),
        compiler_params=_params(("parallel",)),
        name="ada_mod",
    )(cc, w, b)


def _inproj_kernel(x_ref, g_ref, sh_ref, sc_ref, w_ref, *rest, bounds, acts, riders):
    nr = len(riders)
    n_in = nr + (1 if any(riders) else 0)
    rider_in, rest = rest[:n_in], rest[n_in:]
    outs, rider_out, xn_ref = rest[:len(bounds)], rest[len(bounds):len(bounds) + nr], rest[-1]
    j = pl.program_id(1)

    @pl.when(j == 0)
    def _():
        x = x_ref[...].astype(F32)
        y = x * lax.rsqrt(jnp.mean(x * x, axis=-1, keepdims=True) + NORM_EPS)
        y = y * g_ref[...]
        y = y * (1.0 + sc_ref[...]) + sh_ref[...]
        xn_ref[...] = y.astype(BF16)

    for (j0, j1), act, o_ref in zip(bounds, acts, outs):
        @pl.when((j >= j0) & (j < j1))
        def _(o_ref=o_ref, act=act):
            r = _dot(xn_ref[...], w_ref[...])
            if act:
                r = jax.nn.sigmoid(r)
            o_ref[...] = r.astype(o_ref.dtype)

    for r_in, r_out, deint in zip(rider_in, rider_out, riders):
        if deint:
            for b in range(r_in.shape[1] // (2 * LANE)):
                sl = slice(2 * b * LANE, (2 * b + 2) * LANE)
                r_out[:, sl] = _dot(r_in[:, sl].astype(BF16), rider_in[-1][...]).astype(r_out.dtype)
        else:
            r_out[...] = r_in[...].astype(r_out.dtype)


def _deinterleave_perm():
    src = jnp.arange(2 * LANE)
    dst = (src % 2) * LANE + src // 2
    return jnp.zeros((2 * LANE, 2 * LANE), BF16).at[src, dst].set(1)


def _norm_mod_proj(x2d, g, sh, sc, w, widths, acts, rows_per_mod, tm, tn, riders=()):
    n, d = x2d.shape
    ncols = w.shape[1]
    tm = min(tm, n)
    tn = min(tn, min(widths))
    bounds, off = [], 0
    for wd in widths:
        bounds.append((off // tn, (off + wd) // tn))
        off += wd
    tiles_per_mod = rows_per_mod // tm
    nj = ncols // tn
    nsteps = (n // tm) * nj

    def out_map(i, j, j0, nj):
        return (i, jnp.clip(j - j0, 0, nj - 1))

    out_specs = [pl.BlockSpec((tm, tn), functools.partial(out_map, j0=j0, nj=j1 - j0))
                 for (j0, j1) in bounds]
    out_shape = [jax.ShapeDtypeStruct((n, wd), BF16) for wd in widths]
    rider_args, rider_specs = [], []
    for arr, _ in riders:
        spec = pl.BlockSpec((arr.shape[0] // nsteps, arr.shape[1]), lambda i, j: (i * nj + j, 0))
        rider_args.append(arr)
        rider_specs.append(spec)
        out_specs.append(spec)
        out_shape.append(jax.ShapeDtypeStruct(arr.shape, BF16))
    if any(flag for _, flag in riders):
        rider_args.append(_deinterleave_perm())
        rider_specs.append(pl.BlockSpec((2 * LANE, 2 * LANE), lambda i, j: (0, 0)))
    return pl.pallas_call(
        functools.partial(_inproj_kernel, bounds=tuple(bounds), acts=tuple(acts),
                          riders=tuple(flag for _, flag in riders)),
        grid=(n // tm, nj),
        in_specs=[pl.BlockSpec((tm, d), lambda i, j: (i, 0)),
                  pl.BlockSpec((1, d), lambda i, j: (0, 0)),
                  pl.BlockSpec((None, 1, d), lambda i, j: (i // tiles_per_mod, 0, 0)),
                  pl.BlockSpec((None, 1, d), lambda i, j: (i // tiles_per_mod, 0, 0)),
                  pl.BlockSpec((d, tn), lambda i, j: (0, j))] + rider_specs,
        out_specs=out_specs,
        out_shape=out_shape,
        scratch_shapes=[pltpu.VMEM((tm, d), BF16)],
        compiler_params=_params(("parallel", "arbitrary")),
        name="norm_mod_proj",
    )(x2d, g, sh, sc, w, *rider_args)


def _s5_matrices(lam_re, lam_im, log_dt, b_re, b_im, c_re, c_im, d_skip, t):
    lam = lax.complex(lam_re.astype(F32), lam_im.astype(F32))
    dt = jnp.exp(log_dt.astype(F32))[..., None]
    lam_dt = lam * dt
    lam_bar = jnp.exp(lam_dt)
    b_bar = ((lam_bar - 1) / lam)[..., None] * lax.complex(b_re.astype(F32), b_im.astype(F32))
    c_out = lax.complex(c_re.astype(F32), c_im.astype(F32))
    g, p = lam.shape[1], lam.shape[2]
    gs = b_bar.shape[-1]
    k = jnp.arange(t + 1, dtype=F32)
    pw = jnp.exp(lam_dt[..., None] * k)
    kk = jnp.einsum('dgcp,dgpk,dgpe->dgkce', c_out, pw[..., :t], b_bar, precision=HIGHEST).real
    kf, kb = kk[0], kk[1]
    zero_lag = kf[:, :1] + kb[:, :1]
    k_all = jnp.concatenate([kb[:, :0:-1], zero_lag, kf[:, 1:]], axis=1)
    jj = jnp.arange(t)
    lag_idx = jj[None, :] - jj[:, None] + (t - 1)
    m_intra = k_all[:, lag_idx]
    m_intra = m_intra.transpose(0, 1, 4, 2, 3).reshape(g, t * gs, t * gs)
    m_intra = m_intra + jnp.eye(t * gs, dtype=F32)[None] * jnp.tile(
        d_skip.astype(F32).reshape(g, 1, gs), (1, t, 1)).reshape(g, 1, t * gs)
    in_f = pw[0][..., t - 1 - jj][..., None] * b_bar[0][:, :, None, :]
    in_b = pw[1][..., jj][..., None] * b_bar[1][:, :, None, :]

    def to_in(z):
        return z.transpose(0, 2, 3, 1).reshape(g, t * gs, p)

    m_in = jnp.concatenate([to_in(in_f.real), to_in(in_f.imag), to_in(in_b.real), to_in(in_b.imag)], axis=-1)
    out_f = c_out[0][:, :, :, None] * pw[0][:, None, :, 1 + jj]
    out_b = c_out[1][:, :, :, None] * pw[1][:, None, :, t - jj]

    def to_out(z):
        return z.transpose(0, 2, 3, 1).reshape(g, p, t * gs)

    m_out = jnp.concatenate([to_out(out_f.real), -to_out(out_f.imag), to_out(out_b.real), -to_out(out_b.imag)],
                            axis=1)
    mu = pw[..., t]
    mre, mim = mu.real, mu.imag
    coef = jnp.stack([
        jnp.concatenate([mre[0], mre[0], mre[1], mre[1]], axis=-1),
        jnp.concatenate([-mim[0], mim[0], -mim[1], mim[1]], axis=-1),
        jnp.concatenate([mim[0], -mim[0], mim[1], -mim[1]], axis=-1)])
    return m_intra.astype(BF16), m_in.astype(BF16), m_out.astype(BF16), coef


def _s5_in_kernel(a_ref, m_ref, o_ref):
    o_ref[...] = _dot(a_ref[...], m_ref[...])


def _s5_chunk_states(a, m_in):
    g, nc, kdim = a.shape
    s = m_in.shape[2]
    nb = min(nc, 1024)
    return pl.pallas_call(
        _s5_in_kernel,
        grid=(g, nc // nb),
        in_specs=[pl.BlockSpec((None, nb, kdim), lambda gi, i: (gi, i, 0)),
                  pl.BlockSpec((None, kdim, s), lambda gi, i: (gi, 0, 0))],
        out_specs=pl.BlockSpec((None, nb, s), lambda gi, i: (gi, i, 0)),
        out_shape=jax.ShapeDtypeStruct((g, nc, s), F32),
        compiler_params=_params(("parallel", "arbitrary")),
        name="s5_chunk_states",
    )(a, m_in)


def _s5_scan_kernel(sf_ref, sb_ref, init_ref, coef_ref, ef_ref, eb_ref, fin_ref, st_ref, *, nblk, half):
    k = pl.program_id(1)
    nk = pl.num_programs(1)
    lanes = 2 * half

    @pl.when(k == 0)
    def _():
        s0f = init_ref[:, 0:lanes]
        s0b = init_ref[:, lanes:2 * lanes]
        st_ref[0] = s0f
        st_ref[1] = pltpu.roll(s0f, half, 1)
        st_ref[2] = s0b
        st_ref[3] = pltpu.roll(s0b, half, 1)

    def body(r, carry):
        sf, sfw, sb, sbw = carry
        rb = nblk - 1 - r
        xf = sf_ref[r]
        xb = sb_ref[rb]
        ef_ref[r] = sf
        eb_ref[rb] = sb
        xfw = pltpu.roll(xf, half, 1)
        xbw = pltpu.roll(xb, half, 1)
        af, bf, bfw = coef_ref[0, :, 0:lanes], coef_ref[1, :, 0:lanes], coef_ref[2, :, 0:lanes]
        ab, bb, bbw = (coef_ref[0, :, lanes:2 * lanes], coef_ref[1, :, lanes:2 * lanes],
                       coef_ref[2, :, lanes:2 * lanes])
        return (sf * af + sfw * bf + xf, sfw * af + sf * bfw + xfw,
                sb * ab + sbw * bb + xb, sbw * ab + sb * bbw + xbw)

    sf, sfw, sb, sbw = lax.fori_loop(0, nblk, body, (st_ref[0], st_ref[1], st_ref[2], st_ref[3]))
    st_ref[0] = sf
    st_ref[1] = sfw
    st_ref[2] = sb
    st_ref[3] = sbw

    @pl.when(k == nk - 1)
    def _():
        fin_ref[:, 0:lanes] = sf
        fin_ref[:, lanes:2 * lanes] = sb


def _s5_scan(s_t, init, coef):
    bsz, nc, g, s4 = s_t.shape
    lanes = s4 // 2
    nblk = min(nc, 128)
    nk = nc // nblk
    return pl.pallas_call(
        functools.partial(_s5_scan_kernel, nblk=nblk, half=lanes // 2),
        grid=(bsz, nk),
        in_specs=[pl.BlockSpec((None, nblk, g, lanes), lambda b, k: (b, k, 0, 0)),
                  pl.BlockSpec((None, nblk, g, lanes), lambda b, k: (b, nk - 1 - k, 0, 1)),
                  pl.BlockSpec((None, g, s4), lambda b, k: (b, 0, 0)),
                  pl.BlockSpec((3, g, s4), lambda b, k: (0, 0, 0))],
        out_specs=[pl.BlockSpec((None, nblk, g, lanes), lambda b, k: (b, k, 0, 0)),
                   pl.BlockSpec((None, nblk, g, lanes), lambda b, k: (b, nk - 1 - k, 0, 0)),
                   pl.BlockSpec((None, g, s4), lambda b, k: (b, 0, 0))],
        out_shape=[jax.ShapeDtypeStruct((bsz, nc, g, lanes), F32),
                   jax.ShapeDtypeStruct((bsz, nc, g, lanes), F32),
                   jax.ShapeDtypeStruct((bsz, g, s4), F32)],
        scratch_shapes=[pltpu.VMEM((4, g, lanes), F32)],
        compiler_params=_params(("parallel", "arbitrary")),
        name="s5_scan",
    )(s_t, s_t, init, coef)


def _s5_out_kernel(a_ref, e_ref, mi_ref, mo_ref, o_ref):
    y = _dot(a_ref[...], mi_ref[...]) + _dot(e_ref[...], mo_ref[...])
    o_ref[...] = y.astype(o_ref.dtype)


def _s5_chunk_outputs(a, e_in, m_intra, m_out):
    g, nc, kdim = a.shape
    s = e_in.shape[2]
    nb = min(nc, 1024)
    return pl.pallas_call(
        _s5_out_kernel,
        grid=(g, nc // nb),
        in_specs=[pl.BlockSpec((None, nb, kdim), lambda gi, i: (gi, i, 0)),
                  pl.BlockSpec((None, nb, s), lambda gi, i: (gi, i, 0)),
                  pl.BlockSpec((None, kdim, kdim), lambda gi, i: (gi, 0, 0)),
                  pl.BlockSpec((None, s, kdim), lambda gi, i: (gi, 0, 0))],
        out_specs=pl.BlockSpec((None, nb, kdim), lambda gi, i: (gi, i, 0)),
        out_shape=jax.ShapeDtypeStruct((g, nc, kdim), BF16),
        compiler_params=_params(("parallel", "arbitrary")),
        name="s5_chunk_outputs",
    )(a, e_in, m_intra, m_out)


def _s5_readout_kernel(y_ref, wg_ref, o_ref):
    y = jax.nn.gelu(y_ref[...].astype(F32))
    gate = _dot(y.astype(BF16), wg_ref[...])
    o_ref[...] = (y * jax.nn.sigmoid(gate)).astype(o_ref.dtype)


def _s5_readout(y2d, wg):
    n, w = y2d.shape
    tm = min(n, 1024)
    return pl.pallas_call(
        _s5_readout_kernel,
        grid=(n // tm,),
        in_specs=[pl.BlockSpec((tm, w), lambda i: (i, 0)),
                  pl.BlockSpec((w, w), lambda i: (0, 0))],
        out_specs=pl.BlockSpec((tm, w), lambda i: (i, 0)),
        out_shape=jax.ShapeDtypeStruct((n, w), BF16),
        compiler_params=_params(("parallel",)),
        name="s5_readout",
    )(y2d, wg)


def _to_chunks(u2d, g, t):
    n = u2d.shape[0]
    return u2d.reshape(n // t, t, g, S5_GROUP).transpose(2, 0, 1, 3).reshape(g, n // t, t * S5_GROUP)


def _from_chunks(y, g, t):
    nc = y.shape[1]
    return y.reshape(g, nc, t, S5_GROUP).transpose(1, 2, 0, 3).reshape(nc * t, g * S5_GROUP)


def _s5_branch(ua, uca, bsz, lam_re, lam_im, log_dt, b_re, b_im, c_re, c_im, d_skip, w_glu):
    g = lam_re.shape[1]
    t = S5_CHUNK
    m_intra, m_in, m_out, coef = _s5_matrices(lam_re, lam_im, log_dt, b_re, b_im, c_re, c_im, d_skip, t)
    s4 = m_in.shape[2]

    def states(u2d):
        a = _to_chunks(u2d, g, t)
        s = _s5_chunk_states(a, m_in)
        nc = s.shape[1] // bsz
        return a, s.reshape(g, bsz, nc, s4).transpose(1, 2, 0, 3)

    _, s_ctx = states(uca)
    _, _, seed = _s5_scan(s_ctx, jnp.zeros((bsz, g, s4), F32), coef)
    a_lat, s_lat = states(ua)
    e_f, e_b, _ = _s5_scan(s_lat, seed, coef)
    e_in = jnp.concatenate([e_f, e_b], axis=-1).astype(BF16)
    e_in = e_in.transpose(2, 0, 1, 3).reshape(g, -1, s4)
    y = _s5_chunk_outputs(a_lat, e_in, m_intra, m_out)
    return _s5_readout(_from_chunks(y, g, t), w_glu.astype(BF16))


HY_KH = GRID_W + 1
HY_KP = 72
HY_QN = 16


def _hy_phase(w_cols, r_cnt):
    n1_tot = 2 * GRID_W
    n_fft = n1_tot * r_cnt
    kh = GRID_W + 1
    k1 = np.arange(kh, dtype=np.int64)[None, :, None]
    r = np.arange(r_cnt, dtype=np.int64)[:, None, None]
    n1 = np.arange(w_cols, dtype=np.int64)[None, None, :]
    return 2.0 * np.pi * ((k1 * (r_cnt * n1 + r)) % n_fft) / n_fft


def _hy_fwd_table(r_cnt, m):
    th = _hy_phase(GRID_W * m, r_cnt)
    kh = th.shape[1]
    f = np.zeros((r_cnt, 2 * HY_KP, GRID_W * m), np.float32)
    f[:, :kh] = np.cos(th)
    f[:, HY_KP:HY_KP + kh] = -np.sin(th)
    return jnp.asarray(f, BF16)


def _hy_inv_table(r_cnt):
    th = _hy_phase(GRID_W, r_cnt)
    kh = th.shape[1]
    c = np.full((kh,), 2.0)
    c[0] = c[-1] = 1.0
    scale = c[None, :, None] / (2 * GRID_W * r_cnt)
    g = np.zeros((r_cnt, GRID_W, 2 * HY_KP), np.float32)
    g[:, :, :kh] = (scale * np.cos(th)).transpose(0, 2, 1)
    g[:, :, HY_KP:HY_KP + kh] = (-scale * np.sin(th)).transpose(0, 2, 1)
    return jnp.asarray(g, BF16)


def _hy_dft_tables(r_cnt):
    k = np.arange(r_cnt, dtype=np.int64)
    th = 2.0 * np.pi * ((k[:, None] * k[None, :]) % r_cnt) / r_cnt
    fc, fs = np.cos(th), np.sin(th)
    fwd = np.block([[fc, fs], [-fs, fc]]).astype(np.float32)
    inv = np.block([[fc, -fs], [fs, fc]]).astype(np.float32)
    return jnp.asarray(fwd, BF16), jnp.asarray(inv, BF16)


def _hy_short_kernel(x_ref, w_ref, o_ref, *, rows, chunk):
    wk = w_ref[...]
    w0, w1, w2 = wk[0:1], wk[1:2], wk[2:3]
    gw = GRID_W
    col = lax.broadcasted_iota(jnp.int32, (gw, x_ref.shape[1]), 0)

    def piece(a, n):
        return x_ref[pl.ds(a, n), :].astype(F32)

    last = piece(rows - gw, gw)
    prev0 = jnp.where(col == 0, 0.0, pltpu.roll(last, 1, 0))
    o_ref[0:gw, :] = (w0 * prev0 + w1 * piece(0, gw) + w2 * piece(gw, gw)).astype(o_ref.dtype)
    first = piece(0, gw)
    next_l = jnp.where(col == gw - 1, 0.0, pltpu.roll(first, gw - 1, 0))
    o_ref[rows - gw:rows, :] = (w0 * piece(rows - 2 * gw, gw) + w1 * last + w2 * next_l).astype(o_ref.dtype)

    def body(i, carry):
        a = pl.multiple_of(gw + i * chunk, gw)
        o_ref[pl.ds(a, chunk), :] = (w0 * piece(a - gw, chunk) + w1 * piece(a, chunk)
                                     + w2 * piece(a + gw, chunk)).astype(o_ref.dtype)
        return carry

    n_full = (rows - 2 * gw) // chunk
    lax.fori_loop(0, n_full, body, 0)
    rem = rows - 2 * gw - n_full * chunk
    if rem:
        a = gw + n_full * chunk
        o_ref[a:a + rem, :] = (w0 * piece(a - gw, rem) + w1 * piece(a, rem)
                               + w2 * piece(a + gw, rem)).astype(o_ref.dtype)


def _hy_short_conv(uh, short_w, bsz, seq):
    n, ch = uh.shape
    ct = min(ch, LANE)
    chunk = min(512, seq - 2 * GRID_W)
    return pl.pallas_call(
        functools.partial(_hy_short_kernel, rows=seq, chunk=chunk),
        grid=(bsz, ch // ct),
        in_specs=[pl.BlockSpec((seq, ct), lambda b, c: (b, c)),
                  pl.BlockSpec((3, ct), lambda b, c: (0, c))],
        out_specs=pl.BlockSpec((seq, ct), lambda b, c: (b, c)),
        out_shape=jax.ShapeDtypeStruct((n, ch), BF16),
        compiler_params=_params(("parallel", "parallel")),
        name="hy_short_conv",
    )(uh, short_w)


def _hy_fwd1_kernel(*refs, m):
    x_refs, f_ref, o_ref, s_ref = refs[:m], refs[m], refs[m + 1], refs[m + 2]
    gw = GRID_W
    for q in range(HY_QN):
        xq = [x[q * gw:(q + 1) * gw, :] for x in x_refs]
        xq = xq[0] if m == 1 else jnp.concatenate(xq, axis=0)
        s_ref[q] = _dot(f_ref[q], xq)
    t = jnp.swapaxes(s_ref[...], 0, 1).astype(o_ref.dtype)
    o_ref[0] = t[0:HY_KH]
    o_ref[1] = t[HY_KP:HY_KP + HY_KH]


def _hy_fwd1(xs, col_blk0, width, bsz, r_cnt, ct):
    m = len(xs)
    table = _hy_fwd_table(r_cnt, m)
    rb = HY_QN * GRID_W
    nrb = r_cnt // HY_QN
    x_spec = pl.BlockSpec((rb, ct), lambda b, i, c: (b * nrb + i, col_blk0 + c))
    return pl.pallas_call(
        functools.partial(_hy_fwd1_kernel, m=m),
        grid=(bsz, nrb, width // ct),
        in_specs=[x_spec] * m + [pl.BlockSpec((HY_QN, 2 * HY_KP, GRID_W * m), lambda b, i, c: (i, 0, 0))],
        out_specs=pl.BlockSpec((None, 2, HY_KH, HY_QN, ct), lambda b, i, c: (b, 0, 0, i, c)),
        out_shape=jax.ShapeDtypeStruct((bsz, 2, HY_KH, r_cnt, width), BF16),
        scratch_shapes=[pltpu.VMEM((HY_QN, 2 * HY_KP, ct), F32)],
        compiler_params=_params(("parallel", "parallel", "arbitrary")),
        name="hy_fwd_stage1",
    )(*xs, table)


def _hy_spec_kernel(a_ref, f2_ref, o_ref):
    r = a_ref.shape[1]
    x = a_ref[...].reshape(2 * r, a_ref.shape[2])
    o_ref[...] = _dot(f2_ref[...], x).reshape(o_ref.shape)


def _hy_filter_spectrum(a, f2, ct):
    _, _, kh, r_cnt, width = a.shape
    return pl.pallas_call(
        _hy_spec_kernel,
        grid=(kh, width // ct),
        in_specs=[pl.BlockSpec((None, 2, None, r_cnt, ct), lambda k, c: (0, 0, k, 0, c)),
                  pl.BlockSpec((2 * r_cnt, 2 * r_cnt), lambda k, c: (0, 0))],
        out_specs=pl.BlockSpec((2, None, r_cnt, ct), lambda k, c: (0, k, 0, c)),
        out_shape=jax.ShapeDtypeStruct((2, kh, r_cnt, width), F32),
        compiler_params=_params(("parallel", "parallel")),
        name="hy_filter_spectrum",
    )(a, f2)


def _hy_mid_kernel(a_ref, kf_ref, invn_ref, f2_ref, f2i_ref, o_ref):
    r = a_ref.shape[1]
    x = a_ref[...].reshape(2 * r, a_ref.shape[2])
    b = _dot(f2_ref[...], x)
    br, bi = b[:r], b[r:]
    kr, ki = kf_ref[0], kf_ref[1]
    s = invn_ref[...]
    cr = (br * kr - bi * ki) * s
    ci = (br * ki + bi * kr) * s
    c = jnp.concatenate([cr, ci], axis=0).astype(BF16)
    o_ref[...] = _dot(f2i_ref[...], c).reshape(o_ref.shape).astype(o_ref.dtype)


def _hy_mid(a, kf, invn, order, f2, f2i, ct):
    bsz, _, kh, r_cnt, width = a.shape
    nct = width // ct
    return pl.pallas_call(
        _hy_mid_kernel,
        grid=(kh, nct, bsz),
        in_specs=[pl.BlockSpec((None, 2, None, r_cnt, ct), lambda k, c, b: (b, 0, k, 0, c)),
                  pl.BlockSpec((2, None, r_cnt, ct), lambda k, c, b: (0, k, 0, order * nct + c)),
                  pl.BlockSpec((1, ct), lambda k, c, b: (0, order * nct + c)),
                  pl.BlockSpec((2 * r_cnt, 2 * r_cnt), lambda k, c, b: (0, 0)),
                  pl.BlockSpec((2 * r_cnt, 2 * r_cnt), lambda k, c, b: (0, 0))],
        out_specs=pl.BlockSpec((None, 2, None, r_cnt, ct), lambda k, c, b: (b, 0, k, 0, c)),
        out_shape=jax.ShapeDtypeStruct(a.shape, BF16),
        compiler_params=_params(("parallel", "parallel", "arbitrary")),
        name="hy_spectrum_product",
    )(a, kf, invn, f2, f2i)


def _hy_inv1_kernel(d_ref, g_ref, xin_ref, xm_ref, bias_ref, o_ref, t_ref, s_ref):
    gw = GRID_W
    pad = jnp.zeros((HY_KP - HY_KH,) + t_ref.shape[1:], F32)
    t_ref[0:HY_KH] = d_ref[0].astype(F32)
    t_ref[HY_KH:HY_KP] = pad
    t_ref[HY_KP:HY_KP + HY_KH] = d_ref[1].astype(F32)
    t_ref[HY_KP + HY_KH:2 * HY_KP] = pad
    s_ref[...] = jnp.swapaxes(t_ref[...], 0, 1)
    for q in range(HY_QN):
        y = _dot(g_ref[q], s_ref[q].astype(BF16))
        rows = slice(q * gw, (q + 1) * gw)
        xin = xin_ref[rows, :].astype(F32)
        o_ref[rows, :] = (xm_ref[rows, :].astype(F32) * (y + xin * bias_ref[...])).astype(o_ref.dtype)


def _hy_inv1(d, xin, xin_blk0, xm, xm_blk0, bias, ct):
    bsz, _, kh, r_cnt, width = d.shape
    table = _hy_inv_table(r_cnt)
    rb = HY_QN * GRID_W
    nrb = r_cnt // HY_QN
    return pl.pallas_call(
        _hy_inv1_kernel,
        grid=(bsz, nrb, width // ct),
        in_specs=[pl.BlockSpec((None, 2, kh, HY_QN, ct), lambda b, i, c: (b, 0, 0, i, c)),
                  pl.BlockSpec((HY_QN, GRID_W, 2 * HY_KP), lambda b, i, c: (i, 0, 0)),
                  pl.BlockSpec((rb, ct), lambda b, i, c: (b * nrb + i, xin_blk0 + c)),
                  pl.BlockSpec((rb, ct), lambda b, i, c: (b * nrb + i, xm_blk0 + c)),
                  pl.BlockSpec((1, ct), lambda b, i, c: (0, c))],
        out_specs=pl.BlockSpec((rb, ct), lambda b, i, c: (b * nrb + i, c)),
        out_shape=jax.ShapeDtypeStruct((bsz * r_cnt * GRID_W, width), BF16),
        scratch_shapes=[pltpu.VMEM((2 * HY_KP, HY_QN, ct), F32), pltpu.VMEM((HY_QN, 2 * HY_KP, ct), F32)],
        compiler_params=_params(("parallel", "parallel", "arbitrary")),
        name="hy_inv_stage1",
    )(d, table, xin, xm, bias)


def _hy_filter_kernel(w1_ref, b1_ref, f1_ref, w2_ref, b2_ref, f2_ref, w3_ref, b3_ref, dec_ref, om_ref, ph_ref,
                      o_ref, sum_ref, *, direction, seqlen, r_cnt, rows):
    i = pl.program_id(1)
    row = lax.broadcasted_iota(jnp.int32, (rows, LANE), 0) + i * rows
    lane = lax.broadcasted_iota(jnp.int32, (rows, LANE), 1)
    r = jnp.right_shift(row, GRID_W.bit_length() - 1)
    w = jnp.bitwise_and(row, GRID_W - 1)
    n = (w + GRID_W * direction) * r_cnt + r
    p = n if direction == 0 else 2 * seqlen - n
    pf = p.astype(F32)
    t = pf / (seqlen - 1)
    feats = jnp.where(lane == 0, t, jnp.cos(pf * om_ref[...] + ph_ref[...]))
    hid = jnp.sin(f1_ref[...] * (jnp.dot(feats, w1_ref[...], preferred_element_type=F32, precision=HIGHEST)
                                 + b1_ref[...]))
    hid = jnp.sin(f2_ref[...] * (jnp.dot(hid, w2_ref[...], preferred_element_type=F32, precision=HIGHEST)
                                 + b2_ref[...]))
    filt = _dot(hid.astype(BF16), w3_ref[...].astype(BF16)) + b3_ref[...]
    reps = filt.shape[1] // LANE
    t_w = jnp.concatenate([t] * reps, axis=1)
    keep = jnp.concatenate([jnp.where(p < seqlen, 1.0, 0.0)] * reps, axis=1)
    filt = filt * jnp.exp(-t_w * jnp.abs(dec_ref[...])) * keep
    o_ref[...] = filt.astype(o_ref.dtype)

    @pl.when(i == 0)
    def _():
        sum_ref[...] = jnp.zeros_like(sum_ref)

    sum_ref[...] += jnp.sum(jnp.abs(filt), axis=0, keepdims=True)


def _hy_filter_signal(direction, seqlen, w1p, b1, f1, w2, b2, f2, w3, b3, decay, omega, phase, width):
    r_cnt = seqlen // GRID_W
    rows = min(256, seqlen)
    ct = width
    nct = width // ct
    hid = w2.shape[0]
    off = direction * nct
    const = lambda shape: pl.BlockSpec(shape, lambda c, i: (0, 0))
    colv = pl.BlockSpec((1, ct), lambda c, i: (0, off + c))
    return pl.pallas_call(
        functools.partial(_hy_filter_kernel, direction=direction, seqlen=seqlen, r_cnt=r_cnt, rows=rows),
        grid=(nct, seqlen // rows),
        in_specs=[const((LANE, hid)), const((1, hid)), const((1, hid)), const((hid, hid)), const((1, hid)),
                  const((1, hid)), pl.BlockSpec((hid, ct), lambda c, i: (0, off + c)), colv, colv,
                  const((1, LANE)), const((1, LANE))],
        out_specs=[pl.BlockSpec((rows, ct), lambda c, i: (i, c)),
                   pl.BlockSpec((1, ct), lambda c, i: (0, c))],
        out_shape=[jax.ShapeDtypeStruct((seqlen, width), BF16), jax.ShapeDtypeStruct((1, width), F32)],
        compiler_params=_params(("parallel", "arbitrary")),
        name="hy_filter_signal",
    )(w1p, b1, f1, w2, b2, f2, w3, b3, decay, omega, phase)


def _hyena_branch(uh, bsz, seq, short_w, w1, b1, f1, w2, b2, f2, w3, b3, decay, hy_bias):
    width = hy_bias.shape[1]
    r_cnt = seq // GRID_W
    ct = min(width, 512)
    nct = width // ct
    n_emb = w1.shape[0]
    n_bands = (n_emb - 1) // 2
    hid = w1.shape[1]
    bands = jnp.linspace(1e-4, n_bands - 1, n_bands, dtype=F32)
    omega = jnp.zeros((1, LANE), F32).at[0, 1:1 + n_bands].set((2 * jnp.pi / seq) * bands)
    omega = omega.at[0, 1 + n_bands:1 + 2 * n_bands].set((2 * jnp.pi / seq) * bands)
    phase = jnp.zeros((1, LANE), F32).at[0, 1 + n_bands:1 + 2 * n_bands].set(0.5 * jnp.pi)
    w1p = jnp.zeros((LANE, hid), F32).at[:n_emb].set(w1)
    sigs, sums = [], []
    for direction in range(2):
        sig, ssum = _hy_filter_signal(direction, seq, w1p, b1[None], f1[None], w2, b2[None], f2[None], w3,
                                      b3[None], decay[None], omega, phase, HY_ORDER * width)
        sigs.append(sig)
        sums.append(ssum)
    invn = 1.0 / (sums[0] + sums[1] + HY_NORM_EPS)
    f2m, f2i = _hy_dft_tables(r_cnt)
    ct3 = min(width, 1024)
    kf = _hy_filter_spectrum(_hy_fwd1(sigs, 0, HY_ORDER * width, 1, r_cnt, ct), f2m, ct3)

    z = _hy_short_conv(uh, short_w.astype(F32), bsz, seq)
    a = _hy_fwd1([z], 0, width, bsz, r_cnt, ct)
    d = _hy_mid(a, kf, invn, 0, f2m, f2i, ct3)
    y1 = _hy_inv1(d, z, 0, z, nct, hy_bias[0:1].astype(F32), ct)
    a = _hy_fwd1([y1], 0, width, bsz, r_cnt, ct)
    d = _hy_mid(a, kf, invn, 1, f2m, f2i, ct3)
    return _hy_inv1(d, y1, 0, z, 2 * nct, hy_bias[1:2].astype(F32), ct)


def _merge1_kernel(ya_ref, yh_ref, ug_ref, wa_ref, wh_ref, o_ref):
    d = o_ref.shape[1]
    a = _dot(ya_ref[...], wa_ref[...])
    h = _dot(yh_ref[...], wh_ref[...])
    ga = ug_ref[:, :d].astype(F32)
    gh = ug_ref[:, d:].astype(F32)
    o_ref[...] = (ga * a + gh * h).astype(o_ref.dtype)


def _merge1(ya, yh, ug, wa, wh):
    n, ws = ya.shape
    wh_in = yh.shape[1]
    d = wa.shape[1]
    tm = min(n, 512)
    return pl.pallas_call(
        _merge1_kernel,
        grid=(n // tm,),
        in_specs=[pl.BlockSpec((tm, ws), lambda i: (i, 0)),
                  pl.BlockSpec((tm, wh_in), lambda i: (i, 0)),
                  pl.BlockSpec((tm, 2 * d), lambda i: (i, 0)),
                  pl.BlockSpec((ws, d), lambda i: (0, 0)),
                  pl.BlockSpec((wh_in, d), lambda i: (0, 0))],
        out_specs=pl.BlockSpec((tm, d), lambda i: (i, 0)),
        out_shape=jax.ShapeDtypeStruct((n, d), BF16),
        compiler_params=_params(("parallel",)),
        name="merge_branches",
    )(ya, yh, ug, wa, wh)


def _merge2_kernel(m_ref, x_ref, g1_ref, ng_ref, sh_ref, sc_ref, wo_ref, wr_ref,
                   h1_ref, hm_ref, r1_ref, r2_ref):
    mix = _dot(m_ref[...], wo_ref[...])
    h1 = x_ref[...] + g1_ref[...] * mix
    h1_ref[...] = h1
    y = h1 * lax.rsqrt(jnp.mean(h1 * h1, axis=-1, keepdims=True) + NORM_EPS)
    y = y * ng_ref[...]
    y = y * (1.0 + sc_ref[...]) + sh_ref[...]
    hi = y.astype(BF16)
    lo = (y - hi.astype(F32)).astype(BF16)
    n_sub = y.shape[1] // LANE
    for s in range(n_sub):
        hm_ref[pl.ds(s, y.shape[0], stride=n_sub), :] = y[:, s * LANE:(s + 1) * LANE]
    r1_ref[...] = _dot(hi, wr_ref[...])
    r2_ref[...] = _dot(lo, wr_ref[...])


def _merge2(m, x2d, g1, ng, sh2, sc2, wo, wr, rows_per_mod):
    n, d = x2d.shape
    tm = min(n, 512)
    tiles_per_mod = rows_per_mod // tm
    nr = wr.shape[1]
    mod_spec = pl.BlockSpec((None, 1, d), lambda i: (i // tiles_per_mod, 0, 0))
    return pl.pallas_call(
        _merge2_kernel,
        grid=(n // tm,),
        in_specs=[pl.BlockSpec((tm, d), lambda i: (i, 0)),
                  pl.BlockSpec((tm, d), lambda i: (i, 0)),
                  mod_spec,
                  pl.BlockSpec((1, d), lambda i: (0, 0)),
                  mod_spec, mod_spec,
                  pl.BlockSpec((d, d), lambda i: (0, 0)),
                  pl.BlockSpec((d, nr), lambda i: (0, 0))],
        out_specs=[pl.BlockSpec((tm, d), lambda i: (i, 0)),
                   pl.BlockSpec((tm * (d // LANE), LANE), lambda i: (i, 0)),
                   pl.BlockSpec((tm, nr), lambda i: (i, 0)),
                   pl.BlockSpec((tm, nr), lambda i: (i, 0))],
        out_shape=[jax.ShapeDtypeStruct((n, d), F32),
                   jax.ShapeDtypeStruct((n * (d // LANE), LANE), F32),
                   jax.ShapeDtypeStruct((n, nr), F32),
                   jax.ShapeDtypeStruct((n, nr), F32)],
        compiler_params=_params(("parallel",)),
        name="out_proj_norm_router",
    )(m, x2d, g1, ng, sh2, sc2, wo, wr)


def _moe_kernel(be_ref, nv_ref, tok0_ref, tokn_ref, hm_ref, wgu_ref, bgu_ref, wd_ref, bd_ref, o_ref,
                xbuf_ref, acc_ref, sem, *, nj):
    i = pl.program_id(0)
    j = pl.program_id(1)
    nv = nv_ref[0]
    tm = acc_ref.shape[0]

    n_sub = hm_ref.shape[1]

    def row_copy(tok, r, slot):
        return pltpu.make_async_copy(hm_ref.at[tok], xbuf_ref.at[slot, pl.ds(r * MOE_PITCH, n_sub), :],
                                     sem.at[slot])

    def wait_block(slot):
        for r in range(tm):
            row_copy(0, r, slot).wait()

    @pl.when((i == 0) & (j == 0) & (nv > 0))
    def _():
        for r in range(tm):
            row_copy(tok0_ref[0, r], r, 0).start()

    def expert_step(first, last):
        slot = i % 2
        if first:
            wait_block(slot)
            for r in range(tm):
                row_copy(tokn_ref[0, r], r, (i + 1) % 2).start()
        x = jnp.concatenate([xbuf_ref[slot, pl.ds(s, tm, stride=MOE_PITCH), :].astype(BF16)
                             for s in range(n_sub)], axis=1)
        gu = _dot(x, wgu_ref[...]) + bgu_ref[...]
        acts = []
        for b in range(gu.shape[1] // (2 * LANE)):
            glu = jnp.minimum(gu[:, 2 * b * LANE:(2 * b + 1) * LANE], SWIGLU_LIMIT)
            lin = jnp.clip(gu[:, (2 * b + 1) * LANE:(2 * b + 2) * LANE], -SWIGLU_LIMIT, SWIGLU_LIMIT)
            acts.append((glu * jax.nn.sigmoid(SWIGLU_ALPHA * glu) * (lin + 1.0)).astype(BF16))
        part = _dot(jnp.concatenate(acts, axis=1), wd_ref[...])
        if not first:
            part = part + acc_ref[...]
        if last:
            o_ref[...] = (part + bd_ref[...]).astype(o_ref.dtype)
        else:
            acc_ref[...] = part

    variants = {1: [(True, True)], 2: [(True, False), (False, True)]}.get(
        nj, [(True, False), (False, False), (False, True)])
    for first, last in variants:
        cond = (i < nv) & ((j == 0) if first else (j > 0)) & ((j == nj - 1) if last else (j < nj - 1))
        pl.when(cond)(functools.partial(expert_step, first, last))

    @pl.when((i == nv - 1) & (j == nj - 1))
    def _():
        wait_block((i + 1) % 2)

    @pl.when((i >= nv) & (j == nj - 1))
    def _():
        o_ref[...] = jnp.zeros_like(o_ref)


def _moe_blocks(hm, slot_tok, block_expert, n_valid, wgu, bgu, wd, bd):
    d = wd.shape[2]
    assert hm.shape[1] <= MOE_PITCH and hm.shape[1] * hm.shape[2] == d
    f = wd.shape[1]
    tm = MOE_TM
    tf = min(f, MOE_TF)
    nj = f // tf
    nblk = block_expert.shape[0]

    def jeff(i, j, nv):
        return jnp.where(i < nv[0], j, nj - 1)

    grid_spec = pltpu.PrefetchScalarGridSpec(
        num_scalar_prefetch=2,
        grid=(nblk, nj),
        in_specs=[pl.BlockSpec((None, 1, tm), lambda i, j, be, nv: (0, 0, 0), memory_space=pltpu.SMEM),
                  pl.BlockSpec((None, 1, tm), lambda i, j, be, nv: (jnp.minimum(i + 1, nblk - 1), 0, 0),
                               memory_space=pltpu.SMEM),
                  pl.BlockSpec(memory_space=pl.ANY),
                  pl.BlockSpec((None, d, 2 * tf), lambda i, j, be, nv: (be[i], 0, jeff(i, j, nv))),
                  pl.BlockSpec((None, 1, 2 * tf), lambda i, j, be, nv: (be[i], 0, jeff(i, j, nv))),
                  pl.BlockSpec((None, tf, d), lambda i, j, be, nv: (be[i], jeff(i, j, nv), 0)),
                  pl.BlockSpec((None, 1, d), lambda i, j, be, nv: (be[i], 0, 0))],
        out_specs=pl.BlockSpec((tm, d), lambda i, j, be, nv: (i, 0)),
        scratch_shapes=[pltpu.VMEM((2, tm * MOE_PITCH, LANE), F32), pltpu.VMEM((tm, d), F32),
                        pltpu.SemaphoreType.DMA((2,))])
    return pl.pallas_call(
        functools.partial(_moe_kernel, nj=nj),
        grid_spec=grid_spec,
        out_shape=jax.ShapeDtypeStruct((nblk * tm, d), BF16),
        compiler_params=_params(("arbitrary", "arbitrary")),
        name="moe_experts",
    )(block_expert, n_valid, slot_tok, slot_tok, hm, wgu, bgu, wd, bd)


def _route(logits, n_experts):
    n_tok = logits.shape[0]
    top_val, top_idx = lax.top_k(logits, TOP_K)
    gates = jax.nn.softmax(top_val, axis=-1)
    n_assign = n_tok * TOP_K
    flat_e = top_idx.reshape(-1).astype(jnp.int32)
    experts = jnp.arange(n_experts, dtype=jnp.int32)
    onehot = (flat_e[:, None] == experts[None, :]).astype(jnp.int32)
    csum = jnp.cumsum(onehot, axis=0)
    rank = jnp.sum(onehot * csum, axis=1) - 1
    counts = csum[-1]
    padded = (counts + MOE_TM - 1) // MOE_TM * MOE_TM
    start = jnp.cumsum(counts) - counts
    padded_end = jnp.cumsum(padded)
    padded_start = padded_end - padded
    dest = padded_start[flat_e] + rank
    n_blocks = -(-n_assign // MOE_TM) + n_experts
    cap = n_blocks * MOE_TM
    n_valid = (padded_end[-1] // MOE_TM).astype(jnp.int32)
    order = jnp.argsort(flat_e)
    slots = jnp.arange(cap, dtype=jnp.int32)
    slot_e = jnp.minimum(jnp.sum((padded_end[None, :] <= slots[:, None]).astype(jnp.int32), axis=1), n_experts - 1)
    slot_rank = slots - padded_start[slot_e]
    src = jnp.minimum(start[slot_e] + slot_rank, n_assign - 1)
    slot_tok = jnp.where(slot_rank < counts[slot_e], order[src] // TOP_K, 0).astype(jnp.int32)
    block_start = jnp.minimum(jnp.arange(n_blocks, dtype=jnp.int32), n_valid - 1) * MOE_TM
    block_expert = jnp.minimum(
        jnp.sum((padded_end[None, :] <= block_start[:, None]).astype(jnp.int32), axis=1), n_experts - 1)
    return gates, slot_tok, dest, block_expert, n_valid


def _final_kernel(h1_ref, *rest):
    yg_refs = rest[:TOP_K]
    gates_ref, g2_ref, fg_ref, o_ref = rest[TOP_K:]
    gates = gates_ref[...]
    moe = yg_refs[0][...].astype(F32) * gates[:, 0:1]
    for k in range(1, TOP_K):
        moe = moe + yg_refs[k][...].astype(F32) * gates[:, k:k + 1]
    h = h1_ref[...] + g2_ref[...] * moe
    y = h * lax.rsqrt(jnp.mean(h * h, axis=-1, keepdims=True) + NORM_EPS)
    o_ref[...] = (y * fg_ref[...]).astype(o_ref.dtype)


def _final(h1, yg, gates, g2, fg, rows_per_mod):
    n, d = h1.shape
    tm = min(n, 512)
    tiles_per_mod = rows_per_mod // tm
    nt = n // tm
    yg_specs = [pl.BlockSpec((tm, d), functools.partial(lambda i, k: (k * nt + i, 0), k=k)) for k in range(TOP_K)]
    return pl.pallas_call(
        _final_kernel,
        grid=(nt,),
        in_specs=[pl.BlockSpec((tm, d), lambda i: (i, 0))] + yg_specs + [
                  pl.BlockSpec((tm, TOP_K), lambda i: (i, 0)),
                  pl.BlockSpec((None, 1, d), lambda i: (i // tiles_per_mod, 0, 0)),
                  pl.BlockSpec((1, d), lambda i: (0, 0))],
        out_specs=pl.BlockSpec((tm, d), lambda i: (i, 0)),
        out_shape=jax.ShapeDtypeStruct((n, d), F32),
        compiler_params=_params(("parallel",)),
        name="combine_final_norm",
    )(h1, *([yg] * TOP_K), gates, g2, fg)


def kernel(x, c, ctx, c_ctx, w_ada, b_ada, norm1_g, norm2_g, w_in, s5_lam_re, s5_lam_im, s5_log_dt, s5_b_re, s5_b_im, s5_c_re, s5_c_im, s5_d, s5_w_glu, hy_short_w, hy_pos_w1, hy_pos_b1, hy_freq1, hy_pos_w2, hy_pos_b2, hy_freq2, hy_pos_w3, hy_pos_b3, hy_decay, hy_bias, w_branch_s5, w_branch_hy, w_out, router_w, router_b, w_gate_up, b_gate_up, w_down, b_down, final_g):
    bsz, seq, d = x.shape
    ctx_len = ctx.shape[1]
    depth = w_ada.shape[0]
    assert depth == 1, "only the single-layer configuration is implemented"
    n_experts = router_w.shape[2]
    s5_w = s5_d.shape[1]
    hy_w = hy_bias.shape[2]
    n_tok = bsz * seq

    cc = jnp.zeros((8, d), F32).at[:bsz].set(c).at[bsz].set(c_ctx)
    mod = _ada_mod(cc, w_ada[0], b_ada[0][None]).reshape(8, 6, 1, d)
    sh1, sc1, g1, sh2, sc2, g2 = [mod[:bsz, k] for k in range(6)]
    csh1, csc1 = mod[bsz:bsz + 1, 0], mod[bsz:bsz + 1, 1]

    w_in_b = w_in[0].astype(BF16)
    x2d = x.reshape(n_tok, d)
    riders = ((w_gate_up[0].reshape(-1, w_gate_up.shape[3]), True), (w_down[0].reshape(-1, d), False))
    ua, uh, ug, wgu_b, wd_b = _norm_mod_proj(x2d, norm1_g[0][None], sh1, sc1, w_in_b,
                                             (s5_w, 3 * hy_w, 2 * d), (False, False, True), seq, 1024, 512,
                                             riders=riders)
    (uca,) = _norm_mod_proj(ctx.reshape(bsz * ctx_len, d), norm1_g[0][None], csh1, csc1,
                            w_in_b[:, :s5_w], (s5_w,), (False,), bsz * ctx_len, 512, 1024)

    ya = _s5_branch(ua, uca, bsz, s5_lam_re[0], s5_lam_im[0], s5_log_dt[0], s5_b_re[0], s5_b_im[0],
                    s5_c_re[0], s5_c_im[0], s5_d[0], s5_w_glu[0])

    yh = _hyena_branch(uh, bsz, seq, hy_short_w[0], hy_pos_w1[0], hy_pos_b1[0], hy_freq1[0], hy_pos_w2[0],
                       hy_pos_b2[0], hy_freq2[0], hy_pos_w3[0], hy_pos_b3[0], hy_decay[0], hy_bias[0])

    m = _merge1(ya, yh, ug, w_branch_s5[0].astype(BF16), w_branch_hy[0].astype(BF16))
    rw = router_w[0]
    rw_hi = rw.astype(BF16)
    rw_lo = (rw - rw_hi.astype(F32)).astype(BF16)
    wr = jnp.zeros((d, 128), BF16).at[:, :n_experts].set(rw_hi).at[:, n_experts:2 * n_experts].set(rw_lo)
    h1, hm, r1, r2 = _merge2(m, x2d, g1, norm2_g[0][None], sh2, sc2, w_out[0].astype(BF16), wr, seq)
    logits = (r1[:, :n_experts] + r1[:, n_experts:2 * n_experts] + r2[:, :n_experts]) + router_b[0]

    gates, slot_tok, dest, block_expert, n_valid = _route(logits, n_experts)
    f = w_down.shape[2]
    bgu = b_gate_up[0].reshape(n_experts, f // LANE, LANE, 2).swapaxes(2, 3).reshape(n_experts, 1, 2 * f)
    yslots = _moe_blocks(hm.reshape(n_tok, d // LANE, LANE), slot_tok.reshape(-1, 1, MOE_TM), block_expert,
                         n_valid.reshape(1),
                         wgu_b.reshape(w_gate_up.shape[1:]), bgu, wd_b.reshape(w_down.shape[1:]),
                         b_down[0][:, None, :])
    yg = yslots[dest.reshape(n_tok, TOP_K).T.reshape(-1)]
    out = _final(h1, yg, gates, g2, final_g[None], seq)
    return out.reshape(bsz, seq, d)
```

```python
import functools
import math

import jax
import jax.numpy as jnp
import numpy as np
from jax import lax
from jax.experimental import pallas as pl
from jax.experimental.pallas import tpu as pltpu

F32 = jnp.float32
BF16 = jnp.bfloat16
HIGHEST = lax.Precision.HIGHEST

LANE = 128
GRID_W = 64
NORM_EPS = 1e-6
S5_GROUP = 16
S5_CHUNK = 32
HY_ORDER = 2
HY_NORM_EPS = 1e-6
TOP_K = 4
SWIGLU_LIMIT = 7.0
SWIGLU_ALPHA = 1.702
MOE_TM = 512
MOE_TF = 512
MOE_PITCH = 20
VMEM_LIMIT = 56 * 1024 * 1024


def _params(sem, vmem=VMEM_LIMIT):
    return pltpu.CompilerParams(dimension_semantics=sem, vmem_limit_bytes=vmem)


def _dot(a, b):
    return jnp.dot(a, b, preferred_element_type=F32)


def _ada_kernel(c_ref, w_ref, b_ref, o_ref):
    c = c_ref[...]
    a = c * jax.nn.sigmoid(c)
    o_ref[...] = jnp.dot(a, w_ref[...], preferred_element_type=F32, precision=HIGHEST) + b_ref[...]


def _ada_mod(cc, w, b):
    d, n6 = w.shape
    tn = min(512, n6)
    return pl.pallas_call(
        _ada_kernel,
        grid=(n6 // tn,),
        in_specs=[pl.BlockSpec((8, d), lambda j: (0, 0)),
                  pl.BlockSpec((d, tn), lambda j: (0, j)),
                  pl.BlockSpec((1, tn), lambda j: (0, j))],
        out_specs=pl.BlockSpec((8, tn), lambda j: (0, j)),
        out_shape=jax.ShapeDtypeStruct((8, n6), F32),
        compiler_params=_params(("parallel",)),
        name="ada_mod",
    )(cc, w, b)


def _inproj_kernel(x_ref, g_ref, sh_ref, sc_ref, w_ref, *rest, bounds, acts, riders):
    nr = len(riders)
    n_in = nr + (1 if any(riders) else 0)
    rider_in, rest = rest[:n_in], rest[n_in:]
    outs, rider_out, xn_ref = rest[:len(bounds)], rest[len(bounds):len(bounds) + nr], rest[-1]
    j = pl.program_id(1)

    @pl.when(j == 0)
    def _():
        x = x_ref[...].astype(F32)
        y = x * lax.rsqrt(jnp.mean(x * x, axis=-1, keepdims=True) + NORM_EPS)
        y = y * g_ref[...]
        y = y * (1.0 + sc_ref[...]) + sh_ref[...]
        xn_ref[...] = y.astype(BF16)

    for (j0, j1), act, o_ref in zip(bounds, acts, outs):
        @pl.when((j >= j0) & (j < j1))
        def _(o_ref=o_ref, act=act):
            r = _dot(xn_ref[...], w_ref[...])
            if act:
                r = jax.nn.sigmoid(r)
            o_ref[...] = r.astype(o_ref.dtype)

    for r_in, r_out, deint in zip(rider_in, rider_out, riders):
        if deint:
            for b in range(r_in.shape[1] // (2 * LANE)):
                sl = slice(2 * b * LANE, (2 * b + 2) * LANE)
                r_out[:, sl] = _dot(r_in[:, sl].astype(BF16), rider_in[-1][...]).astype(r_out.dtype)
        else:
            r_out[...] = r_in[...].astype(r_out.dtype)


def _deinterleave_perm():
    src = jnp.arange(2 * LANE)
    dst = (src % 2) * LANE + src // 2
    return jnp.zeros((2 * LANE, 2 * LANE), BF16).at[src, dst].set(1)


def _norm_mod_proj(x2d, g, sh, sc, w, widths, acts, rows_per_mod, tm, tn, riders=()):
    n, d = x2d.shape
    ncols = w.shape[1]
    tm = min(tm, n)
    tn = min(tn, min(widths))
    bounds, off = [], 0
    for wd in widths:
        bounds.append((off // tn, (off + wd) // tn))
        off += wd
    tiles_per_mod = rows_per_mod // tm
    nj = ncols // tn
    nsteps = (n // tm) * nj

    def out_map(i, j, j0, nj):
        return (i, jnp.clip(j - j0, 0, nj - 1))

    out_specs = [pl.BlockSpec((tm, tn), functools.partial(out_map, j0=j0, nj=j1 - j0))
                 for (j0, j1) in bounds]
    out_shape = [jax.ShapeDtypeStruct((n, wd), BF16) for wd in widths]
    rider_args, rider_specs = [], []
    for arr, _ in riders:
        spec = pl.BlockSpec((arr.shape[0] // nsteps, arr.shape[1]), lambda i, j: (i * nj + j, 0))
        rider_args.append(arr)
        rider_specs.append(spec)
        out_specs.append(spec)
        out_shape.append(jax.ShapeDtypeStruct(arr.shape, BF16))
    if any(flag for _, flag in riders):
        rider_args.append(_deinterleave_perm())
        rider_specs.append(pl.BlockSpec((2 * LANE, 2 * LANE), lambda i, j: (0, 0)))
    return pl.pallas_call(
        functools.partial(_inproj_kernel, bounds=tuple(bounds), acts=tuple(acts),
                          riders=tuple(flag for _, flag in riders)),
        grid=(n // tm, nj),
        in_specs=[pl.BlockSpec((tm, d), lambda i, j: (i, 0)),
                  pl.BlockSpec((1, d), lambda i, j: (0, 0)),
                  pl.BlockSpec((None, 1, d), lambda i, j: (i // tiles_per_mod, 0, 0)),
                  pl.BlockSpec((None, 1, d), lambda i, j: (i // tiles_per_mod, 0, 0)),
                  pl.BlockSpec((d, tn), lambda i, j: (0, j))] + rider_specs,
        out_specs=out_specs,
        out_shape=out_shape,
        scratch_shapes=[pltpu.VMEM((tm, d), BF16)],
        compiler_params=_params(("parallel", "arbitrary")),
        name="norm_mod_proj",
    )(x2d, g, sh, sc, w, *rider_args)


def _s5_matrices(lam_re, lam_im, log_dt, b_re, b_im, c_re, c_im, d_skip, t):
    lam = lax.complex(lam_re.astype(F32), lam_im.astype(F32))
    dt = jnp.exp(log_dt.astype(F32))[..., None]
    lam_dt = lam * dt
    lam_bar = jnp.exp(lam_dt)
    b_bar = ((lam_bar - 1) / lam)[..., None] * lax.complex(b_re.astype(F32), b_im.astype(F32))
    c_out = lax.complex(c_re.astype(F32), c_im.astype(F32))
    g, p = lam.shape[1], lam.shape[2]
    gs = b_bar.shape[-1]
    k = jnp.arange(t + 1, dtype=F32)
    pw = jnp.exp(lam_dt[..., None] * k)
    kk = jnp.einsum('dgcp,dgpk,dgpe->dgkce', c_out, pw[..., :t], b_bar, precision=HIGHEST).real
    kf, kb = kk[0], kk[1]
    zero_lag = kf[:, :1] + kb[:, :1]
    k_all = jnp.concatenate([kb[:, :0:-1], zero_lag, kf[:, 1:]], axis=1)
    jj = jnp.arange(t)
    lag_idx = jj[None, :] - jj[:, None] + (t - 1)
    m_intra = k_all[:, lag_idx]
    m_intra = m_intra.transpose(0, 1, 4, 2, 3).reshape(g, t * gs, t * gs)
    m_intra = m_intra + jnp.eye(t * gs, dtype=F32)[None] * jnp.tile(
        d_skip.astype(F32).reshape(g, 1, gs), (1, t, 1)).reshape(g, 1, t * gs)
    in_f = pw[0][..., t - 1 - jj][..., None] * b_bar[0][:, :, None, :]
    in_b = pw[1][..., jj][..., None] * b_bar[1][:, :, None, :]

    def to_in(z):
        return z.transpose(0, 2, 3, 1).reshape(g, t * gs, p)

    m_in = jnp.concatenate([to_in(in_f.real), to_in(in_f.imag), to_in(in_b.real), to_in(in_b.imag)], axis=-1)
    out_f = c_out[0][:, :, :, None] * pw[0][:, None, :, 1 + jj]
    out_b = c_out[1][:, :, :, None] * pw[1][:, None, :, t - jj]

    def to_out(z):
        return z.transpose(0, 2, 3, 1).reshape(g, p, t * gs)

    m_out = jnp.concatenate([to_out(out_f.real), -to_out(out_f.imag), to_out(out_b.real), -to_out(out_b.imag)],
                            axis=1)
    mu = pw[..., t]
    mre, mim = mu.real, mu.imag
    coef = jnp.stack([
        jnp.concatenate([mre[0], mre[0], mre[1], mre[1]], axis=-1),
        jnp.concatenate([-mim[0], mim[0], -mim[1], mim[1]], axis=-1),
        jnp.concatenate([mim[0], -mim[0], mim[1], -mim[1]], axis=-1)])
    return m_intra.astype(BF16), m_in.astype(BF16), m_out.astype(BF16), coef


def _s5_in_kernel(a_ref, m_ref, o_ref):
    o_ref[...] = _dot(a_ref[...], m_ref[...])


def _s5_chunk_states(a, m_in):
    g, nc, kdim = a.shape
    s = m_in.shape[2]
    nb = min(nc, 1024)
    return pl.pallas_call(
        _s5_in_kernel,
        grid=(g, nc // nb),
        in_specs=[pl.BlockSpec((None, nb, kdim), lambda gi, i: (gi, i, 0)),
                  pl.BlockSpec((None, kdim, s), lambda gi, i: (gi, 0, 0))],
        out_specs=pl.BlockSpec((None, nb, s), lambda gi, i: (gi, i, 0)),
        out_shape=jax.ShapeDtypeStruct((g, nc, s), F32),
        compiler_params=_params(("parallel", "arbitrary")),
        name="s5_chunk_states",
    )(a, m_in)


def _s5_scan_kernel(sf_ref, sb_ref, init_ref, coef_ref, ef_ref, eb_ref, fin_ref, st_ref, *, nblk, half):
    k = pl.program_id(1)
    nk = pl.num_programs(1)
    lanes = 2 * half

    @pl.when(k == 0)
    def _():
        s0f = init_ref[:, 0:lanes]
        s0b = init_ref[:, lanes:2 * lanes]
        st_ref[0] = s0f
        st_ref[1] = pltpu.roll(s0f, half, 1)
        st_ref[2] = s0b
        st_ref[3] = pltpu.roll(s0b, half, 1)

    def body(r, carry):
        sf, sfw, sb, sbw = carry
        rb = nblk - 1 - r
        xf = sf_ref[r]
        xb = sb_ref[rb]
        ef_ref[r] = sf
        eb_ref[rb] = sb
        xfw = pltpu.roll(xf, half, 1)
        xbw = pltpu.roll(xb, half, 1)
        af, bf, bfw = coef_ref[0, :, 0:lanes], coef_ref[1, :, 0:lanes], coef_ref[2, :, 0:lanes]
        ab, bb, bbw = (coef_ref[0, :, lanes:2 * lanes], coef_ref[1, :, lanes:2 * lanes],
                       coef_ref[2, :, lanes:2 * lanes])
        return (sf * af + sfw * bf + xf, sfw * af + sf * bfw + xfw,
                sb * ab + sbw * bb + xb, sbw * ab + sb * bbw + xbw)

    sf, sfw, sb, sbw = lax.fori_loop(0, nblk, body, (st_ref[0], st_ref[1], st_ref[2], st_ref[3]))
    st_ref[0] = sf
    st_ref[1] = sfw
    st_ref[2] = sb
    st_ref[3] = sbw

    @pl.when(k == nk - 1)
    def _():
        fin_ref[:, 0:lanes] = sf
        fin_ref[:, lanes:2 * lanes] = sb


def _s5_scan(s_t, init, coef):
    bsz, nc, g, s4 = s_t.shape
    lanes = s4 // 2
    nblk = min(nc, 128)
    nk = nc // nblk
    return pl.pallas_call(
        functools.partial(_s5_scan_kernel, nblk=nblk, half=lanes // 2),
        grid=(bsz, nk),
        in_specs=[pl.BlockSpec((None, nblk, g, lanes), lambda b, k: (b, k, 0, 0)),
                  pl.BlockSpec((None, nblk, g, lanes), lambda b, k: (b, nk - 1 - k, 0, 1)),
                  pl.BlockSpec((None, g, s4), lambda b, k: (b, 0, 0)),
                  pl.BlockSpec((3, g, s4), lambda b, k: (0, 0, 0))],
        out_specs=[pl.BlockSpec((None, nblk, g, lanes), lambda b, k: (b, k, 0, 0)),
                   pl.BlockSpec((None, nblk, g, lanes), lambda b, k: (b, nk - 1 - k, 0, 0)),
                   pl.BlockSpec((None, g, s4), lambda b, k: (b, 0, 0))],
        out_shape=[jax.ShapeDtypeStruct((bsz, nc, g, lanes), F32),
                   jax.ShapeDtypeStruct((bsz, nc, g, lanes), F32),
                   jax.ShapeDtypeStruct((bsz, g, s4), F32)],
        scratch_shapes=[pltpu.VMEM((4, g, lanes), F32)],
        compiler_params=_params(("parallel", "arbitrary")),
        name="s5_scan",
    )(s_t, s_t, init, coef)


def _s5_out_kernel(a_ref, e_ref, mi_ref, mo_ref, o_ref):
    y = _dot(a_ref[...], mi_ref[...]) + _dot(e_ref[...], mo_ref[...])
    o_ref[...] = y.astype(o_ref.dtype)


def _s5_chunk_outputs(a, e_in, m_intra, m_out):
    g, nc, kdim = a.shape
    s = e_in.shape[2]
    nb = min(nc, 1024)
    return pl.pallas_call(
        _s5_out_kernel,
        grid=(g, nc // nb),
        in_specs=[pl.BlockSpec((None, nb, kdim), lambda gi, i: (gi, i, 0)),
                  pl.BlockSpec((None, nb, s), lambda gi, i: (gi, i, 0)),
                  pl.BlockSpec((None, kdim, kdim), lambda gi, i: (gi, 0, 0)),
                  pl.BlockSpec((None, s, kdim), lambda gi, i: (gi, 0, 0))],
        out_specs=pl.BlockSpec((None, nb, kdim), lambda gi, i: (gi, i, 0)),
        out_shape=jax.ShapeDtypeStruct((g, nc, kdim), BF16),
        compiler_params=_params(("parallel", "arbitrary")),
        name="s5_chunk_outputs",
    )(a, e_in, m_intra, m_out)


def _s5_readout_kernel(y_ref, wg_ref, o_ref):
    y = jax.nn.gelu(y_ref[...].astype(F32))
    gate = _dot(y.astype(BF16), wg_ref[...])
    o_ref[...] = (y * jax.nn.sigmoid(gate)).astype(o_ref.dtype)


def _s5_readout(y2d, wg):
    n, w = y2d.shape
    tm = min(n, 1024)
    return pl.pallas_call(
        _s5_readout_kernel,
        grid=(n // tm,),
        in_specs=[pl.BlockSpec((tm, w), lambda i: (i, 0)),
                  pl.BlockSpec((w, w), lambda i: (0, 0))],
        out_specs=pl.BlockSpec((tm, w), lambda i: (i, 0)),
        out_shape=jax.ShapeDtypeStruct((n, w), BF16),
        compiler_params=_params(("parallel",)),
        name="s5_readout",
    )(y2d, wg)


def _to_chunks(u2d, g, t):
    n = u2d.shape[0]
    return u2d.reshape(n // t, t, g, S5_GROUP).transpose(2, 0, 1, 3).reshape(g, n // t, t * S5_GROUP)


def _from_chunks(y, g, t):
    nc = y.shape[1]
    return y.reshape(g, nc, t, S5_GROUP).transpose(1, 2, 0, 3).reshape(nc * t, g * S5_GROUP)


def _s5_branch(ua, uca, bsz, lam_re, lam_im, log_dt, b_re, b_im, c_re, c_im, d_skip, w_glu):
    g = lam_re.shape[1]
    t = S5_CHUNK
    m_intra, m_in, m_out, coef = _s5_matrices(lam_re, lam_im, log_dt, b_re, b_im, c_re, c_im, d_skip, t)
    s4 = m_in.shape[2]

    def states(u2d):
        a = _to_chunks(u2d, g, t)
        s = _s5_chunk_states(a, m_in)
        nc = s.shape[1] // bsz
        return a, s.reshape(g, bsz, nc, s4).transpose(1, 2, 0, 3)

    _, s_ctx = states(uca)
    _, _, seed = _s5_scan(s_ctx, jnp.zeros((bsz, g, s4), F32), coef)
    a_lat, s_lat = states(ua)
    e_f, e_b, _ = _s5_scan(s_lat, seed, coef)
    e_in = jnp.concatenate([e_f, e_b], axis=-1).astype(BF16)
    e_in = e_in.transpose(2, 0, 1, 3).reshape(g, -1, s4)
    y = _s5_chunk_outputs(a_lat, e_in, m_intra, m_out)
    return _s5_readout(_from_chunks(y, g, t), w_glu.astype(BF16))


HY_KH = GRID_W + 1
HY_KP = 72
HY_QN = 16


def _hy_phase(w_cols, r_cnt):
    n1_tot = 2 * GRID_W
    n_fft = n1_tot * r_cnt
    kh = GRID_W + 1
    k1 = np.arange(kh, dtype=np.int64)[None, :, None]
    r = np.arange(r_cnt, dtype=np.int64)[:, None, None]
    n1 = np.arange(w_cols, dtype=np.int64)[None, None, :]
    return 2.0 * np.pi * ((k1 * (r_cnt * n1 + r)) % n_fft) / n_fft


def _hy_fwd_table(r_cnt, m):
    th = _hy_phase(GRID_W * m, r_cnt)
    kh = th.shape[1]
    f = np.zeros((r_cnt, 2 * HY_KP, GRID_W * m), np.float32)
    f[:, :kh] = np.cos(th)
    f[:, HY_KP:HY_KP + kh] = -np.sin(th)
    return jnp.asarray(f, BF16)


def _hy_inv_table(r_cnt):
    th = _hy_phase(GRID_W, r_cnt)
    kh = th.shape[1]
    c = np.full((kh,), 2.0)
    c[0] = c[-1] = 1.0
    scale = c[None, :, None] / (2 * GRID_W * r_cnt)
    g = np.zeros((r_cnt, GRID_W, 2 * HY_KP), np.float32)
    g[:, :, :kh] = (scale * np.cos(th)).transpose(0, 2, 1)
    g[:, :, HY_KP:HY_KP + kh] = (-scale * np.sin(th)).transpose(0, 2, 1)
    return jnp.asarray(g, BF16)


def _hy_dft_tables(r_cnt):
    k = np.arange(r_cnt, dtype=np.int64)
    th = 2.0 * np.pi * ((k[:, None] * k[None, :]) % r_cnt) / r_cnt
    fc, fs = np.cos(th), np.sin(th)
    fwd = np.block([[fc, fs], [-fs, fc]]).astype(np.float32)
    inv = np.block([[fc, -fs], [fs, fc]]).astype(np.float32)
    return jnp.asarray(fwd, BF16), jnp.asarray(inv, BF16)


def _hy_short_kernel(x_ref, w_ref, o_ref, *, rows, chunk):
    wk = w_ref[...]
    w0, w1, w2 = wk[0:1], wk[1:2], wk[2:3]
    gw = GRID_W
    col = lax.broadcasted_iota(jnp.int32, (gw, x_ref.shape[1]), 0)

    def piece(a, n):
        return x_ref[pl.ds(a, n), :].astype(F32)

    last = piece(rows - gw, gw)
    prev0 = jnp.where(col == 0, 0.0, pltpu.roll(last, 1, 0))
    o_ref[0:gw, :] = (w0 * prev0 + w1 * piece(0, gw) + w2 * piece(gw, gw)).astype(o_ref.dtype)
    first = piece(0, gw)
    next_l = jnp.where(col == gw - 1, 0.0, pltpu.roll(first, gw - 1, 0))
    o_ref[rows - gw:rows, :] = (w0 * piece(rows - 2 * gw, gw) + w1 * last + w2 * next_l).astype(o_ref.dtype)

    def body(i, carry):
        a = pl.multiple_of(gw + i * chunk, gw)
        o_ref[pl.ds(a, chunk), :] = (w0 * piece(a - gw, chunk) + w1 * piece(a, chunk)
                                     + w2 * piece(a + gw, chunk)).astype(o_ref.dtype)
        return carry

    n_full = (rows - 2 * gw) // chunk
    lax.fori_loop(0, n_full, body, 0)
    rem = rows - 2 * gw - n_full * chunk
    if rem:
        a = gw + n_full * chunk
        o_ref[a:a + rem, :] = (w0 * piece(a - gw, rem) + w1 * piece(a, rem)
                               + w2 * piece(a + gw, rem)).astype(o_ref.dtype)


def _hy_short_conv(uh, short_w, bsz, seq):
    n, ch = uh.shape
    ct = min(ch, LANE)
    chunk = min(512, seq - 2 * GRID_W)
    return pl.pallas_call(
        functools.partial(_hy_short_kernel, rows=seq, chunk=chunk),
        grid=(bsz, ch // ct),
        in_specs=[pl.BlockSpec((seq, ct), lambda b, c: (b, c)),
                  pl.BlockSpec((3, ct), lambda b, c: (0, c))],
        out_specs=pl.BlockSpec((seq, ct), lambda b, c: (b, c)),
        out_shape=jax.ShapeDtypeStruct((n, ch), BF16),
        compiler_params=_params(("parallel", "parallel")),
        name="hy_short_conv",
    )(uh, short_w)


def _hy_fwd1_kernel(*refs, m):
    x_refs, f_ref, o_ref, s_ref = refs[:m], refs[m], refs[m + 1], refs[m + 2]
    gw = GRID_W
    for q in range(HY_QN):
        xq = [x[q * gw:(q + 1) * gw, :] for x in x_refs]
        xq = xq[0] if m == 1 else jnp.concatenate(xq, axis=0)
        s_ref[q] = _dot(f_ref[q], xq)
    t = jnp.swapaxes(s_ref[...], 0, 1).astype(o_ref.dtype)
    o_ref[0] = t[0:HY_KH]
    o_ref[1] = t[HY_KP:HY_KP + HY_KH]


def _hy_fwd1(xs, col_blk0, width, bsz, r_cnt, ct):
    m = len(xs)
    table = _hy_fwd_table(r_cnt, m)
    rb = HY_QN * GRID_W
    nrb = r_cnt // HY_QN
    x_spec = pl.BlockSpec((rb, ct), lambda b, i, c: (b * nrb + i, col_blk0 + c))
    return pl.pallas_call(
        functools.partial(_hy_fwd1_kernel, m=m),
        grid=(bsz, nrb, width // ct),
        in_specs=[x_spec] * m + [pl.BlockSpec((HY_QN, 2 * HY_KP, GRID_W * m), lambda b, i, c: (i, 0, 0))],
        out_specs=pl.BlockSpec((None, 2, HY_KH, HY_QN, ct), lambda b, i, c: (b, 0, 0, i, c)),
        out_shape=jax.ShapeDtypeStruct((bsz, 2, HY_KH, r_cnt, width), BF16),
        scratch_shapes=[pltpu.VMEM((HY_QN, 2 * HY_KP, ct), F32)],
        compiler_params=_params(("parallel", "parallel", "arbitrary")),
        name="hy_fwd_stage1",
    )(*xs, table)


def _hy_spec_kernel(a_ref, f2_ref, o_ref):
    r = a_ref.shape[1]
    x = a_ref[...].reshape(2 * r, a_ref.shape[2])
    o_ref[...] = _dot(f2_ref[...], x).reshape(o_ref.shape)


def _hy_filter_spectrum(a, f2, ct):
    _, _, kh, r_cnt, width = a.shape
    return pl.pallas_call(
        _hy_spec_kernel,
        grid=(kh, width // ct),
        in_specs=[pl.BlockSpec((None, 2, None, r_cnt, ct), lambda k, c: (0, 0, k, 0, c)),
                  pl.BlockSpec((2 * r_cnt, 2 * r_cnt), lambda k, c: (0, 0))],
        out_specs=pl.BlockSpec((2, None, r_cnt, ct), lambda k, c: (0, k, 0, c)),
        out_shape=jax.ShapeDtypeStruct((2, kh, r_cnt, width), F32),
        compiler_params=_params(("parallel", "parallel")),
        name="hy_filter_spectrum",
    )(a, f2)


def _hy_mid_kernel(a_ref, kf_ref, invn_ref, f2_ref, f2i_ref, o_ref):
    r = a_ref.shape[1]
    x = a_ref[...].reshape(2 * r, a_ref.shape[2])
    b = _dot(f2_ref[...], x)
    br, bi = b[:r], b[r:]
    kr, ki = kf_ref[0], kf_ref[1]
    s = invn_ref[...]
    cr = (br * kr - bi * ki) * s
    ci = (br * ki + bi * kr) * s
    c = jnp.concatenate([cr, ci], axis=0).astype(BF16)
    o_ref[...] = _dot(f2i_ref[...], c).reshape(o_ref.shape).astype(o_ref.dtype)


def _hy_mid(a, kf, invn, order, f2, f2i, ct):
    bsz, _, kh, r_cnt, width = a.shape
    nct = width // ct
    return pl.pallas_call(
        _hy_mid_kernel,
        grid=(kh, nct, bsz),
        in_specs=[pl.BlockSpec((None, 2, None, r_cnt, ct), lambda k, c, b: (b, 0, k, 0, c)),
                  pl.BlockSpec((2, None, r_cnt, ct), lambda k, c, b: (0, k, 0, order * nct + c)),
                  pl.BlockSpec((1, ct), lambda k, c, b: (0, order * nct + c)),
                  pl.BlockSpec((2 * r_cnt, 2 * r_cnt), lambda k, c, b: (0, 0)),
                  pl.BlockSpec((2 * r_cnt, 2 * r_cnt), lambda k, c, b: (0, 0))],
        out_specs=pl.BlockSpec((None, 2, None, r_cnt, ct), lambda k, c, b: (b, 0, k, 0, c)),
        out_shape=jax.ShapeDtypeStruct(a.shape, BF16),
        compiler_params=_params(("parallel", "parallel", "arbitrary")),
        name="hy_spectrum_product",
    )(a, kf, invn, f2, f2i)


def _hy_inv1_kernel(d_ref, g_ref, xin_ref, xm_ref, bias_ref, o_ref, t_ref, s_ref):
    gw = GRID_W
    pad = jnp.zeros((HY_KP - HY_KH,) + t_ref.shape[1:], F32)
    t_ref[0:HY_KH] = d_ref[0].astype(F32)
    t_ref[HY_KH:HY_KP] = pad
    t_ref[HY_KP:HY_KP + HY_KH] = d_ref[1].astype(F32)
    t_ref[HY_KP + HY_KH:2 * HY_KP] = pad
    s_ref[...] = jnp.swapaxes(t_ref[...], 0, 1)
    for q in range(HY_QN):
        y = _dot(g_ref[q], s_ref[q].astype(BF16))
        rows = slice(q * gw, (q + 1) * gw)
        xin = xin_ref[rows, :].astype(F32)
        o_ref[rows, :] = (xm_ref[rows, :].astype(F32) * (y + xin * bias_ref[...])).astype(o_ref.dtype)


def _hy_inv1(d, xin, xin_blk0, xm, xm_blk0, bias, ct):
    bsz, _, kh, r_cnt, width = d.shape
    table = _hy_inv_table(r_cnt)
    rb = HY_QN * GRID_W
    nrb = r_cnt // HY_QN
    return pl.pallas_call(
        _hy_inv1_kernel,
        grid=(bsz, nrb, width // ct),
        in_specs=[pl.BlockSpec((None, 2, kh, HY_QN, ct), lambda b, i, c: (b, 0, 0, i, c)),
                  pl.BlockSpec((HY_QN, GRID_W, 2 * HY_KP), lambda b, i, c: (i, 0, 0)),
                  pl.BlockSpec((rb, ct), lambda b, i, c: (b * nrb + i, xin_blk0 + c)),
                  pl.BlockSpec((rb, ct), lambda b, i, c: (b * nrb + i, xm_blk0 + c)),
                  pl.BlockSpec((1, ct), lambda b, i, c: (0, c))],
        out_specs=pl.BlockSpec((rb, ct), lambda b, i, c: (b * nrb + i, c)),
        out_shape=jax.ShapeDtypeStruct((bsz * r_cnt * GRID_W, width), BF16),
        scratch_shapes=[pltpu.VMEM((2 * HY_KP, HY_QN, ct), F32), pltpu.VMEM((HY_QN, 2 * HY_KP, ct), F32)],
        compiler_params=_params(("parallel", "parallel", "arbitrary")),
        name="hy_inv_stage1",
    )(d, table, xin, xm, bias)


def _hy_filter_kernel(w1_ref, b1_ref, f1_ref, w2_ref, b2_ref, f2_ref, w3_ref, b3_ref, dec_ref, om_ref, ph_ref,
                      o_ref, sum_ref, *, direction, seqlen, r_cnt, rows):
    i = pl.program_id(1)
    row = lax.broadcasted_iota(jnp.int32, (rows, LANE), 0) + i * rows
    lane = lax.broadcasted_iota(jnp.int32, (rows, LANE), 1)
    r = jnp.right_shift(row, GRID_W.bit_length() - 1)
    w = jnp.bitwise_and(row, GRID_W - 1)
    n = (w + GRID_W * direction) * r_cnt + r
    p = n if direction == 0 else 2 * seqlen - n
    pf = p.astype(F32)
    t = pf / (seqlen - 1)
    feats = jnp.where(lane == 0, t, jnp.cos(pf * om_ref[...] + ph_ref[...]))
    hid = jnp.sin(f1_ref[...] * (jnp.dot(feats, w1_ref[...], preferred_element_type=F32, precision=HIGHEST)
                                 + b1_ref[...]))
    hid = jnp.sin(f2_ref[...] * (jnp.dot(hid, w2_ref[...], preferred_element_type=F32, precision=HIGHEST)
                                 + b2_ref[...]))
    filt = _dot(hid.astype(BF16), w3_ref[...].astype(BF16)) + b3_ref[...]
    reps = filt.shape[1] // LANE
    t_w = jnp.concatenate([t] * reps, axis=1)
    keep = jnp.concatenate([jnp.where(p < seqlen, 1.0, 0.0)] * reps, axis=1)
    filt = filt * jnp.exp(-t_w * jnp.abs(dec_ref[...])) * keep
    o_ref[...] = filt.astype(o_ref.dtype)

    @pl.when(i == 0)
    def _():
        sum_ref[...] = jnp.zeros_like(sum_ref)

    sum_ref[...] += jnp.sum(jnp.abs(filt), axis=0, keepdims=True)


def _hy_filter_signal(direction, seqlen, w1p, b1, f1, w2, b2, f2, w3, b3, decay, omega, phase, width):
    r_cnt = seqlen // GRID_W
    rows = min(256, seqlen)
    ct = width
    nct = width // ct
    hid = w2.shape[0]
    off = direction * nct
    const = lambda shape: pl.BlockSpec(shape, lambda c, i: (0, 0))
    colv = pl.BlockSpec((1, ct), lambda c, i: (0, off + c))
    return pl.pallas_call(
        functools.partial(_hy_filter_kernel, direction=direction, seqlen=seqlen, r_cnt=r_cnt, rows=rows),
        grid=(nct, seqlen // rows),
        in_specs=[const((LANE, hid)), const((1, hid)), const((1, hid)), const((hid, hid)), const((1, hid)),
                  const((1, hid)), pl.BlockSpec((hid, ct), lambda c, i: (0, off + c)), colv, colv,
                  const((1, LANE)), const((1, LANE))],
        out_specs=[pl.BlockSpec((rows, ct), lambda c, i: (i, c)),
                   pl.BlockSpec((1, ct), lambda c, i: (0, c))],
        out_shape=[jax.ShapeDtypeStruct((seqlen, width), BF16), jax.ShapeDtypeStruct((1, width), F32)],
        compiler_params=_params(("parallel", "arbitrary")),
        name="hy_filter_signal",
    )(w1p, b1, f1, w2, b2, f2, w3, b3, decay, omega, phase)


def _hyena_branch(uh, bsz, seq, short_w, w1, b1, f1, w2, b2, f2, w3, b3, decay, hy_bias):
    width = hy_bias.shape[1]
    r_cnt = seq // GRID_W
    ct = min(width, 512)
    nct = width // ct
    n_emb = w1.shape[0]
    n_bands = (n_emb - 1) // 2
    hid = w1.shape[1]
    bands = jnp.linspace(1e-4, n_bands - 1, n_bands, dtype=F32)
    omega = jnp.zeros((1, LANE), F32).at[0, 1:1 + n_bands].set((2 * jnp.pi / seq) * bands)
    omega = omega.at[0, 1 + n_bands:1 + 2 * n_bands].set((2 * jnp.pi / seq) * bands)
    phase = jnp.zeros((1, LANE), F32).at[0, 1 + n_bands:1 + 2 * n_bands].set(0.5 * jnp.pi)
    w1p = jnp.zeros((LANE, hid), F32).at[:n_emb].set(w1)
    sigs, sums = [], []
    for direction in range(2):
        sig, ssum = _hy_filter_signal(direction, seq, w1p, b1[None], f1[None], w2, b2[None], f2[None], w3,
                                      b3[None], decay[None], omega, phase, HY_ORDER * width)
        sigs.append(sig)
        sums.append(ssum)
    invn = 1.0 / (sums[0] + sums[1] + HY_NORM_EPS)
    f2m, f2i = _hy_dft_tables(r_cnt)
    ct3 = min(width, 1024)
    kf = _hy_filter_spectrum(_hy_fwd1(sigs, 0, HY_ORDER * width, 1, r_cnt, ct), f2m, ct3)

    z = _hy_short_conv(uh, short_w.astype(F32), bsz, seq)
    a = _hy_fwd1([z], 0, width, bsz, r_cnt, ct)
    d = _hy_mid(a, kf, invn, 0, f2m, f2i, ct3)
    y1 = _hy_inv1(d, z, 0, z, nct, hy_bias[0:1].astype(F32), ct)
    a = _hy_fwd1([y1], 0, width, bsz, r_cnt, ct)
    d = _hy_mid(a, kf, invn, 1, f2m, f2i, ct3)
    return _hy_inv1(d, y1, 0, z, 2 * nct, hy_bias[1:2].astype(F32), ct)


def _merge1_kernel(ya_ref, yh_ref, ug_ref, wa_ref, wh_ref, o_ref):
    d = o_ref.shape[1]
    a = _dot(ya_ref[...], wa_ref[...])
    h = _dot(yh_ref[...], wh_ref[...])
    ga = ug_ref[:, :d].astype(F32)
    gh = ug_ref[:, d:].astype(F32)
    o_ref[...] = (ga * a + gh * h).astype(o_ref.dtype)


def _merge1(ya, yh, ug, wa, wh):
    n, ws = ya.shape
    wh_in = yh.shape[1]
    d = wa.shape[1]
    tm = min(n, 512)
    return pl.pallas_call(
        _merge1_kernel,
        grid=(n // tm,),
        in_specs=[pl.BlockSpec((tm, ws), lambda i: (i, 0)),
                  pl.BlockSpec((tm, wh_in), lambda i: (i, 0)),
                  pl.BlockSpec((tm, 2 * d), lambda i: (i, 0)),
                  pl.BlockSpec((ws, d), lambda i: (0, 0)),
                  pl.BlockSpec((wh_in, d), lambda i: (0, 0))],
        out_specs=pl.BlockSpec((tm, d), lambda i: (i, 0)),
        out_shape=jax.ShapeDtypeStruct((n, d), BF16),
        compiler_params=_params(("parallel",)),
        name="merge_branches",
    )(ya, yh, ug, wa, wh)


def _merge2_kernel(m_ref, x_ref, g1_ref, ng_ref, sh_ref, sc_ref, wo_ref, wr_ref,
                   h1_ref, hm_ref, r1_ref, r2_ref):
    mix = _dot(m_ref[...], wo_ref[...])
    h1 = x_ref[...] + g1_ref[...] * mix
    h1_ref[...] = h1
    y = h1 * lax.rsqrt(jnp.mean(h1 * h1, axis=-1, keepdims=True) + NORM_EPS)
    y = y * ng_ref[...]
    y = y * (1.0 + sc_ref[...]) + sh_ref[...]
    hi = y.astype(BF16)
    lo = (y - hi.astype(F32)).astype(BF16)
    n_sub = y.shape[1] // LANE
    for s in range(n_sub):
        hm_ref[pl.ds(s, y.shape[0], stride=n_sub), :] = y[:, s * LANE:(s + 1) * LANE]
    r1_ref[...] = _dot(hi, wr_ref[...])
    r2_ref[...] = _dot(lo, wr_ref[...])


def _merge2(m, x2d, g1, ng, sh2, sc2, wo, wr, rows_per_mod):
    n, d = x2d.shape
    tm = min(n, 512)
    tiles_per_mod = rows_per_mod // tm
    nr = wr.shape[1]
    mod_spec = pl.BlockSpec((None, 1, d), lambda i: (i // tiles_per_mod, 0, 0))
    return pl.pallas_call(
        _merge2_kernel,
        grid=(n // tm,),
        in_specs=[pl.BlockSpec((tm, d), lambda i: (i, 0)),
                  pl.BlockSpec((tm, d), lambda i: (i, 0)),
                  mod_spec,
                  pl.BlockSpec((1, d), lambda i: (0, 0)),
                  mod_spec, mod_spec,
                  pl.BlockSpec((d, d), lambda i: (0, 0)),
                  pl.BlockSpec((d, nr), lambda i: (0, 0))],
        out_specs=[pl.BlockSpec((tm, d), lambda i: (i, 0)),
                   pl.BlockSpec((tm * (d // LANE), LANE), lambda i: (i, 0)),
                   pl.BlockSpec((tm, nr), lambda i: (i, 0)),
                   pl.BlockSpec((tm, nr), lambda i: (i, 0))],
        out_shape=[jax.ShapeDtypeStruct((n, d), F32),
                   jax.ShapeDtypeStruct((n * (d // LANE), LANE), F32),
                   jax.ShapeDtypeStruct((n, nr), F32),
                   jax.ShapeDtypeStruct((n, nr), F32)],
        compiler_params=_params(("parallel",)),
        name="out_proj_norm_router",
    )(m, x2d, g1, ng, sh2, sc2, wo, wr)


def _moe_kernel(be_ref, nv_ref, tok0_ref, tokn_ref, hm_ref, wgu_ref, bgu_ref, wd_ref, bd_ref, o_ref,
                xbuf_ref, acc_ref, sem, *, nj):
    i = pl.program_id(0)
    j = pl.program_id(1)
    nv = nv_ref[0]
    tm = acc_ref.shape[0]

    n_sub = hm_ref.shape[1]

    def row_copy(tok, r, slot):
        return pltpu.make_async_copy(hm_ref.at[tok], xbuf_ref.at[slot, pl.ds(r * MOE_PITCH, n_sub), :],
                                     sem.at[slot])

    def wait_block(slot):
        for r in range(tm):
            row_copy(0, r, slot).wait()

    @pl.when((i == 0) & (j == 0) & (nv > 0))
    def _():
        for r in range(tm):
            row_copy(tok0_ref[0, r], r, 0).start()

    def expert_step(first, last):
        slot = i % 2
        if first:
            wait_block(slot)
            for r in range(tm):
                row_copy(tokn_ref[0, r], r, (i + 1) % 2).start()
        x = jnp.concatenate([xbuf_ref[slot, pl.ds(s, tm, stride=MOE_PITCH), :].astype(BF16)
                             for s in range(n_sub)], axis=1)
        gu = _dot(x, wgu_ref[...]) + bgu_ref[...]
        acts = []
        for b in range(gu.shape[1] // (2 * LANE)):
            glu = jnp.minimum(gu[:, 2 * b * LANE:(2 * b + 1) * LANE], SWIGLU_LIMIT)
            lin = jnp.clip(gu[:, (2 * b + 1) * LANE:(2 * b + 2) * LANE], -SWIGLU_LIMIT, SWIGLU_LIMIT)
            acts.append((glu * jax.nn.sigmoid(SWIGLU_ALPHA * glu) * (lin + 1.0)).astype(BF16))
        part = _dot(jnp.concatenate(acts, axis=1), wd_ref[...])
        if not first:
            part = part + acc_ref[...]
        if last:
            o_ref[...] = (part + bd_ref[...]).astype(o_ref.dtype)
        else:
            acc_ref[...] = part

    variants = {1: [(True, True)], 2: [(True, False), (False, True)]}.get(
        nj, [(True, False), (False, False), (False, True)])
    for first, last in variants:
        cond = (i < nv) & ((j == 0) if first else (j > 0)) & ((j == nj - 1) if last else (j < nj - 1))
        pl.when(cond)(functools.partial(expert_step, first, last))

    @pl.when((i == nv - 1) & (j == nj - 1))
    def _():
        wait_block((i + 1) % 2)

    @pl.when((i >= nv) & (j == nj - 1))
    def _():
        o_ref[...] = jnp.zeros_like(o_ref)


def _moe_blocks(hm, slot_tok, block_expert, n_valid, wgu, bgu, wd, bd):
    d = wd.shape[2]
    assert hm.shape[1] <= MOE_PITCH and hm.shape[1] * hm.shape[2] == d
    f = wd.shape[1]
    tm = MOE_TM
    tf = min(f, MOE_TF)
    nj = f // tf
    nblk = block_expert.shape[0]

    def jeff(i, j, nv):
        return jnp.where(i < nv[0], j, nj - 1)

    grid_spec = pltpu.PrefetchScalarGridSpec(
        num_scalar_prefetch=2,
        grid=(nblk, nj),
        in_specs=[pl.BlockSpec((None, 1, tm), lambda i, j, be, nv: (0, 0, 0), memory_space=pltpu.SMEM),
                  pl.BlockSpec((None, 1, tm), lambda i, j, be, nv: (jnp.minimum(i + 1, nblk - 1), 0, 0),
                               memory_space=pltpu.SMEM),
                  pl.BlockSpec(memory_space=pl.ANY),
                  pl.BlockSpec((None, d, 2 * tf), lambda i, j, be, nv: (be[i], 0, jeff(i, j, nv))),
                  pl.BlockSpec((None, 1, 2 * tf), lambda i, j, be, nv: (be[i], 0, jeff(i, j, nv))),
                  pl.BlockSpec((None, tf, d), lambda i, j, be, nv: (be[i], jeff(i, j, nv), 0)),
                  pl.BlockSpec((None, 1, d), lambda i, j, be, nv: (be[i], 0, 0))],
        out_specs=pl.BlockSpec((tm, d), lambda i, j, be, nv: (i, 0)),
        scratch_shapes=[pltpu.VMEM((2, tm * MOE_PITCH, LANE), F32), pltpu.VMEM((tm, d), F32),
                        pltpu.SemaphoreType.DMA((2,))])
    return pl.pallas_call(
        functools.partial(_moe_kernel, nj=nj),
        grid_spec=grid_spec,
        out_shape=jax.ShapeDtypeStruct((nblk * tm, d), BF16),
        compiler_params=_params(("arbitrary", "arbitrary")),
        name="moe_experts",
    )(block_expert, n_valid, slot_tok, slot_tok, hm, wgu, bgu, wd, bd)


def _route(logits, n_experts):
    n_tok = logits.shape[0]
    top_val, top_idx = lax.top_k(logits, TOP_K)
    gates = jax.nn.softmax(top_val, axis=-1)
    n_assign = n_tok * TOP_K
    flat_e = top_idx.reshape(-1).astype(jnp.int32)
    experts = jnp.arange(n_experts, dtype=jnp.int32)
    onehot = (flat_e[:, None] == experts[None, :]).astype(jnp.int32)
    csum = jnp.cumsum(onehot, axis=0)
    rank = jnp.sum(onehot * csum, axis=1) - 1
    counts = csum[-1]
    padded = (counts + MOE_TM - 1) // MOE_TM * MOE_TM
    start = jnp.cumsum(counts) - counts
    padded_end = jnp.cumsum(padded)
    padded_start = padded_end - padded
    dest = padded_start[flat_e] + rank
    n_blocks = -(-n_assign // MOE_TM) + n_experts
    cap = n_blocks * MOE_TM
    n_valid = (padded_end[-1] // MOE_TM).astype(jnp.int32)
    order = jnp.argsort(flat_e)
    slots = jnp.arange(cap, dtype=jnp.int32)
    slot_e = jnp.minimum(jnp.sum((padded_end[None, :] <= slots[:, None]).astype(jnp.int32), axis=1), n_experts - 1)
    slot_rank = slots - padded_start[slot_e]
    src = jnp.minimum(start[slot_e] + slot_rank, n_assign - 1)
    slot_tok = jnp.where(slot_rank < counts[slot_e], order[src] // TOP_K, 0).astype(jnp.int32)
    block_start = jnp.minimum(jnp.arange(n_blocks, dtype=jnp.int32), n_valid - 1) * MOE_TM
    block_expert = jnp.minimum(
        jnp.sum((padded_end[None, :] <= block_start[:, None]).astype(jnp.int32), axis=1), n_experts - 1)
    return gates, slot_tok, dest, block_expert, n_valid


def _final_kernel(h1_ref, *rest):
    yg_refs = rest[:TOP_K]
    gates_ref, g2_ref, fg_ref, o_ref = rest[TOP_K:]
    gates = gates_ref[...]
    moe = yg_refs[0][...].astype(F32) * gates[:, 0:1]
    for k in range(1, TOP_K):
        moe = moe + yg_refs[k][...].astype(F32) * gates[:, k:k + 1]
    h = h1_ref[...] + g2_ref[...] * moe
    y = h * lax.rsqrt(jnp.mean(h * h, axis=-1, keepdims=True) + NORM_EPS)
    o_ref[...] = (y * fg_ref[...]).astype(o_ref.dtype)


def _final(h1, yg, gates, g2, fg, rows_per_mod):
    n, d = h1.shape
    tm = min(n, 512)
    tiles_per_mod = rows_per_mod // tm
    nt = n // tm
    yg_specs = [pl.BlockSpec((tm, d), functools.partial(lambda i, k: (k * nt + i, 0), k=k)) for k in range(TOP_K)]
    return pl.pallas_call(
        _final_kernel,
        grid=(nt,),
        in_specs=[pl.BlockSpec((tm, d), lambda i: (i, 0))] + yg_specs + [
                  pl.BlockSpec((tm, TOP_K), lambda i: (i, 0)),
                  pl.BlockSpec((None, 1, d), lambda i: (i // tiles_per_mod, 0, 0)),
                  pl.BlockSpec((1, d), lambda i: (0, 0))],
        out_specs=pl.BlockSpec((tm, d), lambda i: (i, 0)),
        out_shape=jax.ShapeDtypeStruct((n, d), F32),
        compiler_params=_params(("parallel",)),
        name="combine_final_norm",
    )(h1, *([yg] * TOP_K), gates, g2, fg)


def kernel(x, c, ctx, c_ctx, w_ada, b_ada, norm1_g, norm2_g, w_in, s5_lam_re, s5_lam_im, s5_log_dt, s5_b_re, s5_b_im, s5_c_re, s5_c_im, s5_d, s5_w_glu, hy_short_w, hy_pos_w1, hy_pos_b1, hy_freq1, hy_pos_w2, hy_pos_b2, hy_freq2, hy_pos_w3, hy_pos_b3, hy_decay, hy_bias, w_branch_s5, w_branch_hy, w_out, router_w, router_b, w_gate_up, b_gate_up, w_down, b_down, final_g):
    bsz, seq, d = x.shape
    ctx_len = ctx.shape[1]
    depth = w_ada.shape[0]
    assert depth == 1, "only the single-layer configuration is implemented"
    n_experts = router_w.shape[2]
    s5_w = s5_d.shape[1]
    hy_w = hy_bias.shape[2]
    n_tok = bsz * seq

    cc = jnp.zeros((8, d), F32).at[:bsz].set(c).at[bsz].set(c_ctx)
    mod = _ada_mod(cc, w_ada[0], b_ada[0][None]).reshape(8, 6, 1, d)
    sh1, sc1, g1, sh2, sc2, g2 = [mod[:bsz, k] for k in range(6)]
    csh1, csc1 = mod[bsz:bsz + 1, 0], mod[bsz:bsz + 1, 1]

    w_in_b = w_in[0].astype(BF16)
    x2d = x.reshape(n_tok, d)
    riders = ((w_gate_up[0].reshape(-1, w_gate_up.shape[3]), True), (w_down[0].reshape(-1, d), False))
    ua, uh, ug, wgu_b, wd_b = _norm_mod_proj(x2d, norm1_g[0][None], sh1, sc1, w_in_b,
                                             (s5_w, 3 * hy_w, 2 * d), (False, False, True), seq, 1024, 512,
                                             riders=riders)
    (uca,) = _norm_mod_proj(ctx.reshape(bsz * ctx_len, d), norm1_g[0][None], csh1, csc1,
                            w_in_b[:, :s5_w], (s5_w,), (False,), bsz * ctx_len, 512, 1024)

    ya = _s5_branch(ua, uca, bsz, s5_lam_re[0], s5_lam_im[0], s5_log_dt[0], s5_b_re[0], s5_b_im[0],
                    s5_c_re[0], s5_c_im[0], s5_d[0], s5_w_glu[0])

    yh = _hyena_branch(uh, bsz, seq, hy_short_w[0], hy_pos_w1[0], hy_pos_b1[0], hy_freq1[0], hy_pos_w2[0],
                       hy_pos_b2[0], hy_freq2[0], hy_pos_w3[0], hy_pos_b3[0], hy_decay[0], hy_bias[0])

    m = _merge1(ya, yh, ug, w_branch_s5[0].astype(BF16), w_branch_hy[0].astype(BF16))
    rw = router_w[0]
    rw_hi = rw.astype(BF16)
    rw_lo = (rw - rw_hi.astype(F32)).astype(BF16)
    wr = jnp.zeros((d, 128), BF16).at[:, :n_experts].set(rw_hi).at[:, n_experts:2 * n_experts].set(rw_lo)
    h1, hm, r1, r2 = _merge2(m, x2d, g1, norm2_g[0][None], sh2, sc2, w_out[0].astype(BF16), wr, seq)
    logits = (r1[:, :n_experts] + r1[:, n_experts:2 * n_experts] + r2[:, :n_experts]) + router_b[0]

    gates, slot_tok, dest, block_expert, n_valid = _route(logits, n_experts)
    f = w_down.shape[2]
    bgu = b_gate_up[0].reshape(n_experts, f // LANE, LANE, 2).swapaxes(2, 3).reshape(n_experts, 1, 2 * f)
    yslots = _moe_blocks(hm.reshape(n_tok, d // LANE, LANE), slot_tok.reshape(-1, 1, MOE_TM), block_expert,
                         n_valid.reshape(1),
                         wgu_b.reshape(w_gate_up.shape[1:]), bgu, wd_b.reshape(w_down.shape[1:]),
                         b_down[0][:, None, :])
    yg = yslots[dest.reshape(n_tok, TOP_K).T.reshape(-1)]
    out = _final(h1, yg, gates, g2, final_g[None], seq)
    return out.reshape(bsz, seq, d)
```

```python
import functools
import math

import jax
import jax.numpy as jnp
import numpy as np
from jax import lax
from jax.experimental import pallas as pl
from jax.experimental.pallas import tpu as pltpu

F32 = jnp.float32
BF16 = jnp.bfloat16
HIGHEST = lax.Precision.HIGHEST

LANE = 128
GRID_W = 64
NORM_EPS = 1e-6
S5_GROUP = 16
S5_CHUNK = 32
HY_ORDER = 2
HY_NORM_EPS = 1e-6
TOP_K = 4
SWIGLU_LIMIT = 7.0
SWIGLU_ALPHA = 1.702
MOE_TM = 512
MOE_TF = 1024
MOE_PITCH = 20
VMEM_LIMIT = 56 * 1024 * 1024


def _params(sem, vmem=VMEM_LIMIT):
    return pltpu.CompilerParams(dimension_semantics=sem, vmem_limit_bytes=vmem)


def _dot(a, b):
    return jnp.dot(a, b, preferred_element_type=F32)


def _ada_kernel(c_ref, w_ref, b_ref, o_ref):
    c = c_ref[...]
    a = c * jax.nn.sigmoid(c)
    o_ref[...] = jnp.dot(a, w_ref[...], preferred_element_type=F32, precision=HIGHEST) + b_ref[...]


def _ada_mod(cc, w, b):
    d, n6 = w.shape
    tn = min(512, n6)
    return pl.pallas_call(
        _ada_kernel,
        grid=(n6 // tn,),
        in_specs=[pl.BlockSpec((8, d), lambda j: (0, 0)),
                  pl.BlockSpec((d, tn), lambda j: (0, j)),
                  pl.BlockSpec((1, tn), lambda j: (0, j))],
        out_specs=pl.BlockSpec((8, tn), lambda j: (0, j)),
        out_shape=jax.ShapeDtypeStruct((8, n6), F32),
        compiler_params=_params(("parallel",)),
        name="ada_mod",
    )(cc, w, b)


def _inproj_kernel(x_ref, g_ref, sh_ref, sc_ref, w_ref, *rest, bounds, acts, riders):
    nr = len(riders)
    n_in = nr + (1 if any(riders) else 0)
    rider_in, rest = rest[:n_in], rest[n_in:]
    outs, rider_out, xn_ref = rest[:len(bounds)], rest[len(bounds):len(bounds) + nr], rest[-1]
    j = pl.program_id(1)

    @pl.when(j == 0)
    def _():
        x = x_ref[...].astype(F32)
        y = x * lax.rsqrt(jnp.mean(x * x, axis=-1, keepdims=True) + NORM_EPS)
        y = y * g_ref[...]
        y = y * (1.0 + sc_ref[...]) + sh_ref[...]
        xn_ref[...] = y.astype(BF16)

    for (j0, j1), act, o_ref in zip(bounds, acts, outs):
        @pl.when((j >= j0) & (j < j1))
        def _(o_ref=o_ref, act=act):
            r = _dot(xn_ref[...], w_ref[...])
            if act:
                r = jax.nn.sigmoid(r)
            o_ref[...] = r.astype(o_ref.dtype)

    for r_in, r_out, deint in zip(rider_in, rider_out, riders):
        if deint:
            for b in range(r_in.shape[1] // (2 * LANE)):
                sl = slice(2 * b * LANE, (2 * b + 2) * LANE)
                r_out[:, sl] = _dot(r_in[:, sl].astype(BF16), rider_in[-1][...]).astype(r_out.dtype)
        else:
            r_out[...] = r_in[...].astype(r_out.dtype)


def _deinterleave_perm():
    src = jnp.arange(2 * LANE)
    dst = (src % 2) * LANE + src // 2
    return jnp.zeros((2 * LANE, 2 * LANE), BF16).at[src, dst].set(1)


def _norm_mod_proj(x2d, g, sh, sc, w, widths, acts, rows_per_mod, tm, tn, riders=()):
    n, d = x2d.shape
    ncols = w.shape[1]
    tm = min(tm, n)
    tn = min(tn, min(widths))
    bounds, off = [], 0
    for wd in widths:
        bounds.append((off // tn, (off + wd) // tn))
        off += wd
    tiles_per_mod = rows_per_mod // tm
    nj = ncols // tn
    nsteps = (n // tm) * nj

    def out_map(i, j, j0, nj):
        return (i, jnp.clip(j - j0, 0, nj - 1))

    out_specs = [pl.BlockSpec((tm, tn), functools.partial(out_map, j0=j0, nj=j1 - j0))
                 for (j0, j1) in bounds]
    out_shape = [jax.ShapeDtypeStruct((n, wd), BF16) for wd in widths]
    rider_args, rider_specs = [], []
    for arr, _ in riders:
        spec = pl.BlockSpec((arr.shape[0] // nsteps, arr.shape[1]), lambda i, j: (i * nj + j, 0))
        rider_args.append(arr)
        rider_specs.append(spec)
        out_specs.append(spec)
        out_shape.append(jax.ShapeDtypeStruct(arr.shape, BF16))
    if any(flag for _, flag in riders):
        rider_args.append(_deinterleave_perm())
        rider_specs.append(pl.BlockSpec((2 * LANE, 2 * LANE), lambda i, j: (0, 0)))
    return pl.pallas_call(
        functools.partial(_inproj_kernel, bounds=tuple(bounds), acts=tuple(acts),
                          riders=tuple(flag for _, flag in riders)),
        grid=(n // tm, nj),
        in_specs=[pl.BlockSpec((tm, d), lambda i, j: (i, 0)),
                  pl.BlockSpec((1, d), lambda i, j: (0, 0)),
                  pl.BlockSpec((None, 1, d), lambda i, j: (i // tiles_per_mod, 0, 0)),
                  pl.BlockSpec((None, 1, d), lambda i, j: (i // tiles_per_mod, 0, 0)),
                  pl.BlockSpec((None, d, tn), lambda i, j: (j, 0, 0))] + rider_specs,
        out_specs=out_specs,
        out_shape=out_shape,
        scratch_shapes=[pltpu.VMEM((tm, d), BF16)],
        compiler_params=_params(("parallel", "arbitrary")),
        name="norm_mod_proj",
    )(x2d, g, sh, sc, w.reshape(d, nj, tn).transpose(1, 0, 2), *rider_args)


def _s5_matrices(lam_re, lam_im, log_dt, b_re, b_im, c_re, c_im, d_skip, t):
    lam = lax.complex(lam_re.astype(F32), lam_im.astype(F32))
    dt = jnp.exp(log_dt.astype(F32))[..., None]
    lam_dt = lam * dt
    lam_bar = jnp.exp(lam_dt)
    b_bar = ((lam_bar - 1) / lam)[..., None] * lax.complex(b_re.astype(F32), b_im.astype(F32))
    c_out = lax.complex(c_re.astype(F32), c_im.astype(F32))
    g, p = lam.shape[1], lam.shape[2]
    gs = b_bar.shape[-1]
    k = jnp.arange(t + 1, dtype=F32)
    pw = jnp.exp(lam_dt[..., None] * k)
    kk = jnp.einsum('dgcp,dgpk,dgpe->dgkce', c_out, pw[..., :t], b_bar, precision=HIGHEST).real
    kf, kb = kk[0], kk[1]
    zero_lag = kf[:, :1] + kb[:, :1]
    k_all = jnp.concatenate([kb[:, :0:-1], zero_lag, kf[:, 1:]], axis=1)
    jj = jnp.arange(t)
    lag_idx = jj[None, :] - jj[:, None] + (t - 1)
    m_intra = k_all[:, lag_idx]
    m_intra = m_intra.transpose(0, 1, 4, 2, 3).reshape(g, t * gs, t * gs)
    m_intra = m_intra + jnp.eye(t * gs, dtype=F32)[None] * jnp.tile(
        d_skip.astype(F32).reshape(g, 1, gs), (1, t, 1)).reshape(g, 1, t * gs)
    in_f = pw[0][..., t - 1 - jj][..., None] * b_bar[0][:, :, None, :]
    in_b = pw[1][..., jj][..., None] * b_bar[1][:, :, None, :]

    def to_in(z):
        return z.transpose(0, 2, 3, 1).reshape(g, t * gs, p)

    m_in = jnp.concatenate([to_in(in_f.real), to_in(in_f.imag), to_in(in_b.real), to_in(in_b.imag)], axis=-1)
    out_f = c_out[0][:, :, :, None] * pw[0][:, None, :, 1 + jj]
    out_b = c_out[1][:, :, :, None] * pw[1][:, None, :, t - jj]

    def to_out(z):
        return z.transpose(0, 2, 3, 1).reshape(g, p, t * gs)

    m_out = jnp.concatenate([to_out(out_f.real), -to_out(out_f.imag), to_out(out_b.real), -to_out(out_b.imag)],
                            axis=1)
    mu = pw[..., t]
    mre, mim = mu.real, mu.imag
    coef = jnp.stack([
        jnp.concatenate([mre[0], mre[0], mre[1], mre[1]], axis=-1),
        jnp.concatenate([-mim[0], mim[0], -mim[1], mim[1]], axis=-1),
        jnp.concatenate([mim[0], -mim[0], mim[1], -mim[1]], axis=-1)])
    return m_intra.astype(BF16), m_in.astype(BF16), m_out.astype(BF16), coef


def _s5_in_kernel(a_ref, m_ref, o_ref):
    o_ref[...] = _dot(a_ref[...], m_ref[...])


def _s5_chunk_states(a, m_in):
    g, nc, kdim = a.shape
    s = m_in.shape[2]
    nb = min(nc, 1024)
    return pl.pallas_call(
        _s5_in_kernel,
        grid=(g, nc // nb),
        in_specs=[pl.BlockSpec((None, nb, kdim), lambda gi, i: (gi, i, 0)),
                  pl.BlockSpec((None, kdim, s), lambda gi, i: (gi, 0, 0))],
        out_specs=pl.BlockSpec((None, nb, s), lambda gi, i: (gi, i, 0)),
        out_shape=jax.ShapeDtypeStruct((g, nc, s), F32),
        compiler_params=_params(("parallel", "arbitrary")),
        name="s5_chunk_states",
    )(a, m_in)


def _s5_scan_kernel(sf_ref, sb_ref, init_ref, coef_ref, ef_ref, eb_ref, fin_ref, st_ref, *, nblk, half):
    k = pl.program_id(1)
    nk = pl.num_programs(1)
    lanes = 2 * half

    @pl.when(k == 0)
    def _():
        s0f = init_ref[:, 0:lanes]
        s0b = init_ref[:, lanes:2 * lanes]
        st_ref[0] = s0f
        st_ref[1] = pltpu.roll(s0f, half, 1)
        st_ref[2] = s0b
        st_ref[3] = pltpu.roll(s0b, half, 1)

    def body(r, carry):
        sf, sfw, sb, sbw = carry
        rb = nblk - 1 - r
        xf = sf_ref[r]
        xb = sb_ref[rb]
        ef_ref[r] = sf
        eb_ref[rb] = sb
        xfw = pltpu.roll(xf, half, 1)
        xbw = pltpu.roll(xb, half, 1)
        af, bf, bfw = coef_ref[0, :, 0:lanes], coef_ref[1, :, 0:lanes], coef_ref[2, :, 0:lanes]
        ab, bb, bbw = (coef_ref[0, :, lanes:2 * lanes], coef_ref[1, :, lanes:2 * lanes],
                       coef_ref[2, :, lanes:2 * lanes])
        return (sf * af + sfw * bf + xf, sfw * af + sf * bfw + xfw,
                sb * ab + sbw * bb + xb, sbw * ab + sb * bbw + xbw)

    sf, sfw, sb, sbw = lax.fori_loop(0, nblk, body, (st_ref[0], st_ref[1], st_ref[2], st_ref[3]))
    st_ref[0] = sf
    st_ref[1] = sfw
    st_ref[2] = sb
    st_ref[3] = sbw

    @pl.when(k == nk - 1)
    def _():
        fin_ref[:, 0:lanes] = sf
        fin_ref[:, lanes:2 * lanes] = sb


def _s5_scan(s_t, init, coef):
    bsz, nc, g, s4 = s_t.shape
    lanes = s4 // 2
    nblk = min(nc, 128)
    nk = nc // nblk
    return pl.pallas_call(
        functools.partial(_s5_scan_kernel, nblk=nblk, half=lanes // 2),
        grid=(bsz, nk),
        in_specs=[pl.BlockSpec((None, nblk, g, lanes), lambda b, k: (b, k, 0, 0)),
                  pl.BlockSpec((None, nblk, g, lanes), lambda b, k: (b, nk - 1 - k, 0, 1)),
                  pl.BlockSpec((None, g, s4), lambda b, k: (b, 0, 0)),
                  pl.BlockSpec((3, g, s4), lambda b, k: (0, 0, 0))],
        out_specs=[pl.BlockSpec((None, nblk, g, lanes), lambda b, k: (b, k, 0, 0)),
                   pl.BlockSpec((None, nblk, g, lanes), lambda b, k: (b, nk - 1 - k, 0, 0)),
                   pl.BlockSpec((None, g, s4), lambda b, k: (b, 0, 0))],
        out_shape=[jax.ShapeDtypeStruct((bsz, nc, g, lanes), F32),
                   jax.ShapeDtypeStruct((bsz, nc, g, lanes), F32),
                   jax.ShapeDtypeStruct((bsz, g, s4), F32)],
        scratch_shapes=[pltpu.VMEM((4, g, lanes), F32)],
        compiler_params=_params(("parallel", "arbitrary")),
        name="s5_scan",
    )(s_t, s_t, init, coef)


def _s5_out_kernel(a_ref, e_ref, mi_ref, mo_ref, o_ref):
    y = _dot(a_ref[...], mi_ref[...]) + _dot(e_ref[...], mo_ref[...])
    o_ref[...] = y.astype(o_ref.dtype)


def _s5_chunk_outputs(a, e_in, m_intra, m_out):
    g, nc, kdim = a.shape
    s = e_in.shape[2]
    nb = min(nc, 1024)
    return pl.pallas_call(
        _s5_out_kernel,
        grid=(g, nc // nb),
        in_specs=[pl.BlockSpec((None, nb, kdim), lambda gi, i: (gi, i, 0)),
                  pl.BlockSpec((None, nb, s), lambda gi, i: (gi, i, 0)),
                  pl.BlockSpec((None, kdim, kdim), lambda gi, i: (gi, 0, 0)),
                  pl.BlockSpec((None, s, kdim), lambda gi, i: (gi, 0, 0))],
        out_specs=pl.BlockSpec((None, nb, kdim), lambda gi, i: (gi, i, 0)),
        out_shape=jax.ShapeDtypeStruct((g, nc, kdim), BF16),
        compiler_params=_params(("parallel", "arbitrary")),
        name="s5_chunk_outputs",
    )(a, e_in, m_intra, m_out)


def _s5_readout_kernel(y_ref, wg_ref, o_ref):
    y = jax.nn.gelu(y_ref[...].astype(F32))
    gate = _dot(y.astype(BF16), wg_ref[...])
    o_ref[...] = (y * jax.nn.sigmoid(gate)).astype(o_ref.dtype)


def _s5_readout(y2d, wg):
    n, w = y2d.shape
    tm = min(n, 1024)
    return pl.pallas_call(
        _s5_readout_kernel,
        grid=(n // tm,),
        in_specs=[pl.BlockSpec((tm, w), lambda i: (i, 0)),
                  pl.BlockSpec((w, w), lambda i: (0, 0))],
        out_specs=pl.BlockSpec((tm, w), lambda i: (i, 0)),
        out_shape=jax.ShapeDtypeStruct((n, w), BF16),
        compiler_params=_params(("parallel",)),
        name="s5_readout",
    )(y2d, wg)


def _to_chunks(u2d, g, t):
    n = u2d.shape[0]
    return u2d.reshape(n // t, t, g, S5_GROUP).transpose(2, 0, 1, 3).reshape(g, n // t, t * S5_GROUP)


def _from_chunks(y, g, t):
    nc = y.shape[1]
    return y.reshape(g, nc, t, S5_GROUP).transpose(1, 2, 0, 3).reshape(nc * t, g * S5_GROUP)


def _s5_branch(ua, uca, bsz, lam_re, lam_im, log_dt, b_re, b_im, c_re, c_im, d_skip, w_glu):
    g = lam_re.shape[1]
    t = S5_CHUNK
    m_intra, m_in, m_out, coef = _s5_matrices(lam_re, lam_im, log_dt, b_re, b_im, c_re, c_im, d_skip, t)
    s4 = m_in.shape[2]

    def states(u2d):
        a = _to_chunks(u2d, g, t)
        s = _s5_chunk_states(a, m_in)
        nc = s.shape[1] // bsz
        return a, s.reshape(g, bsz, nc, s4).transpose(1, 2, 0, 3)

    _, s_ctx = states(uca)
    _, _, seed = _s5_scan(s_ctx, jnp.zeros((bsz, g, s4), F32), coef)
    a_lat, s_lat = states(ua)
    e_f, e_b, _ = _s5_scan(s_lat, seed, coef)
    e_in = jnp.concatenate([e_f, e_b], axis=-1).astype(BF16)
    e_in = e_in.transpose(2, 0, 1, 3).reshape(g, -1, s4)
    y = _s5_chunk_outputs(a_lat, e_in, m_intra, m_out)
    return _s5_readout(_from_chunks(y, g, t), w_glu.astype(BF16))


HY_KH = GRID_W + 1
HY_KP = 72
HY_QN = 16


def _hy_phase(w_cols, r_cnt):
    n1_tot = 2 * GRID_W
    n_fft = n1_tot * r_cnt
    kh = GRID_W + 1
    k1 = np.arange(kh, dtype=np.int64)[None, :, None]
    r = np.arange(r_cnt, dtype=np.int64)[:, None, None]
    n1 = np.arange(w_cols, dtype=np.int64)[None, None, :]
    return 2.0 * np.pi * ((k1 * (r_cnt * n1 + r)) % n_fft) / n_fft


def _hy_fwd_table(r_cnt, m):
    th = _hy_phase(GRID_W * m, r_cnt)
    kh = th.shape[1]
    f = np.zeros((r_cnt, 2 * HY_KP, GRID_W * m), np.float32)
    f[:, :kh] = np.cos(th)
    f[:, HY_KP:HY_KP + kh] = -np.sin(th)
    return jnp.asarray(f, BF16)


def _hy_inv_table(r_cnt):
    th = _hy_phase(GRID_W, r_cnt)
    kh = th.shape[1]
    c = np.full((kh,), 2.0)
    c[0] = c[-1] = 1.0
    scale = c[None, :, None] / (2 * GRID_W * r_cnt)
    g = np.zeros((r_cnt, GRID_W, 2 * HY_KP), np.float32)
    g[:, :, :kh] = (scale * np.cos(th)).transpose(0, 2, 1)
    g[:, :, HY_KP:HY_KP + kh] = (-scale * np.sin(th)).transpose(0, 2, 1)
    return jnp.asarray(g, BF16)


def _hy_dft_tables(r_cnt):
    k = np.arange(r_cnt, dtype=np.int64)
    th = 2.0 * np.pi * ((k[:, None] * k[None, :]) % r_cnt) / r_cnt
    fc, fs = np.cos(th), np.sin(th)
    fwd = np.block([[fc, fs], [-fs, fc]]).astype(np.float32)
    inv = np.block([[fc, -fs], [fs, fc]]).astype(np.float32)
    return jnp.asarray(fwd, BF16), jnp.asarray(inv, BF16)


def _hy_short_kernel(x_ref, w_ref, o_ref, *, rows, chunk):
    wk = w_ref[...]
    w0, w1, w2 = wk[0:1], wk[1:2], wk[2:3]
    gw = GRID_W
    col = lax.broadcasted_iota(jnp.int32, (gw, x_ref.shape[1]), 0)

    def piece(a, n):
        return x_ref[pl.ds(a, n), :].astype(F32)

    last = piece(rows - gw, gw)
    prev0 = jnp.where(col == 0, 0.0, pltpu.roll(last, 1, 0))
    o_ref[0:gw, :] = (w0 * prev0 + w1 * piece(0, gw) + w2 * piece(gw, gw)).astype(o_ref.dtype)
    first = piece(0, gw)
    next_l = jnp.where(col == gw - 1, 0.0, pltpu.roll(first, gw - 1, 0))
    o_ref[rows - gw:rows, :] = (w0 * piece(rows - 2 * gw, gw) + w1 * last + w2 * next_l).astype(o_ref.dtype)

    def body(i, carry):
        a = pl.multiple_of(gw + i * chunk, gw)
        o_ref[pl.ds(a, chunk), :] = (w0 * piece(a - gw, chunk) + w1 * piece(a, chunk)
                                     + w2 * piece(a + gw, chunk)).astype(o_ref.dtype)
        return carry

    n_full = (rows - 2 * gw) // chunk
    lax.fori_loop(0, n_full, body, 0)
    rem = rows - 2 * gw - n_full * chunk
    if rem:
        a = gw + n_full * chunk
        o_ref[a:a + rem, :] = (w0 * piece(a - gw, rem) + w1 * piece(a, rem)
                               + w2 * piece(a + gw, rem)).astype(o_ref.dtype)


def _hy_short_conv(uh, short_w, bsz, seq):
    n, ch = uh.shape
    ct = min(ch, LANE)
    chunk = min(512, seq - 2 * GRID_W)
    return pl.pallas_call(
        functools.partial(_hy_short_kernel, rows=seq, chunk=chunk),
        grid=(bsz, ch // ct),
        in_specs=[pl.BlockSpec((seq, ct), lambda b, c: (b, c)),
                  pl.BlockSpec((3, ct), lambda b, c: (0, c))],
        out_specs=pl.BlockSpec((seq, ct), lambda b, c: (b, c)),
        out_shape=jax.ShapeDtypeStruct((n, ch), BF16),
        compiler_params=_params(("parallel", "parallel")),
        name="hy_short_conv",
    )(uh, short_w)


def _hy_fwd1_kernel(*refs, m):
    x_refs, f_ref, o_ref, s_ref = refs[:m], refs[m], refs[m + 1], refs[m + 2]
    gw = GRID_W
    for q in range(HY_QN):
        xq = [x[q * gw:(q + 1) * gw, :] for x in x_refs]
        xq = xq[0] if m == 1 else jnp.concatenate(xq, axis=0)
        s_ref[q] = _dot(f_ref[q], xq)
    t = jnp.swapaxes(s_ref[...], 0, 1).astype(o_ref.dtype)
    o_ref[0] = t[0:HY_KH]
    o_ref[1] = t[HY_KP:HY_KP + HY_KH]


def _hy_fwd1(xs, col_blk0, width, bsz, r_cnt, ct):
    m = len(xs)
    table = _hy_fwd_table(r_cnt, m)
    rb = HY_QN * GRID_W
    nrb = r_cnt // HY_QN
    x_spec = pl.BlockSpec((rb, ct), lambda b, i, c: (b * nrb + i, col_blk0 + c))
    return pl.pallas_call(
        functools.partial(_hy_fwd1_kernel, m=m),
        grid=(bsz, nrb, width // ct),
        in_specs=[x_spec] * m + [pl.BlockSpec((HY_QN, 2 * HY_KP, GRID_W * m), lambda b, i, c: (i, 0, 0))],
        out_specs=pl.BlockSpec((None, 2, HY_KH, HY_QN, ct), lambda b, i, c: (b, 0, 0, i, c)),
        out_shape=jax.ShapeDtypeStruct((bsz, 2, HY_KH, r_cnt, width), BF16),
        scratch_shapes=[pltpu.VMEM((HY_QN, 2 * HY_KP, ct), F32)],
        compiler_params=_params(("parallel", "parallel", "arbitrary")),
        name="hy_fwd_stage1",
    )(*xs, table)


def _hy_spec_kernel(a_ref, f2_ref, o_ref):
    r = a_ref.shape[1]
    x = a_ref[...].reshape(2 * r, a_ref.shape[2])
    o_ref[...] = _dot(f2_ref[...], x).reshape(o_ref.shape)


def _hy_filter_spectrum(a, f2, ct):
    _, _, kh, r_cnt, width = a.shape
    return pl.pallas_call(
        _hy_spec_kernel,
        grid=(kh, width // ct),
        in_specs=[pl.BlockSpec((None, 2, None, r_cnt, ct), lambda k, c: (0, 0, k, 0, c)),
                  pl.BlockSpec((2 * r_cnt, 2 * r_cnt), lambda k, c: (0, 0))],
        out_specs=pl.BlockSpec((2, None, r_cnt, ct), lambda k, c: (0, k, 0, c)),
        out_shape=jax.ShapeDtypeStruct((2, kh, r_cnt, width), F32),
        compiler_params=_params(("parallel", "parallel")),
        name="hy_filter_spectrum",
    )(a, f2)


def _hy_mid_kernel(a_ref, kf_ref, invn_ref, f2_ref, f2i_ref, o_ref):
    r = a_ref.shape[1]
    x = a_ref[...].reshape(2 * r, a_ref.shape[2])
    b = _dot(f2_ref[...], x)
    br, bi = b[:r], b[r:]
    kr, ki = kf_ref[0], kf_ref[1]
    s = invn_ref[...]
    cr = (br * kr - bi * ki) * s
    ci = (br * ki + bi * kr) * s
    c = jnp.concatenate([cr, ci], axis=0).astype(BF16)
    o_ref[...] = _dot(f2i_ref[...], c).reshape(o_ref.shape).astype(o_ref.dtype)


def _hy_mid(a, kf, invn, order, f2, f2i, ct):
    bsz, _, kh, r_cnt, width = a.shape
    nct = width // ct
    return pl.pallas_call(
        _hy_mid_kernel,
        grid=(kh, nct, bsz),
        in_specs=[pl.BlockSpec((None, 2, None, r_cnt, ct), lambda k, c, b: (b, 0, k, 0, c)),
                  pl.BlockSpec((2, None, r_cnt, ct), lambda k, c, b: (0, k, 0, order * nct + c)),
                  pl.BlockSpec((1, ct), lambda k, c, b: (0, order * nct + c)),
                  pl.BlockSpec((2 * r_cnt, 2 * r_cnt), lambda k, c, b: (0, 0)),
                  pl.BlockSpec((2 * r_cnt, 2 * r_cnt), lambda k, c, b: (0, 0))],
        out_specs=pl.BlockSpec((None, 2, None, r_cnt, ct), lambda k, c, b: (b, 0, k, 0, c)),
        out_shape=jax.ShapeDtypeStruct(a.shape, BF16),
        compiler_params=_params(("parallel", "parallel", "arbitrary")),
        name="hy_spectrum_product",
    )(a, kf, invn, f2, f2i)


def _hy_inv1_kernel(d_ref, g_ref, xin_ref, xm_ref, bias_ref, o_ref, t_ref, s_ref):
    gw = GRID_W
    pad = jnp.zeros((HY_KP - HY_KH,) + t_ref.shape[1:], F32)
    t_ref[0:HY_KH] = d_ref[0].astype(F32)
    t_ref[HY_KH:HY_KP] = pad
    t_ref[HY_KP:HY_KP + HY_KH] = d_ref[1].astype(F32)
    t_ref[HY_KP + HY_KH:2 * HY_KP] = pad
    s_ref[...] = jnp.swapaxes(t_ref[...], 0, 1)
    for q in range(HY_QN):
        y = _dot(g_ref[q], s_ref[q].astype(BF16))
        rows = slice(q * gw, (q + 1) * gw)
        xin = xin_ref[rows, :].astype(F32)
        o_ref[rows, :] = (xm_ref[rows, :].astype(F32) * (y + xin * bias_ref[...])).astype(o_ref.dtype)


def _hy_inv1(d, xin, xin_blk0, xm, xm_blk0, bias, ct):
    bsz, _, kh, r_cnt, width = d.shape
    table = _hy_inv_table(r_cnt)
    rb = HY_QN * GRID_W
    nrb = r_cnt // HY_QN
    return pl.pallas_call(
        _hy_inv1_kernel,
        grid=(bsz, nrb, width // ct),
        in_specs=[pl.BlockSpec((None, 2, kh, HY_QN, ct), lambda b, i, c: (b, 0, 0, i, c)),
                  pl.BlockSpec((HY_QN, GRID_W, 2 * HY_KP), lambda b, i, c: (i, 0, 0)),
                  pl.BlockSpec((rb, ct), lambda b, i, c: (b * nrb + i, xin_blk0 + c)),
                  pl.BlockSpec((rb, ct), lambda b, i, c: (b * nrb + i, xm_blk0 + c)),
                  pl.BlockSpec((1, ct), lambda b, i, c: (0, c))],
        out_specs=pl.BlockSpec((rb, ct), lambda b, i, c: (b * nrb + i, c)),
        out_shape=jax.ShapeDtypeStruct((bsz * r_cnt * GRID_W, width), BF16),
        scratch_shapes=[pltpu.VMEM((2 * HY_KP, HY_QN, ct), F32), pltpu.VMEM((HY_QN, 2 * HY_KP, ct), F32)],
        compiler_params=_params(("parallel", "parallel", "arbitrary")),
        name="hy_inv_stage1",
    )(d, table, xin, xm, bias)


def _hy_filter_kernel(w1_ref, b1_ref, f1_ref, w2_ref, b2_ref, f2_ref, w3_ref, b3_ref, dec_ref, om_ref, ph_ref,
                      o_ref, sum_ref, *, direction, seqlen, r_cnt, rows):
    i = pl.program_id(1)
    row = lax.broadcasted_iota(jnp.int32, (rows, LANE), 0) + i * rows
    lane = lax.broadcasted_iota(jnp.int32, (rows, LANE), 1)
    r = jnp.right_shift(row, GRID_W.bit_length() - 1)
    w = jnp.bitwise_and(row, GRID_W - 1)
    n = (w + GRID_W * direction) * r_cnt + r
    p = n if direction == 0 else 2 * seqlen - n
    pf = p.astype(F32)
    t = pf / (seqlen - 1)
    feats = jnp.where(lane == 0, t, jnp.cos(pf * om_ref[...] + ph_ref[...]))
    hid = jnp.sin(f1_ref[...] * (jnp.dot(feats, w1_ref[...], preferred_element_type=F32, precision=HIGHEST)
                                 + b1_ref[...]))
    hid = jnp.sin(f2_ref[...] * (jnp.dot(hid, w2_ref[...], preferred_element_type=F32, precision=HIGHEST)
                                 + b2_ref[...]))
    filt = _dot(hid.astype(BF16), w3_ref[...].astype(BF16)) + b3_ref[...]
    reps = filt.shape[1] // LANE
    t_w = jnp.concatenate([t] * reps, axis=1)
    keep = jnp.concatenate([jnp.where(p < seqlen, 1.0, 0.0)] * reps, axis=1)
    filt = filt * jnp.exp(-t_w * jnp.abs(dec_ref[...])) * keep
    o_ref[...] = filt.astype(o_ref.dtype)

    @pl.when(i == 0)
    def _():
        sum_ref[...] = jnp.zeros_like(sum_ref)

    sum_ref[...] += jnp.sum(jnp.abs(filt), axis=0, keepdims=True)


def _hy_filter_signal(direction, seqlen, w1p, b1, f1, w2, b2, f2, w3, b3, decay, omega, phase, width):
    r_cnt = seqlen // GRID_W
    rows = min(256, seqlen)
    ct = width
    nct = width // ct
    hid = w2.shape[0]
    off = direction * nct
    const = lambda shape: pl.BlockSpec(shape, lambda c, i: (0, 0))
    colv = pl.BlockSpec((1, ct), lambda c, i: (0, off + c))
    return pl.pallas_call(
        functools.partial(_hy_filter_kernel, direction=direction, seqlen=seqlen, r_cnt=r_cnt, rows=rows),
        grid=(nct, seqlen // rows),
        in_specs=[const((LANE, hid)), const((1, hid)), const((1, hid)), const((hid, hid)), const((1, hid)),
                  const((1, hid)), pl.BlockSpec((hid, ct), lambda c, i: (0, off + c)), colv, colv,
                  const((1, LANE)), const((1, LANE))],
        out_specs=[pl.BlockSpec((rows, ct), lambda c, i: (i, c)),
                   pl.BlockSpec((1, ct), lambda c, i: (0, c))],
        out_shape=[jax.ShapeDtypeStruct((seqlen, width), BF16), jax.ShapeDtypeStruct((1, width), F32)],
        compiler_params=_params(("parallel", "arbitrary")),
        name="hy_filter_signal",
    )(w1p, b1, f1, w2, b2, f2, w3, b3, decay, omega, phase)


def _hyena_branch(uh, bsz, seq, short_w, w1, b1, f1, w2, b2, f2, w3, b3, decay, hy_bias):
    width = hy_bias.shape[1]
    r_cnt = seq // GRID_W
    ct = min(width, 512)
    nct = width // ct
    n_emb = w1.shape[0]
    n_bands = (n_emb - 1) // 2
    hid = w1.shape[1]
    bands = jnp.linspace(1e-4, n_bands - 1, n_bands, dtype=F32)
    omega = jnp.zeros((1, LANE), F32).at[0, 1:1 + n_bands].set((2 * jnp.pi / seq) * bands)
    omega = omega.at[0, 1 + n_bands:1 + 2 * n_bands].set((2 * jnp.pi / seq) * bands)
    phase = jnp.zeros((1, LANE), F32).at[0, 1 + n_bands:1 + 2 * n_bands].set(0.5 * jnp.pi)
    w1p = jnp.zeros((LANE, hid), F32).at[:n_emb].set(w1)
    sigs, sums = [], []
    for direction in range(2):
        sig, ssum = _hy_filter_signal(direction, seq, w1p, b1[None], f1[None], w2, b2[None], f2[None], w3,
                                      b3[None], decay[None], omega, phase, HY_ORDER * width)
        sigs.append(sig)
        sums.append(ssum)
    invn = 1.0 / (sums[0] + sums[1] + HY_NORM_EPS)
    f2m, f2i = _hy_dft_tables(r_cnt)
    ct3 = min(width, 1024)
    kf = _hy_filter_spectrum(_hy_fwd1(sigs, 0, HY_ORDER * width, 1, r_cnt, ct), f2m, ct3)

    z = _hy_short_conv(uh, short_w.astype(F32), bsz, seq)
    a = _hy_fwd1([z], 0, width, bsz, r_cnt, ct)
    d = _hy_mid(a, kf, invn, 0, f2m, f2i, ct3)
    y1 = _hy_inv1(d, z, 0, z, nct, hy_bias[0:1].astype(F32), ct)
    a = _hy_fwd1([y1], 0, width, bsz, r_cnt, ct)
    d = _hy_mid(a, kf, invn, 1, f2m, f2i, ct3)
    return _hy_inv1(d, y1, 0, z, 2 * nct, hy_bias[1:2].astype(F32), ct)


def _merge1_kernel(ya_ref, yh_ref, ug_ref, wa_ref, wh_ref, o_ref):
    d = o_ref.shape[1]
    a = _dot(ya_ref[...], wa_ref[...])
    h = _dot(yh_ref[...], wh_ref[...])
    ga = ug_ref[:, :d].astype(F32)
    gh = ug_ref[:, d:].astype(F32)
    o_ref[...] = (ga * a + gh * h).astype(o_ref.dtype)


def _merge1(ya, yh, ug, wa, wh):
    n, ws = ya.shape
    wh_in = yh.shape[1]
    d = wa.shape[1]
    tm = min(n, 512)
    return pl.pallas_call(
        _merge1_kernel,
        grid=(n // tm,),
        in_specs=[pl.BlockSpec((tm, ws), lambda i: (i, 0)),
                  pl.BlockSpec((tm, wh_in), lambda i: (i, 0)),
                  pl.BlockSpec((tm, 2 * d), lambda i: (i, 0)),
                  pl.BlockSpec((ws, d), lambda i: (0, 0)),
                  pl.BlockSpec((wh_in, d), lambda i: (0, 0))],
        out_specs=pl.BlockSpec((tm, d), lambda i: (i, 0)),
        out_shape=jax.ShapeDtypeStruct((n, d), BF16),
        compiler_params=_params(("parallel",)),
        name="merge_branches",
    )(ya, yh, ug, wa, wh)


def _merge2_kernel(m_ref, x_ref, g1_ref, ng_ref, sh_ref, sc_ref, wo_ref, wr_ref,
                   h1_ref, hm_ref, r1_ref, r2_ref):
    mix = _dot(m_ref[...], wo_ref[...])
    h1 = x_ref[...] + g1_ref[...] * mix
    h1_ref[...] = h1
    y = h1 * lax.rsqrt(jnp.mean(h1 * h1, axis=-1, keepdims=True) + NORM_EPS)
    y = y * ng_ref[...]
    y = y * (1.0 + sc_ref[...]) + sh_ref[...]
    hi = y.astype(BF16)
    lo = (y - hi.astype(F32)).astype(BF16)
    n_sub = y.shape[1] // LANE
    for s in range(n_sub):
        hm_ref[pl.ds(s, y.shape[0], stride=n_sub), :] = y[:, s * LANE:(s + 1) * LANE]
    r1_ref[...] = _dot(hi, wr_ref[...])
    r2_ref[...] = _dot(lo, wr_ref[...])


def _merge2(m, x2d, g1, ng, sh2, sc2, wo, wr, rows_per_mod):
    n, d = x2d.shape
    tm = min(n, 512)
    tiles_per_mod = rows_per_mod // tm
    nr = wr.shape[1]
    mod_spec = pl.BlockSpec((None, 1, d), lambda i: (i // tiles_per_mod, 0, 0))
    return pl.pallas_call(
        _merge2_kernel,
        grid=(n // tm,),
        in_specs=[pl.BlockSpec((tm, d), lambda i: (i, 0)),
                  pl.BlockSpec((tm, d), lambda i: (i, 0)),
                  mod_spec,
                  pl.BlockSpec((1, d), lambda i: (0, 0)),
                  mod_spec, mod_spec,
                  pl.BlockSpec((d, d), lambda i: (0, 0)),
                  pl.BlockSpec((d, nr), lambda i: (0, 0))],
        out_specs=[pl.BlockSpec((tm, d), lambda i: (i, 0)),
                   pl.BlockSpec((tm * (d // LANE), LANE), lambda i: (i, 0)),
                   pl.BlockSpec((tm, nr), lambda i: (i, 0)),
                   pl.BlockSpec((tm, nr), lambda i: (i, 0))],
        out_shape=[jax.ShapeDtypeStruct((n, d), F32),
                   jax.ShapeDtypeStruct((n * (d // LANE), LANE), F32),
                   jax.ShapeDtypeStruct((n, nr), F32),
                   jax.ShapeDtypeStruct((n, nr), F32)],
        compiler_params=_params(("parallel",)),
        name="out_proj_norm_router",
    )(m, x2d, g1, ng, sh2, sc2, wo, wr)


def _moe_kernel(be_ref, nv_ref, tok0_ref, tokn_ref, hm_ref, wgu_ref, bgu_ref, wd_ref, bd_ref, o_ref,
                xbuf_ref, acc_ref, sem, *, nj):
    i = pl.program_id(0)
    j = pl.program_id(1)
    nv = nv_ref[0]
    tm = acc_ref.shape[0]

    n_sub = hm_ref.shape[1]

    def row_copy(tok, r, slot):
        return pltpu.make_async_copy(hm_ref.at[tok], xbuf_ref.at[slot, pl.ds(r * MOE_PITCH, n_sub), :],
                                     sem.at[slot])

    def wait_block(slot):
        for r in range(tm):
            row_copy(0, r, slot).wait()

    @pl.when((i == 0) & (j == 0) & (nv > 0))
    def _():
        for r in range(tm):
            row_copy(tok0_ref[0, r], r, 0).start()

    def expert_step(first, last):
        slot = i % 2
        if first:
            wait_block(slot)
            for r in range(tm):
                row_copy(tokn_ref[0, r], r, (i + 1) % 2).start()
        x = jnp.concatenate([xbuf_ref[slot, pl.ds(s, tm, stride=MOE_PITCH), :].astype(BF16)
                             for s in range(n_sub)], axis=1)
        gu = _dot(x, wgu_ref[...]) + bgu_ref[...]
        acts = []
        for b in range(gu.shape[1] // (2 * LANE)):
            glu = jnp.minimum(gu[:, 2 * b * LANE:(2 * b + 1) * LANE], SWIGLU_LIMIT)
            lin = jnp.clip(gu[:, (2 * b + 1) * LANE:(2 * b + 2) * LANE], -SWIGLU_LIMIT, SWIGLU_LIMIT)
            acts.append((glu * jax.nn.sigmoid(SWIGLU_ALPHA * glu) * (lin + 1.0)).astype(BF16))
        part = _dot(jnp.concatenate(acts, axis=1), wd_ref[...])
        if not first:
            part = part + acc_ref[...]
        if last:
            o_ref[...] = (part + bd_ref[...]).astype(o_ref.dtype)
        else:
            acc_ref[...] = part

    variants = {1: [(True, True)], 2: [(True, False), (False, True)]}.get(
        nj, [(True, False), (False, False), (False, True)])
    for first, last in variants:
        cond = (i < nv) & ((j == 0) if first else (j > 0)) & ((j == nj - 1) if last else (j < nj - 1))
        pl.when(cond)(functools.partial(expert_step, first, last))

    @pl.when((i == nv - 1) & (j == nj - 1))
    def _():
        wait_block((i + 1) % 2)

    @pl.when((i >= nv) & (j == nj - 1))
    def _():
        o_ref[...] = jnp.zeros_like(o_ref)


def _moe_blocks(hm, slot_tok, block_expert, n_valid, wgu, bgu, wd, bd):
    d = wd.shape[2]
    assert hm.shape[1] <= MOE_PITCH and hm.shape[1] * hm.shape[2] == d
    f = wd.shape[1]
    tm = MOE_TM
    tf = min(f, MOE_TF)
    nj = f // tf
    nblk = block_expert.shape[0]

    def jeff(i, j, nv):
        return jnp.where(i < nv[0], j, nj - 1)

    grid_spec = pltpu.PrefetchScalarGridSpec(
        num_scalar_prefetch=2,
        grid=(nblk, nj),
        in_specs=[pl.BlockSpec((None, 1, tm), lambda i, j, be, nv: (0, 0, 0), memory_space=pltpu.SMEM),
                  pl.BlockSpec((None, 1, tm), lambda i, j, be, nv: (jnp.minimum(i + 1, nblk - 1), 0, 0),
                               memory_space=pltpu.SMEM),
                  pl.BlockSpec(memory_space=pl.ANY),
                  pl.BlockSpec((None, d, 2 * tf), lambda i, j, be, nv: (be[i], 0, jeff(i, j, nv))),
                  pl.BlockSpec((None, 1, 2 * tf), lambda i, j, be, nv: (be[i], 0, jeff(i, j, nv))),
                  pl.BlockSpec((None, tf, d), lambda i, j, be, nv: (be[i], jeff(i, j, nv), 0)),
                  pl.BlockSpec((None, 1, d), lambda i, j, be, nv: (be[i], 0, 0))],
        out_specs=pl.BlockSpec((tm, d), lambda i, j, be, nv: (i, 0)),
        scratch_shapes=[pltpu.VMEM((2, tm * MOE_PITCH, LANE), F32), pltpu.VMEM((tm, d), F32),
                        pltpu.SemaphoreType.DMA((2,))])
    return pl.pallas_call(
        functools.partial(_moe_kernel, nj=nj),
        grid_spec=grid_spec,
        out_shape=jax.ShapeDtypeStruct((nblk * tm, d), BF16),
        compiler_params=_params(("arbitrary", "arbitrary")),
        name="moe_experts",
    )(block_expert, n_valid, slot_tok, slot_tok, hm, wgu, bgu, wd, bd)


def _route(logits, n_experts):
    n_tok = logits.shape[0]
    top_val, top_idx = lax.top_k(logits, TOP_K)
    gates = jax.nn.softmax(top_val, axis=-1)
    n_assign = n_tok * TOP_K
    flat_e = top_idx.reshape(-1).astype(jnp.int32)
    experts = jnp.arange(n_experts, dtype=jnp.int32)
    onehot = (flat_e[:, None] == experts[None, :]).astype(jnp.int32)
    csum = jnp.cumsum(onehot, axis=0)
    rank = jnp.sum(onehot * csum, axis=1) - 1
    counts = csum[-1]
    padded = (counts + MOE_TM - 1) // MOE_TM * MOE_TM
    start = jnp.cumsum(counts) - counts
    padded_end = jnp.cumsum(padded)
    padded_start = padded_end - padded
    dest = padded_start[flat_e] + rank
    n_blocks = -(-n_assign // MOE_TM) + n_experts
    cap = n_blocks * MOE_TM
    n_valid = (padded_end[-1] // MOE_TM).astype(jnp.int32)
    order = jnp.argsort(flat_e)
    slots = jnp.arange(cap, dtype=jnp.int32)
    slot_e = jnp.minimum(jnp.sum((padded_end[None, :] <= slots[:, None]).astype(jnp.int32), axis=1), n_experts - 1)
    slot_rank = slots - padded_start[slot_e]
    src = jnp.minimum(start[slot_e] + slot_rank, n_assign - 1)
    slot_tok = jnp.where(slot_rank < counts[slot_e], order[src] // TOP_K, 0).astype(jnp.int32)
    block_start = jnp.minimum(jnp.arange(n_blocks, dtype=jnp.int32), n_valid - 1) * MOE_TM
    block_expert = jnp.minimum(
        jnp.sum((padded_end[None, :] <= block_start[:, None]).astype(jnp.int32), axis=1), n_experts - 1)
    return gates, slot_tok, dest, block_expert, n_valid


def _final_kernel(h1_ref, *rest):
    yg_refs = rest[:TOP_K]
    gates_ref, g2_ref, fg_ref, o_ref = rest[TOP_K:]
    gates = gates_ref[...]
    moe = yg_refs[0][...].astype(F32) * gates[:, 0:1]
    for k in range(1, TOP_K):
        moe = moe + yg_refs[k][...].astype(F32) * gates[:, k:k + 1]
    h = h1_ref[...] + g2_ref[...] * moe
    y = h * lax.rsqrt(jnp.mean(h * h, axis=-1, keepdims=True) + NORM_EPS)
    o_ref[...] = (y * fg_ref[...]).astype(o_ref.dtype)


def _final(h1, yg, gates, g2, fg, rows_per_mod):
    n, d = h1.shape
    tm = min(n, 512)
    tiles_per_mod = rows_per_mod // tm
    nt = n // tm
    yg_specs = [pl.BlockSpec((tm, d), functools.partial(lambda i, k: (k * nt + i, 0), k=k)) for k in range(TOP_K)]
    return pl.pallas_call(
        _final_kernel,
        grid=(nt,),
        in_specs=[pl.BlockSpec((tm, d), lambda i: (i, 0))] + yg_specs + [
                  pl.BlockSpec((tm, TOP_K), lambda i: (i, 0)),
                  pl.BlockSpec((None, 1, d), lambda i: (i // tiles_per_mod, 0, 0)),
                  pl.BlockSpec((1, d), lambda i: (0, 0))],
        out_specs=pl.BlockSpec((tm, d), lambda i: (i, 0)),
        out_shape=jax.ShapeDtypeStruct((n, d), F32),
        compiler_params=_params(("parallel",)),
        name="combine_final_norm",
    )(h1, *([yg] * TOP_K), gates, g2, fg)


def kernel(x, c, ctx, c_ctx, w_ada, b_ada, norm1_g, norm2_g, w_in, s5_lam_re, s5_lam_im, s5_log_dt, s5_b_re, s5_b_im, s5_c_re, s5_c_im, s5_d, s5_w_glu, hy_short_w, hy_pos_w1, hy_pos_b1, hy_freq1, hy_pos_w2, hy_pos_b2, hy_freq2, hy_pos_w3, hy_pos_b3, hy_decay, hy_bias, w_branch_s5, w_branch_hy, w_out, router_w, router_b, w_gate_up, b_gate_up, w_down, b_down, final_g):
    bsz, seq, d = x.shape
    ctx_len = ctx.shape[1]
    depth = w_ada.shape[0]
    assert depth == 1, "only the single-layer configuration is implemented"
    n_experts = router_w.shape[2]
    s5_w = s5_d.shape[1]
    hy_w = hy_bias.shape[2]
    n_tok = bsz * seq

    cc = jnp.zeros((8, d), F32).at[:bsz].set(c).at[bsz].set(c_ctx)
    mod = _ada_mod(cc, w_ada[0], b_ada[0][None]).reshape(8, 6, 1, d)
    sh1, sc1, g1, sh2, sc2, g2 = [mod[:bsz, k] for k in range(6)]
    csh1, csc1 = mod[bsz:bsz + 1, 0], mod[bsz:bsz + 1, 1]

    w_in_b = w_in[0].astype(BF16)
    x2d = x.reshape(n_tok, d)
    riders = ((w_gate_up[0].reshape(-1, w_gate_up.shape[3]), True), (w_down[0].reshape(-1, d), False))
    ua, uh, ug, wgu_b, wd_b = _norm_mod_proj(x2d, norm1_g[0][None], sh1, sc1, w_in_b,
                                             (s5_w, 3 * hy_w, 2 * d), (False, False, True), seq, 1024, 512,
                                             riders=riders)
    (uca,) = _norm_mod_proj(ctx.reshape(bsz * ctx_len, d), norm1_g[0][None], csh1, csc1,
                            w_in_b[:, :s5_w], (s5_w,), (False,), bsz * ctx_len, 512, 1024)

    ya = _s5_branch(ua, uca, bsz, s5_lam_re[0], s5_lam_im[0], s5_log_dt[0], s5_b_re[0], s5_b_im[0],
                    s5_c_re[0], s5_c_im[0], s5_d[0], s5_w_glu[0])

    yh = _hyena_branch(uh, bsz, seq, hy_short_w[0], hy_pos_w1[0], hy_pos_b1[0], hy_freq1[0], hy_pos_w2[0],
                       hy_pos_b2[0], hy_freq2[0], hy_pos_w3[0], hy_pos_b3[0], hy_decay[0], hy_bias[0])

    m = _merge1(ya, yh, ug, w_branch_s5[0].astype(BF16), w_branch_hy[0].astype(BF16))
    rw = router_w[0]
    rw_hi = rw.astype(BF16)
    rw_lo = (rw - rw_hi.astype(F32)).astype(BF16)
    wr = jnp.zeros((d, 128), BF16).at[:, :n_experts].set(rw_hi).at[:, n_experts:2 * n_experts].set(rw_lo)
    h1, hm, r1, r2 = _merge2(m, x2d, g1, norm2_g[0][None], sh2, sc2, w_out[0].astype(BF16), wr, seq)
    logits = (r1[:, :n_experts] + r1[:, n_experts:2 * n_experts] + r2[:, :n_experts]) + router_b[0]

    gates, slot_tok, dest, block_expert, n_valid = _route(logits, n_experts)
    f = w_down.shape[2]
    bgu = b_gate_up[0].reshape(n_experts, f // LANE, LANE, 2).swapaxes(2, 3).reshape(n_experts, 1, 2 * f)
    yslots = _moe_blocks(hm.reshape(n_tok, d // LANE, LANE), slot_tok.reshape(-1, 1, MOE_TM), block_expert,
                         n_valid.reshape(1),
                         wgu_b.reshape(w_gate_up.shape[1:]), bgu, wd_b.reshape(w_down.shape[1:]),
                         b_down[0][:, None, :])
    yg = yslots[dest.reshape(n_tok, TOP_K).T.reshape(-1)]
    out = _final(h1, yg, gates, g2, final_g[None], seq)
    return out.reshape(bsz, seq, d)
```

```python
import functools
import math

import jax
import jax.numpy as jnp
import numpy as np
from jax import lax
from jax.experimental import pallas as pl
from jax.experimental.pallas import tpu as pltpu

F32 = jnp.float32
BF16 = jnp.bfloat16
HIGHEST = lax.Precision.HIGHEST

LANE = 128
GRID_W = 64
NORM_EPS = 1e-6
S5_GROUP = 16
S5_CHUNK = 32
HY_ORDER = 2
HY_NORM_EPS = 1e-6
TOP_K = 4
SWIGLU_LIMIT = 7.0
SWIGLU_ALPHA = 1.702
MOE_TM = 512
MOE_TF = 1024
MOE_PITCH = 20
VMEM_LIMIT = 56 * 1024 * 1024


def _params(sem, vmem=VMEM_LIMIT):
    return pltpu.CompilerParams(dimension_semantics=sem, vmem_limit_bytes=vmem)


def _dot(a, b):
    return jnp.dot(a, b, preferred_element_type=F32)


def _ada_kernel(c_ref, w_ref, b_ref, o_ref):
    c = c_ref[...]
    a = c * jax.nn.sigmoid(c)
    o_ref[...] = jnp.dot(a, w_ref[...], preferred_element_type=F32, precision=HIGHEST) + b_ref[...]


def _ada_mod(cc, w, b):
    d, n6 = w.shape
    tn = min(512, n6)
    return pl.pallas_call(
        _ada_kernel,
        grid=(n6 // tn,),
        in_specs=[pl.BlockSpec((8, d), lambda j: (0, 0)),
                  pl.BlockSpec((d, tn), lambda j: (0, j)),
                  pl.BlockSpec((1, tn), lambda j: (0, j))],
        out_specs=pl.BlockSpec((8, tn), lambda j: (0, j)),
        out_shape=jax.ShapeDtypeStruct((8, n6), F32),
        compiler_params=_params(("parallel",)),
        name="ada_mod",
    )(cc, w, b)


def _inproj_kernel(x_ref, g_ref, sh_ref, sc_ref, w_ref, *rest, bounds, acts, riders):
    nr = len(riders)
    n_in = nr + (1 if any(riders) else 0)
    rider_in, rest = rest[:n_in], rest[n_in:]
    outs, rider_out, xn_ref = rest[:len(bounds)], rest[len(bounds):len(bounds) + nr], rest[-1]
    j = pl.program_id(1)

    @pl.when(j == 0)
    def _():
        x = x_ref[...].astype(F32)
        y = x * lax.rsqrt(jnp.mean(x * x, axis=-1, keepdims=True) + NORM_EPS)
        y = y * g_ref[...]
        y = y * (1.0 + sc_ref[...]) + sh_ref[...]
        xn_ref[...] = y.astype(BF16)

    for (j0, j1), act, o_ref in zip(bounds, acts, outs):
        @pl.when((j >= j0) & (j < j1))
        def _(o_ref=o_ref, act=act):
            r = _dot(xn_ref[...], w_ref[...])
            if act:
                r = jax.nn.sigmoid(r)
            o_ref[...] = r.astype(o_ref.dtype)

    for r_in, r_out, deint in zip(rider_in, rider_out, riders):
        if deint:
            for b in range(r_in.shape[1] // (2 * LANE)):
                sl = slice(2 * b * LANE, (2 * b + 2) * LANE)
                r_out[:, sl] = _dot(r_in[:, sl].astype(BF16), rider_in[-1][...]).astype(r_out.dtype)
        else:
            r_out[...] = r_in[...].astype(r_out.dtype)


def _deinterleave_perm():
    src = jnp.arange(2 * LANE)
    dst = (src % 2) * LANE + src // 2
    return jnp.zeros((2 * LANE, 2 * LANE), BF16).at[src, dst].set(1)


def _norm_mod_proj(x2d, g, sh, sc, w, widths, acts, rows_per_mod, tm, tn, riders=()):
    n, d = x2d.shape
    ncols = w.shape[1]
    tm = min(tm, n)
    tn = min(tn, min(widths))
    bounds, off = [], 0
    for wd in widths:
        bounds.append((off // tn, (off + wd) // tn))
        off += wd
    tiles_per_mod = rows_per_mod // tm
    nj = ncols // tn
    nsteps = (n // tm) * nj

    def out_map(i, j, j0, nj):
        return (i, jnp.clip(j - j0, 0, nj - 1))

    out_specs = [pl.BlockSpec((tm, tn), functools.partial(out_map, j0=j0, nj=j1 - j0))
                 for (j0, j1) in bounds]
    out_shape = [jax.ShapeDtypeStruct((n, wd), BF16) for wd in widths]
    rider_args, rider_specs = [], []
    for arr, _ in riders:
        spec = pl.BlockSpec((arr.shape[0] // nsteps, arr.shape[1]), lambda i, j: (i * nj + j, 0))
        rider_args.append(arr)
        rider_specs.append(spec)
        out_specs.append(spec)
        out_shape.append(jax.ShapeDtypeStruct(arr.shape, BF16))
    if any(flag for _, flag in riders):
        rider_args.append(_deinterleave_perm())
        rider_specs.append(pl.BlockSpec((2 * LANE, 2 * LANE), lambda i, j: (0, 0)))
    return pl.pallas_call(
        functools.partial(_inproj_kernel, bounds=tuple(bounds), acts=tuple(acts),
                          riders=tuple(flag for _, flag in riders)),
        grid=(n // tm, nj),
        in_specs=[pl.BlockSpec((tm, d), lambda i, j: (i, 0)),
                  pl.BlockSpec((1, d), lambda i, j: (0, 0)),
                  pl.BlockSpec((None, 1, d), lambda i, j: (i // tiles_per_mod, 0, 0)),
                  pl.BlockSpec((None, 1, d), lambda i, j: (i // tiles_per_mod, 0, 0)),
                  pl.BlockSpec((d, tn), lambda i, j: (0, j))] + rider_specs,
        out_specs=out_specs,
        out_shape=out_shape,
        scratch_shapes=[pltpu.VMEM((tm, d), BF16)],
        compiler_params=_params(("parallel", "arbitrary")),
        name="norm_mod_proj",
    )(x2d, g, sh, sc, w, *rider_args)


def _s5_matrices(lam_re, lam_im, log_dt, b_re, b_im, c_re, c_im, d_skip, t):
    lam = lax.complex(lam_re.astype(F32), lam_im.astype(F32))
    dt = jnp.exp(log_dt.astype(F32))[..., None]
    lam_dt = lam * dt
    lam_bar = jnp.exp(lam_dt)
    b_bar = ((lam_bar - 1) / lam)[..., None] * lax.complex(b_re.astype(F32), b_im.astype(F32))
    c_out = lax.complex(c_re.astype(F32), c_im.astype(F32))
    g, p = lam.shape[1], lam.shape[2]
    gs = b_bar.shape[-1]
    k = jnp.arange(t + 1, dtype=F32)
    pw = jnp.exp(lam_dt[..., None] * k)
    kk = jnp.einsum('dgcp,dgpk,dgpe->dgkce', c_out, pw[..., :t], b_bar, precision=HIGHEST).real
    kf, kb = kk[0], kk[1]
    zero_lag = kf[:, :1] + kb[:, :1]
    k_all = jnp.concatenate([kb[:, :0:-1], zero_lag, kf[:, 1:]], axis=1)
    jj = jnp.arange(t)
    lag_idx = jj[None, :] - jj[:, None] + (t - 1)
    m_intra = k_all[:, lag_idx]
    m_intra = m_intra.transpose(0, 1, 4, 2, 3).reshape(g, t * gs, t * gs)
    m_intra = m_intra + jnp.eye(t * gs, dtype=F32)[None] * jnp.tile(
        d_skip.astype(F32).reshape(g, 1, gs), (1, t, 1)).reshape(g, 1, t * gs)
    in_f = pw[0][..., t - 1 - jj][..., None] * b_bar[0][:, :, None, :]
    in_b = pw[1][..., jj][..., None] * b_bar[1][:, :, None, :]

    def to_in(z):
        return z.transpose(0, 2, 3, 1).reshape(g, t * gs, p)

    m_in = jnp.concatenate([to_in(in_f.real), to_in(in_f.imag), to_in(in_b.real), to_in(in_b.imag)], axis=-1)
    out_f = c_out[0][:, :, :, None] * pw[0][:, None, :, 1 + jj]
    out_b = c_out[1][:, :, :, None] * pw[1][:, None, :, t - jj]

    def to_out(z):
        return z.transpose(0, 2, 3, 1).reshape(g, p, t * gs)

    m_out = jnp.concatenate([to_out(out_f.real), -to_out(out_f.imag), to_out(out_b.real), -to_out(out_b.imag)],
                            axis=1)
    mu = pw[..., t]
    mre, mim = mu.real, mu.imag
    coef = jnp.stack([
        jnp.concatenate([mre[0], mre[0], mre[1], mre[1]], axis=-1),
        jnp.concatenate([-mim[0], mim[0], -mim[1], mim[1]], axis=-1),
        jnp.concatenate([mim[0], -mim[0], mim[1], -mim[1]], axis=-1)])
    return m_intra.astype(BF16), m_in.astype(BF16), m_out.astype(BF16), coef


def _s5_in_kernel(a_ref, m_ref, o_ref):
    o_ref[...] = _dot(a_ref[...], m_ref[...])


def _s5_chunk_states(a, m_in):
    g, nc, kdim = a.shape
    s = m_in.shape[2]
    nb = min(nc, 1024)
    return pl.pallas_call(
        _s5_in_kernel,
        grid=(g, nc // nb),
        in_specs=[pl.BlockSpec((None, nb, kdim), lambda gi, i: (gi, i, 0)),
                  pl.BlockSpec((None, kdim, s), lambda gi, i: (gi, 0, 0))],
        out_specs=pl.BlockSpec((None, nb, s), lambda gi, i: (gi, i, 0)),
        out_shape=jax.ShapeDtypeStruct((g, nc, s), F32),
        compiler_params=_params(("parallel", "arbitrary")),
        name="s5_chunk_states",
    )(a, m_in)


def _s5_scan_kernel(sf_ref, sb_ref, init_ref, coef_ref, ef_ref, eb_ref, fin_ref, st_ref, *, nblk, half):
    k = pl.program_id(1)
    nk = pl.num_programs(1)
    lanes = 2 * half

    @pl.when(k == 0)
    def _():
        s0f = init_ref[:, 0:lanes]
        s0b = init_ref[:, lanes:2 * lanes]
        st_ref[0] = s0f
        st_ref[1] = pltpu.roll(s0f, half, 1)
        st_ref[2] = s0b
        st_ref[3] = pltpu.roll(s0b, half, 1)

    def body(r, carry):
        sf, sfw, sb, sbw = carry
        rb = nblk - 1 - r
        xf = sf_ref[r]
        xb = sb_ref[rb]
        ef_ref[r] = sf
        eb_ref[rb] = sb
        xfw = pltpu.roll(xf, half, 1)
        xbw = pltpu.roll(xb, half, 1)
        af, bf, bfw = coef_ref[0, :, 0:lanes], coef_ref[1, :, 0:lanes], coef_ref[2, :, 0:lanes]
        ab, bb, bbw = (coef_ref[0, :, lanes:2 * lanes], coef_ref[1, :, lanes:2 * lanes],
                       coef_ref[2, :, lanes:2 * lanes])
        return (sf * af + sfw * bf + xf, sfw * af + sf * bfw + xfw,
                sb * ab + sbw * bb + xb, sbw * ab + sb * bbw + xbw)

    sf, sfw, sb, sbw = lax.fori_loop(0, nblk, body, (st_ref[0], st_ref[1], st_ref[2], st_ref[3]))
    st_ref[0] = sf
    st_ref[1] = sfw
    st_ref[2] = sb
    st_ref[3] = sbw

    @pl.when(k == nk - 1)
    def _():
        fin_ref[:, 0:lanes] = sf
        fin_ref[:, lanes:2 * lanes] = sb


def _s5_scan(s_t, init, coef):
    bsz, nc, g, s4 = s_t.shape
    lanes = s4 // 2
    nblk = min(nc, 128)
    nk = nc // nblk
    return pl.pallas_call(
        functools.partial(_s5_scan_kernel, nblk=nblk, half=lanes // 2),
        grid=(bsz, nk),
        in_specs=[pl.BlockSpec((None, nblk, g, lanes), lambda b, k: (b, k, 0, 0)),
                  pl.BlockSpec((None, nblk, g, lanes), lambda b, k: (b, nk - 1 - k, 0, 1)),
                  pl.BlockSpec((None, g, s4), lambda b, k: (b, 0, 0)),
                  pl.BlockSpec((3, g, s4), lambda b, k: (0, 0, 0))],
        out_specs=[pl.BlockSpec((None, nblk, g, lanes), lambda b, k: (b, k, 0, 0)),
                   pl.BlockSpec((None, nblk, g, lanes), lambda b, k: (b, nk - 1 - k, 0, 0)),
                   pl.BlockSpec((None, g, s4), lambda b, k: (b, 0, 0))],
        out_shape=[jax.ShapeDtypeStruct((bsz, nc, g, lanes), F32),
                   jax.ShapeDtypeStruct((bsz, nc, g, lanes), F32),
                   jax.ShapeDtypeStruct((bsz, g, s4), F32)],
        scratch_shapes=[pltpu.VMEM((4, g, lanes), F32)],
        compiler_params=_params(("parallel", "arbitrary")),
        name="s5_scan",
    )(s_t, s_t, init, coef)


def _s5_out_kernel(a_ref, e_ref, mi_ref, mo_ref, o_ref):
    y = _dot(a_ref[...], mi_ref[...]) + _dot(e_ref[...], mo_ref[...])
    o_ref[...] = y.astype(o_ref.dtype)


def _s5_chunk_outputs(a, e_in, m_intra, m_out):
    g, nc, kdim = a.shape
    s = e_in.shape[2]
    nb = min(nc, 1024)
    return pl.pallas_call(
        _s5_out_kernel,
        grid=(g, nc // nb),
        in_specs=[pl.BlockSpec((None, nb, kdim), lambda gi, i: (gi, i, 0)),
                  pl.BlockSpec((None, nb, s), lambda gi, i: (gi, i, 0)),
                  pl.BlockSpec((None, kdim, kdim), lambda gi, i: (gi, 0, 0)),
                  pl.BlockSpec((None, s, kdim), lambda gi, i: (gi, 0, 0))],
        out_specs=pl.BlockSpec((None, nb, kdim), lambda gi, i: (gi, i, 0)),
        out_shape=jax.ShapeDtypeStruct((g, nc, kdim), BF16),
        compiler_params=_params(("parallel", "arbitrary")),
        name="s5_chunk_outputs",
    )(a, e_in, m_intra, m_out)


def _s5_readout_kernel(y_ref, wg_ref, o_ref):
    y = jax.nn.gelu(y_ref[...].astype(F32))
    gate = _dot(y.astype(BF16), wg_ref[...])
    o_ref[...] = (y * jax.nn.sigmoid(gate)).astype(o_ref.dtype)


def _s5_readout(y2d, wg):
    n, w = y2d.shape
    tm = min(n, 1024)
    return pl.pallas_call(
        _s5_readout_kernel,
        grid=(n // tm,),
        in_specs=[pl.BlockSpec((tm, w), lambda i: (i, 0)),
                  pl.BlockSpec((w, w), lambda i: (0, 0))],
        out_specs=pl.BlockSpec((tm, w), lambda i: (i, 0)),
        out_shape=jax.ShapeDtypeStruct((n, w), BF16),
        compiler_params=_params(("parallel",)),
        name="s5_readout",
    )(y2d, wg)


def _to_chunks(u2d, g, t):
    n = u2d.shape[0]
    return u2d.reshape(n // t, t, g, S5_GROUP).transpose(2, 0, 1, 3).reshape(g, n // t, t * S5_GROUP)


def _from_chunks(y, g, t):
    nc = y.shape[1]
    return y.reshape(g, nc, t, S5_GROUP).transpose(1, 2, 0, 3).reshape(nc * t, g * S5_GROUP)


def _s5_branch(ua, uca, bsz, lam_re, lam_im, log_dt, b_re, b_im, c_re, c_im, d_skip, w_glu):
    g = lam_re.shape[1]
    t = S5_CHUNK
    m_intra, m_in, m_out, coef = _s5_matrices(lam_re, lam_im, log_dt, b_re, b_im, c_re, c_im, d_skip, t)
    s4 = m_in.shape[2]

    def states(u2d):
        a = _to_chunks(u2d, g, t)
        s = _s5_chunk_states(a, m_in)
        nc = s.shape[1] // bsz
        return a, s.reshape(g, bsz, nc, s4).transpose(1, 2, 0, 3)

    _, s_ctx = states(uca)
    _, _, seed = _s5_scan(s_ctx, jnp.zeros((bsz, g, s4), F32), coef)
    a_lat, s_lat = states(ua)
    e_f, e_b, _ = _s5_scan(s_lat, seed, coef)
    e_in = jnp.concatenate([e_f, e_b], axis=-1).astype(BF16)
    e_in = e_in.transpose(2, 0, 1, 3).reshape(g, -1, s4)
    y = _s5_chunk_outputs(a_lat, e_in, m_intra, m_out)
    return _s5_readout(_from_chunks(y, g, t), w_glu.astype(BF16))


HY_KH = GRID_W + 1
HY_KP = 72
HY_QN = 16


def _hy_phase(w_cols, r_cnt):
    n1_tot = 2 * GRID_W
    n_fft = n1_tot * r_cnt
    kh = GRID_W + 1
    k1 = np.arange(kh, dtype=np.int64)[None, :, None]
    r = np.arange(r_cnt, dtype=np.int64)[:, None, None]
    n1 = np.arange(w_cols, dtype=np.int64)[None, None, :]
    return 2.0 * np.pi * ((k1 * (r_cnt * n1 + r)) % n_fft) / n_fft


def _hy_fwd_table(r_cnt, m):
    th = _hy_phase(GRID_W * m, r_cnt)
    kh = th.shape[1]
    f = np.zeros((r_cnt, 2 * HY_KP, GRID_W * m), np.float32)
    f[:, :kh] = np.cos(th)
    f[:, HY_KP:HY_KP + kh] = -np.sin(th)
    return jnp.asarray(f, BF16)


def _hy_inv_table(r_cnt):
    th = _hy_phase(GRID_W, r_cnt)
    kh = th.shape[1]
    c = np.full((kh,), 2.0)
    c[0] = c[-1] = 1.0
    scale = c[None, :, None] / (2 * GRID_W * r_cnt)
    g = np.zeros((r_cnt, GRID_W, 2 * HY_KP), np.float32)
    g[:, :, :kh] = (scale * np.cos(th)).transpose(0, 2, 1)
    g[:, :, HY_KP:HY_KP + kh] = (-scale * np.sin(th)).transpose(0, 2, 1)
    return jnp.asarray(g, BF16)


def _hy_dft_tables(r_cnt):
    k = np.arange(r_cnt, dtype=np.int64)
    th = 2.0 * np.pi * ((k[:, None] * k[None, :]) % r_cnt) / r_cnt
    fc, fs = np.cos(th), np.sin(th)
    fwd = np.block([[fc, fs], [-fs, fc]]).astype(np.float32)
    inv = np.block([[fc, -fs], [fs, fc]]).astype(np.float32)
    return jnp.asarray(fwd, BF16), jnp.asarray(inv, BF16)


def _hy_short_kernel(x_ref, w_ref, o_ref, *, rows, chunk):
    wk = w_ref[...]
    w0, w1, w2 = wk[0:1], wk[1:2], wk[2:3]
    gw = GRID_W
    col = lax.broadcasted_iota(jnp.int32, (gw, x_ref.shape[1]), 0)

    def piece(a, n):
        return x_ref[pl.ds(a, n), :].astype(F32)

    last = piece(rows - gw, gw)
    prev0 = jnp.where(col == 0, 0.0, pltpu.roll(last, 1, 0))
    o_ref[0:gw, :] = (w0 * prev0 + w1 * piece(0, gw) + w2 * piece(gw, gw)).astype(o_ref.dtype)
    first = piece(0, gw)
    next_l = jnp.where(col == gw - 1, 0.0, pltpu.roll(first, gw - 1, 0))
    o_ref[rows - gw:rows, :] = (w0 * piece(rows - 2 * gw, gw) + w1 * last + w2 * next_l).astype(o_ref.dtype)

    def body(i, carry):
        a = pl.multiple_of(gw + i * chunk, gw)
        o_ref[pl.ds(a, chunk), :] = (w0 * piece(a - gw, chunk) + w1 * piece(a, chunk)
                                     + w2 * piece(a + gw, chunk)).astype(o_ref.dtype)
        return carry

    n_full = (rows - 2 * gw) // chunk
    lax.fori_loop(0, n_full, body, 0)
    rem = rows - 2 * gw - n_full * chunk
    if rem:
        a = gw + n_full * chunk
        o_ref[a:a + rem, :] = (w0 * piece(a - gw, rem) + w1 * piece(a, rem)
                               + w2 * piece(a + gw, rem)).astype(o_ref.dtype)


def _hy_short_conv(uh, short_w, bsz, seq):
    n, ch = uh.shape
    ct = min(ch, LANE)
    chunk = min(512, seq - 2 * GRID_W)
    return pl.pallas_call(
        functools.partial(_hy_short_kernel, rows=seq, chunk=chunk),
        grid=(bsz, ch // ct),
        in_specs=[pl.BlockSpec((seq, ct), lambda b, c: (b, c)),
                  pl.BlockSpec((3, ct), lambda b, c: (0, c))],
        out_specs=pl.BlockSpec((seq, ct), lambda b, c: (b, c)),
        out_shape=jax.ShapeDtypeStruct((n, ch), BF16),
        compiler_params=_params(("parallel", "parallel")),
        name="hy_short_conv",
    )(uh, short_w)


def _hy_fwd1_kernel(*refs, m):
    x_refs, f_ref, o_ref, s_ref = refs[:m], refs[m], refs[m + 1], refs[m + 2]
    gw = GRID_W
    for q in range(HY_QN):
        xq = [x[q * gw:(q + 1) * gw, :] for x in x_refs]
        xq = xq[0] if m == 1 else jnp.concatenate(xq, axis=0)
        s_ref[q] = _dot(f_ref[q], xq)
    t = jnp.swapaxes(s_ref[...], 0, 1).astype(o_ref.dtype)
    o_ref[0] = t[0:HY_KH]
    o_ref[1] = t[HY_KP:HY_KP + HY_KH]


def _hy_fwd1(xs, col_blk0, width, bsz, r_cnt, ct):
    m = len(xs)
    table = _hy_fwd_table(r_cnt, m)
    rb = HY_QN * GRID_W
    nrb = r_cnt // HY_QN
    x_spec = pl.BlockSpec((rb, ct), lambda b, i, c: (b * nrb + i, col_blk0 + c))
    return pl.pallas_call(
        functools.partial(_hy_fwd1_kernel, m=m),
        grid=(bsz, nrb, width // ct),
        in_specs=[x_spec] * m + [pl.BlockSpec((HY_QN, 2 * HY_KP, GRID_W * m), lambda b, i, c: (i, 0, 0))],
        out_specs=pl.BlockSpec((None, 2, HY_KH, HY_QN, ct), lambda b, i, c: (b, 0, 0, i, c)),
        out_shape=jax.ShapeDtypeStruct((bsz, 2, HY_KH, r_cnt, width), BF16),
        scratch_shapes=[pltpu.VMEM((HY_QN, 2 * HY_KP, ct), F32)],
        compiler_params=_params(("parallel", "parallel", "arbitrary")),
        name="hy_fwd_stage1",
    )(*xs, table)


def _hy_spec_kernel(a_ref, f2_ref, o_ref):
    r = a_ref.shape[1]
    x = a_ref[...].reshape(2 * r, a_ref.shape[2])
    o_ref[...] = _dot(f2_ref[...], x).reshape(o_ref.shape)


def _hy_filter_spectrum(a, f2, ct):
    _, _, kh, r_cnt, width = a.shape
    return pl.pallas_call(
        _hy_spec_kernel,
        grid=(kh, width // ct),
        in_specs=[pl.BlockSpec((None, 2, None, r_cnt, ct), lambda k, c: (0, 0, k, 0, c)),
                  pl.BlockSpec((2 * r_cnt, 2 * r_cnt), lambda k, c: (0, 0))],
        out_specs=pl.BlockSpec((2, None, r_cnt, ct), lambda k, c: (0, k, 0, c)),
        out_shape=jax.ShapeDtypeStruct((2, kh, r_cnt, width), F32),
        compiler_params=_params(("parallel", "parallel")),
        name="hy_filter_spectrum",
    )(a, f2)


def _hy_mid_kernel(a_ref, kf_ref, invn_ref, f2_ref, f2i_ref, o_ref):
    r = a_ref.shape[1]
    x = a_ref[...].reshape(2 * r, a_ref.shape[2])
    b = _dot(f2_ref[...], x)
    br, bi = b[:r], b[r:]
    kr, ki = kf_ref[0], kf_ref[1]
    s = invn_ref[...]
    cr = (br * kr - bi * ki) * s
    ci = (br * ki + bi * kr) * s
    c = jnp.concatenate([cr, ci], axis=0).astype(BF16)
    o_ref[...] = _dot(f2i_ref[...], c).reshape(o_ref.shape).astype(o_ref.dtype)


def _hy_mid(a, kf, invn, order, f2, f2i, ct):
    bsz, _, kh, r_cnt, width = a.shape
    nct = width // ct
    return pl.pallas_call(
        _hy_mid_kernel,
        grid=(kh, nct, bsz),
        in_specs=[pl.BlockSpec((None, 2, None, r_cnt, ct), lambda k, c, b: (b, 0, k, 0, c)),
                  pl.BlockSpec((2, None, r_cnt, ct), lambda k, c, b: (0, k, 0, order * nct + c)),
                  pl.BlockSpec((1, ct), lambda k, c, b: (0, order * nct + c)),
                  pl.BlockSpec((2 * r_cnt, 2 * r_cnt), lambda k, c, b: (0, 0)),
                  pl.BlockSpec((2 * r_cnt, 2 * r_cnt), lambda k, c, b: (0, 0))],
        out_specs=pl.BlockSpec((None, 2, None, r_cnt, ct), lambda k, c, b: (b, 0, k, 0, c)),
        out_shape=jax.ShapeDtypeStruct(a.shape, BF16),
        compiler_params=_params(("parallel", "parallel", "arbitrary")),
        name="hy_spectrum_product",
    )(a, kf, invn, f2, f2i)


def _hy_inv1_kernel(d_ref, g_ref, xin_ref, xm_ref, bias_ref, o_ref, t_ref, s_ref):
    gw = GRID_W
    pad = jnp.zeros((HY_KP - HY_KH,) + t_ref.shape[1:], F32)
    t_ref[0:HY_KH] = d_ref[0].astype(F32)
    t_ref[HY_KH:HY_KP] = pad
    t_ref[HY_KP:HY_KP + HY_KH] = d_ref[1].astype(F32)
    t_ref[HY_KP + HY_KH:2 * HY_KP] = pad
    s_ref[...] = jnp.swapaxes(t_ref[...], 0, 1)
    for q in range(HY_QN):
        y = _dot(g_ref[q], s_ref[q].astype(BF16))
        rows = slice(q * gw, (q + 1) * gw)
        xin = xin_ref[rows, :].astype(F32)
        o_ref[rows, :] = (xm_ref[rows, :].astype(F32) * (y + xin * bias_ref[...])).astype(o_ref.dtype)


def _hy_inv1(d, xin, xin_blk0, xm, xm_blk0, bias, ct):
    bsz, _, kh, r_cnt, width = d.shape
    table = _hy_inv_table(r_cnt)
    rb = HY_QN * GRID_W
    nrb = r_cnt // HY_QN
    return pl.pallas_call(
        _hy_inv1_kernel,
        grid=(bsz, nrb, width // ct),
        in_specs=[pl.BlockSpec((None, 2, kh, HY_QN, ct), lambda b, i, c: (b, 0, 0, i, c)),
                  pl.BlockSpec((HY_QN, GRID_W, 2 * HY_KP), lambda b, i, c: (i, 0, 0)),
                  pl.BlockSpec((rb, ct), lambda b, i, c: (b * nrb + i, xin_blk0 + c)),
                  pl.BlockSpec((rb, ct), lambda b, i, c: (b * nrb + i, xm_blk0 + c)),
                  pl.BlockSpec((1, ct), lambda b, i, c: (0, c))],
        out_specs=pl.BlockSpec((rb, ct), lambda b, i, c: (b * nrb + i, c)),
        out_shape=jax.ShapeDtypeStruct((bsz * r_cnt * GRID_W, width), BF16),
        scratch_shapes=[pltpu.VMEM((2 * HY_KP, HY_QN, ct), F32), pltpu.VMEM((HY_QN, 2 * HY_KP, ct), F32)],
        compiler_params=_params(("parallel", "parallel", "arbitrary")),
        name="hy_inv_stage1",
    )(d, table, xin, xm, bias)


def _hy_filter_kernel(w1_ref, b1_ref, f1_ref, w2_ref, b2_ref, f2_ref, w3_ref, b3_ref, dec_ref, om_ref, ph_ref,
                      o_ref, sum_ref, *, direction, seqlen, r_cnt, rows):
    i = pl.program_id(1)
    row = lax.broadcasted_iota(jnp.int32, (rows, LANE), 0) + i * rows
    lane = lax.broadcasted_iota(jnp.int32, (rows, LANE), 1)
    r = jnp.right_shift(row, GRID_W.bit_length() - 1)
    w = jnp.bitwise_and(row, GRID_W - 1)
    n = (w + GRID_W * direction) * r_cnt + r
    p = n if direction == 0 else 2 * seqlen - n
    pf = p.astype(F32)
    t = pf / (seqlen - 1)
    feats = jnp.where(lane == 0, t, jnp.cos(pf * om_ref[...] + ph_ref[...]))
    hid = jnp.sin(f1_ref[...] * (jnp.dot(feats, w1_ref[...], preferred_element_type=F32, precision=HIGHEST)
                                 + b1_ref[...]))
    hid = jnp.sin(f2_ref[...] * (jnp.dot(hid, w2_ref[...], preferred_element_type=F32, precision=HIGHEST)
                                 + b2_ref[...]))
    filt = _dot(hid.astype(BF16), w3_ref[...].astype(BF16)) + b3_ref[...]
    reps = filt.shape[1] // LANE
    t_w = jnp.concatenate([t] * reps, axis=1)
    keep = jnp.concatenate([jnp.where(p < seqlen, 1.0, 0.0)] * reps, axis=1)
    filt = filt * jnp.exp(-t_w * jnp.abs(dec_ref[...])) * keep
    o_ref[...] = filt.astype(o_ref.dtype)

    @pl.when(i == 0)
    def _():
        sum_ref[...] = jnp.zeros_like(sum_ref)

    sum_ref[...] += jnp.sum(jnp.abs(filt), axis=0, keepdims=True)


def _hy_filter_signal(direction, seqlen, w1p, b1, f1, w2, b2, f2, w3, b3, decay, omega, phase, width):
    r_cnt = seqlen // GRID_W
    rows = min(256, seqlen)
    ct = width
    nct = width // ct
    hid = w2.shape[0]
    off = direction * nct
    const = lambda shape: pl.BlockSpec(shape, lambda c, i: (0, 0))
    colv = pl.BlockSpec((1, ct), lambda c, i: (0, off + c))
    return pl.pallas_call(
        functools.partial(_hy_filter_kernel, direction=direction, seqlen=seqlen, r_cnt=r_cnt, rows=rows),
        grid=(nct, seqlen // rows),
        in_specs=[const((LANE, hid)), const((1, hid)), const((1, hid)), const((hid, hid)), const((1, hid)),
                  const((1, hid)), pl.BlockSpec((hid, ct), lambda c, i: (0, off + c)), colv, colv,
                  const((1, LANE)), const((1, LANE))],
        out_specs=[pl.BlockSpec((rows, ct), lambda c, i: (i, c)),
                   pl.BlockSpec((1, ct), lambda c, i: (0, c))],
        out_shape=[jax.ShapeDtypeStruct((seqlen, width), BF16), jax.ShapeDtypeStruct((1, width), F32)],
        compiler_params=_params(("parallel", "arbitrary")),
        name="hy_filter_signal",
    )(w1p, b1, f1, w2, b2, f2, w3, b3, decay, omega, phase)


def _hyena_branch(uh, bsz, seq, short_w, w1, b1, f1, w2, b2, f2, w3, b3, decay, hy_bias):
    width = hy_bias.shape[1]
    r_cnt = seq // GRID_W
    ct = min(width, 512)
    nct = width // ct
    n_emb = w1.shape[0]
    n_bands = (n_emb - 1) // 2
    hid = w1.shape[1]
    bands = jnp.linspace(1e-4, n_bands - 1, n_bands, dtype=F32)
    omega = jnp.zeros((1, LANE), F32).at[0, 1:1 + n_bands].set((2 * jnp.pi / seq) * bands)
    omega = omega.at[0, 1 + n_bands:1 + 2 * n_bands].set((2 * jnp.pi / seq) * bands)
    phase = jnp.zeros((1, LANE), F32).at[0, 1 + n_bands:1 + 2 * n_bands].set(0.5 * jnp.pi)
    w1p = jnp.zeros((LANE, hid), F32).at[:n_emb].set(w1)
    sigs, sums = [], []
    for direction in range(2):
        sig, ssum = _hy_filter_signal(direction, seq, w1p, b1[None], f1[None], w2, b2[None], f2[None], w3,
                                      b3[None], decay[None], omega, phase, HY_ORDER * width)
        sigs.append(sig)
        sums.append(ssum)
    invn = 1.0 / (sums[0] + sums[1] + HY_NORM_EPS)
    f2m, f2i = _hy_dft_tables(r_cnt)
    ct3 = min(width, 1024)
    kf = _hy_filter_spectrum(_hy_fwd1(sigs, 0, HY_ORDER * width, 1, r_cnt, ct), f2m, ct3)

    z = _hy_short_conv(uh, short_w.astype(F32), bsz, seq)
    a = _hy_fwd1([z], 0, width, bsz, r_cnt, ct)
    d = _hy_mid(a, kf, invn, 0, f2m, f2i, ct3)
    y1 = _hy_inv1(d, z, 0, z, nct, hy_bias[0:1].astype(F32), ct)
    a = _hy_fwd1([y1], 0, width, bsz, r_cnt, ct)
    d = _hy_mid(a, kf, invn, 1, f2m, f2i, ct3)
    return _hy_inv1(d, y1, 0, z, 2 * nct, hy_bias[1:2].astype(F32), ct)


def _merge1_kernel(ya_ref, yh_ref, ug_ref, wa_ref, wh_ref, o_ref):
    d = o_ref.shape[1]
    a = _dot(ya_ref[...], wa_ref[...])
    h = _dot(yh_ref[...], wh_ref[...])
    ga = ug_ref[:, :d].astype(F32)
    gh = ug_ref[:, d:].astype(F32)
    o_ref[...] = (ga * a + gh * h).astype(o_ref.dtype)


def _merge1(ya, yh, ug, wa, wh):
    n, ws = ya.shape
    wh_in = yh.shape[1]
    d = wa.shape[1]
    tm = min(n, 512)
    return pl.pallas_call(
        _merge1_kernel,
        grid=(n // tm,),
        in_specs=[pl.BlockSpec((tm, ws), lambda i: (i, 0)),
                  pl.BlockSpec((tm, wh_in), lambda i: (i, 0)),
                  pl.BlockSpec((tm, 2 * d), lambda i: (i, 0)),
                  pl.BlockSpec((ws, d), lambda i: (0, 0)),
                  pl.BlockSpec((wh_in, d), lambda i: (0, 0))],
        out_specs=pl.BlockSpec((tm, d), lambda i: (i, 0)),
        out_shape=jax.ShapeDtypeStruct((n, d), BF16),
        compiler_params=_params(("parallel",)),
        name="merge_branches",
    )(ya, yh, ug, wa, wh)


def _merge2_kernel(m_ref, x_ref, g1_ref, ng_ref, sh_ref, sc_ref, wo_ref, wr_ref,
                   h1_ref, hm_ref, r1_ref, r2_ref):
    mix = _dot(m_ref[...], wo_ref[...])
    h1 = x_ref[...] + g1_ref[...] * mix
    h1_ref[...] = h1
    y = h1 * lax.rsqrt(jnp.mean(h1 * h1, axis=-1, keepdims=True) + NORM_EPS)
    y = y * ng_ref[...]
    y = y * (1.0 + sc_ref[...]) + sh_ref[...]
    hi = y.astype(BF16)
    lo = (y - hi.astype(F32)).astype(BF16)
    n_sub = y.shape[1] // LANE
    for s in range(n_sub):
        hm_ref[pl.ds(s, y.shape[0], stride=n_sub), :] = y[:, s * LANE:(s + 1) * LANE]
    r1_ref[...] = _dot(hi, wr_ref[...])
    r2_ref[...] = _dot(lo, wr_ref[...])


def _merge2(m, x2d, g1, ng, sh2, sc2, wo, wr, rows_per_mod):
    n, d = x2d.shape
    tm = min(n, 512)
    tiles_per_mod = rows_per_mod // tm
    nr = wr.shape[1]
    mod_spec = pl.BlockSpec((None, 1, d), lambda i: (i // tiles_per_mod, 0, 0))
    return pl.pallas_call(
        _merge2_kernel,
        grid=(n // tm,),
        in_specs=[pl.BlockSpec((tm, d), lambda i: (i, 0)),
                  pl.BlockSpec((tm, d), lambda i: (i, 0)),
                  mod_spec,
                  pl.BlockSpec((1, d), lambda i: (0, 0)),
                  mod_spec, mod_spec,
                  pl.BlockSpec((d, d), lambda i: (0, 0)),
                  pl.BlockSpec((d, nr), lambda i: (0, 0))],
        out_specs=[pl.BlockSpec((tm, d), lambda i: (i, 0)),
                   pl.BlockSpec((tm * (d // LANE), LANE), lambda i: (i, 0)),
                   pl.BlockSpec((tm, nr), lambda i: (i, 0)),
                   pl.BlockSpec((tm, nr), lambda i: (i, 0))],
        out_shape=[jax.ShapeDtypeStruct((n, d), F32),
                   jax.ShapeDtypeStruct((n * (d // LANE), LANE), F32),
                   jax.ShapeDtypeStruct((n, nr), F32),
                   jax.ShapeDtypeStruct((n, nr), F32)],
        compiler_params=_params(("parallel",)),
        name="out_proj_norm_router",
    )(m, x2d, g1, ng, sh2, sc2, wo, wr)


def _moe_kernel(be_ref, nv_ref, tok0_ref, tokn_ref, hm_ref, wgu_ref, bgu_ref, wd_ref, bd_ref, o_ref,
                xbuf_ref, acc_ref, sem, *, nj):
    i = pl.program_id(0)
    j = pl.program_id(1)
    nv = nv_ref[0]
    tm = acc_ref.shape[0]

    n_sub = hm_ref.shape[1]

    def row_copy(tok, r, slot):
        return pltpu.make_async_copy(hm_ref.at[tok], xbuf_ref.at[slot, pl.ds(r * MOE_PITCH, n_sub), :],
                                     sem.at[slot])

    def wait_block(slot):
        for r in range(tm):
            row_copy(0, r, slot).wait()

    @pl.when((i == 0) & (j == 0) & (nv > 0))
    def _():
        for r in range(tm):
            row_copy(tok0_ref[0, r], r, 0).start()

    def expert_step(first, last):
        slot = i % 2
        if first:
            wait_block(slot)
            for r in range(tm):
                row_copy(tokn_ref[0, r], r, (i + 1) % 2).start(priority=1)
        x = jnp.concatenate([xbuf_ref[slot, pl.ds(s, tm, stride=MOE_PITCH), :].astype(BF16)
                             for s in range(n_sub)], axis=1)
        gu = _dot(x, wgu_ref[...]) + bgu_ref[...]
        acts = []
        for b in range(gu.shape[1] // (2 * LANE)):
            glu = jnp.minimum(gu[:, 2 * b * LANE:(2 * b + 1) * LANE], SWIGLU_LIMIT)
            lin = jnp.clip(gu[:, (2 * b + 1) * LANE:(2 * b + 2) * LANE], -SWIGLU_LIMIT, SWIGLU_LIMIT)
            acts.append((glu * jax.nn.sigmoid(SWIGLU_ALPHA * glu) * (lin + 1.0)).astype(BF16))
        part = _dot(jnp.concatenate(acts, axis=1), wd_ref[...])
        if not first:
            part = part + acc_ref[...]
        if last:
            o_ref[...] = (part + bd_ref[...]).astype(o_ref.dtype)
        else:
            acc_ref[...] = part

    variants = {1: [(True, True)], 2: [(True, False), (False, True)]}.get(
        nj, [(True, False), (False, False), (False, True)])
    for first, last in variants:
        cond = (i < nv) & ((j == 0) if first else (j > 0)) & ((j == nj - 1) if last else (j < nj - 1))
        pl.when(cond)(functools.partial(expert_step, first, last))

    @pl.when((i == nv - 1) & (j == nj - 1))
    def _():
        wait_block((i + 1) % 2)

    @pl.when((i >= nv) & (j == nj - 1))
    def _():
        o_ref[...] = jnp.zeros_like(o_ref)


def _moe_blocks(hm, slot_tok, block_expert, n_valid, wgu, bgu, wd, bd):
    d = wd.shape[2]
    assert hm.shape[1] <= MOE_PITCH and hm.shape[1] * hm.shape[2] == d
    f = wd.shape[1]
    tm = MOE_TM
    tf = min(f, MOE_TF)
    nj = f // tf
    nblk = block_expert.shape[0]

    def jeff(i, j, nv):
        return jnp.where(i < nv[0], j, nj - 1)

    grid_spec = pltpu.PrefetchScalarGridSpec(
        num_scalar_prefetch=2,
        grid=(nblk, nj),
        in_specs=[pl.BlockSpec((None, 1, tm), lambda i, j, be, nv: (0, 0, 0), memory_space=pltpu.SMEM),
                  pl.BlockSpec((None, 1, tm), lambda i, j, be, nv: (jnp.minimum(i + 1, nblk - 1), 0, 0),
                               memory_space=pltpu.SMEM),
                  pl.BlockSpec(memory_space=pl.ANY),
                  pl.BlockSpec((None, d, 2 * tf), lambda i, j, be, nv: (be[i], 0, jeff(i, j, nv))),
                  pl.BlockSpec((None, 1, 2 * tf), lambda i, j, be, nv: (be[i], 0, jeff(i, j, nv))),
                  pl.BlockSpec((None, tf, d), lambda i, j, be, nv: (be[i], jeff(i, j, nv), 0)),
                  pl.BlockSpec((None, 1, d), lambda i, j, be, nv: (be[i], 0, 0))],
        out_specs=pl.BlockSpec((tm, d), lambda i, j, be, nv: (i, 0)),
        scratch_shapes=[pltpu.VMEM((2, tm * MOE_PITCH, LANE), F32), pltpu.VMEM((tm, d), F32),
                        pltpu.SemaphoreType.DMA((2,))])
    return pl.pallas_call(
        functools.partial(_moe_kernel, nj=nj),
        grid_spec=grid_spec,
        out_shape=jax.ShapeDtypeStruct((nblk * tm, d), BF16),
        compiler_params=_params(("arbitrary", "arbitrary")),
        name="moe_experts",
    )(block_expert, n_valid, slot_tok, slot_tok, hm, wgu, bgu, wd, bd)


def _route(logits, n_experts):
    n_tok = logits.shape[0]
    top_val, top_idx = lax.top_k(logits, TOP_K)
    gates = jax.nn.softmax(top_val, axis=-1)
    n_assign = n_tok * TOP_K
    flat_e = top_idx.reshape(-1).astype(jnp.int32)
    experts = jnp.arange(n_experts, dtype=jnp.int32)
    onehot = (flat_e[:, None] == experts[None, :]).astype(jnp.int32)
    csum = jnp.cumsum(onehot, axis=0)
    rank = jnp.sum(onehot * csum, axis=1) - 1
    counts = csum[-1]
    padded = (counts + MOE_TM - 1) // MOE_TM * MOE_TM
    start = jnp.cumsum(counts) - counts
    padded_end = jnp.cumsum(padded)
    padded_start = padded_end - padded
    dest = padded_start[flat_e] + rank
    n_blocks = -(-n_assign // MOE_TM) + n_experts
    cap = n_blocks * MOE_TM
    n_valid = (padded_end[-1] // MOE_TM).astype(jnp.int32)
    order = jnp.argsort(flat_e)
    slots = jnp.arange(cap, dtype=jnp.int32)
    slot_e = jnp.minimum(jnp.sum((padded_end[None, :] <= slots[:, None]).astype(jnp.int32), axis=1), n_experts - 1)
    slot_rank = slots - padded_start[slot_e]
    src = jnp.minimum(start[slot_e] + slot_rank, n_assign - 1)
    slot_tok = jnp.where(slot_rank < counts[slot_e], order[src] // TOP_K, 0).astype(jnp.int32)
    block_start = jnp.minimum(jnp.arange(n_blocks, dtype=jnp.int32), n_valid - 1) * MOE_TM
    block_expert = jnp.minimum(
        jnp.sum((padded_end[None, :] <= block_start[:, None]).astype(jnp.int32), axis=1), n_experts - 1)
    return gates, slot_tok, dest, block_expert, n_valid


def _final_kernel(h1_ref, *rest):
    yg_refs = rest[:TOP_K]
    gates_ref, g2_ref, fg_ref, o_ref = rest[TOP_K:]
    gates = gates_ref[...]
    moe = yg_refs[0][...].astype(F32) * gates[:, 0:1]
    for k in range(1, TOP_K):
        moe = moe + yg_refs[k][...].astype(F32) * gates[:, k:k + 1]
    h = h1_ref[...] + g2_ref[...] * moe
    y = h * lax.rsqrt(jnp.mean(h * h, axis=-1, keepdims=True) + NORM_EPS)
    o_ref[...] = (y * fg_ref[...]).astype(o_ref.dtype)


def _final(h1, yg, gates, g2, fg, rows_per_mod):
    n, d = h1.shape
    tm = min(n, 512)
    tiles_per_mod = rows_per_mod // tm
    nt = n // tm
    yg_specs = [pl.BlockSpec((tm, d), functools.partial(lambda i, k: (k * nt + i, 0), k=k)) for k in range(TOP_K)]
    return pl.pallas_call(
        _final_kernel,
        grid=(nt,),
        in_specs=[pl.BlockSpec((tm, d), lambda i: (i, 0))] + yg_specs + [
                  pl.BlockSpec((tm, TOP_K), lambda i: (i, 0)),
                  pl.BlockSpec((None, 1, d), lambda i: (i // tiles_per_mod, 0, 0)),
                  pl.BlockSpec((1, d), lambda i: (0, 0))],
        out_specs=pl.BlockSpec((tm, d), lambda i: (i, 0)),
        out_shape=jax.ShapeDtypeStruct((n, d), F32),
        compiler_params=_params(("parallel",)),
        name="combine_final_norm",
    )(h1, *([yg] * TOP_K), gates, g2, fg)


def kernel(x, c, ctx, c_ctx, w_ada, b_ada, norm1_g, norm2_g, w_in, s5_lam_re, s5_lam_im, s5_log_dt, s5_b_re, s5_b_im, s5_c_re, s5_c_im, s5_d, s5_w_glu, hy_short_w, hy_pos_w1, hy_pos_b1, hy_freq1, hy_pos_w2, hy_pos_b2, hy_freq2, hy_pos_w3, hy_pos_b3, hy_decay, hy_bias, w_branch_s5, w_branch_hy, w_out, router_w, router_b, w_gate_up, b_gate_up, w_down, b_down, final_g):
    bsz, seq, d = x.shape
    ctx_len = ctx.shape[1]
    depth = w_ada.shape[0]
    assert depth == 1, "only the single-layer configuration is implemented"
    n_experts = router_w.shape[2]
    s5_w = s5_d.shape[1]
    hy_w = hy_bias.shape[2]
    n_tok = bsz * seq

    cc = jnp.zeros((8, d), F32).at[:bsz].set(c).at[bsz].set(c_ctx)
    mod = _ada_mod(cc, w_ada[0], b_ada[0][None]).reshape(8, 6, 1, d)
    sh1, sc1, g1, sh2, sc2, g2 = [mod[:bsz, k] for k in range(6)]
    csh1, csc1 = mod[bsz:bsz + 1, 0], mod[bsz:bsz + 1, 1]

    w_in_b = w_in[0].astype(BF16)
    x2d = x.reshape(n_tok, d)
    riders = ((w_gate_up[0].reshape(-1, w_gate_up.shape[3]), True), (w_down[0].reshape(-1, d), False))
    ua, uh, ug, wgu_b, wd_b = _norm_mod_proj(x2d, norm1_g[0][None], sh1, sc1, w_in_b,
                                             (s5_w, 3 * hy_w, 2 * d), (False, False, True), seq, 1024, 512,
                                             riders=riders)
    (uca,) = _norm_mod_proj(ctx.reshape(bsz * ctx_len, d), norm1_g[0][None], csh1, csc1,
                            w_in_b[:, :s5_w], (s5_w,), (False,), bsz * ctx_len, 512, 1024)

    ya = _s5_branch(ua, uca, bsz, s5_lam_re[0], s5_lam_im[0], s5_log_dt[0], s5_b_re[0], s5_b_im[0],
                    s5_c_re[0], s5_c_im[0], s5_d[0], s5_w_glu[0])

    yh = _hyena_branch(uh, bsz, seq, hy_short_w[0], hy_pos_w1[0], hy_pos_b1[0], hy_freq1[0], hy_pos_w2[0],
                       hy_pos_b2[0], hy_freq2[0], hy_pos_w3[0], hy_pos_b3[0], hy_decay[0], hy_bias[0])

    m = _merge1(ya, yh, ug, w_branch_s5[0].astype(BF16), w_branch_hy[0].astype(BF16))
    rw = router_w[0]
    rw_hi = rw.astype(BF16)
    rw_lo = (rw - rw_hi.astype(F32)).astype(BF16)
    wr = jnp.zeros((d, 128), BF16).at[:, :n_experts].set(rw_hi).at[:, n_experts:2 * n_experts].set(rw_lo)
    h1, hm, r1, r2 = _merge2(m, x2d, g1, norm2_g[0][None], sh2, sc2, w_out[0].astype(BF16), wr, seq)
    logits = (r1[:, :n_experts] + r1[:, n_experts:2 * n_experts] + r2[:, :n_experts]) + router_b[0]

    gates, slot_tok, dest, block_expert, n_valid = _route(logits, n_experts)
    f = w_down.shape[2]
    bgu = b_gate_up[0].reshape(n_experts, f // LANE, LANE, 2).swapaxes(2, 3).reshape(n_experts, 1, 2 * f)
    yslots = _moe_blocks(hm.reshape(n_tok, d // LANE, LANE), slot_tok.reshape(-1, 1, MOE_TM), block_expert,
                         n_valid.reshape(1),
                         wgu_b.reshape(w_gate_up.shape[1:]), bgu, wd_b.reshape(w_down.shape[1:]),
                         b_down[0][:, None, :])
    yg = yslots[dest.reshape(n_tok, TOP_K).T.reshape(-1)]
    out = _final(h1, yg, gates, g2, final_g[None], seq)
    return out.reshape(bsz, seq, d)
```

```python
import functools
import math

import jax
import jax.numpy as jnp
import numpy as np
from jax import lax
from jax.experimental import pallas as pl
from jax.experimental.pallas import tpu as pltpu

F32 = jnp.float32
BF16 = jnp.bfloat16
HIGHEST = lax.Precision.HIGHEST

LANE = 128
GRID_W = 64
NORM_EPS = 1e-6
S5_GROUP = 16
S5_CHUNK = 32
HY_ORDER = 2
HY_NORM_EPS = 1e-6
TOP_K = 4
SWIGLU_LIMIT = 7.0
SWIGLU_ALPHA = 1.702
MOE_TM = 512
MOE_TF = 1024
MOE_PITCH = 20
VMEM_LIMIT = 56 * 1024 * 1024


def _params(sem, vmem=VMEM_LIMIT):
    return pltpu.CompilerParams(dimension_semantics=sem, vmem_limit_bytes=vmem)


def _dot(a, b):
    return jnp.dot(a, b, preferred_element_type=F32)


def _ada_kernel(c_ref, w_ref, b_ref, o_ref):
    c = c_ref[...]
    a = c * jax.nn.sigmoid(c)
    o_ref[...] = jnp.dot(a, w_ref[...], preferred_element_type=F32, precision=HIGHEST) + b_ref[...]


def _ada_mod(cc, w, b):
    d, n6 = w.shape
    tn = min(512, n6)
    return pl.pallas_call(
        _ada_kernel,
        grid=(n6 // tn,),
        in_specs=[pl.BlockSpec((8, d), lambda j: (0, 0)),
                  pl.BlockSpec((d, tn), lambda j: (0, j)),
                  pl.BlockSpec((1, tn), lambda j: (0, j))],
        out_specs=pl.BlockSpec((8, tn), lambda j: (0, j)),
        out_shape=jax.ShapeDtypeStruct((8, n6), F32),
        compiler_params=_params(("parallel",)),
        name="ada_mod",
    )(cc, w, b)


def _inproj_kernel(x_ref, g_ref, sh_ref, sc_ref, w_ref, *rest, bounds, acts, riders):
    nr = len(riders)
    n_in = nr + (1 if any(riders) else 0)
    rider_in, rest = rest[:n_in], rest[n_in:]
    outs, rider_out, xn_ref = rest[:len(bounds)], rest[len(bounds):len(bounds) + nr], rest[-1]
    j = pl.program_id(1)

    @pl.when(j == 0)
    def _():
        x = x_ref[...].astype(F32)
        y = x * lax.rsqrt(jnp.mean(x * x, axis=-1, keepdims=True) + NORM_EPS)
        y = y * g_ref[...]
        y = y * (1.0 + sc_ref[...]) + sh_ref[...]
        xn_ref[...] = y.astype(BF16)

    for (j0, j1), act, o_ref in zip(bounds, acts, outs):
        @pl.when((j >= j0) & (j < j1))
        def _(o_ref=o_ref, act=act):
            r = _dot(xn_ref[...], w_ref[...])
            if act:
                r = jax.nn.sigmoid(r)
            o_ref[...] = r.astype(o_ref.dtype)

    for r_in, r_out, deint in zip(rider_in, rider_out, riders):
        if deint:
            for b in range(r_in.shape[1] // (2 * LANE)):
                sl = slice(2 * b * LANE, (2 * b + 2) * LANE)
                r_out[:, sl] = _dot(r_in[:, sl].astype(BF16), rider_in[-1][...]).astype(r_out.dtype)
        else:
            r_out[...] = r_in[...].astype(r_out.dtype)


def _deinterleave_perm():
    src = jnp.arange(2 * LANE)
    dst = (src % 2) * LANE + src // 2
    return jnp.zeros((2 * LANE, 2 * LANE), BF16).at[src, dst].set(1)


def _norm_mod_proj(x2d, g, sh, sc, w, widths, acts, rows_per_mod, tm, tn, riders=()):
    n, d = x2d.shape
    ncols = w.shape[1]
    tm = min(tm, n)
    tn = min(tn, min(widths))
    bounds, off = [], 0
    for wd in widths:
        bounds.append((off // tn, (off + wd) // tn))
        off += wd
    tiles_per_mod = rows_per_mod // tm
    nj = ncols // tn
    nsteps = (n // tm) * nj

    def out_map(i, j, j0, nj):
        return (i, jnp.clip(j - j0, 0, nj - 1))

    out_specs = [pl.BlockSpec((tm, tn), functools.partial(out_map, j0=j0, nj=j1 - j0))
                 for (j0, j1) in bounds]
    out_shape = [jax.ShapeDtypeStruct((n, wd), BF16) for wd in widths]
    rider_args, rider_specs = [], []
    for arr, _ in riders:
        spec = pl.BlockSpec((arr.shape[0] // nsteps, arr.shape[1]), lambda i, j: (i * nj + j, 0))
        rider_args.append(arr)
        rider_specs.append(spec)
        out_specs.append(spec)
        out_shape.append(jax.ShapeDtypeStruct(arr.shape, BF16))
    if any(flag for _, flag in riders):
        rider_args.append(_deinterleave_perm())
        rider_specs.append(pl.BlockSpec((2 * LANE, 2 * LANE), lambda i, j: (0, 0)))
    return pl.pallas_call(
        functools.partial(_inproj_kernel, bounds=tuple(bounds), acts=tuple(acts),
                          riders=tuple(flag for _, flag in riders)),
        grid=(n // tm, nj),
        in_specs=[pl.BlockSpec((tm, d), lambda i, j: (i, 0)),
                  pl.BlockSpec((1, d), lambda i, j: (0, 0)),
                  pl.BlockSpec((None, 1, d), lambda i, j: (i // tiles_per_mod, 0, 0)),
                  pl.BlockSpec((None, 1, d), lambda i, j: (i // tiles_per_mod, 0, 0)),
                  pl.BlockSpec((d, tn), lambda i, j: (0, j))] + rider_specs,
        out_specs=out_specs,
        out_shape=out_shape,
        scratch_shapes=[pltpu.VMEM((tm, d), BF16)],
        compiler_params=_params(("parallel", "arbitrary")),
        name="norm_mod_proj",
    )(x2d, g, sh, sc, w, *rider_args)


def _s5_matrices(lam_re, lam_im, log_dt, b_re, b_im, c_re, c_im, d_skip, t):
    lam = lax.complex(lam_re.astype(F32), lam_im.astype(F32))
    dt = jnp.exp(log_dt.astype(F32))[..., None]
    lam_dt = lam * dt
    lam_bar = jnp.exp(lam_dt)
    b_bar = ((lam_bar - 1) / lam)[..., None] * lax.complex(b_re.astype(F32), b_im.astype(F32))
    c_out = lax.complex(c_re.astype(F32), c_im.astype(F32))
    g, p = lam.shape[1], lam.shape[2]
    gs = b_bar.shape[-1]
    k = jnp.arange(t + 1, dtype=F32)
    pw = jnp.exp(lam_dt[..., None] * k)
    kk = jnp.einsum('dgcp,dgpk,dgpe->dgkce', c_out, pw[..., :t], b_bar, precision=HIGHEST).real
    kf, kb = kk[0], kk[1]
    zero_lag = kf[:, :1] + kb[:, :1]
    k_all = jnp.concatenate([kb[:, :0:-1], zero_lag, kf[:, 1:]], axis=1)
    jj = jnp.arange(t)
    lag_idx = jj[None, :] - jj[:, None] + (t - 1)
    m_intra = k_all[:, lag_idx]
    m_intra = m_intra.transpose(0, 1, 4, 2, 3).reshape(g, t * gs, t * gs)
    m_intra = m_intra + jnp.eye(t * gs, dtype=F32)[None] * jnp.tile(
        d_skip.astype(F32).reshape(g, 1, gs), (1, t, 1)).reshape(g, 1, t * gs)
    in_f = pw[0][..., t - 1 - jj][..., None] * b_bar[0][:, :, None, :]
    in_b = pw[1][..., jj][..., None] * b_bar[1][:, :, None, :]

    def to_in(z):
        return z.transpose(0, 2, 3, 1).reshape(g, t * gs, p)

    m_in = jnp.concatenate([to_in(in_f.real), to_in(in_f.imag), to_in(in_b.real), to_in(in_b.imag)], axis=-1)
    out_f = c_out[0][:, :, :, None] * pw[0][:, None, :, 1 + jj]
    out_b = c_out[1][:, :, :, None] * pw[1][:, None, :, t - jj]

    def to_out(z):
        return z.transpose(0, 2, 3, 1).reshape(g, p, t * gs)

    m_out = jnp.concatenate([to_out(out_f.real), -to_out(out_f.imag), to_out(out_b.real), -to_out(out_b.imag)],
                            axis=1)
    mu = pw[..., t]
    mre, mim = mu.real, mu.imag
    coef = jnp.stack([
        jnp.concatenate([mre[0], mre[0], mre[1], mre[1]], axis=-1),
        jnp.concatenate([-mim[0], mim[0], -mim[1], mim[1]], axis=-1),
        jnp.concatenate([mim[0], -mim[0], mim[1], -mim[1]], axis=-1)])
    return m_intra.astype(BF16), m_in.astype(BF16), m_out.astype(BF16), coef


def _s5_in_kernel(a_ref, m_ref, o_ref):
    o_ref[...] = _dot(a_ref[...], m_ref[...])


def _s5_chunk_states(a, m_in):
    g, nc, kdim = a.shape
    s = m_in.shape[2]
    nb = min(nc, 1024)
    return pl.pallas_call(
        _s5_in_kernel,
        grid=(g, nc // nb),
        in_specs=[pl.BlockSpec((None, nb, kdim), lambda gi, i: (gi, i, 0)),
                  pl.BlockSpec((None, kdim, s), lambda gi, i: (gi, 0, 0))],
        out_specs=pl.BlockSpec((None, nb, s), lambda gi, i: (gi, i, 0)),
        out_shape=jax.ShapeDtypeStruct((g, nc, s), F32),
        compiler_params=_params(("parallel", "arbitrary")),
        name="s5_chunk_states",
    )(a, m_in)


def _s5_scan_kernel(sf_ref, sb_ref, init_ref, coef_ref, ef_ref, eb_ref, fin_ref, st_ref, *, nblk, half):
    k = pl.program_id(1)
    nk = pl.num_programs(1)
    lanes = 2 * half

    @pl.when(k == 0)
    def _():
        s0f = init_ref[:, 0:lanes]
        s0b = init_ref[:, lanes:2 * lanes]
        st_ref[0] = s0f
        st_ref[1] = pltpu.roll(s0f, half, 1)
        st_ref[2] = s0b
        st_ref[3] = pltpu.roll(s0b, half, 1)

    def body(r, carry):
        sf, sfw, sb, sbw = carry
        rb = nblk - 1 - r
        xf = sf_ref[r]
        xb = sb_ref[rb]
        ef_ref[r] = sf
        eb_ref[rb] = sb
        xfw = pltpu.roll(xf, half, 1)
        xbw = pltpu.roll(xb, half, 1)
        af, bf, bfw = coef_ref[0, :, 0:lanes], coef_ref[1, :, 0:lanes], coef_ref[2, :, 0:lanes]
        ab, bb, bbw = (coef_ref[0, :, lanes:2 * lanes], coef_ref[1, :, lanes:2 * lanes],
                       coef_ref[2, :, lanes:2 * lanes])
        return (sf * af + sfw * bf + xf, sfw * af + sf * bfw + xfw,
                sb * ab + sbw * bb + xb, sbw * ab + sb * bbw + xbw)

    sf, sfw, sb, sbw = lax.fori_loop(0, nblk, body, (st_ref[0], st_ref[1], st_ref[2], st_ref[3]))
    st_ref[0] = sf
    st_ref[1] = sfw
    st_ref[2] = sb
    st_ref[3] = sbw

    @pl.when(k == nk - 1)
    def _():
        fin_ref[:, 0:lanes] = sf
        fin_ref[:, lanes:2 * lanes] = sb


def _s5_scan(s_t, init, coef):
    bsz, nc, g, s4 = s_t.shape
    lanes = s4 // 2
    nblk = min(nc, 128)
    nk = nc // nblk
    return pl.pallas_call(
        functools.partial(_s5_scan_kernel, nblk=nblk, half=lanes // 2),
        grid=(bsz, nk),
        in_specs=[pl.BlockSpec((None, nblk, g, lanes), lambda b, k: (b, k, 0, 0)),
                  pl.BlockSpec((None, nblk, g, lanes), lambda b, k: (b, nk - 1 - k, 0, 1)),
                  pl.BlockSpec((None, g, s4), lambda b, k: (b, 0, 0)),
                  pl.BlockSpec((3, g, s4), lambda b, k: (0, 0, 0))],
        out_specs=[pl.BlockSpec((None, nblk, g, lanes), lambda b, k: (b, k, 0, 0)),
                   pl.BlockSpec((None, nblk, g, lanes), lambda b, k: (b, nk - 1 - k, 0, 0)),
                   pl.BlockSpec((None, g, s4), lambda b, k: (b, 0, 0))],
        out_shape=[jax.ShapeDtypeStruct((bsz, nc, g, lanes), F32),
                   jax.ShapeDtypeStruct((bsz, nc, g, lanes), F32),
                   jax.ShapeDtypeStruct((bsz, g, s4), F32)],
        scratch_shapes=[pltpu.VMEM((4, g, lanes), F32)],
        compiler_params=_params(("parallel", "arbitrary")),
        name="s5_scan",
    )(s_t, s_t, init, coef)


def _s5_out_kernel(a_ref, e_ref, mi_ref, mo_ref, o_ref):
    y = _dot(a_ref[...], mi_ref[...]) + _dot(e_ref[...], mo_ref[...])
    o_ref[...] = y.astype(o_ref.dtype)


def _s5_chunk_outputs(a, e_in, m_intra, m_out):
    g, nc, kdim = a.shape
    s = e_in.shape[2]
    nb = min(nc, 1024)
    return pl.pallas_call(
        _s5_out_kernel,
        grid=(g, nc // nb),
        in_specs=[pl.BlockSpec((None, nb, kdim), lambda gi, i: (gi, i, 0)),
                  pl.BlockSpec((None, nb, s), lambda gi, i: (gi, i, 0)),
                  pl.BlockSpec((None, kdim, kdim), lambda gi, i: (gi, 0, 0)),
                  pl.BlockSpec((None, s, kdim), lambda gi, i: (gi, 0, 0))],
        out_specs=pl.BlockSpec((None, nb, kdim), lambda gi, i: (gi, i, 0)),
        out_shape=jax.ShapeDtypeStruct((g, nc, kdim), BF16),
        compiler_params=_params(("parallel", "arbitrary")),
        name="s5_chunk_outputs",
    )(a, e_in, m_intra, m_out)


def _s5_readout_kernel(y_ref, wg_ref, o_ref):
    y = jax.nn.gelu(y_ref[...].astype(F32))
    gate = _dot(y.astype(BF16), wg_ref[...])
    o_ref[...] = (y * jax.nn.sigmoid(gate)).astype(o_ref.dtype)


def _s5_readout(y2d, wg):
    n, w = y2d.shape
    tm = min(n, 1024)
    return pl.pallas_call(
        _s5_readout_kernel,
        grid=(n // tm,),
        in_specs=[pl.BlockSpec((tm, w), lambda i: (i, 0)),
                  pl.BlockSpec((w, w), lambda i: (0, 0))],
        out_specs=pl.BlockSpec((tm, w), lambda i: (i, 0)),
        out_shape=jax.ShapeDtypeStruct((n, w), BF16),
        compiler_params=_params(("parallel",)),
        name="s5_readout",
    )(y2d, wg)


def _to_chunks(u2d, g, t):
    n = u2d.shape[0]
    return u2d.reshape(n // t, t, g, S5_GROUP).transpose(2, 0, 1, 3).reshape(g, n // t, t * S5_GROUP)


def _from_chunks(y, g, t):
    nc = y.shape[1]
    return y.reshape(g, nc, t, S5_GROUP).transpose(1, 2, 0, 3).reshape(nc * t, g * S5_GROUP)


def _s5_branch(ua, uca, bsz, lam_re, lam_im, log_dt, b_re, b_im, c_re, c_im, d_skip, w_glu):
    g = lam_re.shape[1]
    t = S5_CHUNK
    m_intra, m_in, m_out, coef = _s5_matrices(lam_re, lam_im, log_dt, b_re, b_im, c_re, c_im, d_skip, t)
    s4 = m_in.shape[2]

    def states(u2d):
        a = _to_chunks(u2d, g, t)
        s = _s5_chunk_states(a, m_in)
        nc = s.shape[1] // bsz
        return a, s.reshape(g, bsz, nc, s4).transpose(1, 2, 0, 3)

    _, s_ctx = states(uca)
    _, _, seed = _s5_scan(s_ctx, jnp.zeros((bsz, g, s4), F32), coef)
    a_lat, s_lat = states(ua)
    e_f, e_b, _ = _s5_scan(s_lat, seed, coef)
    e_in = jnp.concatenate([e_f, e_b], axis=-1).astype(BF16)
    e_in = e_in.transpose(2, 0, 1, 3).reshape(g, -1, s4)
    y = _s5_chunk_outputs(a_lat, e_in, m_intra, m_out)
    return _s5_readout(_from_chunks(y, g, t), w_glu.astype(BF16))


HY_KH = GRID_W + 1
HY_KP = 72
HY_QN = 16


def _hy_phase(w_cols, r_cnt):
    n1_tot = 2 * GRID_W
    n_fft = n1_tot * r_cnt
    kh = GRID_W + 1
    k1 = np.arange(kh, dtype=np.int64)[None, :, None]
    r = np.arange(r_cnt, dtype=np.int64)[:, None, None]
    n1 = np.arange(w_cols, dtype=np.int64)[None, None, :]
    return 2.0 * np.pi * ((k1 * (r_cnt * n1 + r)) % n_fft) / n_fft


def _hy_fwd_table(r_cnt, m):
    th = _hy_phase(GRID_W * m, r_cnt)
    kh = th.shape[1]
    f = np.zeros((r_cnt, 2 * HY_KP, GRID_W * m), np.float32)
    f[:, :kh] = np.cos(th)
    f[:, HY_KP:HY_KP + kh] = -np.sin(th)
    return jnp.asarray(f, BF16)


def _hy_inv_table(r_cnt):
    th = _hy_phase(GRID_W, r_cnt)
    kh = th.shape[1]
    c = np.full((kh,), 2.0)
    c[0] = c[-1] = 1.0
    scale = c[None, :, None] / (2 * GRID_W * r_cnt)
    g = np.zeros((r_cnt, GRID_W, 2 * HY_KP), np.float32)
    g[:, :, :kh] = (scale * np.cos(th)).transpose(0, 2, 1)
    g[:, :, HY_KP:HY_KP + kh] = (-scale * np.sin(th)).transpose(0, 2, 1)
    return jnp.asarray(g, BF16)


def _hy_dft_tables(r_cnt):
    k = np.arange(r_cnt, dtype=np.int64)
    th = 2.0 * np.pi * ((k[:, None] * k[None, :]) % r_cnt) / r_cnt
    fc, fs = np.cos(th), np.sin(th)
    fwd = np.block([[fc, fs], [-fs, fc]]).astype(np.float32)
    inv = np.block([[fc, -fs], [fs, fc]]).astype(np.float32)
    return jnp.asarray(fwd, BF16), jnp.asarray(inv, BF16)


def _hy_short_kernel(x_ref, w_ref, o_ref, *, rows, chunk):
    wk = w_ref[...]
    w0, w1, w2 = wk[0:1], wk[1:2], wk[2:3]
    gw = GRID_W
    col = lax.broadcasted_iota(jnp.int32, (gw, x_ref.shape[1]), 0)

    def piece(a, n):
        return x_ref[pl.ds(a, n), :].astype(F32)

    last = piece(rows - gw, gw)
    prev0 = jnp.where(col == 0, 0.0, pltpu.roll(last, 1, 0))
    o_ref[0:gw, :] = (w0 * prev0 + w1 * piece(0, gw) + w2 * piece(gw, gw)).astype(o_ref.dtype)
    first = piece(0, gw)
    next_l = jnp.where(col == gw - 1, 0.0, pltpu.roll(first, gw - 1, 0))
    o_ref[rows - gw:rows, :] = (w0 * piece(rows - 2 * gw, gw) + w1 * last + w2 * next_l).astype(o_ref.dtype)

    def body(i, carry):
        a = pl.multiple_of(gw + i * chunk, gw)
        o_ref[pl.ds(a, chunk), :] = (w0 * piece(a - gw, chunk) + w1 * piece(a, chunk)
                                     + w2 * piece(a + gw, chunk)).astype(o_ref.dtype)
        return carry

    n_full = (rows - 2 * gw) // chunk
    lax.fori_loop(0, n_full, body, 0)
    rem = rows - 2 * gw - n_full * chunk
    if rem:
        a = gw + n_full * chunk
        o_ref[a:a + rem, :] = (w0 * piece(a - gw, rem) + w1 * piece(a, rem)
                               + w2 * piece(a + gw, rem)).astype(o_ref.dtype)


def _hy_short_conv(uh, short_w, bsz, seq):
    n, ch = uh.shape
    ct = min(ch, LANE)
    chunk = min(512, seq - 2 * GRID_W)
    return pl.pallas_call(
        functools.partial(_hy_short_kernel, rows=seq, chunk=chunk),
        grid=(bsz, ch // ct),
        in_specs=[pl.BlockSpec((seq, ct), lambda b, c: (b, c)),
                  pl.BlockSpec((3, ct), lambda b, c: (0, c))],
        out_specs=pl.BlockSpec((seq, ct), lambda b, c: (b, c)),
        out_shape=jax.ShapeDtypeStruct((n, ch), BF16),
        compiler_params=_params(("parallel", "parallel")),
        name="hy_short_conv",
    )(uh, short_w)


def _hy_fwd1_kernel(*refs, m):
    x_refs, f_ref, o_ref, s_ref = refs[:m], refs[m], refs[m + 1], refs[m + 2]
    gw = GRID_W
    for q in range(HY_QN):
        xq = [x[q * gw:(q + 1) * gw, :] for x in x_refs]
        xq = xq[0] if m == 1 else jnp.concatenate(xq, axis=0)
        s_ref[q] = _dot(f_ref[q], xq)
    t = jnp.swapaxes(s_ref[...], 0, 1).astype(o_ref.dtype)
    o_ref[0] = t[0:HY_KH]
    o_ref[1] = t[HY_KP:HY_KP + HY_KH]


def _hy_fwd1(xs, col_blk0, width, bsz, r_cnt, ct):
    m = len(xs)
    table = _hy_fwd_table(r_cnt, m)
    rb = HY_QN * GRID_W
    nrb = r_cnt // HY_QN
    x_spec = pl.BlockSpec((rb, ct), lambda b, i, c: (b * nrb + i, col_blk0 + c))
    return pl.pallas_call(
        functools.partial(_hy_fwd1_kernel, m=m),
        grid=(bsz, nrb, width // ct),
        in_specs=[x_spec] * m + [pl.BlockSpec((HY_QN, 2 * HY_KP, GRID_W * m), lambda b, i, c: (i, 0, 0))],
        out_specs=pl.BlockSpec((None, 2, HY_KH, HY_QN, ct), lambda b, i, c: (b, 0, 0, i, c)),
        out_shape=jax.ShapeDtypeStruct((bsz, 2, HY_KH, r_cnt, width), BF16),
        scratch_shapes=[pltpu.VMEM((HY_QN, 2 * HY_KP, ct), F32)],
        compiler_params=_params(("parallel", "parallel", "arbitrary")),
        name="hy_fwd_stage1",
    )(*xs, table)


def _hy_spec_kernel(a_ref, f2_ref, o_ref):
    r = a_ref.shape[1]
    x = a_ref[...].reshape(2 * r, a_ref.shape[2])
    o_ref[...] = _dot(f2_ref[...], x).reshape(o_ref.shape)


def _hy_filter_spectrum(a, f2, ct):
    _, _, kh, r_cnt, width = a.shape
    return pl.pallas_call(
        _hy_spec_kernel,
        grid=(kh, width // ct),
        in_specs=[pl.BlockSpec((None, 2, None, r_cnt, ct), lambda k, c: (0, 0, k, 0, c)),
                  pl.BlockSpec((2 * r_cnt, 2 * r_cnt), lambda k, c: (0, 0))],
        out_specs=pl.BlockSpec((2, None, r_cnt, ct), lambda k, c: (0, k, 0, c)),
        out_shape=jax.ShapeDtypeStruct((2, kh, r_cnt, width), F32),
        compiler_params=_params(("parallel", "parallel")),
        name="hy_filter_spectrum",
    )(a, f2)


def _hy_mid_kernel(a_ref, kf_ref, invn_ref, f2_ref, f2i_ref, o_ref):
    r = a_ref.shape[1]
    x = a_ref[...].reshape(2 * r, a_ref.shape[2])
    b = _dot(f2_ref[...], x)
    br, bi = b[:r], b[r:]
    kr, ki = kf_ref[0], kf_ref[1]
    s = invn_ref[...]
    cr = (br * kr - bi * ki) * s
    ci = (br * ki + bi * kr) * s
    c = jnp.concatenate([cr, ci], axis=0).astype(BF16)
    o_ref[...] = _dot(f2i_ref[...], c).reshape(o_ref.shape).astype(o_ref.dtype)


def _hy_mid(a, kf, invn, order, f2, f2i, ct):
    bsz, _, kh, r_cnt, width = a.shape
    nct = width // ct
    return pl.pallas_call(
        _hy_mid_kernel,
        grid=(kh, nct, bsz),
        in_specs=[pl.BlockSpec((None, 2, None, r_cnt, ct), lambda k, c, b: (b, 0, k, 0, c)),
                  pl.BlockSpec((2, None, r_cnt, ct), lambda k, c, b: (0, k, 0, order * nct + c)),
                  pl.BlockSpec((1, ct), lambda k, c, b: (0, order * nct + c)),
                  pl.BlockSpec((2 * r_cnt, 2 * r_cnt), lambda k, c, b: (0, 0)),
                  pl.BlockSpec((2 * r_cnt, 2 * r_cnt), lambda k, c, b: (0, 0))],
        out_specs=pl.BlockSpec((None, 2, None, r_cnt, ct), lambda k, c, b: (b, 0, k, 0, c)),
        out_shape=jax.ShapeDtypeStruct(a.shape, BF16),
        compiler_params=_params(("parallel", "parallel", "arbitrary")),
        name="hy_spectrum_product",
    )(a, kf, invn, f2, f2i)


def _hy_inv1_kernel(d_ref, g_ref, xin_ref, xm_ref, bias_ref, o_ref, t_ref, s_ref):
    gw = GRID_W
    pad = jnp.zeros((HY_KP - HY_KH,) + t_ref.shape[1:], F32)
    t_ref[0:HY_KH] = d_ref[0].astype(F32)
    t_ref[HY_KH:HY_KP] = pad
    t_ref[HY_KP:HY_KP + HY_KH] = d_ref[1].astype(F32)
    t_ref[HY_KP + HY_KH:2 * HY_KP] = pad
    s_ref[...] = jnp.swapaxes(t_ref[...], 0, 1)
    for q in range(HY_QN):
        y = _dot(g_ref[q], s_ref[q].astype(BF16))
        rows = slice(q * gw, (q + 1) * gw)
        xin = xin_ref[rows, :].astype(F32)
        o_ref[rows, :] = (xm_ref[rows, :].astype(F32) * (y + xin * bias_ref[...])).astype(o_ref.dtype)


def _hy_inv1(d, xin, xin_blk0, xm, xm_blk0, bias, ct):
    bsz, _, kh, r_cnt, width = d.shape
    table = _hy_inv_table(r_cnt)
    rb = HY_QN * GRID_W
    nrb = r_cnt // HY_QN
    return pl.pallas_call(
        _hy_inv1_kernel,
        grid=(bsz, nrb, width // ct),
        in_specs=[pl.BlockSpec((None, 2, kh, HY_QN, ct), lambda b, i, c: (b, 0, 0, i, c)),
                  pl.BlockSpec((HY_QN, GRID_W, 2 * HY_KP), lambda b, i, c: (i, 0, 0)),
                  pl.BlockSpec((rb, ct), lambda b, i, c: (b * nrb + i, xin_blk0 + c)),
                  pl.BlockSpec((rb, ct), lambda b, i, c: (b * nrb + i, xm_blk0 + c)),
                  pl.BlockSpec((1, ct), lambda b, i, c: (0, c))],
        out_specs=pl.BlockSpec((rb, ct), lambda b, i, c: (b * nrb + i, c)),
        out_shape=jax.ShapeDtypeStruct((bsz * r_cnt * GRID_W, width), BF16),
        scratch_shapes=[pltpu.VMEM((2 * HY_KP, HY_QN, ct), F32), pltpu.VMEM((HY_QN, 2 * HY_KP, ct), F32)],
        compiler_params=_params(("parallel", "parallel", "arbitrary")),
        name="hy_inv_stage1",
    )(d, table, xin, xm, bias)


def _hy_filter_kernel(w1_ref, b1_ref, f1_ref, w2_ref, b2_ref, f2_ref, w3_ref, b3_ref, dec_ref, om_ref, ph_ref,
                      o_ref, sum_ref, *, direction, seqlen, r_cnt, rows):
    i = pl.program_id(1)
    row = lax.broadcasted_iota(jnp.int32, (rows, LANE), 0) + i * rows
    lane = lax.broadcasted_iota(jnp.int32, (rows, LANE), 1)
    r = jnp.right_shift(row, GRID_W.bit_length() - 1)
    w = jnp.bitwise_and(row, GRID_W - 1)
    n = (w + GRID_W * direction) * r_cnt + r
    p = n if direction == 0 else 2 * seqlen - n
    pf = p.astype(F32)
    t = pf / (seqlen - 1)
    feats = jnp.where(lane == 0, t, jnp.cos(pf * om_ref[...] + ph_ref[...]))
    hid = jnp.sin(f1_ref[...] * (jnp.dot(feats, w1_ref[...], preferred_element_type=F32, precision=HIGHEST)
                                 + b1_ref[...]))
    hid = jnp.sin(f2_ref[...] * (jnp.dot(hid, w2_ref[...], preferred_element_type=F32, precision=HIGHEST)
                                 + b2_ref[...]))
    filt = _dot(hid.astype(BF16), w3_ref[...].astype(BF16)) + b3_ref[...]
    reps = filt.shape[1] // LANE
    t_w = jnp.concatenate([t] * reps, axis=1)
    keep = jnp.concatenate([jnp.where(p < seqlen, 1.0, 0.0)] * reps, axis=1)
    filt = filt * jnp.exp(-t_w * jnp.abs(dec_ref[...])) * keep
    o_ref[...] = filt.astype(o_ref.dtype)

    @pl.when(i == 0)
    def _():
        sum_ref[...] = jnp.zeros_like(sum_ref)

    sum_ref[...] += jnp.sum(jnp.abs(filt), axis=0, keepdims=True)


def _hy_filter_signal(direction, seqlen, w1p, b1, f1, w2, b2, f2, w3, b3, decay, omega, phase, width):
    r_cnt = seqlen // GRID_W
    rows = min(256, seqlen)
    ct = width
    nct = width // ct
    hid = w2.shape[0]
    off = direction * nct
    const = lambda shape: pl.BlockSpec(shape, lambda c, i: (0, 0))
    colv = pl.BlockSpec((1, ct), lambda c, i: (0, off + c))
    return pl.pallas_call(
        functools.partial(_hy_filter_kernel, direction=direction, seqlen=seqlen, r_cnt=r_cnt, rows=rows),
        grid=(nct, seqlen // rows),
        in_specs=[const((LANE, hid)), const((1, hid)), const((1, hid)), const((hid, hid)), const((1, hid)),
                  const((1, hid)), pl.BlockSpec((hid, ct), lambda c, i: (0, off + c)), colv, colv,
                  const((1, LANE)), const((1, LANE))],
        out_specs=[pl.BlockSpec((rows, ct), lambda c, i: (i, c)),
                   pl.BlockSpec((1, ct), lambda c, i: (0, c))],
        out_shape=[jax.ShapeDtypeStruct((seqlen, width), BF16), jax.ShapeDtypeStruct((1, width), F32)],
        compiler_params=_params(("parallel", "arbitrary")),
        name="hy_filter_signal",
    )(w1p, b1, f1, w2, b2, f2, w3, b3, decay, omega, phase)


def _hyena_branch(uh, bsz, seq, short_w, w1, b1, f1, w2, b2, f2, w3, b3, decay, hy_bias):
    width = hy_bias.shape[1]
    r_cnt = seq // GRID_W
    ct = min(width, 512)
    nct = width // ct
    n_emb = w1.shape[0]
    n_bands = (n_emb - 1) // 2
    hid = w1.shape[1]
    bands = jnp.linspace(1e-4, n_bands - 1, n_bands, dtype=F32)
    omega = jnp.zeros((1, LANE), F32).at[0, 1:1 + n_bands].set((2 * jnp.pi / seq) * bands)
    omega = omega.at[0, 1 + n_bands:1 + 2 * n_bands].set((2 * jnp.pi / seq) * bands)
    phase = jnp.zeros((1, LANE), F32).at[0, 1 + n_bands:1 + 2 * n_bands].set(0.5 * jnp.pi)
    w1p = jnp.zeros((LANE, hid), F32).at[:n_emb].set(w1)
    sigs, sums = [], []
    for direction in range(2):
        sig, ssum = _hy_filter_signal(direction, seq, w1p, b1[None], f1[None], w2, b2[None], f2[None], w3,
                                      b3[None], decay[None], omega, phase, HY_ORDER * width)
        sigs.append(sig)
        sums.append(ssum)
    invn = 1.0 / (sums[0] + sums[1] + HY_NORM_EPS)
    f2m, f2i = _hy_dft_tables(r_cnt)
    ct3 = min(width, 1024)
    kf = _hy_filter_spectrum(_hy_fwd1(sigs, 0, HY_ORDER * width, 1, r_cnt, ct), f2m, ct3)

    z = _hy_short_conv(uh, short_w.astype(F32), bsz, seq)
    a = _hy_fwd1([z], 0, width, bsz, r_cnt, ct)
    d = _hy_mid(a, kf, invn, 0, f2m, f2i, ct3)
    y1 = _hy_inv1(d, z, 0, z, nct, hy_bias[0:1].astype(F32), ct)
    a = _hy_fwd1([y1], 0, width, bsz, r_cnt, ct)
    d = _hy_mid(a, kf, invn, 1, f2m, f2i, ct3)
    return _hy_inv1(d, y1, 0, z, 2 * nct, hy_bias[1:2].astype(F32), ct)


def _merge1_kernel(ya_ref, yh_ref, ug_ref, wa_ref, wh_ref, o_ref):
    d = o_ref.shape[1]
    a = _dot(ya_ref[...], wa_ref[...])
    h = _dot(yh_ref[...], wh_ref[...])
    ga = ug_ref[:, :d].astype(F32)
    gh = ug_ref[:, d:].astype(F32)
    o_ref[...] = (ga * a + gh * h).astype(o_ref.dtype)


def _merge1(ya, yh, ug, wa, wh):
    n, ws = ya.shape
    wh_in = yh.shape[1]
    d = wa.shape[1]
    tm = min(n, 512)
    return pl.pallas_call(
        _merge1_kernel,
        grid=(n // tm,),
        in_specs=[pl.BlockSpec((tm, ws), lambda i: (i, 0)),
                  pl.BlockSpec((tm, wh_in), lambda i: (i, 0)),
                  pl.BlockSpec((tm, 2 * d), lambda i: (i, 0)),
                  pl.BlockSpec((ws, d), lambda i: (0, 0)),
                  pl.BlockSpec((wh_in, d), lambda i: (0, 0))],
        out_specs=pl.BlockSpec((tm, d), lambda i: (i, 0)),
        out_shape=jax.ShapeDtypeStruct((n, d), BF16),
        compiler_params=_params(("parallel",)),
        name="merge_branches",
    )(ya, yh, ug, wa, wh)


def _merge_fused_kernel(ya_ref, yh_ref, ug_ref, wa_ref, wh_ref, *rest):
    d = wa_ref.shape[1]
    a = _dot(ya_ref[...], wa_ref[...])
    h = _dot(yh_ref[...], wh_ref[...])
    m = (ug_ref[:, :d].astype(F32) * a + ug_ref[:, d:].astype(F32) * h).astype(BF16)
    _merge2_kernel(m, *rest)


def _merge2_kernel(m_ref, x_ref, g1_ref, ng_ref, sh_ref, sc_ref, wo_ref, wr_ref,
                   h1_ref, hm_ref, r1_ref, r2_ref):
    mix = _dot(m_ref[...], wo_ref[...])
    h1 = x_ref[...] + g1_ref[...] * mix
    h1_ref[...] = h1
    y = h1 * lax.rsqrt(jnp.mean(h1 * h1, axis=-1, keepdims=True) + NORM_EPS)
    y = y * ng_ref[...]
    y = y * (1.0 + sc_ref[...]) + sh_ref[...]
    hi = y.astype(BF16)
    lo = (y - hi.astype(F32)).astype(BF16)
    n_sub = y.shape[1] // LANE
    for s in range(n_sub):
        hm_ref[pl.ds(s, y.shape[0], stride=n_sub), :] = y[:, s * LANE:(s + 1) * LANE]
    r1_ref[...] = _dot(hi, wr_ref[...])
    r2_ref[...] = _dot(lo, wr_ref[...])


def _merge_fused(ya, yh, ug, wa, wh, x2d, g1, ng, sh2, sc2, wo, wr, rows_per_mod):
    n, d = x2d.shape
    ws, wh_in = ya.shape[1], yh.shape[1]
    tm = min(n, 256)
    tiles_per_mod = rows_per_mod // tm
    nr = wr.shape[1]
    mod_spec = pl.BlockSpec((None, 1, d), lambda i: (i // tiles_per_mod, 0, 0))
    const = lambda shape: pl.BlockSpec(shape, lambda i: (0, 0))
    return pl.pallas_call(
        _merge_fused_kernel,
        grid=(n // tm,),
        in_specs=[pl.BlockSpec((tm, ws), lambda i: (i, 0)),
                  pl.BlockSpec((tm, wh_in), lambda i: (i, 0)),
                  pl.BlockSpec((tm, 2 * d), lambda i: (i, 0)),
                  const((ws, d)), const((wh_in, d)),
                  pl.BlockSpec((tm, d), lambda i: (i, 0)),
                  mod_spec, const((1, d)), mod_spec, mod_spec,
                  const((d, d)), const((d, nr))],
        out_specs=[pl.BlockSpec((tm, d), lambda i: (i, 0)),
                   pl.BlockSpec((tm * (d // LANE), LANE), lambda i: (i, 0)),
                   pl.BlockSpec((tm, nr), lambda i: (i, 0)),
                   pl.BlockSpec((tm, nr), lambda i: (i, 0))],
        out_shape=[jax.ShapeDtypeStruct((n, d), F32),
                   jax.ShapeDtypeStruct((n * (d // LANE), LANE), F32),
                   jax.ShapeDtypeStruct((n, nr), F32),
                   jax.ShapeDtypeStruct((n, nr), F32)],
        compiler_params=_params(("parallel",)),
        name="merge_out_proj_norm_router",
    )(ya, yh, ug, wa, wh, x2d, g1, ng, sh2, sc2, wo, wr)


def _merge2(m, x2d, g1, ng, sh2, sc2, wo, wr, rows_per_mod):
    n, d = x2d.shape
    tm = min(n, 512)
    tiles_per_mod = rows_per_mod // tm
    nr = wr.shape[1]
    mod_spec = pl.BlockSpec((None, 1, d), lambda i: (i // tiles_per_mod, 0, 0))
    return pl.pallas_call(
        _merge2_kernel,
        grid=(n // tm,),
        in_specs=[pl.BlockSpec((tm, d), lambda i: (i, 0)),
                  pl.BlockSpec((tm, d), lambda i: (i, 0)),
                  mod_spec,
                  pl.BlockSpec((1, d), lambda i: (0, 0)),
                  mod_spec, mod_spec,
                  pl.BlockSpec((d, d), lambda i: (0, 0)),
                  pl.BlockSpec((d, nr), lambda i: (0, 0))],
        out_specs=[pl.BlockSpec((tm, d), lambda i: (i, 0)),
                   pl.BlockSpec((tm * (d // LANE), LANE), lambda i: (i, 0)),
                   pl.BlockSpec((tm, nr), lambda i: (i, 0)),
                   pl.BlockSpec((tm, nr), lambda i: (i, 0))],
        out_shape=[jax.ShapeDtypeStruct((n, d), F32),
                   jax.ShapeDtypeStruct((n * (d // LANE), LANE), F32),
                   jax.ShapeDtypeStruct((n, nr), F32),
                   jax.ShapeDtypeStruct((n, nr), F32)],
        compiler_params=_params(("parallel",)),
        name="out_proj_norm_router",
    )(m, x2d, g1, ng, sh2, sc2, wo, wr)


def _moe_kernel(be_ref, nv_ref, tok0_ref, tokn_ref, hm_ref, wgu_ref, bgu_ref, wd_ref, bd_ref, o_ref,
                xbuf_ref, acc_ref, sem, *, nj):
    i = pl.program_id(0)
    j = pl.program_id(1)
    nv = nv_ref[0]
    tm = acc_ref.shape[0]

    n_sub = hm_ref.shape[1]

    def row_copy(tok, r, slot):
        return pltpu.make_async_copy(hm_ref.at[tok], xbuf_ref.at[slot, pl.ds(r * MOE_PITCH, n_sub), :],
                                     sem.at[slot])

    def wait_block(slot):
        for r in range(tm):
            row_copy(0, r, slot).wait()

    @pl.when((i == 0) & (j == 0) & (nv > 0))
    def _():
        for r in range(tm):
            row_copy(tok0_ref[0, r], r, 0).start()

    def expert_step(first, last):
        slot = i % 2
        if first:
            wait_block(slot)
            for r in range(tm):
                row_copy(tokn_ref[0, r], r, (i + 1) % 2).start()
        x = jnp.concatenate([xbuf_ref[slot, pl.ds(s, tm, stride=MOE_PITCH), :].astype(BF16)
                             for s in range(n_sub)], axis=1)
        gu = _dot(x, wgu_ref[...]) + bgu_ref[...]
        acts = []
        for b in range(gu.shape[1] // (2 * LANE)):
            glu = jnp.minimum(gu[:, 2 * b * LANE:(2 * b + 1) * LANE], SWIGLU_LIMIT)
            lin = jnp.clip(gu[:, (2 * b + 1) * LANE:(2 * b + 2) * LANE], -SWIGLU_LIMIT, SWIGLU_LIMIT)
            acts.append((glu * jax.nn.sigmoid(SWIGLU_ALPHA * glu) * (lin + 1.0)).astype(BF16))
        part = _dot(jnp.concatenate(acts, axis=1), wd_ref[...])
        if not first:
            part = part + acc_ref[...]
        if last:
            o_ref[...] = (part + bd_ref[...]).astype(o_ref.dtype)
        else:
            acc_ref[...] = part

    variants = {1: [(True, True)], 2: [(True, False), (False, True)]}.get(
        nj, [(True, False), (False, False), (False, True)])
    for first, last in variants:
        cond = (i < nv) & ((j == 0) if first else (j > 0)) & ((j == nj - 1) if last else (j < nj - 1))
        pl.when(cond)(functools.partial(expert_step, first, last))

    @pl.when((i == nv - 1) & (j == nj - 1))
    def _():
        wait_block((i + 1) % 2)

    @pl.when((i >= nv) & (j == nj - 1))
    def _():
        o_ref[...] = jnp.zeros_like(o_ref)


def _moe_blocks(hm, slot_tok, block_expert, n_valid, wgu, bgu, wd, bd):
    d = wd.shape[2]
    assert hm.shape[1] <= MOE_PITCH and hm.shape[1] * hm.shape[2] == d
    f = wd.shape[1]
    tm = MOE_TM
    tf = min(f, MOE_TF)
    nj = f // tf
    nblk = block_expert.shape[0]

    def jeff(i, j, nv):
        return jnp.where(i < nv[0], j, nj - 1)

    grid_spec = pltpu.PrefetchScalarGridSpec(
        num_scalar_prefetch=2,
        grid=(nblk, nj),
        in_specs=[pl.BlockSpec((None, 1, tm), lambda i, j, be, nv: (0, 0, 0), memory_space=pltpu.SMEM),
                  pl.BlockSpec((None, 1, tm), lambda i, j, be, nv: (jnp.minimum(i + 1, nblk - 1), 0, 0),
                               memory_space=pltpu.SMEM),
                  pl.BlockSpec(memory_space=pl.ANY),
                  pl.BlockSpec((None, d, 2 * tf), lambda i, j, be, nv: (be[i], 0, jeff(i, j, nv))),
                  pl.BlockSpec((None, 1, 2 * tf), lambda i, j, be, nv: (be[i], 0, jeff(i, j, nv))),
                  pl.BlockSpec((None, tf, d), lambda i, j, be, nv: (be[i], jeff(i, j, nv), 0)),
                  pl.BlockSpec((None, 1, d), lambda i, j, be, nv: (be[i], 0, 0))],
        out_specs=pl.BlockSpec((tm, d), lambda i, j, be, nv: (i, 0)),
        scratch_shapes=[pltpu.VMEM((2, tm * MOE_PITCH, LANE), F32), pltpu.VMEM((tm, d), F32),
                        pltpu.SemaphoreType.DMA((2,))])
    return pl.pallas_call(
        functools.partial(_moe_kernel, nj=nj),
        grid_spec=grid_spec,
        out_shape=jax.ShapeDtypeStruct((nblk * tm, d), BF16),
        compiler_params=_params(("arbitrary", "arbitrary")),
        name="moe_experts",
    )(block_expert, n_valid, slot_tok, slot_tok, hm, wgu, bgu, wd, bd)


def _route(logits, n_experts):
    n_tok = logits.shape[0]
    top_val, top_idx = lax.top_k(logits, TOP_K)
    gates = jax.nn.softmax(top_val, axis=-1)
    n_assign = n_tok * TOP_K
    flat_e = top_idx.reshape(-1).astype(jnp.int32)
    experts = jnp.arange(n_experts, dtype=jnp.int32)
    onehot = (flat_e[:, None] == experts[None, :]).astype(jnp.int32)
    csum = jnp.cumsum(onehot, axis=0)
    rank = jnp.sum(onehot * csum, axis=1) - 1
    counts = csum[-1]
    padded = (counts + MOE_TM - 1) // MOE_TM * MOE_TM
    start = jnp.cumsum(counts) - counts
    padded_end = jnp.cumsum(padded)
    padded_start = padded_end - padded
    dest = padded_start[flat_e] + rank
    n_blocks = -(-n_assign // MOE_TM) + n_experts
    cap = n_blocks * MOE_TM
    n_valid = (padded_end[-1] // MOE_TM).astype(jnp.int32)
    order = jnp.argsort(flat_e)
    slots = jnp.arange(cap, dtype=jnp.int32)
    slot_e = jnp.minimum(jnp.sum((padded_end[None, :] <= slots[:, None]).astype(jnp.int32), axis=1), n_experts - 1)
    slot_rank = slots - padded_start[slot_e]
    src = jnp.minimum(start[slot_e] + slot_rank, n_assign - 1)
    slot_tok = jnp.where(slot_rank < counts[slot_e], order[src] // TOP_K, 0).astype(jnp.int32)
    block_start = jnp.minimum(jnp.arange(n_blocks, dtype=jnp.int32), n_valid - 1) * MOE_TM
    block_expert = jnp.minimum(
        jnp.sum((padded_end[None, :] <= block_start[:, None]).astype(jnp.int32), axis=1), n_experts - 1)
    return gates, slot_tok, dest, block_expert, n_valid


def _final_kernel(h1_ref, *rest):
    yg_refs = rest[:TOP_K]
    gates_ref, g2_ref, fg_ref, o_ref = rest[TOP_K:]
    gates = gates_ref[...]
    moe = yg_refs[0][...].astype(F32) * gates[:, 0:1]
    for k in range(1, TOP_K):
        moe = moe + yg_refs[k][...].astype(F32) * gates[:, k:k + 1]
    h = h1_ref[...] + g2_ref[...] * moe
    y = h * lax.rsqrt(jnp.mean(h * h, axis=-1, keepdims=True) + NORM_EPS)
    o_ref[...] = (y * fg_ref[...]).astype(o_ref.dtype)


def _final(h1, yg, gates, g2, fg, rows_per_mod):
    n, d = h1.shape
    tm = min(n, 512)
    tiles_per_mod = rows_per_mod // tm
    nt = n // tm
    yg_specs = [pl.BlockSpec((tm, d), functools.partial(lambda i, k: (k * nt + i, 0), k=k)) for k in range(TOP_K)]
    return pl.pallas_call(
        _final_kernel,
        grid=(nt,),
        in_specs=[pl.BlockSpec((tm, d), lambda i: (i, 0))] + yg_specs + [
                  pl.BlockSpec((tm, TOP_K), lambda i: (i, 0)),
                  pl.BlockSpec((None, 1, d), lambda i: (i // tiles_per_mod, 0, 0)),
                  pl.BlockSpec((1, d), lambda i: (0, 0))],
        out_specs=pl.BlockSpec((tm, d), lambda i: (i, 0)),
        out_shape=jax.ShapeDtypeStruct((n, d), F32),
        compiler_params=_params(("parallel",)),
        name="combine_final_norm",
    )(h1, *([yg] * TOP_K), gates, g2, fg)


def kernel(x, c, ctx, c_ctx, w_ada, b_ada, norm1_g, norm2_g, w_in, s5_lam_re, s5_lam_im, s5_log_dt, s5_b_re, s5_b_im, s5_c_re, s5_c_im, s5_d, s5_w_glu, hy_short_w, hy_pos_w1, hy_pos_b1, hy_freq1, hy_pos_w2, hy_pos_b2, hy_freq2, hy_pos_w3, hy_pos_b3, hy_decay, hy_bias, w_branch_s5, w_branch_hy, w_out, router_w, router_b, w_gate_up, b_gate_up, w_down, b_down, final_g):
    bsz, seq, d = x.shape
    ctx_len = ctx.shape[1]
    depth = w_ada.shape[0]
    assert depth == 1, "only the single-layer configuration is implemented"
    n_experts = router_w.shape[2]
    s5_w = s5_d.shape[1]
    hy_w = hy_bias.shape[2]
    n_tok = bsz * seq

    cc = jnp.zeros((8, d), F32).at[:bsz].set(c).at[bsz].set(c_ctx)
    mod = _ada_mod(cc, w_ada[0], b_ada[0][None]).reshape(8, 6, 1, d)
    sh1, sc1, g1, sh2, sc2, g2 = [mod[:bsz, k] for k in range(6)]
    csh1, csc1 = mod[bsz:bsz + 1, 0], mod[bsz:bsz + 1, 1]

    w_in_b = w_in[0].astype(BF16)
    x2d = x.reshape(n_tok, d)
    riders = ((w_gate_up[0].reshape(-1, w_gate_up.shape[3]), True), (w_down[0].reshape(-1, d), False))
    ua, uh, ug, wgu_b, wd_b = _norm_mod_proj(x2d, norm1_g[0][None], sh1, sc1, w_in_b,
                                             (s5_w, 3 * hy_w, 2 * d), (False, False, True), seq, 1024, 512,
                                             riders=riders)
    (uca,) = _norm_mod_proj(ctx.reshape(bsz * ctx_len, d), norm1_g[0][None], csh1, csc1,
                            w_in_b[:, :s5_w], (s5_w,), (False,), bsz * ctx_len, 512, 1024)

    ya = _s5_branch(ua, uca, bsz, s5_lam_re[0], s5_lam_im[0], s5_log_dt[0], s5_b_re[0], s5_b_im[0],
                    s5_c_re[0], s5_c_im[0], s5_d[0], s5_w_glu[0])

    yh = _hyena_branch(uh, bsz, seq, hy_short_w[0], hy_pos_w1[0], hy_pos_b1[0], hy_freq1[0], hy_pos_w2[0],
                       hy_pos_b2[0], hy_freq2[0], hy_pos_w3[0], hy_pos_b3[0], hy_decay[0], hy_bias[0])

    rw = router_w[0]
    rw_hi = rw.astype(BF16)
    rw_lo = (rw - rw_hi.astype(F32)).astype(BF16)
    wr = jnp.zeros((d, 128), BF16).at[:, :n_experts].set(rw_hi).at[:, n_experts:2 * n_experts].set(rw_lo)
    h1, hm, r1, r2 = _merge_fused(ya, yh, ug, w_branch_s5[0].astype(BF16), w_branch_hy[0].astype(BF16), x2d, g1,
                                  norm2_g[0][None], sh2, sc2, w_out[0].astype(BF16), wr, seq)
    logits = (r1[:, :n_experts] + r1[:, n_experts:2 * n_experts] + r2[:, :n_experts]) + router_b[0]

    gates, slot_tok, dest, block_expert, n_valid = _route(logits, n_experts)
    f = w_down.shape[2]
    bgu = b_gate_up[0].reshape(n_experts, f // LANE, LANE, 2).swapaxes(2, 3).reshape(n_experts, 1, 2 * f)
    yslots = _moe_blocks(hm.reshape(n_tok, d // LANE, LANE), slot_tok.reshape(-1, 1, MOE_TM), block_expert,
                         n_valid.reshape(1),
                         wgu_b.reshape(w_gate_up.shape[1:]), bgu, wd_b.reshape(w_down.shape[1:]),
                         b_down[0][:, None, :])
    yg = yslots[dest.reshape(n_tok, TOP_K).T.reshape(-1)]
    out = _final(h1, yg, gates, g2, final_g[None], seq)
    return out.reshape(bsz, seq, d)
```
